```python
import math
import jax, jax.numpy as jnp
from jax import lax
import numpy as np

D_MODEL = 1024
BATCH = 8
SEQ = 8192
DEPTH = 4

CHUNK = 64
N_PREV = 8
BAND = (N_PREV + 1) * CHUNK
N_HEADS = 16
HEAD_DIM = D_MODEL // N_HEADS
E_MIX = N_HEADS * HEAD_DIM
REL_CLIP = 128
N_REL = 2 * REL_CLIP + 1
CONV_W = 3
N_MEM = 256
MEM_HEADS = 4
MEM_HEAD_DIM = 128
E_MEM = MEM_HEADS * MEM_HEAD_DIM
E_BRANCH = E_MIX + E_MEM
N_IN = 3 * E_MIX + E_MEM + E_BRANCH
N_MIXERS = 2
N_ATTN_LAYERS = (DEPTH + 1) // 2
N_CONV_LAYERS = DEPTH // 2
DN_ALPHA = (2.0 * DEPTH) ** 0.25
DN_BETA = (8.0 * DEPTH) ** -0.25
LN_EPS = 1e-5

kernel_name = "hybrid_chunk_attn_shortconv_mem_deepnorm"


def layer_norm(x, g, b):
    xf = x.astype(jnp.float32)
    mu = jnp.mean(xf, axis=-1, keepdims=True)
    var = jnp.mean(jnp.square(xf - mu), axis=-1, keepdims=True)
    y = (xf - mu) * lax.rsqrt(var + LN_EPS) * g.astype(jnp.float32) + b.astype(jnp.float32)
    return y.astype(x.dtype)


def rel_bias_band(table):
    i = jnp.arange(CHUNK)[:, None]
    m = jnp.arange(BAND)[None, :]
    rel = N_PREV * CHUNK + i - m
    idx = jnp.clip(rel, -REL_CLIP, REL_CLIP) + REL_CLIP
    return table[:, idx]


def chunked_attention(q, k, v, bias_table):
    b, s, h, dh = q.shape
    n_chunks = s // CHUNK
    pad = ((0, 0), (N_PREV * CHUNK, 0), (0, 0), (0, 0))
    k_pad = jnp.pad(k, pad)
    v_pad = jnp.pad(v, pad)
    bias = rel_bias_band(bias_table).astype(jnp.float32)
    scale = 1.0 / math.sqrt(dh)
    q_blocks = jnp.moveaxis(q.reshape(b, n_chunks, CHUNK, h, dh), 1, 0)
    neg = jnp.finfo(jnp.float32).min

    def one_chunk(args):
        q_blk, c = args
        k_band = lax.dynamic_slice_in_dim(k_pad, c * CHUNK, BAND, axis=1)
        v_band = lax.dynamic_slice_in_dim(v_pad, c * CHUNK, BAND, axis=1)
        sc = jnp.einsum('bqhd,bkhd->bhqk', q_blk, k_band).astype(jnp.float32) * scale + bias[None]
        key_pos = (c - N_PREV) * CHUNK + jnp.arange(BAND)
        sc = jnp.where((key_pos >= 0)[None, None, None, :], sc, neg)
        p = jax.nn.softmax(sc, axis=-1).astype(v_band.dtype)
        return jnp.einsum('bhqk,bkhd->bqhd', p, v_band)

    out = lax.map(one_chunk, (q_blocks, jnp.arange(n_chunks)))
    return jnp.moveaxis(out, 0, 1).reshape(b, s, h * dh)


def causal_dwconv(u, w):
    c = u.shape[-1]
    return lax.conv_general_dilated(
        u, w[:, None, :].astype(u.dtype), window_strides=(1,),
        padding=[(CONV_W - 1, 0)], dimension_numbers=('NWC', 'WIO', 'NWC'),
        feature_group_count=c)


def short_gated_conv(bg, cg, u, w):
    return bg * causal_dwconv(cg * u, w)


def memory_attention(q_mem, kv_mem):
    b, s, _ = q_mem.shape
    q = q_mem.reshape(b, s, MEM_HEADS, MEM_HEAD_DIM)
    k, v = jnp.split(kv_mem, 2, axis=-1)
    k = k.reshape(b, -1, MEM_HEADS, MEM_HEAD_DIM)
    v = v.reshape(b, -1, MEM_HEADS, MEM_HEAD_DIM)
    sc = jnp.einsum('bshd,bmhd->bhsm', q, k).astype(jnp.float32) / math.sqrt(MEM_HEAD_DIM)
    p = jax.nn.softmax(sc, axis=-1).astype(v.dtype)
    return jnp.einsum('bhsm,bmhd->bshd', p, v).reshape(b, s, E_MEM)


def _fwd_setup_inputs(seed: int = 0) -> dict:
    key = jax.random.key(seed)
    ks = jax.random.split(key, 10)
    x = jax.random.normal(ks[0], (BATCH, SEQ, D_MODEL), jnp.float32)
    mem = jax.random.normal(ks[1], (BATCH, N_MEM, D_MODEL), jnp.float32)
    w_in = jax.random.normal(ks[2], (DEPTH, D_MODEL, N_IN), jnp.float32) * D_MODEL ** -0.5
    w_mem_kv = jax.random.normal(ks[3], (DEPTH, D_MODEL, 2 * E_MEM), jnp.float32) * D_MODEL ** -0.5
    w_out = jax.random.normal(ks[4], (DEPTH, E_BRANCH, D_MODEL), jnp.float32) * (E_BRANCH ** -0.5 * DN_BETA)
    rel_bias = jax.random.normal(ks[5], (N_ATTN_LAYERS, N_HEADS, N_REL), jnp.float32) * 0.5
    conv_w = jax.random.normal(ks[6], (N_CONV_LAYERS, CONV_W, E_MIX), jnp.float32) * CONV_W ** -0.5
    ln_g = 1.0 + 0.05 * jax.random.normal(ks[7], (DEPTH, D_MODEL), jnp.float32)
    ln_b = 0.02 * jax.random.normal(ks[8], (DEPTH, D_MODEL), jnp.float32)
    return {"x": x, "mem": mem, "w_in": w_in, "w_mem_kv": w_mem_kv, "w_out": w_out,
            "rel_bias": rel_bias, "conv_w": conv_w, "ln_g": ln_g, "ln_b": ln_b}


def _fwd_reference(x, mem, w_in, w_mem_kv, w_out, rel_bias, conv_w, ln_g, ln_b):
    b, s, _ = x.shape
    for layer in range(DEPTH):
        h = jnp.einsum('bsd,de->bse', x, w_in[layer])
        mix_in = h[..., :3 * E_MIX]
        q_mem = h[..., 3 * E_MIX:3 * E_MIX + E_MEM]
        z = h[..., 3 * E_MIX + E_MEM:]
        p0, p1, p2 = jnp.split(mix_in, 3, axis=-1)
        if layer % N_MIXERS == 0:
            q = p0.reshape(b, s, N_HEADS, HEAD_DIM)
            k = p1.reshape(b, s, N_HEADS, HEAD_DIM)
            v = p2.reshape(b, s, N_HEADS, HEAD_DIM)
            mix_out = chunked_attention(q, k, v, rel_bias[layer // N_MIXERS])
        else:
            mix_out = short_gated_conv(p0, p1, p2, conv_w[layer // N_MIXERS])
        kv_mem = jnp.einsum('bmd,de->bme', mem, w_mem_kv[layer])
        mem_out = memory_attention(q_mem, kv_mem)
        y = jnp.concatenate([mix_out, mem_out], axis=-1) * jax.nn.silu(z)
        out = jnp.einsum('bse,ed->bsd', y, w_out[layer])
        x = layer_norm(DN_ALPHA * x + out, ln_g[layer], ln_b[layer])
    return x


import jax as _jax
import jax.numpy as _jnp

TWIN_FORMAT = 'train_step'
FWD_PARAMS = ['x', 'mem', 'w_in', 'w_mem_kv', 'w_out', 'rel_bias', 'conv_w', 'ln_g', 'ln_b']
TWIN_WEIGHTS = ['w_in', 'w_mem_kv', 'w_out', 'rel_bias', 'conv_w', 'ln_g', 'ln_b']
TWIN_DIFF_INPUT = 'x'
TWIN_INPUTS = ['x', 'mem', 'w_in', 'w_mem_kv', 'w_out', 'rel_bias', 'conv_w', 'ln_g', 'ln_b', 'loss_target', 'm_w_in', 'm_w_mem_kv', 'm_w_out', 'm_rel_bias', 'm_conv_w', 'm_ln_g', 'm_ln_b', 'v_w_in', 'v_w_mem_kv', 'v_w_out', 'v_rel_bias', 'v_conv_w', 'v_ln_g', 'v_ln_b']
TWIN_OUTPUTS = ['loss', 'grad_x', 'grad_w_in', 'grad_w_mem_kv', 'grad_w_out', 'grad_rel_bias', 'grad_conv_w', 'grad_ln_g', 'grad_ln_b', 'delta_w_in', 'delta_w_mem_kv', 'delta_w_out', 'delta_rel_bias', 'delta_conv_w', 'delta_ln_g', 'delta_ln_b', 'new_m_w_in', 'new_m_w_mem_kv', 'new_m_w_out', 'new_m_rel_bias', 'new_m_conv_w', 'new_m_ln_g', 'new_m_ln_b', 'new_v_w_in', 'new_v_w_mem_kv', 'new_v_w_out', 'new_v_rel_bias', 'new_v_conv_w', 'new_v_ln_g', 'new_v_ln_b']
TWIN_LEAF_KINDS = {'loss': 'loss', 'grad_x': 'grad_x', 'grad_w_in': 'grad_w', 'grad_w_mem_kv': 'grad_w', 'grad_w_out': 'grad_w', 'grad_rel_bias': 'grad_w', 'grad_conv_w': 'grad_w', 'grad_ln_g': 'grad_w', 'grad_ln_b': 'grad_w', 'delta_w_in': 'delta_w', 'delta_w_mem_kv': 'delta_w', 'delta_w_out': 'delta_w', 'delta_rel_bias': 'delta_w', 'delta_conv_w': 'delta_w', 'delta_ln_g': 'delta_w', 'delta_ln_b': 'delta_w', 'new_m_w_in': 'new_m', 'new_m_w_mem_kv': 'new_m', 'new_m_w_out': 'new_m', 'new_m_rel_bias': 'new_m', 'new_m_conv_w': 'new_m', 'new_m_ln_g': 'new_m', 'new_m_ln_b': 'new_m', 'new_v_w_in': 'new_v', 'new_v_w_mem_kv': 'new_v', 'new_v_w_out': 'new_v', 'new_v_rel_bias': 'new_v', 'new_v_conv_w': 'new_v', 'new_v_ln_g': 'new_v', 'new_v_ln_b': 'new_v'}


def _forward(args):
    return _fwd_reference(*[args[k] for k in FWD_PARAMS])


def _output_shape():
    def fwd():
        inp = _fwd_setup_inputs(0)
        return _fwd_reference(*[inp[k] for k in FWD_PARAMS])
    out = _jax.eval_shape(fwd)
    return out.shape, out.dtype

N_MICROBATCH = 1
ADAM_LR = 0.001
ADAM_B1 = 0.9
ADAM_B2 = 0.999
ADAM_EPS = 1e-08
ADAM_WD = 0.01
ADAM_STEP = 10
PER_EXAMPLE_BATCH_AXIS = {'x': 0, 'mem': 0, 'loss_target': 0}
SHARED_INPUTS = []
_WEIGHT_DTYPES = {'w_in': _jnp.float32, 'w_mem_kv': _jnp.float32, 'w_out': _jnp.float32, 'rel_bias': _jnp.float32, 'conv_w': _jnp.float32, 'ln_g': _jnp.float32, 'ln_b': _jnp.float32}
MOMENT_SCALE = {'w_in': 2.001682e-02, 'w_mem_kv': 3.337138e-03, 'w_out': 5.297034e-02, 'rel_bias': 2.365559e-03, 'conv_w': 3.099266e-02, 'ln_g': 3.291884e+01, 'ln_b': 1.406438e+00}


def _to_microbatches(a, axis):
    t = _jnp.moveaxis(a, axis, 0)
    t = t.reshape((N_MICROBATCH, t.shape[0] // N_MICROBATCH) + t.shape[1:])
    return _jnp.moveaxis(t, 1, axis + 1)


def setup_inputs(seed: int = 0) -> dict:
    inp = _fwd_setup_inputs(seed)
    key = _jax.random.fold_in(_jax.random.key(seed), 7919)
    shape, _ = _output_shape()
    out = dict(inp)
    out["loss_target"] = _jax.random.normal(_jax.random.fold_in(key, 0), shape, _jnp.float32)
    for i, name in enumerate(TWIN_WEIGHTS):
        w = inp[name].astype(_jnp.float32)
        if MOMENT_SCALE is None:
            s = _jnp.sqrt(_jnp.mean(_jnp.square(w)) + 1e-30)
        else:
            s = MOMENT_SCALE[name]
        km, kv = _jax.random.split(_jax.random.fold_in(key, i + 1))
        out[name] = w
        out["m_" + name] = s * _jax.random.normal(km, w.shape, _jnp.float32)
        out["v_" + name] = (s * s) * _jax.random.uniform(kv, w.shape, _jnp.float32, 0.5, 1.5)
    if N_MICROBATCH > 1:
        for name, axis in PER_EXAMPLE_BATCH_AXIS.items():
            out[name] = _to_microbatches(out[name], axis)
    return {'x': out['x'], 'mem': out['mem'], 'w_in': out['w_in'], 'w_mem_kv': out['w_mem_kv'], 'w_out': out['w_out'], 'rel_bias': out['rel_bias'], 'conv_w': out['conv_w'], 'ln_g': out['ln_g'], 'ln_b': out['ln_b'], 'loss_target': out['loss_target'], 'm_w_in': out['m_w_in'], 'm_w_mem_kv': out['m_w_mem_kv'], 'm_w_out': out['m_w_out'], 'm_rel_bias': out['m_rel_bias'], 'm_conv_w': out['m_conv_w'], 'm_ln_g': out['m_ln_g'], 'm_ln_b': out['m_ln_b'], 'v_w_in': out['v_w_in'], 'v_w_mem_kv': out['v_w_mem_kv'], 'v_w_out': out['v_w_out'], 'v_rel_bias': out['v_rel_bias'], 'v_conv_w': out['v_conv_w'], 'v_ln_g': out['v_ln_g'], 'v_ln_b': out['v_ln_b']}


def _loss(weights, diff, rest, loss_target):
    with _jax.named_scope("forward"):
        args = {**rest, TWIN_DIFF_INPUT: diff, **{k: w.astype(_WEIGHT_DTYPES[k]) for k, w in weights.items()}}
        y = _forward(args)
    with _jax.named_scope("loss_head"):
        err = _jnp.square(y.astype(_jnp.float32) - loss_target)
        return 0.5 * _jnp.sum(_jnp.mean(err, axis=-1)) if err.ndim else 0.5 * err


def _adamw(w, g, m, v):
    m = ADAM_B1 * m + (1.0 - ADAM_B1) * g
    v = ADAM_B2 * v + (1.0 - ADAM_B2) * _jnp.square(g)
    m_hat = m / (1.0 - ADAM_B1 ** ADAM_STEP)
    v_hat = v / (1.0 - ADAM_B2 ** ADAM_STEP)
    delta = -ADAM_LR * (m_hat / (_jnp.sqrt(v_hat) + ADAM_EPS) + ADAM_WD * w)
    return delta, m, v


def reference(x, mem, w_in, w_mem_kv, w_out, rel_bias, conv_w, ln_g, ln_b, loss_target, m_w_in, m_w_mem_kv, m_w_out, m_rel_bias, m_conv_w, m_ln_g, m_ln_b, v_w_in, v_w_mem_kv, v_w_out, v_rel_bias, v_conv_w, v_ln_g, v_ln_b):
    given = dict(x=x, mem=mem, w_in=w_in, w_mem_kv=w_mem_kv, w_out=w_out, rel_bias=rel_bias, conv_w=conv_w, ln_g=ln_g, ln_b=ln_b, loss_target=loss_target, m_w_in=m_w_in, m_w_mem_kv=m_w_mem_kv, m_w_out=m_w_out, m_rel_bias=m_rel_bias, m_conv_w=m_conv_w, m_ln_g=m_ln_g, m_ln_b=m_ln_b, v_w_in=v_w_in, v_w_mem_kv=v_w_mem_kv, v_w_out=v_w_out, v_rel_bias=v_rel_bias, v_conv_w=v_conv_w, v_ln_g=v_ln_g, v_ln_b=v_ln_b)
    weights = {n: given[n] for n in TWIN_WEIGHTS}
    shared = {n: given[n] for n in SHARED_INPUTS}
    per_example = {n: given[n] for n in ['x', 'mem']}
    grad_fn = _jax.value_and_grad(_loss, argnums=(0, 1))

    def one_microbatch(ex, loss_target):
        ex = dict(ex)
        diff = ex.pop(TWIN_DIFF_INPUT)
        return grad_fn(weights, diff, {**shared, **ex}, loss_target)

    if N_MICROBATCH == 1:
        loss, (grad_w, grad_x) = one_microbatch(per_example, given["loss_target"])
    else:
        def body(carry, xs):
            loss_sum, grad_sum = carry
            l_k, (gw_k, gx_k) = one_microbatch(xs[0], xs[1])
            with _jax.named_scope("update"):
                return (loss_sum + l_k, _jax.tree.map(_jnp.add, grad_sum, gw_k)), gx_k

        init = (_jnp.zeros((), _jnp.float32), _jax.tree.map(_jnp.zeros_like, weights))
        (loss, grad_w), grad_x = _jax.lax.scan(body, init, (per_example, given["loss_target"]))
    with _jax.named_scope("update"):
        delta_w, new_m, new_v = {}, {}, {}
        for n in TWIN_WEIGHTS:
            delta_w[n], new_m[n], new_v[n] = _adamw(weights[n], grad_w[n], given["m_" + n], given["v_" + n])
    return (loss, grad_x, *[grad_w[n] for n in TWIN_WEIGHTS], *[delta_w[n] for n in TWIN_WEIGHTS],
            *[new_m[n] for n in TWIN_WEIGHTS], *[new_v[n] for n in TWIN_WEIGHTS])
```

```python
import functools
import math

import jax
import jax.numpy as jnp
from jax import lax
from jax.experimental import pallas as pl
from jax.experimental.pallas import tpu as pltpu

f32, bf16 = jnp.float32, jnp.bfloat16

D_MODEL = 1024
DEPTH = 4
CHUNK = 64
N_PREV = 8
N_HEADS = 16
HEAD_DIM = 64
E_MIX = 1024
REL_CLIP = 128
N_REL = 2 * REL_CLIP + 1
N_REL_PAD = 384
CONV_W = 3
N_MEM = 256
MEM_HEADS = 4
MEM_HEAD_DIM = 128
E_MEM = 512
E_BRANCH = E_MIX + E_MEM
N_IN = 3 * E_MIX + E_MEM + E_BRANCH
N_CHIPS = 4
W_IN_COLS = N_IN // N_CHIPS
W_KV_ROWS = D_MODEL // N_CHIPS
W_OUT_ROWS = E_BRANCH // N_CHIPS
DN_ALPHA = (2.0 * DEPTH) ** 0.25
LN_EPS = 1e-5
ADAM_LR, ADAM_B1, ADAM_B2, ADAM_EPS, ADAM_WD, ADAM_STEP = 0.001, 0.9, 0.999, 1e-08, 0.01, 10

QG = 4 * CHUNK
KG = QG + N_PREV * CHUNK
NEG = -1e30
VMEM_LIMIT = 56 * 1024 * 1024

NT = (((1,), (1,)), ((), ()))
TN = (((0,), (0,)), ((), ()))
MESH = pl.DeviceIdType.MESH
ANY = pl.BlockSpec(memory_space=pl.ANY)


def _pcall(body, **kw):
    return pl.pallas_call(body, **kw)


def _params(*sem):
    return pltpu.CompilerParams(dimension_semantics=sem, vmem_limit_bytes=VMEM_LIMIT)


def _silu_parts(z):
    sig = 1.0 / (1.0 + jnp.exp(-z))
    return z * sig, sig


def _gather_weights(win_s, wkv_s, wout_s, cw_s):
    def body(win_ref, wkv_ref, wout_ref, cw_ref, win_g, wkv_g, wout_g, cw_g, send_sems, recv_sems, local_sems):
        x, y, c = lax.axis_index("x"), lax.axis_index("y"), lax.axis_index("c")
        me = 2 * x + y
        sibling = (x, y, 1 - c)
        chips = [(1 - x, y), (x, 1 - y), (1 - x, 1 - y)]
        mine, theirs = pl.ds(2 * c, 2), pl.ds(2 * (1 - c), 2)
        big = [(win_ref, win_g), (wkv_ref, wkv_g), (wout_ref, wout_g)]

        def rcopy(k, src, dst, to):
            return pltpu.make_async_remote_copy(src_ref=src, dst_ref=dst, send_sem=send_sems.at[k],
                                                recv_sem=recv_sems.at[k], device_id=to, device_id_type=MESH)

        local = [pltpu.make_async_copy(s, g.at[:, me], local_sems.at[a]) for a, (s, g) in enumerate(big)]
        local.append(pltpu.make_async_copy(cw_ref, cw_g.at[me], local_sems.at[3]))
        for cp in local:
            cp.start()
        sends = []
        for a, (s, g) in enumerate(big):
            for p, (px, py) in enumerate(chips):
                sends.append(rcopy(3 * a + p, s.at[mine], g.at[mine, me], (px, py, c)))
        for p, (px, py) in enumerate(chips):
            sends.append(rcopy(18 + p, cw_ref, cw_g.at[me], (px, py, c)))
        for cp in sends:
            cp.start()
        for a, (s, g) in enumerate(big):
            for p, (px, py) in enumerate(chips):
                jp = 2 * px + py
                rcopy(3 * a + p, s.at[mine], g.at[mine, jp], (px, py, c)).wait_recv()
                fwd = rcopy(9 + 3 * a + p, g.at[mine, jp], g.at[mine, jp], sibling)
                fwd.start()
                sends.append(fwd)
        for p, (px, py) in enumerate(chips):
            rcopy(18 + p, cw_ref, cw_g.at[2 * px + py], (px, py, c)).wait_recv()
        for a, (s, g) in enumerate(big):
            for p, (px, py) in enumerate(chips):
                jp = 2 * px + py
                rcopy(9 + 3 * a + p, g.at[theirs, jp], g.at[theirs, jp], sibling).wait_recv()
        for cp in sends:
            cp.wait_send()
        for cp in local:
            cp.wait()

    out_shape = (
        jax.ShapeDtypeStruct((DEPTH, N_CHIPS) + win_s.shape[1:], bf16),
        jax.ShapeDtypeStruct((DEPTH, N_CHIPS) + wkv_s.shape[1:], bf16),
        jax.ShapeDtypeStruct((DEPTH, N_CHIPS) + wout_s.shape[1:], bf16),
        jax.ShapeDtypeStruct((N_CHIPS,) + cw_s.shape, f32),
    )
    return _pcall(body, name="gather_weights", out_shape=out_shape, in_specs=[ANY] * 4, out_specs=[ANY] * 4,
                  scratch_shapes=[pltpu.SemaphoreType.DMA((21,)), pltpu.SemaphoreType.DMA((21,)),
                                  pltpu.SemaphoreType.DMA((4,))])(win_s, wkv_s, wout_s, cw_s)


def _small_allreduce(buf):
    rows, cols = buf.shape

    def body(b_ref, o_ref, slots, send_sems, recv_sems):
        x, y, c = lax.axis_index("x"), lax.axis_index("y"), lax.axis_index("c")
        me = 4 * x + 2 * y + c
        slots[me] = b_ref[...]
        copies = []
        for r in range(1, 8):
            fx, fy, fc = (r >> 2) & 1, (r >> 1) & 1, r & 1
            px, py, pc = x ^ fx, y ^ fy, c ^ fc
            copies.append(pltpu.make_async_remote_copy(
                src_ref=b_ref, dst_ref=slots.at[me], send_sem=send_sems.at[r - 1], recv_sem=recv_sems.at[r - 1],
                device_id=(px, py, pc), device_id_type=MESH))
        for cp in copies:
            cp.start()
        for r in range(1, 8):
            fx, fy, fc = (r >> 2) & 1, (r >> 1) & 1, r & 1
            peer = 4 * (x ^ fx) + 2 * (y ^ fy) + (c ^ fc)
            pltpu.make_async_remote_copy(
                src_ref=b_ref, dst_ref=slots.at[peer], send_sem=send_sems.at[r - 1], recv_sem=recv_sems.at[r - 1],
                device_id=(x ^ fx, y ^ fy, c ^ fc), device_id_type=MESH).wait_recv()
        for cp in copies:
            cp.wait_send()
        acc = slots[0]
        for d in range(1, 8):
            acc = acc + slots[d]
        o_ref[...] = acc

    return _pcall(body, name="small_allreduce", out_shape=jax.ShapeDtypeStruct((rows, cols), f32),
                  in_specs=[pl.BlockSpec(memory_space=pltpu.VMEM)], out_specs=pl.BlockSpec(memory_space=pltpu.VMEM),
                  scratch_shapes=[pltpu.VMEM((8, rows, cols), f32), pltpu.SemaphoreType.DMA((7,)),
                                  pltpu.SemaphoreType.DMA((7,))])(buf)


def _sibling_exchange(gs):
    def body(*refs):
        n = len(gs)
        g_refs, r_refs, send_sems, recv_sems = refs[:n], refs[n:2 * n], refs[2 * n], refs[2 * n + 1]
        x, y, c = lax.axis_index("x"), lax.axis_index("y"), lax.axis_index("c")
        theirs = pl.ds(2 * (1 - c), 2)
        copies = [pltpu.make_async_remote_copy(src_ref=g.at[theirs], dst_ref=r, send_sem=send_sems.at[a],
                                               recv_sem=recv_sems.at[a], device_id=(x, y, 1 - c), device_id_type=MESH)
                  for a, (g, r) in enumerate(zip(g_refs, r_refs))]
        for cp in copies:
            cp.start()
        for cp in copies:
            cp.wait()

    out_shape = tuple(jax.ShapeDtypeStruct((2,) + g.shape[1:], g.dtype) for g in gs)
    return _pcall(body, name="sibling_exchange", out_shape=out_shape, in_specs=[ANY] * len(gs), out_specs=[ANY] * len(gs),
                  scratch_shapes=[pltpu.SemaphoreType.DMA((len(gs),)), pltpu.SemaphoreType.DMA((len(gs),))])(*gs)


def _chip_scatter(ss):
    def body(*refs):
        n = len(ss)
        s_refs, r_refs, send_sems, recv_sems = refs[:n], refs[n:2 * n], refs[2 * n], refs[2 * n + 1]
        x, y, c = lax.axis_index("x"), lax.axis_index("y"), lax.axis_index("c")
        chips = [(1 - x, y), (x, 1 - y), (1 - x, 1 - y)]
        copies = []
        for a, (s, r) in enumerate(zip(s_refs, r_refs)):
            for p, (px, py) in enumerate(chips):
                copies.append(pltpu.make_async_remote_copy(
                    src_ref=s.at[:, 2 * px + py], dst_ref=r.at[p], send_sem=send_sems.at[3 * a + p],
                    recv_sem=recv_sems.at[3 * a + p], device_id=(px, py, c), device_id_type=MESH))
        for cp in copies:
            cp.start()
        for cp in copies:
            cp.wait()

    out_shape = tuple(jax.ShapeDtypeStruct((3, 2) + s.shape[2:], s.dtype) for s in ss)
    return _pcall(body, name="chip_scatter", out_shape=out_shape, in_specs=[ANY] * len(ss), out_specs=[ANY] * len(ss),
                  scratch_shapes=[pltpu.SemaphoreType.DMA((3 * len(ss),)), pltpu.SemaphoreType.DMA((3 * len(ss),))])(*ss)


def _sibling_share(fs):
    def body(*refs):
        n = len(fs)
        f_refs, o_refs, send_sems, recv_sems, local_sems = refs[:n], refs[n:2 * n], refs[2 * n], refs[2 * n + 1], refs[2 * n + 2]
        x, y, c = lax.axis_index("x"), lax.axis_index("y"), lax.axis_index("c")
        mine = pl.ds(2 * c, 2)
        local = [pltpu.make_async_copy(f, o.at[mine], local_sems.at[a]) for a, (f, o) in enumerate(zip(f_refs, o_refs))]
        copies = [pltpu.make_async_remote_copy(src_ref=f, dst_ref=o.at[mine], send_sem=send_sems.at[a],
                                               recv_sem=recv_sems.at[a], device_id=(x, y, 1 - c), device_id_type=MESH)
                  for a, (f, o) in enumerate(zip(f_refs, o_refs))]
        for cp in local + copies:
            cp.start()
        for cp in copies:
            cp.wait()
        for cp in local:
            cp.wait()

    out_shape = tuple(jax.ShapeDtypeStruct((4,) + f.shape[1:], f.dtype) for f in fs)
    return _pcall(body, name="sibling_share", out_shape=out_shape, in_specs=[ANY] * len(fs), out_specs=[ANY] * len(fs),
                  scratch_shapes=[pltpu.SemaphoreType.DMA((len(fs),)), pltpu.SemaphoreType.DMA((len(fs),)),
                                  pltpu.SemaphoreType.DMA((len(fs),))])(*fs)


def _pair_sum(g, r, c_arr):
    _, _, rows, cols = g.shape
    g3, r3 = g.reshape(4, 4 * rows, cols), r.reshape(2, 4 * rows, cols)
    br = 512 if (4 * rows) % 512 == 0 else 256

    def body(c_ref, g_ref, r_ref, o_ref):
        o_ref[...] = (g_ref[...].astype(f32) + r_ref[...].astype(f32)).astype(bf16)

    out = _pcall(
        body, name="pair_sum", out_shape=jax.ShapeDtypeStruct(r3.shape, bf16),
        grid_spec=pltpu.PrefetchScalarGridSpec(
            num_scalar_prefetch=1, grid=(2, 4 * rows // br),
            in_specs=[pl.BlockSpec((1, br, cols), lambda a, i, c_ref: (2 * c_ref[0] + a, i, 0)),
                      pl.BlockSpec((1, br, cols), lambda a, i, c_ref: (a, i, 0))],
            out_specs=pl.BlockSpec((1, br, cols), lambda a, i, c_ref: (a, i, 0))),
        compiler_params=_params("arbitrary", "arbitrary"))(c_arr, g3, r3)
    return out.reshape(2, 4, rows, cols)


def _chip_sum(s, r, me_arr):
    _, _, rows, cols = s.shape
    br = 256 if rows % 256 == 0 else 128

    def body(me_ref, s_ref, r_ref, o_ref):
        acc = s_ref[0, 0].astype(f32)
        for p in range(3):
            acc = acc + r_ref[p, 0].astype(f32)
        o_ref[0] = acc

    return _pcall(
        body, name="chip_sum", out_shape=jax.ShapeDtypeStruct((2, rows, cols), f32),
        grid_spec=pltpu.PrefetchScalarGridSpec(
            num_scalar_prefetch=1, grid=(2, rows // br),
            in_specs=[pl.BlockSpec((1, 1, br, cols), lambda a, i, me_ref: (a, me_ref[0], i, 0)),
                      pl.BlockSpec((3, 1, br, cols), lambda a, i, me_ref: (0, a, i, 0))],
            out_specs=pl.BlockSpec((1, br, cols), lambda a, i, me_ref: (a, i, 0))),
        compiler_params=_params("arbitrary", "arbitrary"))(me_arr, s, r)


def _inproj(xb, win_g, layer):
    T = xb.shape[0]
    tm = 512

    def body(x_ref, w_ref, o_ref):
        xt = x_ref[...]
        for j in range(N_CHIPS):
            o_ref[:, j * W_IN_COLS:(j + 1) * W_IN_COLS] = jnp.dot(xt, w_ref[j], preferred_element_type=f32).astype(bf16)

    return _pcall(
        body, name=f"inproj_{layer}", grid=(T // tm,), out_shape=jax.ShapeDtypeStruct((T, N_IN), bf16),
        in_specs=[pl.BlockSpec((tm, D_MODEL), lambda i: (i, 0)),
                  pl.BlockSpec((None, N_CHIPS, D_MODEL, W_IN_COLS), lambda i: (layer, 0, 0, 0), pipeline_mode=pl.Buffered(1))],
        out_specs=pl.BlockSpec((tm, N_IN), lambda i: (i, 0)),
        compiler_params=_params("arbitrary"))(xb, win_g)


def _rel_index_rows():
    j = lax.broadcasted_iota(jnp.int32, (N_REL_PAD, KG), 1)
    r = lax.broadcasted_iota(jnp.int32, (N_REL_PAD, KG), 0)
    off = jnp.where(j < KG - 2 * CHUNK, j, j - KG)
    idx = jnp.clip(N_PREV * CHUNK - off, -REL_CLIP, REL_CLIP) + REL_CLIP
    return (idx == r).astype(f32)


def _bias_expand(table_pad, layer):
    def body(t_ref, o_ref, row_scr):
        row_scr[...] = jnp.dot(t_ref[...], _rel_index_rows(), precision=lax.Precision.HIGHEST, preferred_element_type=f32)
        q = lax.broadcasted_iota(jnp.int32, (QG, KG), 0)
        k = lax.broadcasted_iota(jnp.int32, (QG, KG), 1)
        band = (k // CHUNK >= q // CHUNK) & (k // CHUNK <= q // CHUNK + N_PREV)
        for h in range(N_HEADS):
            t = jnp.broadcast_to(row_scr[h:h + 1, :], (QG, KG))
            for b in range(8):
                t = jnp.where(((q >> b) & 1) == 1, pltpu.roll(t, 1 << b, axis=1), t)
            o_ref[h] = jnp.where(band, t, NEG)

    return _pcall(body, name=f"bias_expand_{layer}", out_shape=jax.ShapeDtypeStruct((N_HEADS, QG, KG), f32),
                  scratch_shapes=[pltpu.VMEM((N_HEADS, KG), f32)],
                  compiler_params=pltpu.CompilerParams(vmem_limit_bytes=VMEM_LIMIT))(table_pad)


def _bias_reduce(dbias, layer):
    def body(d_ref, o_ref, row_scr):
        q = lax.broadcasted_iota(jnp.int32, (QG, KG), 0)
        for h in range(N_HEADS):
            t = d_ref[h]
            for b in range(8):
                t = jnp.where(((q >> b) & 1) == 1, pltpu.roll(t, KG - (1 << b), axis=1), t)
            row_scr[h:h + 1, :] = jnp.sum(t, axis=0, keepdims=True)
        o_ref[...] = lax.dot_general(row_scr[...], _rel_index_rows(), NT, precision=lax.Precision.HIGHEST,
                                     preferred_element_type=f32)

    return _pcall(body, name=f"bias_reduce_{layer}", out_shape=jax.ShapeDtypeStruct((N_HEADS, N_REL_PAD), f32),
                  scratch_shapes=[pltpu.VMEM((N_HEADS, KG), f32)],
                  compiler_params=pltpu.CompilerParams(vmem_limit_bytes=VMEM_LIMIT))(dbias)


def _key_specs(n_groups, col0):
    return [pl.BlockSpec((QG, 128), functools.partial(
        lambda hp, g, jj: (jnp.clip(g - 2 + jj, 0, n_groups - 1), col0 + hp), jj=jj)) for jj in range(3)]


def _attn_fwd(h, bias, layer):
    T = h.shape[0]
    n_groups = T // QG
    scale = 1.0 / math.sqrt(HEAD_DIM)

    def body(q_ref, k0, k1, k2, v0, v1, v2, b_ref, o_ref):
        g = pl.program_id(1)
        q2 = q_ref[...]
        kc = jnp.concatenate([k0[...], k1[...], k2[...]], axis=0)
        vc = jnp.concatenate([v0[...], v1[...], v2[...]], axis=0)
        lane = lax.broadcasted_iota(jnp.int32, (1, 128), 1)
        col = lax.broadcasted_iota(jnp.int32, (1, KG), 1)
        kvalid = col >= (2 - g) * QG
        outs = []
        for hh in range(2):
            qm = jnp.where(lane // HEAD_DIM == hh, q2, jnp.zeros_like(q2))
            s = lax.dot_general(qm, kc, NT, preferred_element_type=f32) * scale + b_ref[hh]
            s = jnp.where(kvalid, s, NEG)
            p = jnp.exp(s - jnp.max(s, axis=1, keepdims=True))
            l = jnp.sum(p, axis=1, keepdims=True)
            outs.append(jnp.dot(p.astype(bf16), vc, preferred_element_type=f32) / l)
        o_ref[...] = jnp.where(lane // HEAD_DIM == 0, outs[0], outs[1]).astype(bf16)

    return _pcall(
        body, name=f"attn_fwd_{layer}", grid=(N_HEADS // 2, n_groups), out_shape=jax.ShapeDtypeStruct((T, E_MIX), bf16),
        in_specs=[pl.BlockSpec((QG, 128), lambda hp, g: (g, hp))] + _key_specs(n_groups, 8) + _key_specs(n_groups, 16)
        + [pl.BlockSpec((2, QG, KG), lambda hp, g: (hp, 0, 0))],
        out_specs=pl.BlockSpec((QG, 128), lambda hp, g: (g, hp)),
        compiler_params=_params("arbitrary", "arbitrary"))(h, h, h, h, h, h, h, bias)


def _halo_rows(ref, r):
    return ref[r:r + 1, :].astype(f32)


def _conv_taps(cu, p6, p7, w_ref):
    row = lax.broadcasted_iota(jnp.int32, cu.shape, 0)
    r1 = jnp.where(row == 0, p7, pltpu.roll(cu, 1, axis=0))
    r2 = jnp.where(row == 0, p6, jnp.where(row == 1, p7, pltpu.roll(cu, 2, axis=0)))
    return w_ref[2:3, :] * cu + w_ref[1:2, :] * r1 + w_ref[0:1, :] * r2, r1, r2


def _conv_fwd(h, w, layer):
    T = h.shape[0]
    tm = 512

    def body(bg_ref, cg_ref, u_ref, cgp_ref, up_ref, w_ref, o_ref):
        first = (pl.program_id(0) == 0).astype(f32)
        cu = cg_ref[...].astype(f32) * u_ref[...].astype(f32)
        p6 = _halo_rows(cgp_ref, 14) * _halo_rows(up_ref, 14) * (1.0 - first)
        p7 = _halo_rows(cgp_ref, 15) * _halo_rows(up_ref, 15) * (1.0 - first)
        conv, _, _ = _conv_taps(cu, p6, p7, w_ref)
        o_ref[...] = (bg_ref[...].astype(f32) * conv).astype(bf16)

    prev = lambda slab: pl.BlockSpec((16, E_MIX), lambda i: (jnp.maximum(i * (tm // 16) - 1, 0), slab))
    return _pcall(
        body, name=f"conv_fwd_{layer}", grid=(T // tm,), out_shape=jax.ShapeDtypeStruct((T, E_MIX), bf16),
        in_specs=[pl.BlockSpec((tm, E_MIX), lambda i: (i, 0)), pl.BlockSpec((tm, E_MIX), lambda i: (i, 1)),
                  pl.BlockSpec((tm, E_MIX), lambda i: (i, 2)), prev(1), prev(2),
                  pl.BlockSpec((CONV_W, E_MIX), lambda i: (0, 0))],
        out_specs=pl.BlockSpec((tm, E_MIX), lambda i: (i, 0)),
        compiler_params=_params("arbitrary"))(h, h, h, h, h, w)


def _kv_mem(memb, wkv):
    def body(m_ref, w_ref, o_ref):
        o_ref[...] = jnp.dot(m_ref[...], w_ref[...], preferred_element_type=f32).astype(bf16)

    return _pcall(body, name="kv_mem", out_shape=jax.ShapeDtypeStruct((N_MEM, 2 * E_MEM), bf16),
                  compiler_params=pltpu.CompilerParams(vmem_limit_bytes=VMEM_LIMIT))(memb, wkv)


def _mem_probs(qm_ref, kv_ref, hh):
    qh = qm_ref[:, hh * MEM_HEAD_DIM:(hh + 1) * MEM_HEAD_DIM]
    kh = kv_ref[:, hh * MEM_HEAD_DIM:(hh + 1) * MEM_HEAD_DIM]
    vh = kv_ref[:, E_MEM + hh * MEM_HEAD_DIM:E_MEM + (hh + 1) * MEM_HEAD_DIM]
    s = lax.dot_general(qh, kh, NT, preferred_element_type=f32) * (1.0 / math.sqrt(MEM_HEAD_DIM))
    e = jnp.exp(s - jnp.max(s, axis=1, keepdims=True))
    return e / jnp.sum(e, axis=1, keepdims=True), qh, kh, vh


def _h_tail_specs(tm):
    return [pl.BlockSpec((tm, E_MEM), functools.partial(lambda i, cb: (i, cb), cb=cb)) for cb in (6, 7, 8, 9)]


def _post_fwd(h, mix, kv, wout, x, g, b, layer):
    T = x.shape[0]
    tm = 512

    def body(qm_ref, z0, z1, z2, mix_ref, kv_ref, w_ref, x_ref, g_ref, b_ref, xn_ref, xb_ref, xh_ref, rs_ref):
        mem = jnp.concatenate(
            [jnp.dot(_mem_probs(qm_ref, kv_ref, hh)[0].astype(bf16), kv_ref[:, E_MEM + hh * MEM_HEAD_DIM:E_MEM + (hh + 1) * MEM_HEAD_DIM],
                     preferred_element_type=f32) for hh in range(MEM_HEADS)], axis=1)
        z = jnp.concatenate([z0[...], z1[...], z2[...]], axis=1).astype(f32)
        act, _ = _silu_parts(z)
        y = jnp.concatenate([mix_ref[...].astype(f32), mem], axis=1) * act
        out = jnp.dot(y.astype(bf16), w_ref[...], preferred_element_type=f32)
        r = DN_ALPHA * x_ref[...] + out
        mu = jnp.mean(r, axis=1, keepdims=True)
        var = jnp.mean(jnp.square(r - mu), axis=1, keepdims=True)
        rstd = lax.rsqrt(var + LN_EPS)
        xhat = (r - mu) * rstd
        xn = xhat * g_ref[...] + b_ref[...]
        xn_ref[...] = xn
        xb_ref[...] = xn.astype(bf16)
        xh_ref[...] = xhat
        rs_ref[...] = rstd

    tile = lambda w: pl.BlockSpec((tm, w), lambda i: (i, 0))
    const = lambda r, c: pl.BlockSpec((r, c), lambda i: (0, 0))
    return _pcall(
        body, name=f"post_fwd_{layer}", grid=(T // tm,),
        out_shape=(jax.ShapeDtypeStruct((T, D_MODEL), f32), jax.ShapeDtypeStruct((T, D_MODEL), bf16),
                   jax.ShapeDtypeStruct((T, D_MODEL), f32), jax.ShapeDtypeStruct((T, 1), f32)),
        in_specs=_h_tail_specs(tm) + [tile(E_MIX), const(N_MEM, 2 * E_MEM), const(E_BRANCH, D_MODEL), tile(D_MODEL),
                                      const(1, D_MODEL), const(1, D_MODEL)],
        out_specs=(tile(D_MODEL), tile(D_MODEL), tile(D_MODEL), tile(1)),
        compiler_params=_params("arbitrary"))(h, h, h, h, mix, kv, wout, x, g, b)


def _loss_head(y, target):
    T = y.shape[0]
    tm = 512

    def body(y_ref, t_ref, l_ref, d_ref):
        @pl.when(pl.program_id(0) == 0)
        def _():
            l_ref[...] = jnp.zeros_like(l_ref)
        err = y_ref[...] - t_ref[...]
        d_ref[...] = err * (1.0 / D_MODEL)
        l_ref[...] += jnp.sum(jnp.square(err))

    return _pcall(
        body, name="loss_head", grid=(T // tm,),
        out_shape=(jax.ShapeDtypeStruct((8, 128), f32), jax.ShapeDtypeStruct((T, D_MODEL), f32)),
        in_specs=[pl.BlockSpec((tm, D_MODEL), lambda i: (i, 0))] * 2,
        out_specs=(pl.BlockSpec((8, 128), lambda i: (0, 0)), pl.BlockSpec((tm, D_MODEL), lambda i: (i, 0))),
        compiler_params=_params("arbitrary"))(y, target)


def _post_bwd(dxn, xhat, rstd, g, h, mix, kv, wout, layer):
    T = dxn.shape[0]
    tm = 256
    inv = 1.0 / math.sqrt(MEM_HEAD_DIM)

    def body(dxn_ref, xh_ref, rs_ref, g_ref, qm_ref, z0, z1, z2, mix_ref, kv_ref, w_ref,
             dxp_ref, dhb_ref, dmix_ref, dkv_ref, dwo_ref, dg_ref, db_ref):
        @pl.when(pl.program_id(0) == 0)
        def _():
            dkv_ref[...] = jnp.zeros_like(dkv_ref)
            dwo_ref[...] = jnp.zeros_like(dwo_ref)
            dg_ref[...] = jnp.zeros_like(dg_ref)
            db_ref[...] = jnp.zeros_like(db_ref)

        dy, xh = dxn_ref[...], xh_ref[...]
        dg_ref[0:1, :] += jnp.sum(dy * xh, axis=0, keepdims=True)
        db_ref[0:1, :] += jnp.sum(dy, axis=0, keepdims=True)
        gx = dy * g_ref[...]
        dr = rs_ref[...] * (gx - jnp.mean(gx, axis=1, keepdims=True) - xh * jnp.mean(gx * xh, axis=1, keepdims=True))
        dxp_ref[...] = DN_ALPHA * dr
        dob = dr.astype(bf16)

        probs = [_mem_probs(qm_ref, kv_ref, hh) for hh in range(MEM_HEADS)]
        mem = jnp.concatenate([jnp.dot(p.astype(bf16), vh, preferred_element_type=f32) for p, _, _, vh in probs], axis=1)
        z = jnp.concatenate([z0[...], z1[...], z2[...]], axis=1).astype(f32)
        act, sig = _silu_parts(z)
        cat = jnp.concatenate([mix_ref[...].astype(f32), mem], axis=1)
        yb = (cat * act).astype(bf16)
        dwo_ref[...] += lax.dot_general(yb, dob, TN, preferred_element_type=f32)
        dyv = lax.dot_general(dob, w_ref[...], NT, preferred_element_type=f32)
        dz = dyv * cat * (sig * (1.0 + z * (1.0 - sig)))
        dcat = dyv * act
        dmix_ref[...] = dcat[:, :E_MIX].astype(bf16)
        dqs = []
        for hh, (p, qh, kh, vh) in enumerate(probs):
            dmem = dcat[:, E_MIX + hh * MEM_HEAD_DIM:E_MIX + (hh + 1) * MEM_HEAD_DIM].astype(bf16)
            dp = lax.dot_general(dmem, vh, NT, preferred_element_type=f32)
            ds = (p * (dp - jnp.sum(p * dp, axis=1, keepdims=True))).astype(bf16)
            dqs.append(jnp.dot(ds, kh, preferred_element_type=f32) * inv)
            dkv_ref[:, hh * MEM_HEAD_DIM:(hh + 1) * MEM_HEAD_DIM] += lax.dot_general(ds, qh, TN, preferred_element_type=f32) * inv
            dkv_ref[:, E_MEM + hh * MEM_HEAD_DIM:E_MEM + (hh + 1) * MEM_HEAD_DIM] += lax.dot_general(
                p.astype(bf16), dmem, TN, preferred_element_type=f32)
        dhb_ref[...] = jnp.concatenate(dqs + [dz], axis=1).astype(bf16)

    tile = lambda w: pl.BlockSpec((tm, w), lambda i: (i, 0))
    const = lambda r, c: pl.BlockSpec((r, c), lambda i: (0, 0))
    return _pcall(
        body, name=f"post_bwd_{layer}", grid=(T // tm,),
        out_shape=(jax.ShapeDtypeStruct((T, D_MODEL), f32), jax.ShapeDtypeStruct((T, E_MEM + E_BRANCH), bf16),
                   jax.ShapeDtypeStruct((T, E_MIX), bf16), jax.ShapeDtypeStruct((N_MEM, 2 * E_MEM), f32),
                   jax.ShapeDtypeStruct((E_BRANCH, D_MODEL), f32), jax.ShapeDtypeStruct((8, D_MODEL), f32),
                   jax.ShapeDtypeStruct((8, D_MODEL), f32)),
        in_specs=[tile(D_MODEL), tile(D_MODEL), tile(1), const(1, D_MODEL)] + _h_tail_specs(tm)
        + [tile(E_MIX), const(N_MEM, 2 * E_MEM), const(E_BRANCH, D_MODEL)],
        out_specs=(tile(D_MODEL), tile(E_MEM + E_BRANCH), tile(E_MIX), const(N_MEM, 2 * E_MEM), const(E_BRANCH, D_MODEL),
                   const(8, D_MODEL), const(8, D_MODEL)),
        compiler_params=_params("arbitrary"))(dxn, xhat, rstd, g, h, h, h, h, mix, kv, wout)


def _attn_bwd(h, bias, dmix, layer):
    T = h.shape[0]
    n_groups = T // QG
    scale = 1.0 / math.sqrt(HEAD_DIM)

    def body(q_ref, k0, k1, k2, v0, v1, v2, do_ref, b_ref, dq_ref, dk_ref, dv_ref, db_ref, acck, accv):
        g = pl.program_id(1)

        @pl.when(g == 0)
        def _():
            acck[...] = jnp.zeros_like(acck)
            accv[...] = jnp.zeros_like(accv)
            db_ref[...] = jnp.zeros_like(db_ref)

        @pl.when(g < n_groups)
        def _():
            q2, do2 = q_ref[...], do_ref[...]
            kc = jnp.concatenate([k0[...], k1[...], k2[...]], axis=0)
            vc = jnp.concatenate([v0[...], v1[...], v2[...]], axis=0)
            lane = lax.broadcasted_iota(jnp.int32, (1, 128), 1)
            col = lax.broadcasted_iota(jnp.int32, (1, KG), 1)
            kvalid = col >= (2 - g) * QG
            dqs, dks, dvs = [], [], []
            for hh in range(2):
                hm = lane // HEAD_DIM == hh
                qm = jnp.where(hm, q2, jnp.zeros_like(q2))
                dom = jnp.where(hm, do2, jnp.zeros_like(do2))
                s = lax.dot_general(qm, kc, NT, preferred_element_type=f32) * scale + b_ref[hh]
                s = jnp.where(kvalid, s, NEG)
                e = jnp.exp(s - jnp.max(s, axis=1, keepdims=True))
                p = e / jnp.sum(e, axis=1, keepdims=True)
                dp = lax.dot_general(dom, vc, NT, preferred_element_type=f32)
                ds = p * (dp - jnp.sum(p * dp, axis=1, keepdims=True))
                db_ref[hh] += ds
                dsb, pb = ds.astype(bf16), p.astype(bf16)
                dqs.append(jnp.dot(dsb, kc, preferred_element_type=f32) * scale)
                dks.append(lax.dot_general(dsb, q2, TN, preferred_element_type=f32) * scale)
                dvs.append(lax.dot_general(pb, do2, TN, preferred_element_type=f32))
            first = lane // HEAD_DIM == 0
            dq_ref[...] = jnp.where(first, dqs[0], dqs[1]).astype(bf16)
            dkc = jnp.where(first, dks[0], dks[1])
            dvc = jnp.where(first, dvs[0], dvs[1])
            for jj in range(3):
                slot = (g + 1 + jj) % 3
                if jj == 2:
                    acck[slot] = dkc[jj * QG:(jj + 1) * QG]
                    accv[slot] = dvc[jj * QG:(jj + 1) * QG]
                else:
                    acck[slot] += dkc[jj * QG:(jj + 1) * QG]
                    accv[slot] += dvc[jj * QG:(jj + 1) * QG]

        done = (g + 1) % 3
        dk_ref[...] = acck[done].astype(bf16)
        dv_ref[...] = accv[done].astype(bf16)

    last = n_groups - 1
    qspec = pl.BlockSpec((QG, 128), lambda hp, g: (jnp.minimum(g, last), hp))
    kout = pl.BlockSpec((QG, 128), lambda hp, g: (jnp.clip(g - 2, 0, last), hp))
    bspec = pl.BlockSpec((2, QG, KG), lambda hp, g: (hp, 0, 0))
    return _pcall(
        body, name=f"attn_bwd_{layer}", grid=(N_HEADS // 2, n_groups + 2),
        out_shape=(jax.ShapeDtypeStruct((T, E_MIX), bf16),) * 3 + (jax.ShapeDtypeStruct((N_HEADS, QG, KG), f32),),
        in_specs=[qspec] + _key_specs(n_groups, 8) + _key_specs(n_groups, 16) + [qspec, bspec],
        out_specs=(qspec, kout, kout, bspec),
        scratch_shapes=[pltpu.VMEM((3, QG, 128), f32), pltpu.VMEM((3, QG, 128), f32)],
        compiler_params=_params("arbitrary", "arbitrary"))(h, h, h, h, h, h, h, dmix, bias)


def _conv_bwd(h, w, dmix, layer):
    T = h.shape[0]
    tm = 512
    n_tiles = T // tm

    def body(bg_ref, cg_ref, u_ref, cgp_ref, up_ref, dy_ref, bgn_ref, dyn_ref, w_ref, dbg_ref, dcg_ref, du_ref, dw_ref):
        i = pl.program_id(0)

        @pl.when(i == 0)
        def _():
            dw_ref[...] = jnp.zeros_like(dw_ref)

        first = (i == 0).astype(f32)
        final = (i == n_tiles - 1).astype(f32)
        bg, cg, u = bg_ref[...].astype(f32), cg_ref[...].astype(f32), u_ref[...].astype(f32)
        dy = dy_ref[...].astype(f32)
        cu = cg * u
        p6 = _halo_rows(cgp_ref, 14) * _halo_rows(up_ref, 14) * (1.0 - first)
        p7 = _halo_rows(cgp_ref, 15) * _halo_rows(up_ref, 15) * (1.0 - first)
        conv, r1, r2 = _conv_taps(cu, p6, p7, w_ref)
        dbg_ref[...] = (dy * conv).astype(bf16)
        dc = dy * bg
        n0 = _halo_rows(dyn_ref, 0) * _halo_rows(bgn_ref, 0) * (1.0 - final)
        n1 = _halo_rows(dyn_ref, 1) * _halo_rows(bgn_ref, 1) * (1.0 - final)
        row = lax.broadcasted_iota(jnp.int32, dc.shape, 0)
        f1 = jnp.where(row == tm - 1, n0, pltpu.roll(dc, tm - 1, axis=0))
        f2 = jnp.where(row == tm - 2, n0, jnp.where(row == tm - 1, n1, pltpu.roll(dc, tm - 2, axis=0)))
        dcu = w_ref[2:3, :] * dc + w_ref[1:2, :] * f1 + w_ref[0:1, :] * f2
        dcg_ref[...] = (dcu * u).astype(bf16)
        du_ref[...] = (dcu * cg).astype(bf16)
        dw_ref[0:1, :] += jnp.sum(dc * r2, axis=0, keepdims=True)
        dw_ref[1:2, :] += jnp.sum(dc * r1, axis=0, keepdims=True)
        dw_ref[2:3, :] += jnp.sum(dc * cu, axis=0, keepdims=True)

    tile = lambda slab: pl.BlockSpec((tm, E_MIX), lambda i: (i, slab))
    prev = lambda slab: pl.BlockSpec((16, E_MIX), lambda i: (jnp.maximum(i * (tm // 16) - 1, 0), slab))
    nxt = lambda slab: pl.BlockSpec((16, E_MIX), lambda i: (jnp.minimum((i + 1) * (tm // 16), T // 16 - 1), slab))
    return _pcall(
        body, name=f"conv_bwd_{layer}", grid=(n_tiles,),
        out_shape=(jax.ShapeDtypeStruct((T, E_MIX), bf16),) * 3 + (jax.ShapeDtypeStruct((8, E_MIX), f32),),
        in_specs=[tile(0), tile(1), tile(2), prev(1), prev(2), tile(0), nxt(0), nxt(0),
                  pl.BlockSpec((CONV_W, E_MIX), lambda i: (0, 0))],
        out_specs=(tile(0), tile(0), tile(0), pl.BlockSpec((8, E_MIX), lambda i: (0, 0))),
        compiler_params=_params("arbitrary"))(h, h, h, h, h, dmix, h, dmix, w)


def _inproj_bwd_dx(da, db, dc, dhb, dxp, win_g, layer):
    T = dxp.shape[0]
    tm = 512

    def body(da_ref, db_ref, dc_ref, dhb_ref, dxp_ref, w_ref, o_ref):
        dh = jnp.concatenate([da_ref[...], db_ref[...], dc_ref[...], dhb_ref[...]], axis=1)
        acc = dxp_ref[...]
        for j in range(N_CHIPS):
            acc = acc + lax.dot_general(dh[:, j * W_IN_COLS:(j + 1) * W_IN_COLS], w_ref[j], NT, preferred_element_type=f32)
        o_ref[...] = acc

    tile = lambda w: pl.BlockSpec((tm, w), lambda i: (i, 0))
    return _pcall(
        body, name=f"inproj_bwd_dx_{layer}", grid=(T // tm,), out_shape=jax.ShapeDtypeStruct((T, D_MODEL), f32),
        in_specs=[tile(E_MIX), tile(E_MIX), tile(E_MIX), tile(E_MEM + E_BRANCH), tile(D_MODEL),
                  pl.BlockSpec((None, N_CHIPS, D_MODEL, W_IN_COLS), lambda i: (layer, 0, 0, 0), pipeline_mode=pl.Buffered(1))],
        out_specs=tile(D_MODEL),
        compiler_params=_params("arbitrary"))(da, db, dc, dhb, dxp, win_g)


def _inproj_bwd_dw(da, db, dc, dhb, xb, layer):
    T = xb.shape[0]
    tm = 512
    n_tiles = T // tm

    def body(da_ref, db_ref, dc_ref, dhb_ref, x_ref, o_ref, acc, stage, sem):
        i = pl.program_id(0)

        @pl.when(i == 0)
        def _():
            acc[...] = jnp.zeros_like(acc)

        dh = jnp.concatenate([da_ref[...], db_ref[...], dc_ref[...], dhb_ref[...]], axis=1)
        xt = x_ref[...]
        for j in range(N_CHIPS):
            acc[j] += lax.dot_general(xt, dh[:, j * W_IN_COLS:(j + 1) * W_IN_COLS], TN, preferred_element_type=f32)

        @pl.when(i == n_tiles - 1)
        def _():
            for j in range(N_CHIPS):
                stage[...] = acc[j].astype(bf16)
                cp = pltpu.make_async_copy(stage, o_ref.at[j], sem)
                cp.start()
                cp.wait()

    tile = lambda w: pl.BlockSpec((tm, w), lambda i: (i, 0))
    return _pcall(
        body, name=f"inproj_bwd_dw_{layer}", grid=(n_tiles,),
        out_shape=jax.ShapeDtypeStruct((N_CHIPS, D_MODEL, W_IN_COLS), bf16),
        in_specs=[tile(E_MIX), tile(E_MIX), tile(E_MIX), tile(E_MEM + E_BRANCH), tile(D_MODEL)],
        out_specs=ANY,
        scratch_shapes=[pltpu.VMEM((N_CHIPS, D_MODEL, W_IN_COLS), f32), pltpu.VMEM((D_MODEL, W_IN_COLS), bf16),
                        pltpu.SemaphoreType.DMA],
        compiler_params=_params("arbitrary"))(da, db, dc, dhb, xb)


def _kv_mem_bwd(memb, dkv):
    def body(m_ref, d_ref, o_ref):
        o_ref[...] = lax.dot_general(m_ref[...], d_ref[...].astype(bf16), TN, preferred_element_type=f32).astype(bf16)

    return _pcall(body, name="kv_mem_bwd", out_shape=jax.ShapeDtypeStruct((D_MODEL, 2 * E_MEM), bf16),
                  compiler_params=pltpu.CompilerParams(vmem_limit_bytes=VMEM_LIMIT))(memb, dkv)


def _to_bf16(a, name):
    rows, cols = a.shape
    br = 512

    def body(a_ref, o_ref):
        o_ref[...] = a_ref[...].astype(bf16)

    return _pcall(body, name=name, grid=(rows // br,), out_shape=jax.ShapeDtypeStruct((rows, cols), bf16),
                  in_specs=[pl.BlockSpec((br, cols), lambda i: (i, 0))], out_specs=pl.BlockSpec((br, cols), lambda i: (i, 0)),
                  compiler_params=_params("arbitrary"))(a)


def _adamw(w, g, m, v, name):
    shape = w.shape
    cols = shape[-1]
    rows = w.size // cols
    args = [a.reshape(rows, cols) for a in (w, g, m, v)]
    br = 256 if rows % 256 == 0 and rows > 256 else rows

    def body(w_ref, g_ref, m_ref, v_ref, d_ref, nm_ref, nv_ref):
        gg = g_ref[...]
        nm = ADAM_B1 * m_ref[...] + (1.0 - ADAM_B1) * gg
        nv = ADAM_B2 * v_ref[...] + (1.0 - ADAM_B2) * jnp.square(gg)
        m_hat = nm / (1.0 - ADAM_B1 ** ADAM_STEP)
        v_hat = nv / (1.0 - ADAM_B2 ** ADAM_STEP)
        d_ref[...] = -ADAM_LR * (m_hat / (jnp.sqrt(v_hat) + ADAM_EPS) + ADAM_WD * w_ref[...])
        nm_ref[...] = nm
        nv_ref[...] = nv

    spec = pl.BlockSpec((br, cols), lambda i: (i, 0))
    outs = _pcall(body, name=name, grid=(rows // br,), out_shape=(jax.ShapeDtypeStruct((rows, cols), f32),) * 3,
                  in_specs=[spec] * 4, out_specs=(spec,) * 3, compiler_params=_params("arbitrary"))(*args)
    return tuple(o.reshape(shape) for o in outs)


def kernel(x, mem, w_in, w_mem_kv, w_out, rel_bias, conv_w, ln_g, ln_b, loss_target, m_w_in, m_w_mem_kv, m_w_out, m_rel_bias, m_conv_w, m_ln_g, m_ln_b, v_w_in, v_w_mem_kv, v_w_out, v_rel_bias, v_conv_w, v_ln_g, v_ln_b):
    T = x.shape[1]
    x0 = x.reshape(T, D_MODEL)
    target = loss_target.reshape(T, D_MODEL)
    memb = mem.reshape(N_MEM, D_MODEL).astype(bf16)
    chip = 2 * lax.axis_index("x") + lax.axis_index("y")
    core = lax.axis_index("c")
    chip_arr = jnp.reshape(chip, (1,)).astype(jnp.int32)
    core_arr = jnp.reshape(core, (1,)).astype(jnp.int32)

    win_g, wkv_g, wout_g, cw_g = _gather_weights(w_in.astype(bf16), w_mem_kv.astype(bf16), w_out.astype(bf16), conv_w)
    wkv_full = wkv_g.reshape(DEPTH, D_MODEL, 2 * E_MEM)
    wout_full = wout_g.reshape(DEPTH, E_BRANCH, D_MODEL)
    conv_full = jnp.transpose(cw_g, (1, 2, 0, 3)).reshape(DEPTH // 2, CONV_W, E_MIX)
    tables = jnp.pad(rel_bias, ((0, 0), (0, 0), (0, N_REL_PAD - N_REL)))

    xs, xbs, hs, mixes, kvs, xhats, rstds, biases = [x0], [_to_bf16(x0, "cast_x")], [], [], [], [], [], {}
    for layer in range(DEPTH):
        h = _inproj(xbs[layer], win_g, layer)
        if layer % 2 == 0:
            biases[layer] = _bias_expand(tables[layer // 2], layer)
            mix = _attn_fwd(h, biases[layer], layer)
        else:
            mix = _conv_fwd(h, conv_full[layer // 2], layer)
        kv = _kv_mem(memb, wkv_full[layer])
        xn, xnb, xhat, rstd = _post_fwd(h, mix, kv, wout_full[layer], xs[layer], ln_g[layer][None, :], ln_b[layer][None, :], layer)
        xs.append(xn); xbs.append(xnb); hs.append(h); mixes.append(mix); kvs.append(kv); xhats.append(xhat); rstds.append(rstd)

    lsum, dx = _loss_head(xs[DEPTH], target)
    loss = lax.psum(lsum[0, 0], ("x", "y", "c")) * (0.5 / D_MODEL)

    g_win, g_wkv, g_wout = [None] * DEPTH, [None] * DEPTH, [None] * DEPTH
    dgs, dbs, dconvs, dtables = [None] * DEPTH, [None] * DEPTH, [None] * (DEPTH // 2), [None] * ((DEPTH + 1) // 2)
    for layer in reversed(range(DEPTH)):
        h = hs[layer]
        dxp, dhb, dmix, dkv, dwo, dgs[layer], dbs[layer] = _post_bwd(
            dx, xhats[layer], rstds[layer], ln_g[layer][None, :], h, mixes[layer], kvs[layer], wout_full[layer], layer)
        if layer % 2 == 0:
            da, db, dc, dbias = _attn_bwd(h, biases[layer], dmix, layer)
            dtables[layer // 2] = _bias_reduce(dbias, layer)
        else:
            da, db, dc, dconvs[layer // 2] = _conv_bwd(h, conv_full[layer // 2], dmix, layer)
        dx = _inproj_bwd_dx(da, db, dc, dhb, dxp, win_g, layer)
        g_win[layer] = _inproj_bwd_dw(da, db, dc, dhb, xbs[layer], layer)
        g_wkv[layer] = _kv_mem_bwd(memb, dkv).reshape(N_CHIPS, W_KV_ROWS, 2 * E_MEM)
        g_wout[layer] = _to_bf16(dwo, f"cast_dwout_{layer}").reshape(N_CHIPS, W_OUT_ROWS, D_MODEL)
    grad_x = dx.reshape(1, T, D_MODEL)

    gs = [jnp.stack(g_win), jnp.stack(g_wkv), jnp.stack(g_wout)]
    rs = _sibling_exchange(gs)
    ss = [_pair_sum(g, r, core_arr) for g, r in zip(gs, rs)]
    rb = _chip_scatter(ss)
    fs = [_chip_sum(s, r, chip_arr) for s, r in zip(ss, rb)]
    grad_w_in, grad_w_mem_kv, grad_w_out = _sibling_share(fs)

    pad8 = lambda a: jnp.pad(a, ((0, 8 - a.shape[0]), (0, 0)))
    parts = dgs + dbs + dconvs + [pad8(t.reshape(-1, D_MODEL)) for t in dtables]
    small = _small_allreduce(jnp.concatenate(parts, axis=0))
    grad_ln_g = jnp.stack([small[8 * l] for l in range(DEPTH)])
    grad_ln_b = jnp.stack([small[8 * (DEPTH + l)] for l in range(DEPTH)])
    conv_all = jnp.stack([small[8 * (2 * DEPTH + a):8 * (2 * DEPTH + a) + CONV_W] for a in range(DEPTH // 2)])
    grad_conv_w = lax.dynamic_slice_in_dim(conv_all, chip * (E_MIX // N_CHIPS), E_MIX // N_CHIPS, axis=2)
    t0 = 8 * (2 * DEPTH + DEPTH // 2)
    grad_rel_bias = jnp.stack([small[t0 + 8 * a:t0 + 8 * a + 6].reshape(N_HEADS, N_REL_PAD)[:, :N_REL]
                               for a in range((DEPTH + 1) // 2)])

    grads = [grad_w_in, grad_w_mem_kv, grad_w_out, grad_rel_bias, grad_conv_w, grad_ln_g, grad_ln_b]
    weights = [w_in, w_mem_kv, w_out, rel_bias, conv_w, ln_g, ln_b]
    moms = [m_w_in, m_w_mem_kv, m_w_out, m_rel_bias, m_conv_w, m_ln_g, m_ln_b]
    vels = [v_w_in, v_w_mem_kv, v_w_out, v_rel_bias, v_conv_w, v_ln_g, v_ln_b]
    names = ["w_in", "w_mem_kv", "w_out", "rel_bias", "conv_w", "ln_g", "ln_b"]
    upd = [_adamw(w, g, m, v, f"adamw_{n}") for w, g, m, v, n in zip(weights, grads, moms, vels, names)]
    deltas, new_m, new_v = zip(*upd)
    return (loss, grad_x, *grads, *deltas, *new_m, *new_v)
```

```python
import functools
import math

import jax
import jax.numpy as jnp
from jax import lax
from jax.experimental import pallas as pl
from jax.experimental.pallas import tpu as pltpu

f32, bf16 = jnp.float32, jnp.bfloat16

D_MODEL = 1024
DEPTH = 4
CHUNK = 64
N_PREV = 8
N_HEADS = 16
HEAD_DIM = 64
E_MIX = 1024
REL_CLIP = 128
N_REL = 2 * REL_CLIP + 1
N_REL_PAD = 384
CONV_W = 3
N_MEM = 256
MEM_HEADS = 4
MEM_HEAD_DIM = 128
E_MEM = 512
E_BRANCH = E_MIX + E_MEM
N_IN = 3 * E_MIX + E_MEM + E_BRANCH
N_CHIPS = 4
W_IN_COLS = N_IN // N_CHIPS
W_KV_ROWS = D_MODEL // N_CHIPS
W_OUT_ROWS = E_BRANCH // N_CHIPS
DN_ALPHA = (2.0 * DEPTH) ** 0.25
LN_EPS = 1e-5
ADAM_LR, ADAM_B1, ADAM_B2, ADAM_EPS, ADAM_WD, ADAM_STEP = 0.001, 0.9, 0.999, 1e-08, 0.01, 10

QG = 4 * CHUNK
KG = QG + N_PREV * CHUNK
DB_COLS = KG // 2
NEG = -1e30
VMEM_LIMIT = 56 * 1024 * 1024

NT = (((1,), (1,)), ((), ()))
TN = (((0,), (0,)), ((), ()))
MESH = pl.DeviceIdType.MESH
ANY = pl.BlockSpec(memory_space=pl.ANY)


def _pcall(body, **kw):
    return pl.pallas_call(body, **kw)


def _params(*sem):
    return pltpu.CompilerParams(dimension_semantics=sem, vmem_limit_bytes=VMEM_LIMIT)


def _silu_parts(z):
    sig = 1.0 / (1.0 + jnp.exp(-z))
    return z * sig, sig


def _gather_weights(win_s, wkv_s, wout_s, cw_s):
    def body(win_ref, wkv_ref, wout_ref, cw_ref, win_g, wkv_g, wout_g, cw_g, send_sems, recv_sems):
        x, y, c = lax.axis_index("x"), lax.axis_index("y"), lax.axis_index("c")
        me = 2 * x + y
        sibling = (x, y, 1 - c)
        chips = [(1 - x, y), (x, 1 - y), (1 - x, 1 - y)]
        mine, theirs = pl.ds(2 * c, 2), pl.ds(2 * (1 - c), 2)
        big = [(win_ref, win_g), (wkv_ref, wkv_g), (wout_ref, wout_g)]

        def rcopy(k, src, dst, to):
            return pltpu.make_async_remote_copy(src_ref=src, dst_ref=dst, send_sem=send_sems.at[k],
                                                recv_sem=recv_sems.at[k], device_id=to, device_id_type=MESH)

        sends = [rcopy(21 + a, s, g.at[:, me], sibling) for a, (s, g) in enumerate(big)]
        sends.append(rcopy(24, cw_ref, cw_g.at[me], sibling))
        for a, (s, g) in enumerate(big):
            for p, (px, py) in enumerate(chips):
                sends.append(rcopy(3 * a + p, s.at[mine], g.at[mine, me], (px, py, c)))
        for p, (px, py) in enumerate(chips):
            sends.append(rcopy(18 + p, cw_ref, cw_g.at[me], (px, py, c)))
        for cp in sends:
            cp.start()
        for a, (s, g) in enumerate(big):
            for p, (px, py) in enumerate(chips):
                jp = 2 * px + py
                rcopy(3 * a + p, s.at[mine], g.at[mine, jp], (px, py, c)).wait_recv()
                fwd = rcopy(9 + 3 * a + p, g.at[mine, jp], g.at[mine, jp], sibling)
                fwd.start()
                sends.append(fwd)
        for p, (px, py) in enumerate(chips):
            rcopy(18 + p, cw_ref, cw_g.at[2 * px + py], (px, py, c)).wait_recv()
        for a, (s, g) in enumerate(big):
            for p, (px, py) in enumerate(chips):
                jp = 2 * px + py
                rcopy(9 + 3 * a + p, g.at[theirs, jp], g.at[theirs, jp], sibling).wait_recv()
        for a, (s, g) in enumerate(big):
            rcopy(21 + a, s, g.at[:, me], sibling).wait_recv()
        rcopy(24, cw_ref, cw_g.at[me], sibling).wait_recv()
        for cp in sends:
            cp.wait_send()

    out_shape = (
        jax.ShapeDtypeStruct((DEPTH, N_CHIPS) + win_s.shape[1:], bf16),
        jax.ShapeDtypeStruct((DEPTH, N_CHIPS) + wkv_s.shape[1:], bf16),
        jax.ShapeDtypeStruct((DEPTH, N_CHIPS) + wout_s.shape[1:], bf16),
        jax.ShapeDtypeStruct((N_CHIPS,) + cw_s.shape, f32),
    )
    return _pcall(body, name="gather_weights", out_shape=out_shape, in_specs=[ANY] * 4, out_specs=[ANY] * 4,
                  scratch_shapes=[pltpu.SemaphoreType.DMA((25,)), pltpu.SemaphoreType.DMA((25,))])(win_s, wkv_s, wout_s, cw_s)


def _small_allreduce(buf):
    rows, cols = buf.shape

    def body(b_ref, o_ref, slots, send_sems, recv_sems):
        x, y, c = lax.axis_index("x"), lax.axis_index("y"), lax.axis_index("c")
        me = 4 * x + 2 * y + c
        slots[me] = b_ref[...]
        copies = []
        for r in range(1, 8):
            fx, fy, fc = (r >> 2) & 1, (r >> 1) & 1, r & 1
            px, py, pc = x ^ fx, y ^ fy, c ^ fc
            copies.append(pltpu.make_async_remote_copy(
                src_ref=b_ref, dst_ref=slots.at[me], send_sem=send_sems.at[r - 1], recv_sem=recv_sems.at[r - 1],
                device_id=(px, py, pc), device_id_type=MESH))
        for cp in copies:
            cp.start()
        for r in range(1, 8):
            fx, fy, fc = (r >> 2) & 1, (r >> 1) & 1, r & 1
            peer = 4 * (x ^ fx) + 2 * (y ^ fy) + (c ^ fc)
            pltpu.make_async_remote_copy(
                src_ref=b_ref, dst_ref=slots.at[peer], send_sem=send_sems.at[r - 1], recv_sem=recv_sems.at[r - 1],
                device_id=(x ^ fx, y ^ fy, c ^ fc), device_id_type=MESH).wait_recv()
        for cp in copies:
            cp.wait_send()
        acc = slots[0]
        for d in range(1, 8):
            acc = acc + slots[d]
        o_ref[...] = acc

    return _pcall(body, name="small_allreduce", out_shape=jax.ShapeDtypeStruct((rows, cols), f32),
                  in_specs=[pl.BlockSpec(memory_space=pltpu.VMEM)], out_specs=pl.BlockSpec(memory_space=pltpu.VMEM),
                  scratch_shapes=[pltpu.VMEM((8, rows, cols), f32), pltpu.SemaphoreType.DMA((7,)),
                                  pltpu.SemaphoreType.DMA((7,))])(buf)


def _sibling_exchange(gs):
    def body(*refs):
        n = len(gs)
        g_refs, r_refs, send_sems, recv_sems = refs[:n], refs[n:2 * n], refs[2 * n], refs[2 * n + 1]
        x, y, c = lax.axis_index("x"), lax.axis_index("y"), lax.axis_index("c")
        theirs = pl.ds(2 * (1 - c), 2)
        copies = [pltpu.make_async_remote_copy(src_ref=g.at[theirs], dst_ref=r, send_sem=send_sems.at[a],
                                               recv_sem=recv_sems.at[a], device_id=(x, y, 1 - c), device_id_type=MESH)
                  for a, (g, r) in enumerate(zip(g_refs, r_refs))]
        for cp in copies:
            cp.start()
        for cp in copies:
            cp.wait()

    out_shape = tuple(jax.ShapeDtypeStruct((2,) + g.shape[1:], g.dtype) for g in gs)
    return _pcall(body, name="sibling_exchange", out_shape=out_shape, in_specs=[ANY] * len(gs), out_specs=[ANY] * len(gs),
                  scratch_shapes=[pltpu.SemaphoreType.DMA((len(gs),)), pltpu.SemaphoreType.DMA((len(gs),))])(*gs)


def _chip_scatter(ss):
    def body(*refs):
        n = len(ss)
        s_refs, r_refs, send_sems, recv_sems = refs[:n], refs[n:2 * n], refs[2 * n], refs[2 * n + 1]
        x, y, c = lax.axis_index("x"), lax.axis_index("y"), lax.axis_index("c")
        chips = [(1 - x, y), (x, 1 - y), (1 - x, 1 - y)]
        copies = []
        for a, (s, r) in enumerate(zip(s_refs, r_refs)):
            for p, (px, py) in enumerate(chips):
                copies.append(pltpu.make_async_remote_copy(
                    src_ref=s.at[:, 2 * px + py], dst_ref=r.at[p], send_sem=send_sems.at[3 * a + p],
                    recv_sem=recv_sems.at[3 * a + p], device_id=(px, py, c), device_id_type=MESH))
        for cp in copies:
            cp.start()
        for cp in copies:
            cp.wait()

    out_shape = tuple(jax.ShapeDtypeStruct((3, 2) + s.shape[2:], s.dtype) for s in ss)
    return _pcall(body, name="chip_scatter", out_shape=out_shape, in_specs=[ANY] * len(ss), out_specs=[ANY] * len(ss),
                  scratch_shapes=[pltpu.SemaphoreType.DMA((3 * len(ss),)), pltpu.SemaphoreType.DMA((3 * len(ss),))])(*ss)


def _sibling_share(fs):
    def body(*refs):
        n = len(fs)
        o_refs, send_sems, recv_sems = refs[n:2 * n], refs[2 * n], refs[2 * n + 1]
        x, y, c = lax.axis_index("x"), lax.axis_index("y"), lax.axis_index("c")
        mine = pl.ds(2 * c, 2)
        copies = [pltpu.make_async_remote_copy(src_ref=o.at[mine], dst_ref=o.at[mine], send_sem=send_sems.at[a],
                                               recv_sem=recv_sems.at[a], device_id=(x, y, 1 - c), device_id_type=MESH)
                  for a, o in enumerate(o_refs)]
        for cp in copies:
            cp.start()
        for a, o in enumerate(o_refs):
            theirs = o.at[pl.ds(2 * (1 - c), 2)]
            pltpu.make_async_remote_copy(src_ref=theirs, dst_ref=theirs, send_sem=send_sems.at[a], recv_sem=recv_sems.at[a],
                                         device_id=(x, y, 1 - c), device_id_type=MESH).wait_recv()
        for cp in copies:
            cp.wait_send()

    out_shape = tuple(jax.ShapeDtypeStruct(f.shape, f.dtype) for f in fs)
    return _pcall(body, name="sibling_share", out_shape=out_shape, in_specs=[ANY] * len(fs), out_specs=[ANY] * len(fs),
                  input_output_aliases={a: a for a in range(len(fs))},
                  scratch_shapes=[pltpu.SemaphoreType.DMA((len(fs),)), pltpu.SemaphoreType.DMA((len(fs),))])(*fs)


def _pair_sum(g, r, c_arr):
    _, _, rows, cols = g.shape
    g3, r3 = g.reshape(4, 4 * rows, cols), r.reshape(2, 4 * rows, cols)
    br = 512 if (4 * rows) % 512 == 0 else 256

    def body(c_ref, g_ref, r_ref, o_ref):
        o_ref[...] = (g_ref[...].astype(f32) + r_ref[...].astype(f32)).astype(bf16)

    out = _pcall(
        body, name="pair_sum", out_shape=jax.ShapeDtypeStruct(r3.shape, bf16),
        grid_spec=pltpu.PrefetchScalarGridSpec(
            num_scalar_prefetch=1, grid=(2, 4 * rows // br),
            in_specs=[pl.BlockSpec((1, br, cols), lambda a, i, c_ref: (2 * c_ref[0] + a, i, 0)),
                      pl.BlockSpec((1, br, cols), lambda a, i, c_ref: (a, i, 0))],
            out_specs=pl.BlockSpec((1, br, cols), lambda a, i, c_ref: (a, i, 0))),
        compiler_params=_params("arbitrary", "arbitrary"))(c_arr, g3, r3)
    return out.reshape(2, 4, rows, cols)


def _chip_sum(s, r, place):
    _, _, rows, cols = s.shape
    br = 256 if rows % 256 == 0 else 128

    def body(place_ref, s_ref, r_ref, o_ref):
        acc = s_ref[0, 0].astype(f32)
        for p in range(3):
            acc = acc + r_ref[p, 0].astype(f32)
        o_ref[0] = acc

    return _pcall(
        body, name="chip_sum", out_shape=jax.ShapeDtypeStruct((4, rows, cols), f32),
        grid_spec=pltpu.PrefetchScalarGridSpec(
            num_scalar_prefetch=1, grid=(2, rows // br),
            in_specs=[pl.BlockSpec((1, 1, br, cols), lambda a, i, place_ref: (a, place_ref[0], i, 0)),
                      pl.BlockSpec((3, 1, br, cols), lambda a, i, place_ref: (0, a, i, 0))],
            out_specs=pl.BlockSpec((1, br, cols), lambda a, i, place_ref: (2 * place_ref[1] + a, i, 0))),
        compiler_params=_params("arbitrary", "arbitrary"))(place, s, r)


def _inproj(xb, win_g, layer):
    T = xb.shape[0]
    tm = 512

    def body(x_ref, w_ref, o_ref):
        xt = x_ref[...]
        for j in range(N_CHIPS):
            o_ref[:, j * W_IN_COLS:(j + 1) * W_IN_COLS] = jnp.dot(xt, w_ref[j], preferred_element_type=f32).astype(bf16)

    return _pcall(
        body, name=f"inproj_{layer}", grid=(T // tm,), out_shape=jax.ShapeDtypeStruct((T, N_IN), bf16),
        in_specs=[pl.BlockSpec((tm, D_MODEL), lambda i: (i, 0)),
                  pl.BlockSpec((None, N_CHIPS, D_MODEL, W_IN_COLS), lambda i: (layer, 0, 0, 0), pipeline_mode=pl.Buffered(1))],
        out_specs=pl.BlockSpec((tm, N_IN), lambda i: (i, 0)),
        compiler_params=_params("arbitrary"))(xb, win_g)


def _rel_index_rows():
    j = lax.broadcasted_iota(jnp.int32, (N_REL_PAD, KG), 1)
    r = lax.broadcasted_iota(jnp.int32, (N_REL_PAD, KG), 0)
    off = jnp.where(j < KG - 2 * CHUNK, j, j - KG)
    idx = jnp.clip(N_PREV * CHUNK - off, -REL_CLIP, REL_CLIP) + REL_CLIP
    return (idx == r).astype(f32)


def _bias_expand(table_pad, layer):
    def body(t_ref, o_ref, row_scr):
        h = pl.program_id(0)

        @pl.when(h == 0)
        def _():
            row_scr[...] = jnp.dot(t_ref[...], _rel_index_rows(), precision=lax.Precision.HIGHEST,
                                   preferred_element_type=f32)

        q = lax.broadcasted_iota(jnp.int32, (QG, KG), 0)
        k = lax.broadcasted_iota(jnp.int32, (QG, KG), 1)
        band = (k // CHUNK >= q // CHUNK) & (k // CHUNK <= q // CHUNK + N_PREV)
        t = jnp.broadcast_to(row_scr[pl.ds(h, 1), :], (QG, KG))
        for b in range(8):
            t = jnp.where(((q >> b) & 1) == 1, pltpu.roll(t, 1 << b, axis=1), t)
        for v in range(3):
            o_ref[v] = jnp.where(band & (k >= (2 - v) * QG), t, NEG)

    return _pcall(body, name=f"bias_expand_{layer}", grid=(N_HEADS,),
                  out_shape=jax.ShapeDtypeStruct((3, N_HEADS, QG, KG), f32),
                  in_specs=[pl.BlockSpec((N_HEADS, N_REL_PAD), lambda h: (0, 0))],
                  out_specs=pl.BlockSpec((3, None, QG, KG), lambda h: (0, h, 0, 0)),
                  scratch_shapes=[pltpu.VMEM((N_HEADS, KG), f32)], compiler_params=_params("arbitrary"))(table_pad)


def _bias_reduce(dbias, layer):
    def body(d_ref, o_ref, row_scr):
        q = lax.broadcasted_iota(jnp.int32, (QG, DB_COLS), 0)
        k = lax.broadcasted_iota(jnp.int32, (QG, DB_COLS), 1)
        for h in range(N_HEADS):
            t = jnp.where(k > q, d_ref[h], 0.0)
            for b in range(8):
                t = jnp.where(((q >> b) & 1) == 1, pltpu.roll(t, DB_COLS - (1 << b), axis=1), t)
            row_scr[h:h + 1, :] = jnp.sum(t, axis=0, keepdims=True)
        r = lax.broadcasted_iota(jnp.int32, (N_REL_PAD, DB_COLS), 0)
        off = lax.broadcasted_iota(jnp.int32, (N_REL_PAD, DB_COLS), 1)
        own = (off >= 1) & (off < REL_CLIP + CHUNK)
        sel = jnp.where(own & (r == 2 * REL_CLIP - off), 1.0, 0.0) - jnp.where(own & (r == 2 * REL_CLIP), 1.0, 0.0)
        o_ref[...] = lax.dot_general(row_scr[...], sel, NT, precision=lax.Precision.HIGHEST, preferred_element_type=f32)

    return _pcall(body, name=f"bias_reduce_{layer}", out_shape=jax.ShapeDtypeStruct((N_HEADS, N_REL_PAD), f32),
                  scratch_shapes=[pltpu.VMEM((N_HEADS, DB_COLS), f32)],
                  compiler_params=pltpu.CompilerParams(vmem_limit_bytes=VMEM_LIMIT))(dbias)


def _key_specs(n_groups, col0):
    return [pl.BlockSpec((QG, 128), functools.partial(
        lambda hp, g, jj: (jnp.clip(g - 2 + jj, 0, n_groups - 1), col0 + hp), jj=jj)) for jj in range(3)]


def _bias_spec():
    return pl.BlockSpec((None, 2, QG, KG), lambda hp, g: (jnp.minimum(g, 2), hp, 0, 0))


def _attn_fwd(h, bias, layer):
    T = h.shape[0]
    n_groups = T // QG
    scale = 1.0 / math.sqrt(HEAD_DIM)

    def body(q_ref, k0, k1, k2, v0, v1, v2, b_ref, o_ref):
        q2 = q_ref[...] * scale
        kc = jnp.concatenate([k0[...], k1[...], k2[...]], axis=0)
        vc = jnp.concatenate([jnp.concatenate([v0[...], v1[...], v2[...]], axis=0), jnp.ones((KG, 128), bf16)], axis=1)
        lane = lax.broadcasted_iota(jnp.int32, (1, 128), 1)
        outs = []
        for hh in range(2):
            qm = jnp.where(lane // HEAD_DIM == hh, q2, jnp.zeros_like(q2))
            s = lax.dot_general(qm, kc, NT, preferred_element_type=f32) + b_ref[hh]
            p = jnp.exp(s - jnp.max(s, axis=1, keepdims=True))
            ol = jnp.dot(p.astype(bf16), vc, preferred_element_type=f32)
            outs.append(ol[:, :128] / ol[:, 128:])
        o_ref[...] = jnp.where(lane // HEAD_DIM == 0, outs[0], outs[1]).astype(bf16)

    return _pcall(
        body, name=f"attn_fwd_{layer}", grid=(N_HEADS // 2, n_groups), out_shape=jax.ShapeDtypeStruct((T, E_MIX), bf16),
        in_specs=[pl.BlockSpec((QG, 128), lambda hp, g: (g, hp))] + _key_specs(n_groups, 8) + _key_specs(n_groups, 16)
        + [_bias_spec()],
        out_specs=pl.BlockSpec((QG, 128), lambda hp, g: (g, hp)),
        compiler_params=_params("arbitrary", "arbitrary"))(h, h, h, h, h, h, h, bias)


def _halo_rows(ref, r):
    return ref[r:r + 1, :].astype(f32)


def _conv_taps(cu, p6, p7, w_ref):
    row = lax.broadcasted_iota(jnp.int32, cu.shape, 0)
    r1 = jnp.where(row == 0, p7, pltpu.roll(cu, 1, axis=0))
    r2 = jnp.where(row == 0, p6, jnp.where(row == 1, p7, pltpu.roll(cu, 2, axis=0)))
    return w_ref[2:3, :] * cu + w_ref[1:2, :] * r1 + w_ref[0:1, :] * r2, r1, r2


def _conv_fwd(h, w, layer):
    T = h.shape[0]
    tm = 512

    def body(bg_ref, cg_ref, u_ref, cgp_ref, up_ref, w_ref, o_ref):
        first = (pl.program_id(0) == 0).astype(f32)
        cu = cg_ref[...].astype(f32) * u_ref[...].astype(f32)
        p6 = _halo_rows(cgp_ref, 14) * _halo_rows(up_ref, 14) * (1.0 - first)
        p7 = _halo_rows(cgp_ref, 15) * _halo_rows(up_ref, 15) * (1.0 - first)
        conv, _, _ = _conv_taps(cu, p6, p7, w_ref)
        o_ref[...] = (bg_ref[...].astype(f32) * conv).astype(bf16)

    prev = lambda slab: pl.BlockSpec((16, E_MIX), lambda i: (jnp.maximum(i * (tm // 16) - 1, 0), slab))
    return _pcall(
        body, name=f"conv_fwd_{layer}", grid=(T // tm,), out_shape=jax.ShapeDtypeStruct((T, E_MIX), bf16),
        in_specs=[pl.BlockSpec((tm, E_MIX), lambda i: (i, 0)), pl.BlockSpec((tm, E_MIX), lambda i: (i, 1)),
                  pl.BlockSpec((tm, E_MIX), lambda i: (i, 2)), prev(1), prev(2),
                  pl.BlockSpec((CONV_W, E_MIX), lambda i: (0, 0))],
        out_specs=pl.BlockSpec((tm, E_MIX), lambda i: (i, 0)),
        compiler_params=_params("arbitrary"))(h, h, h, h, h, w)


def _kv_mem(memb, wkv):
    def body(m_ref, w_ref, o_ref):
        o_ref[...] = jnp.dot(m_ref[...], w_ref[...], preferred_element_type=f32).astype(bf16)

    return _pcall(body, name="kv_mem", out_shape=jax.ShapeDtypeStruct((N_MEM, 2 * E_MEM), bf16),
                  compiler_params=pltpu.CompilerParams(vmem_limit_bytes=VMEM_LIMIT))(memb, wkv)


def _mem_probs(qm_ref, kv_ref, hh):
    qh = qm_ref[:, hh * MEM_HEAD_DIM:(hh + 1) * MEM_HEAD_DIM]
    kh = kv_ref[:, hh * MEM_HEAD_DIM:(hh + 1) * MEM_HEAD_DIM]
    vh = kv_ref[:, E_MEM + hh * MEM_HEAD_DIM:E_MEM + (hh + 1) * MEM_HEAD_DIM]
    s = lax.dot_general(qh, kh, NT, preferred_element_type=f32) * (1.0 / math.sqrt(MEM_HEAD_DIM))
    e = jnp.exp(s - jnp.max(s, axis=1, keepdims=True))
    return e / jnp.sum(e, axis=1, keepdims=True), qh, kh, vh


def _h_tail_specs(tm):
    return [pl.BlockSpec((tm, E_MEM), functools.partial(lambda i, cb: (i, cb), cb=cb)) for cb in (6, 7, 8, 9)]


def _post_fwd(h, mix, kv, wout, x, g, b, layer):
    T = x.shape[0]
    tm = 512

    def body(qm_ref, z0, z1, z2, mix_ref, kv_ref, w_ref, x_ref, g_ref, b_ref, xn_ref, xb_ref, xh_ref, rs_ref):
        mem = jnp.concatenate(
            [jnp.dot(_mem_probs(qm_ref, kv_ref, hh)[0].astype(bf16), kv_ref[:, E_MEM + hh * MEM_HEAD_DIM:E_MEM + (hh + 1) * MEM_HEAD_DIM],
                     preferred_element_type=f32) for hh in range(MEM_HEADS)], axis=1)
        z = jnp.concatenate([z0[...], z1[...], z2[...]], axis=1).astype(f32)
        act, _ = _silu_parts(z)
        y = jnp.concatenate([mix_ref[...].astype(f32), mem], axis=1) * act
        out = jnp.dot(y.astype(bf16), w_ref[...], preferred_element_type=f32)
        r = DN_ALPHA * x_ref[...] + out
        mu = jnp.mean(r, axis=1, keepdims=True)
        var = jnp.mean(jnp.square(r - mu), axis=1, keepdims=True)
        rstd = lax.rsqrt(var + LN_EPS)
        xhat = (r - mu) * rstd
        xn = xhat * g_ref[...] + b_ref[...]
        xn_ref[...] = xn
        xb_ref[...] = xn.astype(bf16)
        xh_ref[...] = xhat
        rs_ref[...] = rstd

    tile = lambda w: pl.BlockSpec((tm, w), lambda i: (i, 0))
    const = lambda r, c: pl.BlockSpec((r, c), lambda i: (0, 0))
    return _pcall(
        body, name=f"post_fwd_{layer}", grid=(T // tm,),
        out_shape=(jax.ShapeDtypeStruct((T, D_MODEL), f32), jax.ShapeDtypeStruct((T, D_MODEL), bf16),
                   jax.ShapeDtypeStruct((T, D_MODEL), f32), jax.ShapeDtypeStruct((T, 1), f32)),
        in_specs=_h_tail_specs(tm) + [tile(E_MIX), const(N_MEM, 2 * E_MEM), const(E_BRANCH, D_MODEL), tile(D_MODEL),
                                      const(1, D_MODEL), const(1, D_MODEL)],
        out_specs=(tile(D_MODEL), tile(D_MODEL), tile(D_MODEL), tile(1)),
        compiler_params=_params("arbitrary"))(h, h, h, h, mix, kv, wout, x, g, b)


def _loss_head(y, target):
    T = y.shape[0]
    tm = 512

    def body(y_ref, t_ref, l_ref, d_ref):
        @pl.when(pl.program_id(0) == 0)
        def _():
            l_ref[...] = jnp.zeros_like(l_ref)
        err = y_ref[...] - t_ref[...]
        d_ref[...] = err * (1.0 / D_MODEL)
        l_ref[...] += jnp.sum(jnp.square(err))

    return _pcall(
        body, name="loss_head", grid=(T // tm,),
        out_shape=(jax.ShapeDtypeStruct((8, 128), f32), jax.ShapeDtypeStruct((T, D_MODEL), f32)),
        in_specs=[pl.BlockSpec((tm, D_MODEL), lambda i: (i, 0))] * 2,
        out_specs=(pl.BlockSpec((8, 128), lambda i: (0, 0)), pl.BlockSpec((tm, D_MODEL), lambda i: (i, 0))),
        compiler_params=_params("arbitrary"))(y, target)


def _post_bwd(dxn, xhat, rstd, g, h, mix, kv, wout, layer, stacked):
    T = dxn.shape[0]
    tm = 256
    n_tiles = T // tm
    inv = 1.0 / math.sqrt(MEM_HEAD_DIM)

    def body(dxn_ref, xh_ref, rs_ref, g_ref, qm_ref, z0, z1, z2, mix_ref, kv_ref, w_ref, *rest):
        dxp_ref, dhb_ref, dmix_ref, dkv_ref, dwo_out, dg_ref, db_ref, dwo_ref = rest[-8:]

        @pl.when(pl.program_id(0) == 0)
        def _():
            dkv_ref[...] = jnp.zeros_like(dkv_ref)
            dwo_ref[...] = jnp.zeros_like(dwo_ref)
            dg_ref[...] = jnp.zeros_like(dg_ref)
            db_ref[...] = jnp.zeros_like(db_ref)

        dy, xh = dxn_ref[...], xh_ref[...]
        dg_ref[0:1, :] += jnp.sum(dy * xh, axis=0, keepdims=True)
        db_ref[0:1, :] += jnp.sum(dy, axis=0, keepdims=True)
        gx = dy * g_ref[...]
        dr = rs_ref[...] * (gx - jnp.mean(gx, axis=1, keepdims=True) - xh * jnp.mean(gx * xh, axis=1, keepdims=True))
        dxp_ref[...] = DN_ALPHA * dr
        dob = dr.astype(bf16)

        probs = [_mem_probs(qm_ref, kv_ref, hh) for hh in range(MEM_HEADS)]
        mem = jnp.concatenate([jnp.dot(p.astype(bf16), vh, preferred_element_type=f32) for p, _, _, vh in probs], axis=1)
        z = jnp.concatenate([z0[...], z1[...], z2[...]], axis=1).astype(f32)
        act, sig = _silu_parts(z)
        cat = jnp.concatenate([mix_ref[...].astype(f32), mem], axis=1)
        yb = (cat * act).astype(bf16)
        dwo_ref[...] += lax.dot_general(yb, dob, TN, preferred_element_type=f32)
        dyv = lax.dot_general(dob, w_ref[...], NT, preferred_element_type=f32)
        dz = dyv * cat * (sig * (1.0 + z * (1.0 - sig)))
        dcat = dyv * act
        dmix_ref[...] = dcat[:, :E_MIX].astype(bf16)
        dqs = []
        for hh, (p, qh, kh, vh) in enumerate(probs):
            dmem = dcat[:, E_MIX + hh * MEM_HEAD_DIM:E_MIX + (hh + 1) * MEM_HEAD_DIM].astype(bf16)
            dp = lax.dot_general(dmem, vh, NT, preferred_element_type=f32)
            ds = (p * (dp - jnp.sum(p * dp, axis=1, keepdims=True))).astype(bf16)
            dqs.append(jnp.dot(ds, kh, preferred_element_type=f32) * inv)
            dkv_ref[:, hh * MEM_HEAD_DIM:(hh + 1) * MEM_HEAD_DIM] += lax.dot_general(ds, qh, TN, preferred_element_type=f32) * inv
            dkv_ref[:, E_MEM + hh * MEM_HEAD_DIM:E_MEM + (hh + 1) * MEM_HEAD_DIM] += lax.dot_general(
                p.astype(bf16), dmem, TN, preferred_element_type=f32)
        dhb_ref[...] = jnp.concatenate(dqs + [dz], axis=1).astype(bf16)

        @pl.when(pl.program_id(0) == n_tiles - 1)
        def _():
            dwo_out[...] = dwo_ref[...].astype(bf16)

    tile = lambda w: pl.BlockSpec((tm, w), lambda i: (i, 0))
    const = lambda r, c: pl.BlockSpec((r, c), lambda i: (0, 0))
    carried = [] if stacked is None else [stacked]
    return _pcall(
        body, name=f"post_bwd_{layer}", grid=(n_tiles,),
        out_shape=(jax.ShapeDtypeStruct((T, D_MODEL), f32), jax.ShapeDtypeStruct((T, E_MEM + E_BRANCH), bf16),
                   jax.ShapeDtypeStruct((T, E_MIX), bf16), jax.ShapeDtypeStruct((N_MEM, 2 * E_MEM), f32),
                   jax.ShapeDtypeStruct((DEPTH, E_BRANCH, D_MODEL), bf16), jax.ShapeDtypeStruct((8, D_MODEL), f32),
                   jax.ShapeDtypeStruct((8, D_MODEL), f32)),
        in_specs=[tile(D_MODEL), tile(D_MODEL), tile(1), const(1, D_MODEL)] + _h_tail_specs(tm)
        + [tile(E_MIX), const(N_MEM, 2 * E_MEM), const(E_BRANCH, D_MODEL)] + [ANY] * len(carried),
        out_specs=(tile(D_MODEL), tile(E_MEM + E_BRANCH), tile(E_MIX), const(N_MEM, 2 * E_MEM),
                   pl.BlockSpec((None, E_BRANCH, D_MODEL), lambda i: (layer, 0, 0)), const(8, D_MODEL), const(8, D_MODEL)),
        input_output_aliases={11: 4} if carried else {},
        scratch_shapes=[pltpu.VMEM((E_BRANCH, D_MODEL), f32)],
        compiler_params=_params("arbitrary"))(dxn, xhat, rstd, g, h, h, h, h, mix, kv, wout, *carried)


def _attn_bwd(h, bias, dmix, layer):
    T = h.shape[0]
    n_groups = T // QG
    scale = 1.0 / math.sqrt(HEAD_DIM)

    def body(q_ref, k0, k1, k2, v0, v1, v2, do_ref, b_ref, dq_ref, dk_ref, dv_ref, db_ref, acck, accv):
        g = pl.program_id(1)

        @pl.when(g == 0)
        def _():
            acck[...] = jnp.zeros_like(acck)
            accv[...] = jnp.zeros_like(accv)
            db_ref[...] = jnp.zeros_like(db_ref)

        @pl.when(g < n_groups)
        def _():
            do2 = do_ref[...]
            q2 = q_ref[...] * scale
            kc = jnp.concatenate([k0[...], k1[...], k2[...]], axis=0)
            vc = jnp.concatenate([v0[...], v1[...], v2[...]], axis=0)
            lane = lax.broadcasted_iota(jnp.int32, (1, 128), 1)
            dqs, dks, dvs = [], [], []
            for hh in range(2):
                hm = lane // HEAD_DIM == hh
                qm = jnp.where(hm, q2, jnp.zeros_like(q2))
                dom = jnp.where(hm, do2, jnp.zeros_like(do2))
                s = lax.dot_general(qm, kc, NT, preferred_element_type=f32) + b_ref[hh]
                e = jnp.exp(s - jnp.max(s, axis=1, keepdims=True))
                p = e * (1.0 / jnp.sum(e, axis=1, keepdims=True))
                dp = lax.dot_general(dom, vc, NT, preferred_element_type=f32)
                ds = p * (dp - jnp.sum(p * dp, axis=1, keepdims=True))
                db_ref[hh] += ds[:, KG - DB_COLS:]
                dsb, pb = ds.astype(bf16), p.astype(bf16)
                dqs.append(jnp.dot(dsb, kc, preferred_element_type=f32) * scale)
                dks.append(lax.dot_general(dsb, q2, TN, preferred_element_type=f32))
                dvs.append(lax.dot_general(pb, do2, TN, preferred_element_type=f32))
            first = lane // HEAD_DIM == 0
            dq_ref[...] = jnp.where(first, dqs[0], dqs[1]).astype(bf16)
            dkc = jnp.where(first, dks[0], dks[1])
            dvc = jnp.where(first, dvs[0], dvs[1])
            for jj in range(3):
                slot = (g + 1 + jj) % 3
                if jj == 2:
                    acck[slot] = dkc[jj * QG:(jj + 1) * QG]
                    accv[slot] = dvc[jj * QG:(jj + 1) * QG]
                else:
                    acck[slot] += dkc[jj * QG:(jj + 1) * QG]
                    accv[slot] += dvc[jj * QG:(jj + 1) * QG]

        done = (g + 1) % 3
        dk_ref[...] = acck[done].astype(bf16)
        dv_ref[...] = accv[done].astype(bf16)

    last = n_groups - 1
    qspec = pl.BlockSpec((QG, 128), lambda hp, g: (jnp.minimum(g, last), hp))
    kout = pl.BlockSpec((QG, 128), lambda hp, g: (jnp.clip(g - 2, 0, last), hp))
    dbspec = pl.BlockSpec((2, QG, DB_COLS), lambda hp, g: (hp, 0, 0))
    return _pcall(
        body, name=f"attn_bwd_{layer}", grid=(N_HEADS // 2, n_groups + 2),
        out_shape=(jax.ShapeDtypeStruct((T, E_MIX), bf16),) * 3 + (jax.ShapeDtypeStruct((N_HEADS, QG, DB_COLS), f32),),
        in_specs=[qspec] + _key_specs(n_groups, 8) + _key_specs(n_groups, 16) + [qspec, _bias_spec()],
        out_specs=(qspec, kout, kout, dbspec),
        scratch_shapes=[pltpu.VMEM((3, QG, 128), f32), pltpu.VMEM((3, QG, 128), f32)],
        compiler_params=_params("arbitrary", "arbitrary"))(h, h, h, h, h, h, h, dmix, bias)


def _conv_bwd(h, w, dmix, layer):
    T = h.shape[0]
    tm = 512
    n_tiles = T // tm

    def body(bg_ref, cg_ref, u_ref, cgp_ref, up_ref, dy_ref, bgn_ref, dyn_ref, w_ref, dbg_ref, dcg_ref, du_ref, dw_ref):
        i = pl.program_id(0)

        @pl.when(i == 0)
        def _():
            dw_ref[...] = jnp.zeros_like(dw_ref)

        first = (i == 0).astype(f32)
        final = (i == n_tiles - 1).astype(f32)
        bg, cg, u = bg_ref[...].astype(f32), cg_ref[...].astype(f32), u_ref[...].astype(f32)
        dy = dy_ref[...].astype(f32)
        cu = cg * u
        p6 = _halo_rows(cgp_ref, 14) * _halo_rows(up_ref, 14) * (1.0 - first)
        p7 = _halo_rows(cgp_ref, 15) * _halo_rows(up_ref, 15) * (1.0 - first)
        conv, r1, r2 = _conv_taps(cu, p6, p7, w_ref)
        dbg_ref[...] = (dy * conv).astype(bf16)
        dc = dy * bg
        n0 = _halo_rows(dyn_ref, 0) * _halo_rows(bgn_ref, 0) * (1.0 - final)
        n1 = _halo_rows(dyn_ref, 1) * _halo_rows(bgn_ref, 1) * (1.0 - final)
        row = lax.broadcasted_iota(jnp.int32, dc.shape, 0)
        f1 = jnp.where(row == tm - 1, n0, pltpu.roll(dc, tm - 1, axis=0))
        f2 = jnp.where(row == tm - 2, n0, jnp.where(row == tm - 1, n1, pltpu.roll(dc, tm - 2, axis=0)))
        dcu = w_ref[2:3, :] * dc + w_ref[1:2, :] * f1 + w_ref[0:1, :] * f2
        dcg_ref[...] = (dcu * u).astype(bf16)
        du_ref[...] = (dcu * cg).astype(bf16)
        dw_ref[0:1, :] += jnp.sum(dc * r2, axis=0, keepdims=True)
        dw_ref[1:2, :] += jnp.sum(dc * r1, axis=0, keepdims=True)
        dw_ref[2:3, :] += jnp.sum(dc * cu, axis=0, keepdims=True)

    tile = lambda slab: pl.BlockSpec((tm, E_MIX), lambda i: (i, slab))
    prev = lambda slab: pl.BlockSpec((16, E_MIX), lambda i: (jnp.maximum(i * (tm // 16) - 1, 0), slab))
    nxt = lambda slab: pl.BlockSpec((16, E_MIX), lambda i: (jnp.minimum((i + 1) * (tm // 16), T // 16 - 1), slab))
    return _pcall(
        body, name=f"conv_bwd_{layer}", grid=(n_tiles,),
        out_shape=(jax.ShapeDtypeStruct((T, E_MIX), bf16),) * 3 + (jax.ShapeDtypeStruct((8, E_MIX), f32),),
        in_specs=[tile(0), tile(1), tile(2), prev(1), prev(2), tile(0), nxt(0), nxt(0),
                  pl.BlockSpec((CONV_W, E_MIX), lambda i: (0, 0))],
        out_specs=(tile(0), tile(0), tile(0), pl.BlockSpec((8, E_MIX), lambda i: (0, 0))),
        compiler_params=_params("arbitrary"))(h, h, h, h, h, dmix, h, dmix, w)


def _inproj_bwd_dx(da, db, dc, dhb, dxp, win_g, layer):
    T = dxp.shape[0]
    tm = 512

    def body(da_ref, db_ref, dc_ref, dhb_ref, dxp_ref, w_ref, o_ref):
        dh = jnp.concatenate([da_ref[...], db_ref[...], dc_ref[...], dhb_ref[...]], axis=1)
        acc = dxp_ref[...]
        for j in range(N_CHIPS):
            acc = acc + lax.dot_general(dh[:, j * W_IN_COLS:(j + 1) * W_IN_COLS], w_ref[j], NT, preferred_element_type=f32)
        o_ref[...] = acc

    tile = lambda w: pl.BlockSpec((tm, w), lambda i: (i, 0))
    return _pcall(
        body, name=f"inproj_bwd_dx_{layer}", grid=(T // tm,), out_shape=jax.ShapeDtypeStruct((T, D_MODEL), f32),
        in_specs=[tile(E_MIX), tile(E_MIX), tile(E_MIX), tile(E_MEM + E_BRANCH), tile(D_MODEL),
                  pl.BlockSpec((None, N_CHIPS, D_MODEL, W_IN_COLS), lambda i: (layer, 0, 0, 0), pipeline_mode=pl.Buffered(1))],
        out_specs=tile(D_MODEL),
        compiler_params=_params("arbitrary"))(da, db, dc, dhb, dxp, win_g)


def _inproj_bwd_dw(da, db, dc, dhb, xb, layer, stacked):
    T = xb.shape[0]
    tm = 512
    n_tiles = T // tm

    def body(da_ref, db_ref, dc_ref, dhb_ref, x_ref, *rest):
        o_ref, acc, stage, sem = rest[-4:]
        i = pl.program_id(0)

        @pl.when(i == 0)
        def _():
            acc[...] = jnp.zeros_like(acc)

        dh = jnp.concatenate([da_ref[...], db_ref[...], dc_ref[...], dhb_ref[...]], axis=1)
        xt = x_ref[...]
        for j in range(N_CHIPS):
            acc[j] += lax.dot_general(xt, dh[:, j * W_IN_COLS:(j + 1) * W_IN_COLS], TN, preferred_element_type=f32)

        @pl.when(i == n_tiles - 1)
        def _():
            for j in range(N_CHIPS):
                stage[...] = acc[j].astype(bf16)
                cp = pltpu.make_async_copy(stage, o_ref.at[layer, j], sem)
                cp.start()
                cp.wait()

    tile = lambda w: pl.BlockSpec((tm, w), lambda i: (i, 0))
    carried = [] if stacked is None else [stacked]
    return _pcall(
        body, name=f"inproj_bwd_dw_{layer}", grid=(n_tiles,),
        out_shape=jax.ShapeDtypeStruct((DEPTH, N_CHIPS, D_MODEL, W_IN_COLS), bf16),
        in_specs=[tile(E_MIX), tile(E_MIX), tile(E_MIX), tile(E_MEM + E_BRANCH), tile(D_MODEL)] + [ANY] * len(carried),
        out_specs=ANY, input_output_aliases={5: 0} if carried else {},
        scratch_shapes=[pltpu.VMEM((N_CHIPS, D_MODEL, W_IN_COLS), f32), pltpu.VMEM((D_MODEL, W_IN_COLS), bf16),
                        pltpu.SemaphoreType.DMA],
        compiler_params=_params("arbitrary"))(da, db, dc, dhb, xb, *carried)


def _kv_mem_bwd(memb, dkv, layer, stacked):
    def body(m_ref, d_ref, *rest):
        o_ref = rest[-1]
        o_ref[...] = lax.dot_general(m_ref[...], d_ref[...].astype(bf16), TN, preferred_element_type=f32).astype(bf16)

    carried = [] if stacked is None else [stacked]
    return _pcall(body, name=f"kv_mem_bwd_{layer}", grid=(1,),
                  out_shape=jax.ShapeDtypeStruct((DEPTH, D_MODEL, 2 * E_MEM), bf16),
                  in_specs=[pl.BlockSpec((N_MEM, D_MODEL), lambda i: (0, 0)), pl.BlockSpec((N_MEM, 2 * E_MEM), lambda i: (0, 0))]
                  + [ANY] * len(carried),
                  out_specs=pl.BlockSpec((None, D_MODEL, 2 * E_MEM), lambda i: (layer, 0, 0)),
                  input_output_aliases={2: 0} if carried else {},
                  compiler_params=_params("arbitrary"))(memb, dkv, *carried)


def _to_bf16(a, name):
    rows, cols = a.shape
    br = 512

    def body(a_ref, o_ref):
        o_ref[...] = a_ref[...].astype(bf16)

    return _pcall(body, name=name, grid=(rows // br,), out_shape=jax.ShapeDtypeStruct((rows, cols), bf16),
                  in_specs=[pl.BlockSpec((br, cols), lambda i: (i, 0))], out_specs=pl.BlockSpec((br, cols), lambda i: (i, 0)),
                  compiler_params=_params("arbitrary"))(a)


def _adamw(w, g, m, v, name):
    shape = w.shape
    cols = shape[-1]
    rows = w.size // cols
    args = [a.reshape(rows, cols) for a in (w, g, m, v)]
    br = 256 if rows % 256 == 0 and rows > 256 else rows

    def body(w_ref, g_ref, m_ref, v_ref, d_ref, nm_ref, nv_ref):
        gg = g_ref[...]
        nm = ADAM_B1 * m_ref[...] + (1.0 - ADAM_B1) * gg
        nv = ADAM_B2 * v_ref[...] + (1.0 - ADAM_B2) * jnp.square(gg)
        m_hat = nm / (1.0 - ADAM_B1 ** ADAM_STEP)
        v_hat = nv / (1.0 - ADAM_B2 ** ADAM_STEP)
        d_ref[...] = -ADAM_LR * (m_hat / (jnp.sqrt(v_hat) + ADAM_EPS) + ADAM_WD * w_ref[...])
        nm_ref[...] = nm
        nv_ref[...] = nv

    spec = pl.BlockSpec((br, cols), lambda i: (i, 0))
    outs = _pcall(body, name=name, grid=(rows // br,), out_shape=(jax.ShapeDtypeStruct((rows, cols), f32),) * 3,
                  in_specs=[spec] * 4, out_specs=(spec,) * 3, compiler_params=_params("arbitrary"))(*args)
    return tuple(o.reshape(shape) for o in outs)


def kernel(x, mem, w_in, w_mem_kv, w_out, rel_bias, conv_w, ln_g, ln_b, loss_target, m_w_in, m_w_mem_kv, m_w_out, m_rel_bias, m_conv_w, m_ln_g, m_ln_b, v_w_in, v_w_mem_kv, v_w_out, v_rel_bias, v_conv_w, v_ln_g, v_ln_b):
    T = x.shape[1]
    x0 = x.reshape(T, D_MODEL)
    target = loss_target.reshape(T, D_MODEL)
    memb = mem.reshape(N_MEM, D_MODEL).astype(bf16)
    chip = 2 * lax.axis_index("x") + lax.axis_index("y")
    core = lax.axis_index("c")
    chip_arr = jnp.reshape(chip, (1,)).astype(jnp.int32)
    core_arr = jnp.reshape(core, (1,)).astype(jnp.int32)

    win_g, wkv_g, wout_g, cw_g = _gather_weights(w_in.astype(bf16), w_mem_kv.astype(bf16), w_out.astype(bf16), conv_w)
    wkv_full = wkv_g.reshape(DEPTH, D_MODEL, 2 * E_MEM)
    wout_full = wout_g.reshape(DEPTH, E_BRANCH, D_MODEL)
    conv_full = jnp.transpose(cw_g, (1, 2, 0, 3)).reshape(DEPTH // 2, CONV_W, E_MIX)
    tables = jnp.pad(rel_bias, ((0, 0), (0, 0), (0, N_REL_PAD - N_REL)))

    xs, xbs, hs, mixes, kvs, xhats, rstds, biases = [x0], [_to_bf16(x0, "cast_x")], [], [], [], [], [], {}
    for layer in range(DEPTH):
        h = _inproj(xbs[layer], win_g, layer)
        if layer % 2 == 0:
            biases[layer] = _bias_expand(tables[layer // 2], layer)
            mix = _attn_fwd(h, biases[layer], layer)
        else:
            mix = _conv_fwd(h, conv_full[layer // 2], layer)
        kv = _kv_mem(memb, wkv_full[layer])
        xn, xnb, xhat, rstd = _post_fwd(h, mix, kv, wout_full[layer], xs[layer], ln_g[layer][None, :], ln_b[layer][None, :], layer)
        xs.append(xn); xbs.append(xnb); hs.append(h); mixes.append(mix); kvs.append(kv); xhats.append(xhat); rstds.append(rstd)

    lsum, dx = _loss_head(xs[DEPTH], target)
    loss = lax.psum(lsum[0, 0], ("x", "y", "c")) * (0.5 / D_MODEL)

    g_win = g_wkv = g_wout = None
    dgs, dbs, dconvs, dtables = [None] * DEPTH, [None] * DEPTH, [None] * (DEPTH // 2), [None] * ((DEPTH + 1) // 2)
    for layer in reversed(range(DEPTH)):
        h = hs[layer]
        dxp, dhb, dmix, dkv, g_wout, dgs[layer], dbs[layer] = _post_bwd(
            dx, xhats[layer], rstds[layer], ln_g[layer][None, :], h, mixes[layer], kvs[layer], wout_full[layer], layer, g_wout)
        if layer % 2 == 0:
            da, db, dc, dbias = _attn_bwd(h, biases[layer], dmix, layer)
            dtables[layer // 2] = _bias_reduce(dbias, layer)
        else:
            da, db, dc, dconvs[layer // 2] = _conv_bwd(h, conv_full[layer // 2], dmix, layer)
        dx = _inproj_bwd_dx(da, db, dc, dhb, dxp, win_g, layer)
        g_win = _inproj_bwd_dw(da, db, dc, dhb, xbs[layer], layer, g_win)
        g_wkv = _kv_mem_bwd(memb, dkv, layer, g_wkv)
    grad_x = dx.reshape(1, T, D_MODEL)

    gs = [g_win, g_wkv.reshape(DEPTH, N_CHIPS, W_KV_ROWS, 2 * E_MEM), g_wout.reshape(DEPTH, N_CHIPS, W_OUT_ROWS, D_MODEL)]
    rs = _sibling_exchange(gs)
    ss = [_pair_sum(g, r, core_arr) for g, r in zip(gs, rs)]
    rb = _chip_scatter(ss)
    place = jnp.concatenate([chip_arr, core_arr])
    fs = [_chip_sum(s, r, place) for s, r in zip(ss, rb)]
    grad_w_in, grad_w_mem_kv, grad_w_out = _sibling_share(fs)

    pad8 = lambda a: jnp.pad(a, ((0, 8 - a.shape[0]), (0, 0)))
    parts = dgs + dbs + dconvs + [pad8(t.reshape(-1, D_MODEL)) for t in dtables]
    small = _small_allreduce(jnp.concatenate(parts, axis=0))
    grad_ln_g = jnp.stack([small[8 * l] for l in range(DEPTH)])
    grad_ln_b = jnp.stack([small[8 * (DEPTH + l)] for l in range(DEPTH)])
    conv_all = jnp.stack([small[8 * (2 * DEPTH + a):8 * (2 * DEPTH + a) + CONV_W] for a in range(DEPTH // 2)])
    grad_conv_w = lax.dynamic_slice_in_dim(conv_all, chip * (E_MIX // N_CHIPS), E_MIX // N_CHIPS, axis=2)
    t0 = 8 * (2 * DEPTH + DEPTH // 2)
    grad_rel_bias = jnp.stack([small[t0 + 8 * a:t0 + 8 * a + 6].reshape(N_HEADS, N_REL_PAD)[:, :N_REL]
                               for a in range((DEPTH + 1) // 2)])

    grads = [grad_w_in, grad_w_mem_kv, grad_w_out, grad_rel_bias, grad_conv_w, grad_ln_g, grad_ln_b]
    weights = [w_in, w_mem_kv, w_out, rel_bias, conv_w, ln_g, ln_b]
    moms = [m_w_in, m_w_mem_kv, m_w_out, m_rel_bias, m_conv_w, m_ln_g, m_ln_b]
    vels = [v_w_in, v_w_mem_kv, v_w_out, v_rel_bias, v_conv_w, v_ln_g, v_ln_b]
    names = ["w_in", "w_mem_kv", "w_out", "rel_bias", "conv_w", "ln_g", "ln_b"]
    upd = [_adamw(w, g, m, v, f"adamw_{n}") for w, g, m, v, n in zip(weights, grads, moms, vels, names)]
    deltas, new_m, new_v = zip(*upd)
    return (loss, grad_x, *grads, *deltas, *new_m, *new_v)
```

```python
import functools
import math

import jax
import jax.numpy as jnp
from jax import lax
from jax.experimental import pallas as pl
from jax.experimental.pallas import tpu as pltpu

f32, bf16 = jnp.float32, jnp.bfloat16

D_MODEL = 1024
DEPTH = 4
CHUNK = 64
N_PREV = 8
N_HEADS = 16
HEAD_DIM = 64
E_MIX = 1024
REL_CLIP = 128
N_REL = 2 * REL_CLIP + 1
N_REL_PAD = 384
CONV_W = 3
N_MEM = 256
MEM_HEADS = 4
MEM_HEAD_DIM = 128
E_MEM = 512
E_BRANCH = E_MIX + E_MEM
N_IN = 3 * E_MIX + E_MEM + E_BRANCH
N_CHIPS = 4
W_IN_COLS = N_IN // N_CHIPS
W_KV_ROWS = D_MODEL // N_CHIPS
W_OUT_ROWS = E_BRANCH // N_CHIPS
DN_ALPHA = (2.0 * DEPTH) ** 0.25
LN_EPS = 1e-5
ADAM_LR, ADAM_B1, ADAM_B2, ADAM_EPS, ADAM_WD, ADAM_STEP = 0.001, 0.9, 0.999, 1e-08, 0.01, 10

QG = 4 * CHUNK
KG = QG + N_PREV * CHUNK
DB_COLS = KG // 2
NEG = -1e30
VMEM_LIMIT = 56 * 1024 * 1024

NT = (((1,), (1,)), ((), ()))
TN = (((0,), (0,)), ((), ()))
MESH = pl.DeviceIdType.MESH
ANY = pl.BlockSpec(memory_space=pl.ANY)


def _pcall(body, **kw):
    return pl.pallas_call(body, **kw)


def _params(*sem):
    return pltpu.CompilerParams(dimension_semantics=sem, vmem_limit_bytes=VMEM_LIMIT)


def _silu_parts(z):
    sig = 1.0 / (1.0 + jnp.exp(-z))
    return z * sig, sig


def _gather_weights(win_s, wkv_s, wout_s, cw_s):
    def body(win_ref, wkv_ref, wout_ref, cw_ref, win_g, wkv_g, wout_g, cw_g, send_sems, recv_sems):
        x, y, c = lax.axis_index("x"), lax.axis_index("y"), lax.axis_index("c")
        me = 2 * x + y
        sibling = (x, y, 1 - c)
        chips = [(1 - x, y), (x, 1 - y), (1 - x, 1 - y)]
        mine, theirs = pl.ds(2 * c, 2), pl.ds(2 * (1 - c), 2)
        big = [(win_ref, win_g), (wkv_ref, wkv_g), (wout_ref, wout_g)]

        def rcopy(k, src, dst, to):
            return pltpu.make_async_remote_copy(src_ref=src, dst_ref=dst, send_sem=send_sems.at[k],
                                                recv_sem=recv_sems.at[k], device_id=to, device_id_type=MESH)

        sends = [rcopy(21 + a, s, g.at[:, me], sibling) for a, (s, g) in enumerate(big)]
        sends.append(rcopy(24, cw_ref, cw_g.at[me], sibling))
        for a, (s, g) in enumerate(big):
            for p, (px, py) in enumerate(chips):
                sends.append(rcopy(3 * a + p, s.at[mine], g.at[mine, me], (px, py, c)))
        for p, (px, py) in enumerate(chips):
            sends.append(rcopy(18 + p, cw_ref, cw_g.at[me], (px, py, c)))
        for cp in sends:
            cp.start()
        for a, (s, g) in enumerate(big):
            for p, (px, py) in enumerate(chips):
                jp = 2 * px + py
                rcopy(3 * a + p, s.at[mine], g.at[mine, jp], (px, py, c)).wait_recv()
                fwd = rcopy(9 + 3 * a + p, g.at[mine, jp], g.at[mine, jp], sibling)
                fwd.start()
                sends.append(fwd)
        for p, (px, py) in enumerate(chips):
            rcopy(18 + p, cw_ref, cw_g.at[2 * px + py], (px, py, c)).wait_recv()
        for a, (s, g) in enumerate(big):
            for p, (px, py) in enumerate(chips):
                jp = 2 * px + py
                rcopy(9 + 3 * a + p, g.at[theirs, jp], g.at[theirs, jp], sibling).wait_recv()
        for a, (s, g) in enumerate(big):
            rcopy(21 + a, s, g.at[:, me], sibling).wait_recv()
        rcopy(24, cw_ref, cw_g.at[me], sibling).wait_recv()
        for cp in sends:
            cp.wait_send()

    out_shape = (
        jax.ShapeDtypeStruct((DEPTH, N_CHIPS) + win_s.shape[1:], bf16),
        jax.ShapeDtypeStruct((DEPTH, N_CHIPS) + wkv_s.shape[1:], bf16),
        jax.ShapeDtypeStruct((DEPTH, N_CHIPS) + wout_s.shape[1:], bf16),
        jax.ShapeDtypeStruct((N_CHIPS,) + cw_s.shape, f32),
    )
    return _pcall(body, name="gather_weights", out_shape=out_shape, in_specs=[ANY] * 4, out_specs=[ANY] * 4,
                  scratch_shapes=[pltpu.SemaphoreType.DMA((25,)), pltpu.SemaphoreType.DMA((25,))])(win_s, wkv_s, wout_s, cw_s)


def _small_allreduce(buf):
    rows, cols = buf.shape

    def body(b_ref, o_ref, slots, send_sems, recv_sems):
        x, y, c = lax.axis_index("x"), lax.axis_index("y"), lax.axis_index("c")
        me = 4 * x + 2 * y + c
        slots[me] = b_ref[...]
        copies = []
        for r in range(1, 8):
            fx, fy, fc = (r >> 2) & 1, (r >> 1) & 1, r & 1
            px, py, pc = x ^ fx, y ^ fy, c ^ fc
            copies.append(pltpu.make_async_remote_copy(
                src_ref=b_ref, dst_ref=slots.at[me], send_sem=send_sems.at[r - 1], recv_sem=recv_sems.at[r - 1],
                device_id=(px, py, pc), device_id_type=MESH))
        for cp in copies:
            cp.start()
        for r in range(1, 8):
            fx, fy, fc = (r >> 2) & 1, (r >> 1) & 1, r & 1
            peer = 4 * (x ^ fx) + 2 * (y ^ fy) + (c ^ fc)
            pltpu.make_async_remote_copy(
                src_ref=b_ref, dst_ref=slots.at[peer], send_sem=send_sems.at[r - 1], recv_sem=recv_sems.at[r - 1],
                device_id=(x ^ fx, y ^ fy, c ^ fc), device_id_type=MESH).wait_recv()
        for cp in copies:
            cp.wait_send()
        acc = slots[0]
        for d in range(1, 8):
            acc = acc + slots[d]
        o_ref[...] = acc

    return _pcall(body, name="small_allreduce", out_shape=jax.ShapeDtypeStruct((rows, cols), f32),
                  in_specs=[pl.BlockSpec(memory_space=pltpu.VMEM)], out_specs=pl.BlockSpec(memory_space=pltpu.VMEM),
                  scratch_shapes=[pltpu.VMEM((8, rows, cols), f32), pltpu.SemaphoreType.DMA((7,)),
                                  pltpu.SemaphoreType.DMA((7,))])(buf)


def _sibling_exchange(gs):
    def body(*refs):
        n = len(gs)
        g_refs, r_refs, send_sems, recv_sems = refs[:n], refs[n:2 * n], refs[2 * n], refs[2 * n + 1]
        x, y, c = lax.axis_index("x"), lax.axis_index("y"), lax.axis_index("c")
        theirs = pl.ds(2 * (1 - c), 2)
        copies = [pltpu.make_async_remote_copy(src_ref=g.at[theirs], dst_ref=r, send_sem=send_sems.at[a],
                                               recv_sem=recv_sems.at[a], device_id=(x, y, 1 - c), device_id_type=MESH)
                  for a, (g, r) in enumerate(zip(g_refs, r_refs))]
        for cp in copies:
            cp.start()
        for cp in copies:
            cp.wait()

    out_shape = tuple(jax.ShapeDtypeStruct((2,) + g.shape[1:], g.dtype) for g in gs)
    return _pcall(body, name="sibling_exchange", out_shape=out_shape, in_specs=[ANY] * len(gs), out_specs=[ANY] * len(gs),
                  scratch_shapes=[pltpu.SemaphoreType.DMA((len(gs),)), pltpu.SemaphoreType.DMA((len(gs),))])(*gs)


def _chip_scatter(ss):
    def body(*refs):
        n = len(ss)
        s_refs, r_refs, send_sems, recv_sems = refs[:n], refs[n:2 * n], refs[2 * n], refs[2 * n + 1]
        x, y, c = lax.axis_index("x"), lax.axis_index("y"), lax.axis_index("c")
        chips = [(1 - x, y), (x, 1 - y), (1 - x, 1 - y)]
        copies = []
        for a, (s, r) in enumerate(zip(s_refs, r_refs)):
            for p, (px, py) in enumerate(chips):
                copies.append(pltpu.make_async_remote_copy(
                    src_ref=s.at[:, 2 * px + py], dst_ref=r.at[p], send_sem=send_sems.at[3 * a + p],
                    recv_sem=recv_sems.at[3 * a + p], device_id=(px, py, c), device_id_type=MESH))
        for cp in copies:
            cp.start()
        for cp in copies:
            cp.wait()

    out_shape = tuple(jax.ShapeDtypeStruct((3, 2) + s.shape[2:], s.dtype) for s in ss)
    return _pcall(body, name="chip_scatter", out_shape=out_shape, in_specs=[ANY] * len(ss), out_specs=[ANY] * len(ss),
                  scratch_shapes=[pltpu.SemaphoreType.DMA((3 * len(ss),)), pltpu.SemaphoreType.DMA((3 * len(ss),))])(*ss)


def _sibling_share(fs):
    def body(*refs):
        n = len(fs)
        o_refs, send_sems, recv_sems = refs[n:2 * n], refs[2 * n], refs[2 * n + 1]
        x, y, c = lax.axis_index("x"), lax.axis_index("y"), lax.axis_index("c")
        mine = pl.ds(2 * c, 2)
        copies = [pltpu.make_async_remote_copy(src_ref=o.at[mine], dst_ref=o.at[mine], send_sem=send_sems.at[a],
                                               recv_sem=recv_sems.at[a], device_id=(x, y, 1 - c), device_id_type=MESH)
                  for a, o in enumerate(o_refs)]
        for cp in copies:
            cp.start()
        for a, o in enumerate(o_refs):
            theirs = o.at[pl.ds(2 * (1 - c), 2)]
            pltpu.make_async_remote_copy(src_ref=theirs, dst_ref=theirs, send_sem=send_sems.at[a], recv_sem=recv_sems.at[a],
                                         device_id=(x, y, 1 - c), device_id_type=MESH).wait_recv()
        for cp in copies:
            cp.wait_send()

    out_shape = tuple(jax.ShapeDtypeStruct(f.shape, f.dtype) for f in fs)
    return _pcall(body, name="sibling_share", out_shape=out_shape, in_specs=[ANY] * len(fs), out_specs=[ANY] * len(fs),
                  input_output_aliases={a: a for a in range(len(fs))},
                  scratch_shapes=[pltpu.SemaphoreType.DMA((len(fs),)), pltpu.SemaphoreType.DMA((len(fs),))])(*fs)


def _pair_sum(g, r, c_arr):
    _, _, rows, cols = g.shape
    g3, r3 = g.reshape(4, 4 * rows, cols), r.reshape(2, 4 * rows, cols)
    br = 512 if (4 * rows) % 512 == 0 else 256

    def body(c_ref, g_ref, r_ref, o_ref):
        o_ref[...] = (g_ref[...].astype(f32) + r_ref[...].astype(f32)).astype(bf16)

    out = _pcall(
        body, name="pair_sum", out_shape=jax.ShapeDtypeStruct(r3.shape, bf16),
        grid_spec=pltpu.PrefetchScalarGridSpec(
            num_scalar_prefetch=1, grid=(2, 4 * rows // br),
            in_specs=[pl.BlockSpec((1, br, cols), lambda a, i, c_ref: (2 * c_ref[0] + a, i, 0)),
                      pl.BlockSpec((1, br, cols), lambda a, i, c_ref: (a, i, 0))],
            out_specs=pl.BlockSpec((1, br, cols), lambda a, i, c_ref: (a, i, 0))),
        compiler_params=_params("arbitrary", "arbitrary"))(c_arr, g3, r3)
    return out.reshape(2, 4, rows, cols)


def _chip_sum(s, r, place):
    _, _, rows, cols = s.shape
    br = 256 if rows % 256 == 0 else 128

    def body(place_ref, s_ref, r_ref, o_ref):
        acc = s_ref[0, 0].astype(f32)
        for p in range(3):
            acc = acc + r_ref[p, 0].astype(f32)
        o_ref[0] = acc

    return _pcall(
        body, name="chip_sum", out_shape=jax.ShapeDtypeStruct((4, rows, cols), f32),
        grid_spec=pltpu.PrefetchScalarGridSpec(
            num_scalar_prefetch=1, grid=(2, rows // br),
            in_specs=[pl.BlockSpec((1, 1, br, cols), lambda a, i, place_ref: (a, place_ref[0], i, 0)),
                      pl.BlockSpec((3, 1, br, cols), lambda a, i, place_ref: (0, a, i, 0))],
            out_specs=pl.BlockSpec((1, br, cols), lambda a, i, place_ref: (2 * place_ref[1] + a, i, 0))),
        compiler_params=_params("arbitrary", "arbitrary"))(place, s, r)


def _inproj(xb, win_g, layer):
    T = xb.shape[0]
    tm = 512

    def body(x_ref, w_ref, o_ref):
        xt = x_ref[...]
        for j in range(N_CHIPS):
            o_ref[:, j * W_IN_COLS:(j + 1) * W_IN_COLS] = jnp.dot(xt, w_ref[j], preferred_element_type=f32).astype(bf16)

    return _pcall(
        body, name=f"inproj_{layer}", grid=(T // tm,), out_shape=jax.ShapeDtypeStruct((T, N_IN), bf16),
        in_specs=[pl.BlockSpec((tm, D_MODEL), lambda i: (i, 0)),
                  pl.BlockSpec((None, N_CHIPS, D_MODEL, W_IN_COLS), lambda i: (layer, 0, 0, 0), pipeline_mode=pl.Buffered(1))],
        out_specs=pl.BlockSpec((tm, N_IN), lambda i: (i, 0)),
        compiler_params=_params("arbitrary"))(xb, win_g)


def _rel_index_rows():
    j = lax.broadcasted_iota(jnp.int32, (N_REL_PAD, KG), 1)
    r = lax.broadcasted_iota(jnp.int32, (N_REL_PAD, KG), 0)
    off = jnp.where(j < KG - 2 * CHUNK, j, j - KG)
    idx = jnp.clip(N_PREV * CHUNK - off, -REL_CLIP, REL_CLIP) + REL_CLIP
    return (idx == r).astype(f32)


def _bias_expand(table_pad, layer):
    def body(t_ref, o_ref, row_scr):
        h = pl.program_id(0)

        @pl.when(h == 0)
        def _():
            row_scr[...] = jnp.dot(t_ref[...], _rel_index_rows(), precision=lax.Precision.HIGHEST,
                                   preferred_element_type=f32)

        q = lax.broadcasted_iota(jnp.int32, (QG, KG), 0)
        k = lax.broadcasted_iota(jnp.int32, (QG, KG), 1)
        band = (k // CHUNK >= q // CHUNK) & (k // CHUNK <= q // CHUNK + N_PREV)
        t = jnp.broadcast_to(row_scr[pl.ds(h, 1), :], (QG, KG))
        for b in range(8):
            t = jnp.where(((q >> b) & 1) == 1, pltpu.roll(t, 1 << b, axis=1), t)
        for v in range(3):
            o_ref[v] = jnp.where(band & (k >= (2 - v) * QG), t, NEG)

    return _pcall(body, name=f"bias_expand_{layer}", grid=(N_HEADS,),
                  out_shape=jax.ShapeDtypeStruct((3, N_HEADS, QG, KG), f32),
                  in_specs=[pl.BlockSpec((N_HEADS, N_REL_PAD), lambda h: (0, 0))],
                  out_specs=pl.BlockSpec((3, None, QG, KG), lambda h: (0, h, 0, 0)),
                  scratch_shapes=[pltpu.VMEM((N_HEADS, KG), f32)], compiler_params=_params("arbitrary"))(table_pad)


def _bias_reduce(dbias, layer):
    def body(d_ref, o_ref, row_scr):
        q = lax.broadcasted_iota(jnp.int32, (QG, DB_COLS), 0)
        k = lax.broadcasted_iota(jnp.int32, (QG, DB_COLS), 1)
        for h in range(N_HEADS):
            t = jnp.where(k > q, d_ref[h], 0.0)
            for b in range(8):
                t = jnp.where(((q >> b) & 1) == 1, pltpu.roll(t, DB_COLS - (1 << b), axis=1), t)
            row_scr[h:h + 1, :] = jnp.sum(t, axis=0, keepdims=True)
        r = lax.broadcasted_iota(jnp.int32, (N_REL_PAD, DB_COLS), 0)
        off = lax.broadcasted_iota(jnp.int32, (N_REL_PAD, DB_COLS), 1)
        own = (off >= 1) & (off < REL_CLIP + CHUNK)
        sel = jnp.where(own & (r == 2 * REL_CLIP - off), 1.0, 0.0) - jnp.where(own & (r == 2 * REL_CLIP), 1.0, 0.0)
        o_ref[...] = lax.dot_general(row_scr[...], sel, NT, precision=lax.Precision.HIGHEST, preferred_element_type=f32)

    return _pcall(body, name=f"bias_reduce_{layer}", out_shape=jax.ShapeDtypeStruct((N_HEADS, N_REL_PAD), f32),
                  scratch_shapes=[pltpu.VMEM((N_HEADS, DB_COLS), f32)],
                  compiler_params=pltpu.CompilerParams(vmem_limit_bytes=VMEM_LIMIT))(dbias)


FWD_PAIRS = 8
BWD_PAIRS = 4


def _key_specs(n_groups, npairs, slab):
    per_slab = E_MIX // (128 * npairs)
    return [pl.BlockSpec((QG, 128 * npairs), functools.partial(
        lambda hp, g, jj: (jnp.clip(g - 2 + jj, 0, n_groups - 1), slab * per_slab + hp), jj=jj)) for jj in range(3)]


def _bias_spec(npairs):
    return pl.BlockSpec((None, 2 * npairs, QG, KG), lambda hp, g: (jnp.minimum(g, 2), hp, 0, 0))


def _attn_fwd(h, bias, layer):
    T = h.shape[0]
    n_groups = T // QG
    scale = 1.0 / math.sqrt(HEAD_DIM)

    def body(q_ref, k0, k1, k2, v0, v1, v2, b_ref, o_ref, lse_ref):
        lane = lax.broadcasted_iota(jnp.int32, (1, 128), 1)
        ones = jnp.ones((KG, 128), bf16)
        lse = jnp.zeros((QG, 128), f32)
        for pp in range(FWD_PAIRS):
            cs = slice(pp * 128, (pp + 1) * 128)
            q2 = q_ref[:, cs] * scale
            kc = jnp.concatenate([k0[:, cs], k1[:, cs], k2[:, cs]], axis=0)
            vc = jnp.concatenate([jnp.concatenate([v0[:, cs], v1[:, cs], v2[:, cs]], axis=0), ones], axis=1)
            outs = []
            for hh in range(2):
                qm = jnp.where(lane // HEAD_DIM == hh, q2, jnp.zeros_like(q2))
                s = lax.dot_general(qm, kc, NT, preferred_element_type=f32) + b_ref[2 * pp + hh]
                m = jnp.max(s, axis=1, keepdims=True)
                ol = jnp.dot(jnp.exp(s - m).astype(bf16), vc, preferred_element_type=f32)
                outs.append(ol[:, :128] / ol[:, 128:])
                lse = jnp.where(lane == 2 * pp + hh, m + jnp.log(ol[:, 128:]), lse)
            o_ref[:, cs] = jnp.where(lane // HEAD_DIM == 0, outs[0], outs[1]).astype(bf16)
        lse_ref[...] = lse

    return _pcall(
        body, name=f"attn_fwd_{layer}", grid=(N_HEADS // (2 * FWD_PAIRS), n_groups),
        out_shape=(jax.ShapeDtypeStruct((T, E_MIX), bf16), jax.ShapeDtypeStruct((T, 128), f32)),
        in_specs=[pl.BlockSpec((QG, 128 * FWD_PAIRS), lambda hp, g: (g, hp))] + _key_specs(n_groups, FWD_PAIRS, 1)
        + _key_specs(n_groups, FWD_PAIRS, 2) + [_bias_spec(FWD_PAIRS)],
        out_specs=(pl.BlockSpec((QG, 128 * FWD_PAIRS), lambda hp, g: (g, hp)), pl.BlockSpec((QG, 128), lambda hp, g: (g, 0))),
        compiler_params=_params("arbitrary", "arbitrary"))(h, h, h, h, h, h, h, bias)


def _halo_rows(ref, r):
    return ref[r:r + 1, :].astype(f32)


def _conv_taps(cu, p6, p7, w_ref):
    row = lax.broadcasted_iota(jnp.int32, cu.shape, 0)
    r1 = jnp.where(row == 0, p7, pltpu.roll(cu, 1, axis=0))
    r2 = jnp.where(row == 0, p6, jnp.where(row == 1, p7, pltpu.roll(cu, 2, axis=0)))
    return w_ref[2:3, :] * cu + w_ref[1:2, :] * r1 + w_ref[0:1, :] * r2, r1, r2


def _conv_fwd(h, w, layer):
    T = h.shape[0]
    tm = 512

    def body(bg_ref, cg_ref, u_ref, cgp_ref, up_ref, w_ref, o_ref):
        first = (pl.program_id(0) == 0).astype(f32)
        cu = cg_ref[...].astype(f32) * u_ref[...].astype(f32)
        p6 = _halo_rows(cgp_ref, 14) * _halo_rows(up_ref, 14) * (1.0 - first)
        p7 = _halo_rows(cgp_ref, 15) * _halo_rows(up_ref, 15) * (1.0 - first)
        conv, _, _ = _conv_taps(cu, p6, p7, w_ref)
        o_ref[...] = (bg_ref[...].astype(f32) * conv).astype(bf16)

    prev = lambda slab: pl.BlockSpec((16, E_MIX), lambda i: (jnp.maximum(i * (tm // 16) - 1, 0), slab))
    return _pcall(
        body, name=f"conv_fwd_{layer}", grid=(T // tm,), out_shape=jax.ShapeDtypeStruct((T, E_MIX), bf16),
        in_specs=[pl.BlockSpec((tm, E_MIX), lambda i: (i, 0)), pl.BlockSpec((tm, E_MIX), lambda i: (i, 1)),
                  pl.BlockSpec((tm, E_MIX), lambda i: (i, 2)), prev(1), prev(2),
                  pl.BlockSpec((CONV_W, E_MIX), lambda i: (0, 0))],
        out_specs=pl.BlockSpec((tm, E_MIX), lambda i: (i, 0)),
        compiler_params=_params("arbitrary"))(h, h, h, h, h, w)


def _kv_mem(memb, wkv):
    def body(m_ref, w_ref, o_ref):
        o_ref[...] = jnp.dot(m_ref[...], w_ref[...], preferred_element_type=f32).astype(bf16)

    return _pcall(body, name="kv_mem", out_shape=jax.ShapeDtypeStruct((N_MEM, 2 * E_MEM), bf16),
                  compiler_params=pltpu.CompilerParams(vmem_limit_bytes=VMEM_LIMIT))(memb, wkv)


def _mem_probs(qm_ref, kv_ref, hh):
    qh = qm_ref[:, hh * MEM_HEAD_DIM:(hh + 1) * MEM_HEAD_DIM]
    kh = kv_ref[:, hh * MEM_HEAD_DIM:(hh + 1) * MEM_HEAD_DIM]
    vh = kv_ref[:, E_MEM + hh * MEM_HEAD_DIM:E_MEM + (hh + 1) * MEM_HEAD_DIM]
    s = lax.dot_general(qh, kh, NT, preferred_element_type=f32) * (1.0 / math.sqrt(MEM_HEAD_DIM))
    e = jnp.exp(s - jnp.max(s, axis=1, keepdims=True))
    return e / jnp.sum(e, axis=1, keepdims=True), qh, kh, vh


def _h_tail_specs(tm):
    return [pl.BlockSpec((tm, E_MEM), functools.partial(lambda i, cb: (i, cb), cb=cb)) for cb in (6, 7, 8, 9)]


def _post_fwd(h, mix, kv, wout, x, g, b, layer):
    T = x.shape[0]
    tm = 512

    def body(qm_ref, z0, z1, z2, mix_ref, kv_ref, w_ref, x_ref, g_ref, b_ref, xn_ref, xb_ref, xh_ref, rs_ref):
        mem = jnp.concatenate(
            [jnp.dot(_mem_probs(qm_ref, kv_ref, hh)[0].astype(bf16), kv_ref[:, E_MEM + hh * MEM_HEAD_DIM:E_MEM + (hh + 1) * MEM_HEAD_DIM],
                     preferred_element_type=f32) for hh in range(MEM_HEADS)], axis=1)
        z = jnp.concatenate([z0[...], z1[...], z2[...]], axis=1).astype(f32)
        act, _ = _silu_parts(z)
        y = jnp.concatenate([mix_ref[...].astype(f32), mem], axis=1) * act
        out = jnp.dot(y.astype(bf16), w_ref[...], preferred_element_type=f32)
        r = DN_ALPHA * x_ref[...] + out
        mu = jnp.mean(r, axis=1, keepdims=True)
        var = jnp.mean(jnp.square(r - mu), axis=1, keepdims=True)
        rstd = lax.rsqrt(var + LN_EPS)
        xhat = (r - mu) * rstd
        xn = xhat * g_ref[...] + b_ref[...]
        xn_ref[...] = xn
        xb_ref[...] = xn.astype(bf16)
        xh_ref[...] = xhat
        rs_ref[...] = rstd

    tile = lambda w: pl.BlockSpec((tm, w), lambda i: (i, 0))
    const = lambda r, c: pl.BlockSpec((r, c), lambda i: (0, 0))
    return _pcall(
        body, name=f"post_fwd_{layer}", grid=(T // tm,),
        out_shape=(jax.ShapeDtypeStruct((T, D_MODEL), f32), jax.ShapeDtypeStruct((T, D_MODEL), bf16),
                   jax.ShapeDtypeStruct((T, D_MODEL), f32), jax.ShapeDtypeStruct((T, 1), f32)),
        in_specs=_h_tail_specs(tm) + [tile(E_MIX), const(N_MEM, 2 * E_MEM), const(E_BRANCH, D_MODEL), tile(D_MODEL),
                                      const(1, D_MODEL), const(1, D_MODEL)],
        out_specs=(tile(D_MODEL), tile(D_MODEL), tile(D_MODEL), tile(1)),
        compiler_params=_params("arbitrary"))(h, h, h, h, mix, kv, wout, x, g, b)


def _loss_head(y, target):
    T = y.shape[0]
    tm = 512

    def body(y_ref, t_ref, l_ref, d_ref):
        @pl.when(pl.program_id(0) == 0)
        def _():
            l_ref[...] = jnp.zeros_like(l_ref)
        err = y_ref[...] - t_ref[...]
        d_ref[...] = err * (1.0 / D_MODEL)
        l_ref[...] += jnp.sum(jnp.square(err))

    return _pcall(
        body, name="loss_head", grid=(T // tm,),
        out_shape=(jax.ShapeDtypeStruct((8, 128), f32), jax.ShapeDtypeStruct((T, D_MODEL), f32)),
        in_specs=[pl.BlockSpec((tm, D_MODEL), lambda i: (i, 0))] * 2,
        out_specs=(pl.BlockSpec((8, 128), lambda i: (0, 0)), pl.BlockSpec((tm, D_MODEL), lambda i: (i, 0))),
        compiler_params=_params("arbitrary"))(y, target)


def _post_bwd(dxn, xhat, rstd, g, h, mix, kv, wout, layer, stacked):
    T = dxn.shape[0]
    tm = 256
    n_tiles = T // tm
    inv = 1.0 / math.sqrt(MEM_HEAD_DIM)

    def body(dxn_ref, xh_ref, rs_ref, g_ref, qm_ref, z0, z1, z2, mix_ref, kv_ref, w_ref, *rest):
        dxp_ref, dhb_ref, dmix_ref, dkv_ref, dwo_out, dg_ref, db_ref, dwo_ref = rest[-8:]

        @pl.when(pl.program_id(0) == 0)
        def _():
            dkv_ref[...] = jnp.zeros_like(dkv_ref)
            dwo_ref[...] = jnp.zeros_like(dwo_ref)
            dg_ref[...] = jnp.zeros_like(dg_ref)
            db_ref[...] = jnp.zeros_like(db_ref)

        dy, xh = dxn_ref[...], xh_ref[...]
        dg_ref[0:1, :] += jnp.sum(dy * xh, axis=0, keepdims=True)
        db_ref[0:1, :] += jnp.sum(dy, axis=0, keepdims=True)
        gx = dy * g_ref[...]
        dr = rs_ref[...] * (gx - jnp.mean(gx, axis=1, keepdims=True) - xh * jnp.mean(gx * xh, axis=1, keepdims=True))
        dxp_ref[...] = DN_ALPHA * dr
        dob = dr.astype(bf16)

        probs = [_mem_probs(qm_ref, kv_ref, hh) for hh in range(MEM_HEADS)]
        mem = jnp.concatenate([jnp.dot(p.astype(bf16), vh, preferred_element_type=f32) for p, _, _, vh in probs], axis=1)
        z = jnp.concatenate([z0[...], z1[...], z2[...]], axis=1).astype(f32)
        act, sig = _silu_parts(z)
        cat = jnp.concatenate([mix_ref[...].astype(f32), mem], axis=1)
        yb = (cat * act).astype(bf16)
        dwo_ref[...] += lax.dot_general(yb, dob, TN, preferred_element_type=f32)
        dyv = lax.dot_general(dob, w_ref[...], NT, preferred_element_type=f32)
        dz = dyv * cat * (sig * (1.0 + z * (1.0 - sig)))
        dcat = dyv * act
        dmix_ref[...] = dcat[:, :E_MIX].astype(bf16)
        dqs = []
        for hh, (p, qh, kh, vh) in enumerate(probs):
            dmem = dcat[:, E_MIX + hh * MEM_HEAD_DIM:E_MIX + (hh + 1) * MEM_HEAD_DIM].astype(bf16)
            dp = lax.dot_general(dmem, vh, NT, preferred_element_type=f32)
            ds = (p * (dp - jnp.sum(p * dp, axis=1, keepdims=True))).astype(bf16)
            dqs.append(jnp.dot(ds, kh, preferred_element_type=f32) * inv)
            dkv_ref[:, hh * MEM_HEAD_DIM:(hh + 1) * MEM_HEAD_DIM] += lax.dot_general(ds, qh, TN, preferred_element_type=f32) * inv
            dkv_ref[:, E_MEM + hh * MEM_HEAD_DIM:E_MEM + (hh + 1) * MEM_HEAD_DIM] += lax.dot_general(
                p.astype(bf16), dmem, TN, preferred_element_type=f32)
        dhb_ref[...] = jnp.concatenate(dqs + [dz], axis=1).astype(bf16)

        @pl.when(pl.program_id(0) == n_tiles - 1)
        def _():
            dwo_out[...] = dwo_ref[...].astype(bf16)

    tile = lambda w: pl.BlockSpec((tm, w), lambda i: (i, 0))
    const = lambda r, c: pl.BlockSpec((r, c), lambda i: (0, 0))
    carried = [] if stacked is None else [stacked]
    return _pcall(
        body, name=f"post_bwd_{layer}", grid=(n_tiles,),
        out_shape=(jax.ShapeDtypeStruct((T, D_MODEL), f32), jax.ShapeDtypeStruct((T, E_MEM + E_BRANCH), bf16),
                   jax.ShapeDtypeStruct((T, E_MIX), bf16), jax.ShapeDtypeStruct((N_MEM, 2 * E_MEM), f32),
                   jax.ShapeDtypeStruct((DEPTH, E_BRANCH, D_MODEL), bf16), jax.ShapeDtypeStruct((8, D_MODEL), f32),
                   jax.ShapeDtypeStruct((8, D_MODEL), f32)),
        in_specs=[tile(D_MODEL), tile(D_MODEL), tile(1), const(1, D_MODEL)] + _h_tail_specs(tm)
        + [tile(E_MIX), const(N_MEM, 2 * E_MEM), const(E_BRANCH, D_MODEL)] + [ANY] * len(carried),
        out_specs=(tile(D_MODEL), tile(E_MEM + E_BRANCH), tile(E_MIX), const(N_MEM, 2 * E_MEM),
                   pl.BlockSpec((None, E_BRANCH, D_MODEL), lambda i: (layer, 0, 0)), const(8, D_MODEL), const(8, D_MODEL)),
        input_output_aliases={11: 4} if carried else {},
        scratch_shapes=[pltpu.VMEM((E_BRANCH, D_MODEL), f32)],
        compiler_params=_params("arbitrary"))(dxn, xhat, rstd, g, h, h, h, h, mix, kv, wout, *carried)


def _attn_bwd(h, bias, dmix, lse, layer):
    T = h.shape[0]
    n_groups = T // QG
    scale = 1.0 / math.sqrt(HEAD_DIM)

    def body(q_ref, k0, k1, k2, v0, v1, v2, do_ref, b_ref, lse_ref, dq_ref, dk_ref, dv_ref, db_ref, acck, accv):
        g = pl.program_id(1)

        @pl.when(g == 0)
        def _():
            acck[...] = jnp.zeros_like(acck)
            accv[...] = jnp.zeros_like(accv)
            db_ref[...] = jnp.zeros_like(db_ref)

        @pl.when(g < n_groups)
        def _():
            lane = lax.broadcasted_iota(jnp.int32, (1, 128), 1)
            first = lane // HEAD_DIM == 0
            for pp in range(BWD_PAIRS):
                cs = slice(pp * 128, (pp + 1) * 128)
                do2 = do_ref[:, cs]
                q2 = q_ref[:, cs] * scale
                kc = jnp.concatenate([k0[:, cs], k1[:, cs], k2[:, cs]], axis=0)
                vc = jnp.concatenate([v0[:, cs], v1[:, cs], v2[:, cs]], axis=0)
                dqs, dks, dvs = [], [], []
                for hh in range(2):
                    hm = lane // HEAD_DIM == hh
                    head = (pl.program_id(0) * BWD_PAIRS + pp) * 2 + hh
                    lse = jnp.sum(jnp.where(lane == head, lse_ref[...], 0.0), axis=1, keepdims=True)
                    qm = jnp.where(hm, q2, jnp.zeros_like(q2))
                    dom = jnp.where(hm, do2, jnp.zeros_like(do2))
                    s = lax.dot_general(qm, kc, NT, preferred_element_type=f32) + b_ref[2 * pp + hh]
                    p = jnp.exp(s - lse)
                    dp = lax.dot_general(dom, vc, NT, preferred_element_type=f32)
                    ds = p * (dp - jnp.sum(p * dp, axis=1, keepdims=True))
                    db_ref[2 * pp + hh] += ds[:, KG - DB_COLS:]
                    dsb, pb = ds.astype(bf16), p.astype(bf16)
                    dqs.append(jnp.dot(dsb, kc, preferred_element_type=f32) * scale)
                    dks.append(lax.dot_general(dsb, q2, TN, preferred_element_type=f32))
                    dvs.append(lax.dot_general(pb, do2, TN, preferred_element_type=f32))
                dq_ref[:, cs] = jnp.where(first, dqs[0], dqs[1]).astype(bf16)
                dkc = jnp.where(first, dks[0], dks[1])
                dvc = jnp.where(first, dvs[0], dvs[1])
                for jj in range(3):
                    slot = (g + 1 + jj) % 3
                    if jj == 2:
                        acck[slot, :, cs] = dkc[jj * QG:(jj + 1) * QG]
                        accv[slot, :, cs] = dvc[jj * QG:(jj + 1) * QG]
                    else:
                        acck[slot, :, cs] += dkc[jj * QG:(jj + 1) * QG]
                        accv[slot, :, cs] += dvc[jj * QG:(jj + 1) * QG]

        done = (g + 1) % 3
        dk_ref[...] = acck[done].astype(bf16)
        dv_ref[...] = accv[done].astype(bf16)

    last = n_groups - 1
    width = 128 * BWD_PAIRS
    qspec = pl.BlockSpec((QG, width), lambda hp, g: (jnp.minimum(g, last), hp))
    kout = pl.BlockSpec((QG, width), lambda hp, g: (jnp.clip(g - 2, 0, last), hp))
    dbspec = pl.BlockSpec((2 * BWD_PAIRS, QG, DB_COLS), lambda hp, g: (hp, 0, 0))
    lspec = pl.BlockSpec((QG, 128), lambda hp, g: (jnp.minimum(g, last), 0))
    return _pcall(
        body, name=f"attn_bwd_{layer}", grid=(N_HEADS // (2 * BWD_PAIRS), n_groups + 2),
        out_shape=(jax.ShapeDtypeStruct((T, E_MIX), bf16),) * 3 + (jax.ShapeDtypeStruct((N_HEADS, QG, DB_COLS), f32),),
        in_specs=[qspec] + _key_specs(n_groups, BWD_PAIRS, 1) + _key_specs(n_groups, BWD_PAIRS, 2)
        + [qspec, _bias_spec(BWD_PAIRS), lspec],
        out_specs=(qspec, kout, kout, dbspec),
        scratch_shapes=[pltpu.VMEM((3, QG, width), f32), pltpu.VMEM((3, QG, width), f32)],
        compiler_params=_params("arbitrary", "arbitrary"))(h, h, h, h, h, h, h, dmix, bias, lse)


def _conv_bwd(h, w, dmix, layer):
    T = h.shape[0]
    tm = 512
    n_tiles = T // tm

    def body(bg_ref, cg_ref, u_ref, cgp_ref, up_ref, dy_ref, bgn_ref, dyn_ref, w_ref, dbg_ref, dcg_ref, du_ref, dw_ref):
        i = pl.program_id(0)

        @pl.when(i == 0)
        def _():
            dw_ref[...] = jnp.zeros_like(dw_ref)

        first = (i == 0).astype(f32)
        final = (i == n_tiles - 1).astype(f32)
        bg, cg, u = bg_ref[...].astype(f32), cg_ref[...].astype(f32), u_ref[...].astype(f32)
        dy = dy_ref[...].astype(f32)
        cu = cg * u
        p6 = _halo_rows(cgp_ref, 14) * _halo_rows(up_ref, 14) * (1.0 - first)
        p7 = _halo_rows(cgp_ref, 15) * _halo_rows(up_ref, 15) * (1.0 - first)
        conv, r1, r2 = _conv_taps(cu, p6, p7, w_ref)
        dbg_ref[...] = (dy * conv).astype(bf16)
        dc = dy * bg
        n0 = _halo_rows(dyn_ref, 0) * _halo_rows(bgn_ref, 0) * (1.0 - final)
        n1 = _halo_rows(dyn_ref, 1) * _halo_rows(bgn_ref, 1) * (1.0 - final)
        row = lax.broadcasted_iota(jnp.int32, dc.shape, 0)
        f1 = jnp.where(row == tm - 1, n0, pltpu.roll(dc, tm - 1, axis=0))
        f2 = jnp.where(row == tm - 2, n0, jnp.where(row == tm - 1, n1, pltpu.roll(dc, tm - 2, axis=0)))
        dcu = w_ref[2:3, :] * dc + w_ref[1:2, :] * f1 + w_ref[0:1, :] * f2
        dcg_ref[...] = (dcu * u).astype(bf16)
        du_ref[...] = (dcu * cg).astype(bf16)
        dw_ref[0:1, :] += jnp.sum(dc * r2, axis=0, keepdims=True)
        dw_ref[1:2, :] += jnp.sum(dc * r1, axis=0, keepdims=True)
        dw_ref[2:3, :] += jnp.sum(dc * cu, axis=0, keepdims=True)

    tile = lambda slab: pl.BlockSpec((tm, E_MIX), lambda i: (i, slab))
    prev = lambda slab: pl.BlockSpec((16, E_MIX), lambda i: (jnp.maximum(i * (tm // 16) - 1, 0), slab))
    nxt = lambda slab: pl.BlockSpec((16, E_MIX), lambda i: (jnp.minimum((i + 1) * (tm // 16), T // 16 - 1), slab))
    return _pcall(
        body, name=f"conv_bwd_{layer}", grid=(n_tiles,),
        out_shape=(jax.ShapeDtypeStruct((T, E_MIX), bf16),) * 3 + (jax.ShapeDtypeStruct((8, E_MIX), f32),),
        in_specs=[tile(0), tile(1), tile(2), prev(1), prev(2), tile(0), nxt(0), nxt(0),
                  pl.BlockSpec((CONV_W, E_MIX), lambda i: (0, 0))],
        out_specs=(tile(0), tile(0), tile(0), pl.BlockSpec((8, E_MIX), lambda i: (0, 0))),
        compiler_params=_params("arbitrary"))(h, h, h, h, h, dmix, h, dmix, w)


def _inproj_bwd_dx(da, db, dc, dhb, dxp, win_g, layer):
    T = dxp.shape[0]
    tm = 512

    def body(da_ref, db_ref, dc_ref, dhb_ref, dxp_ref, w_ref, o_ref):
        dh = jnp.concatenate([da_ref[...], db_ref[...], dc_ref[...], dhb_ref[...]], axis=1)
        acc = dxp_ref[...]
        for j in range(N_CHIPS):
            acc = acc + lax.dot_general(dh[:, j * W_IN_COLS:(j + 1) * W_IN_COLS], w_ref[j], NT, preferred_element_type=f32)
        o_ref[...] = acc

    tile = lambda w: pl.BlockSpec((tm, w), lambda i: (i, 0))
    return _pcall(
        body, name=f"inproj_bwd_dx_{layer}", grid=(T // tm,), out_shape=jax.ShapeDtypeStruct((T, D_MODEL), f32),
        in_specs=[tile(E_MIX), tile(E_MIX), tile(E_MIX), tile(E_MEM + E_BRANCH), tile(D_MODEL),
                  pl.BlockSpec((None, N_CHIPS, D_MODEL, W_IN_COLS), lambda i: (layer, 0, 0, 0), pipeline_mode=pl.Buffered(1))],
        out_specs=tile(D_MODEL),
        compiler_params=_params("arbitrary"))(da, db, dc, dhb, dxp, win_g)


def _inproj_bwd_dw(da, db, dc, dhb, xb, layer, stacked):
    T = xb.shape[0]
    tm = 512
    n_tiles = T // tm

    def body(da_ref, db_ref, dc_ref, dhb_ref, x_ref, *rest):
        o_ref, acc, stage, sem = rest[-4:]
        i = pl.program_id(0)

        @pl.when(i == 0)
        def _():
            acc[...] = jnp.zeros_like(acc)

        dh = jnp.concatenate([da_ref[...], db_ref[...], dc_ref[...], dhb_ref[...]], axis=1)
        xt = x_ref[...]
        for j in range(N_CHIPS):
            acc[j] += lax.dot_general(xt, dh[:, j * W_IN_COLS:(j + 1) * W_IN_COLS], TN, preferred_element_type=f32)

        @pl.when(i == n_tiles - 1)
        def _():
            for j in range(N_CHIPS):
                stage[...] = acc[j].astype(bf16)
                cp = pltpu.make_async_copy(stage, o_ref.at[layer, j], sem)
                cp.start()
                cp.wait()

    tile = lambda w: pl.BlockSpec((tm, w), lambda i: (i, 0))
    carried = [] if stacked is None else [stacked]
    return _pcall(
        body, name=f"inproj_bwd_dw_{layer}", grid=(n_tiles,),
        out_shape=jax.ShapeDtypeStruct((DEPTH, N_CHIPS, D_MODEL, W_IN_COLS), bf16),
        in_specs=[tile(E_MIX), tile(E_MIX), tile(E_MIX), tile(E_MEM + E_BRANCH), tile(D_MODEL)] + [ANY] * len(carried),
        out_specs=ANY, input_output_aliases={5: 0} if carried else {},
        scratch_shapes=[pltpu.VMEM((N_CHIPS, D_MODEL, W_IN_COLS), f32), pltpu.VMEM((D_MODEL, W_IN_COLS), bf16),
                        pltpu.SemaphoreType.DMA],
        compiler_params=_params("arbitrary"))(da, db, dc, dhb, xb, *carried)


def _kv_mem_bwd(memb, dkv, layer, stacked):
    def body(m_ref, d_ref, *rest):
        o_ref = rest[-1]
        o_ref[...] = lax.dot_general(m_ref[...], d_ref[...].astype(bf16), TN, preferred_element_type=f32).astype(bf16)

    carried = [] if stacked is None else [stacked]
    return _pcall(body, name=f"kv_mem_bwd_{layer}", grid=(1,),
                  out_shape=jax.ShapeDtypeStruct((DEPTH, D_MODEL, 2 * E_MEM), bf16),
                  in_specs=[pl.BlockSpec((N_MEM, D_MODEL), lambda i: (0, 0)), pl.BlockSpec((N_MEM, 2 * E_MEM), lambda i: (0, 0))]
                  + [ANY] * len(carried),
                  out_specs=pl.BlockSpec((None, D_MODEL, 2 * E_MEM), lambda i: (layer, 0, 0)),
                  input_output_aliases={2: 0} if carried else {},
                  compiler_params=_params("arbitrary"))(memb, dkv, *carried)


def _to_bf16(a, name):
    rows, cols = a.shape
    br = 512

    def body(a_ref, o_ref):
        o_ref[...] = a_ref[...].astype(bf16)

    return _pcall(body, name=name, grid=(rows // br,), out_shape=jax.ShapeDtypeStruct((rows, cols), bf16),
                  in_specs=[pl.BlockSpec((br, cols), lambda i: (i, 0))], out_specs=pl.BlockSpec((br, cols), lambda i: (i, 0)),
                  compiler_params=_params("arbitrary"))(a)


def _adamw(w, g, m, v, name):
    shape = w.shape
    cols = shape[-1]
    rows = w.size // cols
    args = [a.reshape(rows, cols) for a in (w, g, m, v)]
    br = 256 if rows % 256 == 0 and rows > 256 else rows

    def body(w_ref, g_ref, m_ref, v_ref, d_ref, nm_ref, nv_ref):
        gg = g_ref[...]
        nm = ADAM_B1 * m_ref[...] + (1.0 - ADAM_B1) * gg
        nv = ADAM_B2 * v_ref[...] + (1.0 - ADAM_B2) * jnp.square(gg)
        m_hat = nm / (1.0 - ADAM_B1 ** ADAM_STEP)
        v_hat = nv / (1.0 - ADAM_B2 ** ADAM_STEP)
        d_ref[...] = -ADAM_LR * (m_hat / (jnp.sqrt(v_hat) + ADAM_EPS) + ADAM_WD * w_ref[...])
        nm_ref[...] = nm
        nv_ref[...] = nv

    spec = pl.BlockSpec((br, cols), lambda i: (i, 0))
    outs = _pcall(body, name=name, grid=(rows // br,), out_shape=(jax.ShapeDtypeStruct((rows, cols), f32),) * 3,
                  in_specs=[spec] * 4, out_specs=(spec,) * 3, compiler_params=_params("arbitrary"))(*args)
    return tuple(o.reshape(shape) for o in outs)


def kernel(x, mem, w_in, w_mem_kv, w_out, rel_bias, conv_w, ln_g, ln_b, loss_target, m_w_in, m_w_mem_kv, m_w_out, m_rel_bias, m_conv_w, m_ln_g, m_ln_b, v_w_in, v_w_mem_kv, v_w_out, v_rel_bias, v_conv_w, v_ln_g, v_ln_b):
    T = x.shape[1]
    x0 = x.reshape(T, D_MODEL)
    target = loss_target.reshape(T, D_MODEL)
    memb = mem.reshape(N_MEM, D_MODEL).astype(bf16)
    chip = 2 * lax.axis_index("x") + lax.axis_index("y")
    core = lax.axis_index("c")
    chip_arr = jnp.reshape(chip, (1,)).astype(jnp.int32)
    core_arr = jnp.reshape(core, (1,)).astype(jnp.int32)

    win_g, wkv_g, wout_g, cw_g = _gather_weights(w_in.astype(bf16), w_mem_kv.astype(bf16), w_out.astype(bf16), conv_w)
    wkv_full = wkv_g.reshape(DEPTH, D_MODEL, 2 * E_MEM)
    wout_full = wout_g.reshape(DEPTH, E_BRANCH, D_MODEL)
    conv_full = jnp.transpose(cw_g, (1, 2, 0, 3)).reshape(DEPTH // 2, CONV_W, E_MIX)
    tables = jnp.pad(rel_bias, ((0, 0), (0, 0), (0, N_REL_PAD - N_REL)))

    xs, xbs, hs, mixes, kvs, xhats, rstds, biases, lses = [x0], [_to_bf16(x0, "cast_x")], [], [], [], [], [], {}, {}
    for layer in range(DEPTH):
        h = _inproj(xbs[layer], win_g, layer)
        if layer % 2 == 0:
            biases[layer] = _bias_expand(tables[layer // 2], layer)
            mix, lses[layer] = _attn_fwd(h, biases[layer], layer)
        else:
            mix = _conv_fwd(h, conv_full[layer // 2], layer)
        kv = _kv_mem(memb, wkv_full[layer])
        xn, xnb, xhat, rstd = _post_fwd(h, mix, kv, wout_full[layer], xs[layer], ln_g[layer][None, :], ln_b[layer][None, :], layer)
        xs.append(xn); xbs.append(xnb); hs.append(h); mixes.append(mix); kvs.append(kv); xhats.append(xhat); rstds.append(rstd)

    lsum, dx = _loss_head(xs[DEPTH], target)
    loss = lax.psum(lsum[0, 0], ("x", "y", "c")) * (0.5 / D_MODEL)

    g_win = g_wkv = g_wout = None
    dgs, dbs, dconvs, dtables = [None] * DEPTH, [None] * DEPTH, [None] * (DEPTH // 2), [None] * ((DEPTH + 1) // 2)
    for layer in reversed(range(DEPTH)):
        h = hs[layer]
        dxp, dhb, dmix, dkv, g_wout, dgs[layer], dbs[layer] = _post_bwd(
            dx, xhats[layer], rstds[layer], ln_g[layer][None, :], h, mixes[layer], kvs[layer], wout_full[layer], layer, g_wout)
        if layer % 2 == 0:
            da, db, dc, dbias = _attn_bwd(h, biases[layer], dmix, lses[layer], layer)
            dtables[layer // 2] = _bias_reduce(dbias, layer)
        else:
            da, db, dc, dconvs[layer // 2] = _conv_bwd(h, conv_full[layer // 2], dmix, layer)
        dx = _inproj_bwd_dx(da, db, dc, dhb, dxp, win_g, layer)
        g_win = _inproj_bwd_dw(da, db, dc, dhb, xbs[layer], layer, g_win)
        g_wkv = _kv_mem_bwd(memb, dkv, layer, g_wkv)
    grad_x = dx.reshape(1, T, D_MODEL)

    gs = [g_win, g_wkv.reshape(DEPTH, N_CHIPS, W_KV_ROWS, 2 * E_MEM), g_wout.reshape(DEPTH, N_CHIPS, W_OUT_ROWS, D_MODEL)]
    rs = _sibling_exchange(gs)
    ss = [_pair_sum(g, r, core_arr) for g, r in zip(gs, rs)]
    rb = _chip_scatter(ss)
    place = jnp.concatenate([chip_arr, core_arr])
    fs = [_chip_sum(s, r, place) for s, r in zip(ss, rb)]
    grad_w_in, grad_w_mem_kv, grad_w_out = _sibling_share(fs)

    pad8 = lambda a: jnp.pad(a, ((0, 8 - a.shape[0]), (0, 0)))
    parts = dgs + dbs + dconvs + [pad8(t.reshape(-1, D_MODEL)) for t in dtables]
    small = _small_allreduce(jnp.concatenate(parts, axis=0))
    grad_ln_g = jnp.stack([small[8 * l] for l in range(DEPTH)])
    grad_ln_b = jnp.stack([small[8 * (DEPTH + l)] for l in range(DEPTH)])
    conv_all = jnp.stack([small[8 * (2 * DEPTH + a):8 * (2 * DEPTH + a) + CONV_W] for a in range(DEPTH // 2)])
    grad_conv_w = lax.dynamic_slice_in_dim(conv_all, chip * (E_MIX // N_CHIPS), E_MIX // N_CHIPS, axis=2)
    t0 = 8 * (2 * DEPTH + DEPTH // 2)
    grad_rel_bias = jnp.stack([small[t0 + 8 * a:t0 + 8 * a + 6].reshape(N_HEADS, N_REL_PAD)[:, :N_REL]
                               for a in range((DEPTH + 1) // 2)])

    grads = [grad_w_in, grad_w_mem_kv, grad_w_out, grad_rel_bias, grad_conv_w, grad_ln_g, grad_ln_b]
    weights = [w_in, w_mem_kv, w_out, rel_bias, conv_w, ln_g, ln_b]
    moms = [m_w_in, m_w_mem_kv, m_w_out, m_rel_bias, m_conv_w, m_ln_g, m_ln_b]
    vels = [v_w_in, v_w_mem_kv, v_w_out, v_rel_bias, v_conv_w, v_ln_g, v_ln_b]
    names = ["w_in", "w_mem_kv", "w_out", "rel_bias", "conv_w", "ln_g", "ln_b"]
    upd = [_adamw(w, g, m, v, f"adamw_{n}") for w, g, m, v, n in zip(weights, grads, moms, vels, names)]
    deltas, new_m, new_v = zip(*upd)
    return (loss, grad_x, *grads, *deltas, *new_m, *new_v)
```

```python
import functools
import math

import jax
import jax.numpy as jnp
from jax import lax
from jax.experimental import pallas as pl
from jax.experimental.pallas import tpu as pltpu

f32, bf16 = jnp.float32, jnp.bfloat16

D_MODEL = 1024
DEPTH = 4
CHUNK = 64
N_PREV = 8
N_HEADS = 16
HEAD_DIM = 64
E_MIX = 1024
REL_CLIP = 128
N_REL = 2 * REL_CLIP + 1
N_REL_PAD = 384
CONV_W = 3
N_MEM = 256
MEM_HEADS = 4
MEM_HEAD_DIM = 128
E_MEM = 512
E_BRANCH = E_MIX + E_MEM
N_IN = 3 * E_MIX + E_MEM + E_BRANCH
N_CHIPS = 4
W_IN_COLS = N_IN // N_CHIPS
W_KV_ROWS = D_MODEL // N_CHIPS
W_OUT_ROWS = E_BRANCH // N_CHIPS
DN_ALPHA = (2.0 * DEPTH) ** 0.25
LN_EPS = 1e-5
ADAM_LR, ADAM_B1, ADAM_B2, ADAM_EPS, ADAM_WD, ADAM_STEP = 0.001, 0.9, 0.999, 1e-08, 0.01, 10

QG = 4 * CHUNK
KG = QG + N_PREV * CHUNK
DB_COLS = KG // 2
NEG = -1e30
VMEM_LIMIT = 56 * 1024 * 1024

NT = (((1,), (1,)), ((), ()))
TN = (((0,), (0,)), ((), ()))
MESH = pl.DeviceIdType.MESH
ANY = pl.BlockSpec(memory_space=pl.ANY)


def _pcall(body, **kw):
    return pl.pallas_call(body, **kw)


def _params(*sem):
    return pltpu.CompilerParams(dimension_semantics=sem, vmem_limit_bytes=VMEM_LIMIT)


def _silu_parts(z):
    sig = 1.0 / (1.0 + jnp.exp(-z))
    return z * sig, sig


class _Comm:
    def __init__(self, inputs, out_shapes, aliases, n_sems, copies):
        self.inputs, self.out_shapes, self.aliases, self.n_sems, self.copies = inputs, out_shapes, aliases, n_sems, copies

    def start(self, cin, cout, send, recv):
        for cp in self.copies(cin, cout, send, recv)[0]:
            cp.start()

    def wait(self, cin, cout, send, recv):
        sends, recvs = self.copies(cin, cout, send, recv)
        for cp in recvs:
            cp.wait_recv()
        for cp in sends:
            cp.wait_send()


def _pcall_carry(body, comm, *, n_in, n_out, **kw):
    if comm is None:
        return lambda *args: (_pcall(body, **kw)(*args), ())
    grid = kw["grid"]
    k_in, k_out = len(comm.inputs), len(comm.out_shapes)

    def carried(*refs):
        ins, cin = refs[:n_in], refs[n_in:n_in + k_in]
        outs = refs[n_in + k_in:n_in + k_in + n_out]
        cout = refs[n_in + k_in + n_out:n_in + k_in + n_out + k_out]
        scratch, send, recv = refs[n_in + k_in + n_out + k_out:-2], refs[-2], refs[-1]
        ids = [pl.program_id(a) for a in range(len(grid))]
        first = functools.reduce(jnp.logical_and, [i == 0 for i in ids])
        last = functools.reduce(jnp.logical_and, [i == n - 1 for i, n in zip(ids, grid)])

        @pl.when(first)
        def _():
            comm.start(cin, cout, send, recv)

        body(*ins, *outs, *scratch)

        @pl.when(last)
        def _():
            comm.wait(cin, cout, send, recv)

    kw = dict(kw)
    kw["in_specs"] = list(kw["in_specs"]) + [ANY] * k_in
    kw["out_specs"] = tuple(kw["out_specs"]) + (ANY,) * k_out
    kw["out_shape"] = tuple(kw["out_shape"]) + tuple(comm.out_shapes)
    kw["scratch_shapes"] = list(kw.get("scratch_shapes", ())) + [pltpu.SemaphoreType.DMA((comm.n_sems,))] * 2
    aliases = dict(kw.get("input_output_aliases", {}))
    aliases.update({n_in + ci: n_out + co for ci, co in comm.aliases.items()})
    kw["input_output_aliases"] = aliases

    def run(*args):
        res = _pcall(carried, **kw)(*args, *comm.inputs)
        return res[:n_out], res[n_out:]

    return run


def _comm_call(comm, name):
    k_in = len(comm.inputs)

    def body(*refs):
        cin, cout, send, recv = refs[:k_in], refs[k_in:-2], refs[-2], refs[-1]
        comm.start(cin, cout, send, recv)
        comm.wait(cin, cout, send, recv)

    return _pcall(body, name=name, out_shape=tuple(comm.out_shapes), in_specs=[ANY] * k_in,
                  out_specs=(ANY,) * len(comm.out_shapes), input_output_aliases=dict(comm.aliases),
                  scratch_shapes=[pltpu.SemaphoreType.DMA((comm.n_sems,))] * 2)(*comm.inputs)


def _place():
    x, y, c = lax.axis_index("x"), lax.axis_index("y"), lax.axis_index("c")
    return x, y, c, 2 * x + y, (x, y, 1 - c), [(1 - x, y), (x, 1 - y), (1 - x, 1 - y)]


def _rcopy(send, recv, k, src, dst, to):
    return pltpu.make_async_remote_copy(src_ref=src, dst_ref=dst, send_sem=send.at[k], recv_sem=recv.at[k],
                                        device_id=to, device_id_type=MESH)


def _half(ref_rows, core):
    return pl.ds(core * (ref_rows // 2), ref_rows // 2)


def _gather_ici(shards, layer, extra=None):
    extras = [] if extra is None else [extra]
    n = len(shards)

    def copies(cin, cout, send, recv):
        x, y, c, me, sibling, chips = _place()
        sends, recvs = [], []
        for a in range(n):
            s, g = cin[a], cout[a]
            rows = s.shape[1]
            mine = _half(rows, c)
            sends.append(_rcopy(send, recv, 4 * a, s.at[layer], g.at[me], sibling))
            recvs.append(_rcopy(send, recv, 4 * a, s.at[layer], g.at[me], sibling))
            for p, (px, py) in enumerate(chips):
                sends.append(_rcopy(send, recv, 4 * a + 1 + p, s.at[layer, mine], g.at[me, mine], (px, py, c)))
                recvs.append(_rcopy(send, recv, 4 * a + 1 + p, s.at[layer, mine], g.at[2 * px + py, mine], (px, py, c)))
        for e in range(len(extras)):
            s, g = cin[n + e], cout[n + e]
            k = 4 * (n + e)
            sends.append(_rcopy(send, recv, k, s, g.at[me], sibling))
            recvs.append(_rcopy(send, recv, k, s, g.at[me], sibling))
            for p, (px, py) in enumerate(chips):
                sends.append(_rcopy(send, recv, k + 1 + p, s, g.at[me], (px, py, c)))
                recvs.append(_rcopy(send, recv, k + 1 + p, s, g.at[2 * px + py], (px, py, c)))
        return sends, recvs

    out_shapes = [jax.ShapeDtypeStruct((N_CHIPS,) + s.shape[1:], s.dtype) for s in shards]
    out_shapes += [jax.ShapeDtypeStruct((N_CHIPS,) + e.shape, e.dtype) for e in extras]
    return _Comm(list(shards) + extras, out_shapes, {}, 4 * (n + len(extras)), copies)


def _gather_d2d(gathered):
    n = len(gathered)

    def copies(cin, cout, send, recv):
        x, y, c, me, sibling, chips = _place()
        sends, recvs = [], []
        for a in range(n):
            g = cout[a]
            rows = g.shape[1]
            for p, (px, py) in enumerate(chips):
                mine, theirs = g.at[2 * px + py, _half(rows, c)], g.at[2 * px + py, _half(rows, 1 - c)]
                sends.append(_rcopy(send, recv, 3 * a + p, mine, mine, sibling))
                recvs.append(_rcopy(send, recv, 3 * a + p, theirs, theirs, sibling))
        return sends, recvs

    return _Comm(list(gathered), [jax.ShapeDtypeStruct(g.shape, g.dtype) for g in gathered],
                 {a: a for a in range(n)}, 3 * n, copies)


def _small_allreduce(buf):
    rows, cols = buf.shape

    def body(b_ref, o_ref, slots, send_sems, recv_sems):
        x, y, c = lax.axis_index("x"), lax.axis_index("y"), lax.axis_index("c")
        me = 4 * x + 2 * y + c
        slots[me] = b_ref[...]
        copies = []
        for r in range(1, 8):
            fx, fy, fc = (r >> 2) & 1, (r >> 1) & 1, r & 1
            px, py, pc = x ^ fx, y ^ fy, c ^ fc
            copies.append(pltpu.make_async_remote_copy(
                src_ref=b_ref, dst_ref=slots.at[me], send_sem=send_sems.at[r - 1], recv_sem=recv_sems.at[r - 1],
                device_id=(px, py, pc), device_id_type=MESH))
        for cp in copies:
            cp.start()
        for r in range(1, 8):
            fx, fy, fc = (r >> 2) & 1, (r >> 1) & 1, r & 1
            peer = 4 * (x ^ fx) + 2 * (y ^ fy) + (c ^ fc)
            pltpu.make_async_remote_copy(
                src_ref=b_ref, dst_ref=slots.at[peer], send_sem=send_sems.at[r - 1], recv_sem=recv_sems.at[r - 1],
                device_id=(x ^ fx, y ^ fy, c ^ fc), device_id_type=MESH).wait_recv()
        for cp in copies:
            cp.wait_send()
        acc = slots[0]
        for d in range(1, 8):
            acc = acc + slots[d]
        o_ref[...] = acc

    return _pcall(body, name="small_allreduce", out_shape=jax.ShapeDtypeStruct((rows, cols), f32),
                  in_specs=[pl.BlockSpec(memory_space=pltpu.VMEM)], out_specs=pl.BlockSpec(memory_space=pltpu.VMEM),
                  scratch_shapes=[pltpu.VMEM((8, rows, cols), f32), pltpu.SemaphoreType.DMA((7,)),
                                  pltpu.SemaphoreType.DMA((7,))])(buf)


def _sibling_exchange(gs):
    def copies(cin, cout, send, recv):
        x, y, c, me, sibling, chips = _place()
        sends = [_rcopy(send, recv, a, g.at[:, _half(g.shape[1], 1 - c)], r, sibling) for a, (g, r) in enumerate(zip(cin, cout))]
        return sends, sends

    shapes = [jax.ShapeDtypeStruct((N_CHIPS, g.shape[1] // 2, g.shape[2]), g.dtype) for g in gs]
    return _Comm(list(gs), shapes, {}, len(gs), copies)


def _chip_scatter(ss):
    def copies(cin, cout, send, recv):
        x, y, c, me, sibling, chips = _place()
        sends = [_rcopy(send, recv, 3 * a + p, s.at[2 * px + py], r.at[p], (px, py, c))
                 for a, (s, r) in enumerate(zip(cin, cout)) for p, (px, py) in enumerate(chips)]
        return sends, sends

    shapes = [jax.ShapeDtypeStruct((3,) + s.shape[1:], s.dtype) for s in ss]
    return _Comm(list(ss), shapes, {}, 3 * len(ss), copies)


def _sibling_share(fs, layer):
    def copies(cin, cout, send, recv):
        x, y, c, me, sibling, chips = _place()
        sends, recvs = [], []
        for a, f in enumerate(cout):
            mine, theirs = f.at[layer, _half(f.shape[1], c)], f.at[layer, _half(f.shape[1], 1 - c)]
            sends.append(_rcopy(send, recv, a, mine, mine, sibling))
            recvs.append(_rcopy(send, recv, a, theirs, theirs, sibling))
        return sends, recvs

    return _Comm(list(fs), [jax.ShapeDtypeStruct(f.shape, f.dtype) for f in fs], {a: a for a in range(len(fs))},
                 len(fs), copies)


def _sum_rows(rows):
    return next(b for b in (256, 192, 128) if rows % b == 0)


def _pair_sum(g, r, core_arr, layer):
    _, rows, cols = r.shape
    br = _sum_rows(rows)
    nb = rows // br

    def body(c_ref, g_ref, r_ref, o_ref):
        o_ref[...] = (g_ref[...].astype(f32) + r_ref[...].astype(f32)).astype(bf16)

    return _pcall(
        body, name=f"pair_sum_{layer}", out_shape=jax.ShapeDtypeStruct(r.shape, bf16),
        grid_spec=pltpu.PrefetchScalarGridSpec(
            num_scalar_prefetch=1, grid=(N_CHIPS, nb),
            in_specs=[pl.BlockSpec((1, br, cols), lambda j, i, c_ref: (j, c_ref[0] * nb + i, 0)),
                      pl.BlockSpec((1, br, cols), lambda j, i, c_ref: (j, i, 0))],
            out_specs=pl.BlockSpec((1, br, cols), lambda j, i, c_ref: (j, i, 0))),
        compiler_params=_params("arbitrary", "arbitrary"))(core_arr, g, r)


def _chip_sum(s, r, place, layer, final):
    _, rows, cols = s.shape
    br = _sum_rows(rows)
    nb = rows // br

    def body(place_ref, s_ref, r_ref, *rest):
        o_ref = rest[-1]
        acc = s_ref[0].astype(f32)
        for p in range(3):
            acc = acc + r_ref[p].astype(f32)
        o_ref[...] = acc

    carried = [] if final is None else [final]
    return _pcall(
        body, name=f"chip_sum_{layer}", out_shape=jax.ShapeDtypeStruct((DEPTH, 2 * rows, cols), f32),
        grid_spec=pltpu.PrefetchScalarGridSpec(
            num_scalar_prefetch=1, grid=(nb,),
            in_specs=[pl.BlockSpec((1, br, cols), lambda i, place_ref: (place_ref[0], i, 0)),
                      pl.BlockSpec((3, br, cols), lambda i, place_ref: (0, i, 0))] + [ANY] * len(carried),
            out_specs=pl.BlockSpec((None, br, cols), lambda i, place_ref: (layer, place_ref[1] * nb + i, 0))),
        input_output_aliases={3: 0} if carried else {},
        compiler_params=_params("arbitrary"))(place, s, r, *carried)


def _inproj(xb, win, layer, comm=None):
    T = xb.shape[0]
    tm = 512

    def body(x_ref, w_ref, o_ref):
        xt = x_ref[...]
        for j in range(N_CHIPS):
            o_ref[:, j * W_IN_COLS:(j + 1) * W_IN_COLS] = jnp.dot(xt, w_ref[j], preferred_element_type=f32).astype(bf16)

    (h,), carried = _pcall_carry(
        body, comm, n_in=2, n_out=1, name=f"inproj_{layer}", grid=(T // tm,),
        out_shape=(jax.ShapeDtypeStruct((T, N_IN), bf16),),
        in_specs=[pl.BlockSpec((tm, D_MODEL), lambda i: (i, 0)),
                  pl.BlockSpec((N_CHIPS, D_MODEL, W_IN_COLS), lambda i: (0, 0, 0), pipeline_mode=pl.Buffered(1))],
        out_specs=(pl.BlockSpec((tm, N_IN), lambda i: (i, 0)),),
        compiler_params=_params("arbitrary"))(xb, win)
    return h, carried


def _rel_index_rows():
    j = lax.broadcasted_iota(jnp.int32, (N_REL_PAD, KG), 1)
    r = lax.broadcasted_iota(jnp.int32, (N_REL_PAD, KG), 0)
    off = jnp.where(j < KG - 2 * CHUNK, j, j - KG)
    idx = jnp.clip(N_PREV * CHUNK - off, -REL_CLIP, REL_CLIP) + REL_CLIP
    return (idx == r).astype(f32)


def _bias_expand(table_pad, layer):
    def body(t_ref, o_ref, row_scr):
        h = pl.program_id(0)

        @pl.when(h == 0)
        def _():
            row_scr[...] = jnp.dot(t_ref[...], _rel_index_rows(), precision=lax.Precision.HIGHEST,
                                   preferred_element_type=f32)

        q = lax.broadcasted_iota(jnp.int32, (QG, KG), 0)
        k = lax.broadcasted_iota(jnp.int32, (QG, KG), 1)
        band = (k // CHUNK >= q // CHUNK) & (k // CHUNK <= q // CHUNK + N_PREV)
        t = jnp.broadcast_to(row_scr[pl.ds(h, 1), :], (QG, KG))
        for b in range(8):
            t = jnp.where(((q >> b) & 1) == 1, pltpu.roll(t, 1 << b, axis=1), t)
        for v in range(3):
            o_ref[v] = jnp.where(band & (k >= (2 - v) * QG), t, NEG)

    return _pcall(body, name=f"bias_expand_{layer}", grid=(N_HEADS,),
                  out_shape=jax.ShapeDtypeStruct((3, N_HEADS, QG, KG), f32),
                  in_specs=[pl.BlockSpec((N_HEADS, N_REL_PAD), lambda h: (0, 0))],
                  out_specs=pl.BlockSpec((3, None, QG, KG), lambda h: (0, h, 0, 0)),
                  scratch_shapes=[pltpu.VMEM((N_HEADS, KG), f32)], compiler_params=_params("arbitrary"))(table_pad)


def _bias_reduce(dbias, layer):
    def body(d_ref, o_ref, row_scr):
        q = lax.broadcasted_iota(jnp.int32, (QG, DB_COLS), 0)
        k = lax.broadcasted_iota(jnp.int32, (QG, DB_COLS), 1)
        for h in range(N_HEADS):
            t = jnp.where(k > q, d_ref[h], 0.0)
            for b in range(8):
                t = jnp.where(((q >> b) & 1) == 1, pltpu.roll(t, DB_COLS - (1 << b), axis=1), t)
            row_scr[h:h + 1, :] = jnp.sum(t, axis=0, keepdims=True)
        r = lax.broadcasted_iota(jnp.int32, (N_REL_PAD, DB_COLS), 0)
        off = lax.broadcasted_iota(jnp.int32, (N_REL_PAD, DB_COLS), 1)
        own = (off >= 1) & (off < REL_CLIP + CHUNK)
        sel = jnp.where(own & (r == 2 * REL_CLIP - off), 1.0, 0.0) - jnp.where(own & (r == 2 * REL_CLIP), 1.0, 0.0)
        o_ref[...] = lax.dot_general(row_scr[...], sel, NT, precision=lax.Precision.HIGHEST, preferred_element_type=f32)

    return _pcall(body, name=f"bias_reduce_{layer}", out_shape=jax.ShapeDtypeStruct((N_HEADS, N_REL_PAD), f32),
                  scratch_shapes=[pltpu.VMEM((N_HEADS, DB_COLS), f32)],
                  compiler_params=pltpu.CompilerParams(vmem_limit_bytes=VMEM_LIMIT))(dbias)


FWD_PAIRS = 8
BWD_PAIRS = 4


def _key_specs(n_groups, npairs, slab):
    per_slab = E_MIX // (128 * npairs)
    return [pl.BlockSpec((QG, 128 * npairs), functools.partial(
        lambda hp, g, jj: (jnp.clip(g - 2 + jj, 0, n_groups - 1), slab * per_slab + hp), jj=jj)) for jj in range(3)]


def _bias_spec(npairs):
    return pl.BlockSpec((None, 2 * npairs, QG, KG), lambda hp, g: (jnp.minimum(g, 2), hp, 0, 0))


def _attn_fwd(h, bias, layer, comm=None):
    T = h.shape[0]
    n_groups = T // QG
    scale = 1.0 / math.sqrt(HEAD_DIM)

    def body(q_ref, k0, k1, k2, v0, v1, v2, b_ref, o_ref, lse_ref):
        lane = lax.broadcasted_iota(jnp.int32, (1, 128), 1)
        ones = jnp.ones((KG, 128), bf16)
        lse = jnp.zeros((QG, 128), f32)
        for pp in range(FWD_PAIRS):
            cs = slice(pp * 128, (pp + 1) * 128)
            q2 = q_ref[:, cs] * scale
            kc = jnp.concatenate([k0[:, cs], k1[:, cs], k2[:, cs]], axis=0)
            vc = jnp.concatenate([jnp.concatenate([v0[:, cs], v1[:, cs], v2[:, cs]], axis=0), ones], axis=1)
            outs = []
            for hh in range(2):
                qm = jnp.where(lane // HEAD_DIM == hh, q2, jnp.zeros_like(q2))
                s = lax.dot_general(qm, kc, NT, preferred_element_type=f32) + b_ref[2 * pp + hh]
                m = jnp.max(s, axis=1, keepdims=True)
                ol = jnp.dot(jnp.exp(s - m).astype(bf16), vc, preferred_element_type=f32)
                outs.append(ol[:, :128] / ol[:, 128:])
                lse = jnp.where(lane == 2 * pp + hh, m + jnp.log(ol[:, 128:]), lse)
            o_ref[:, cs] = jnp.where(lane // HEAD_DIM == 0, outs[0], outs[1]).astype(bf16)
        lse_ref[...] = lse

    (mix, lse), carried = _pcall_carry(
        body, comm, n_in=8, n_out=2, name=f"attn_fwd_{layer}", grid=(N_HEADS // (2 * FWD_PAIRS), n_groups),
        out_shape=(jax.ShapeDtypeStruct((T, E_MIX), bf16), jax.ShapeDtypeStruct((T, 128), f32)),
        in_specs=[pl.BlockSpec((QG, 128 * FWD_PAIRS), lambda hp, g: (g, hp))] + _key_specs(n_groups, FWD_PAIRS, 1)
        + _key_specs(n_groups, FWD_PAIRS, 2) + [_bias_spec(FWD_PAIRS)],
        out_specs=(pl.BlockSpec((QG, 128 * FWD_PAIRS), lambda hp, g: (g, hp)), pl.BlockSpec((QG, 128), lambda hp, g: (g, 0))),
        compiler_params=_params("arbitrary", "arbitrary"))(h, h, h, h, h, h, h, bias)
    return mix, lse, carried


def _halo_rows(ref, r):
    return ref[r:r + 1, :].astype(f32)


def _conv_taps(cu, p6, p7, w_ref):
    row = lax.broadcasted_iota(jnp.int32, cu.shape, 0)
    r1 = jnp.where(row == 0, p7, pltpu.roll(cu, 1, axis=0))
    r2 = jnp.where(row == 0, p6, jnp.where(row == 1, p7, pltpu.roll(cu, 2, axis=0)))
    return w_ref[2:3, :] * cu + w_ref[1:2, :] * r1 + w_ref[0:1, :] * r2, r1, r2


def _conv_fwd(h, w, layer, comm=None):
    T = h.shape[0]
    tm = 512

    def body(bg_ref, cg_ref, u_ref, cgp_ref, up_ref, w_ref, o_ref):
        first = (pl.program_id(0) == 0).astype(f32)
        cu = cg_ref[...].astype(f32) * u_ref[...].astype(f32)
        p6 = _halo_rows(cgp_ref, 14) * _halo_rows(up_ref, 14) * (1.0 - first)
        p7 = _halo_rows(cgp_ref, 15) * _halo_rows(up_ref, 15) * (1.0 - first)
        conv, _, _ = _conv_taps(cu, p6, p7, w_ref)
        o_ref[...] = (bg_ref[...].astype(f32) * conv).astype(bf16)

    prev = lambda slab: pl.BlockSpec((16, E_MIX), lambda i: (jnp.maximum(i * (tm // 16) - 1, 0), slab))
    (mix,), carried = _pcall_carry(
        body, comm, n_in=6, n_out=1, name=f"conv_fwd_{layer}", grid=(T // tm,),
        out_shape=(jax.ShapeDtypeStruct((T, E_MIX), bf16),),
        in_specs=[pl.BlockSpec((tm, E_MIX), lambda i: (i, 0)), pl.BlockSpec((tm, E_MIX), lambda i: (i, 1)),
                  pl.BlockSpec((tm, E_MIX), lambda i: (i, 2)), prev(1), prev(2),
                  pl.BlockSpec((CONV_W, E_MIX), lambda i: (0, 0))],
        out_specs=(pl.BlockSpec((tm, E_MIX), lambda i: (i, 0)),),
        compiler_params=_params("arbitrary"))(h, h, h, h, h, w)
    return mix, carried


def _kv_mem(memb, wkv):
    def body(m_ref, w_ref, o_ref):
        o_ref[...] = jnp.dot(m_ref[...], w_ref[...], preferred_element_type=f32).astype(bf16)

    return _pcall(body, name="kv_mem", out_shape=jax.ShapeDtypeStruct((N_MEM, 2 * E_MEM), bf16),
                  compiler_params=pltpu.CompilerParams(vmem_limit_bytes=VMEM_LIMIT))(memb, wkv)


def _mem_probs(qm_ref, kv_ref, hh):
    qh = qm_ref[:, hh * MEM_HEAD_DIM:(hh + 1) * MEM_HEAD_DIM]
    kh = kv_ref[:, hh * MEM_HEAD_DIM:(hh + 1) * MEM_HEAD_DIM]
    vh = kv_ref[:, E_MEM + hh * MEM_HEAD_DIM:E_MEM + (hh + 1) * MEM_HEAD_DIM]
    s = lax.dot_general(qh, kh, NT, preferred_element_type=f32) * (1.0 / math.sqrt(MEM_HEAD_DIM))
    e = jnp.exp(s - jnp.max(s, axis=1, keepdims=True))
    return e / jnp.sum(e, axis=1, keepdims=True), qh, kh, vh


def _h_tail_specs(tm):
    return [pl.BlockSpec((tm, E_MEM), functools.partial(lambda i, cb: (i, cb), cb=cb)) for cb in (6, 7, 8, 9)]


def _post_fwd(h, mix, kv, wout, x, g, b, layer):
    T = x.shape[0]
    tm = 512

    def body(qm_ref, z0, z1, z2, mix_ref, kv_ref, w_ref, x_ref, g_ref, b_ref, xn_ref, xb_ref, xh_ref, rs_ref):
        mem = jnp.concatenate(
            [jnp.dot(_mem_probs(qm_ref, kv_ref, hh)[0].astype(bf16), kv_ref[:, E_MEM + hh * MEM_HEAD_DIM:E_MEM + (hh + 1) * MEM_HEAD_DIM],
                     preferred_element_type=f32) for hh in range(MEM_HEADS)], axis=1)
        z = jnp.concatenate([z0[...], z1[...], z2[...]], axis=1).astype(f32)
        act, _ = _silu_parts(z)
        y = jnp.concatenate([mix_ref[...].astype(f32), mem], axis=1) * act
        out = jnp.dot(y.astype(bf16), w_ref[...], preferred_element_type=f32)
        r = DN_ALPHA * x_ref[...] + out
        mu = jnp.mean(r, axis=1, keepdims=True)
        var = jnp.mean(jnp.square(r - mu), axis=1, keepdims=True)
        rstd = lax.rsqrt(var + LN_EPS)
        xhat = (r - mu) * rstd
        xn = xhat * g_ref[...] + b_ref[...]
        xn_ref[...] = xn
        xb_ref[...] = xn.astype(bf16)
        xh_ref[...] = xhat
        rs_ref[...] = rstd

    tile = lambda w: pl.BlockSpec((tm, w), lambda i: (i, 0))
    const = lambda r, c: pl.BlockSpec((r, c), lambda i: (0, 0))
    return _pcall(
        body, name=f"post_fwd_{layer}", grid=(T // tm,),
        out_shape=(jax.ShapeDtypeStruct((T, D_MODEL), f32), jax.ShapeDtypeStruct((T, D_MODEL), bf16),
                   jax.ShapeDtypeStruct((T, D_MODEL), f32), jax.ShapeDtypeStruct((T, 1), f32)),
        in_specs=_h_tail_specs(tm) + [tile(E_MIX), const(N_MEM, 2 * E_MEM), const(E_BRANCH, D_MODEL), tile(D_MODEL),
                                      const(1, D_MODEL), const(1, D_MODEL)],
        out_specs=(tile(D_MODEL), tile(D_MODEL), tile(D_MODEL), tile(1)),
        compiler_params=_params("arbitrary"))(h, h, h, h, mix, kv, wout, x, g, b)


def _loss_head(y, target):
    T = y.shape[0]
    tm = 512

    def body(y_ref, t_ref, l_ref, d_ref):
        @pl.when(pl.program_id(0) == 0)
        def _():
            l_ref[...] = jnp.zeros_like(l_ref)
        err = y_ref[...] - t_ref[...]
        d_ref[...] = err * (1.0 / D_MODEL)
        l_ref[...] += jnp.sum(jnp.square(err))

    return _pcall(
        body, name="loss_head", grid=(T // tm,),
        out_shape=(jax.ShapeDtypeStruct((8, 128), f32), jax.ShapeDtypeStruct((T, D_MODEL), f32)),
        in_specs=[pl.BlockSpec((tm, D_MODEL), lambda i: (i, 0))] * 2,
        out_specs=(pl.BlockSpec((8, 128), lambda i: (0, 0)), pl.BlockSpec((tm, D_MODEL), lambda i: (i, 0))),
        compiler_params=_params("arbitrary"))(y, target)


def _post_bwd(dxn, xhat, rstd, g, h, mix, kv, wout, layer, comm=None):
    T = dxn.shape[0]
    tm = 256
    n_tiles = T // tm
    inv = 1.0 / math.sqrt(MEM_HEAD_DIM)

    def body(dxn_ref, xh_ref, rs_ref, g_ref, qm_ref, z0, z1, z2, mix_ref, kv_ref, w_ref,
             dxp_ref, dhb_ref, dmix_ref, dkv_ref, dwo_out, dg_ref, db_ref, dwo_ref):
        @pl.when(pl.program_id(0) == 0)
        def _():
            dkv_ref[...] = jnp.zeros_like(dkv_ref)
            dwo_ref[...] = jnp.zeros_like(dwo_ref)
            dg_ref[...] = jnp.zeros_like(dg_ref)
            db_ref[...] = jnp.zeros_like(db_ref)

        dy, xh = dxn_ref[...], xh_ref[...]
        dg_ref[0:1, :] += jnp.sum(dy * xh, axis=0, keepdims=True)
        db_ref[0:1, :] += jnp.sum(dy, axis=0, keepdims=True)
        gx = dy * g_ref[...]
        dr = rs_ref[...] * (gx - jnp.mean(gx, axis=1, keepdims=True) - xh * jnp.mean(gx * xh, axis=1, keepdims=True))
        dxp_ref[...] = DN_ALPHA * dr
        dob = dr.astype(bf16)

        probs = [_mem_probs(qm_ref, kv_ref, hh) for hh in range(MEM_HEADS)]
        mem = jnp.concatenate([jnp.dot(p.astype(bf16), vh, preferred_element_type=f32) for p, _, _, vh in probs], axis=1)
        z = jnp.concatenate([z0[...], z1[...], z2[...]], axis=1).astype(f32)
        act, sig = _silu_parts(z)
        cat = jnp.concatenate([mix_ref[...].astype(f32), mem], axis=1)
        yb = (cat * act).astype(bf16)
        dwo_ref[...] += lax.dot_general(yb, dob, TN, preferred_element_type=f32)
        dyv = lax.dot_general(dob, w_ref[...], NT, preferred_element_type=f32)
        dz = dyv * cat * (sig * (1.0 + z * (1.0 - sig)))
        dcat = dyv * act
        dmix_ref[...] = dcat[:, :E_MIX].astype(bf16)
        dqs = []
        for hh, (p, qh, kh, vh) in enumerate(probs):
            dmem = dcat[:, E_MIX + hh * MEM_HEAD_DIM:E_MIX + (hh + 1) * MEM_HEAD_DIM].astype(bf16)
            dp = lax.dot_general(dmem, vh, NT, preferred_element_type=f32)
            ds = (p * (dp - jnp.sum(p * dp, axis=1, keepdims=True))).astype(bf16)
            dqs.append(jnp.dot(ds, kh, preferred_element_type=f32) * inv)
            dkv_ref[:, hh * MEM_HEAD_DIM:(hh + 1) * MEM_HEAD_DIM] += lax.dot_general(ds, qh, TN, preferred_element_type=f32) * inv
            dkv_ref[:, E_MEM + hh * MEM_HEAD_DIM:E_MEM + (hh + 1) * MEM_HEAD_DIM] += lax.dot_general(
                p.astype(bf16), dmem, TN, preferred_element_type=f32)
        dhb_ref[...] = jnp.concatenate(dqs + [dz], axis=1).astype(bf16)

        @pl.when(pl.program_id(0) == n_tiles - 1)
        def _():
            dwo_out[...] = dwo_ref[...].astype(bf16)

    tile = lambda w: pl.BlockSpec((tm, w), lambda i: (i, 0))
    const = lambda r, c: pl.BlockSpec((r, c), lambda i: (0, 0))
    return _pcall_carry(
        body, comm, n_in=11, n_out=7, name=f"post_bwd_{layer}", grid=(n_tiles,),
        out_shape=(jax.ShapeDtypeStruct((T, D_MODEL), f32), jax.ShapeDtypeStruct((T, E_MEM + E_BRANCH), bf16),
                   jax.ShapeDtypeStruct((T, E_MIX), bf16), jax.ShapeDtypeStruct((N_MEM, 2 * E_MEM), f32),
                   jax.ShapeDtypeStruct((E_BRANCH, D_MODEL), bf16), jax.ShapeDtypeStruct((8, D_MODEL), f32),
                   jax.ShapeDtypeStruct((8, D_MODEL), f32)),
        in_specs=[tile(D_MODEL), tile(D_MODEL), tile(1), const(1, D_MODEL)] + _h_tail_specs(tm)
        + [tile(E_MIX), const(N_MEM, 2 * E_MEM), const(E_BRANCH, D_MODEL)],
        out_specs=(tile(D_MODEL), tile(E_MEM + E_BRANCH), tile(E_MIX), const(N_MEM, 2 * E_MEM),
                   const(E_BRANCH, D_MODEL), const(8, D_MODEL), const(8, D_MODEL)),
        scratch_shapes=[pltpu.VMEM((E_BRANCH, D_MODEL), f32)],
        compiler_params=_params("arbitrary"))(dxn, xhat, rstd, g, h, h, h, h, mix, kv, wout)


def _attn_bwd(h, bias, dmix, lse, layer, comm=None):
    T = h.shape[0]
    n_groups = T // QG
    scale = 1.0 / math.sqrt(HEAD_DIM)

    def body(q_ref, k0, k1, k2, v0, v1, v2, do_ref, b_ref, lse_ref, dq_ref, dk_ref, dv_ref, db_ref, acck, accv):
        g = pl.program_id(1)

        @pl.when(g == 0)
        def _():
            acck[...] = jnp.zeros_like(acck)
            accv[...] = jnp.zeros_like(accv)
            db_ref[...] = jnp.zeros_like(db_ref)

        @pl.when(g < n_groups)
        def _():
            lane = lax.broadcasted_iota(jnp.int32, (1, 128), 1)
            first = lane // HEAD_DIM == 0
            for pp in range(BWD_PAIRS):
                cs = slice(pp * 128, (pp + 1) * 128)
                do2 = do_ref[:, cs]
                q2 = q_ref[:, cs] * scale
                kc = jnp.concatenate([k0[:, cs], k1[:, cs], k2[:, cs]], axis=0)
                vc = jnp.concatenate([v0[:, cs], v1[:, cs], v2[:, cs]], axis=0)
                dqs, dks, dvs = [], [], []
                for hh in range(2):
                    hm = lane // HEAD_DIM == hh
                    head = (pl.program_id(0) * BWD_PAIRS + pp) * 2 + hh
                    lse = jnp.sum(jnp.where(lane == head, lse_ref[...], 0.0), axis=1, keepdims=True)
                    qm = jnp.where(hm, q2, jnp.zeros_like(q2))
                    dom = jnp.where(hm, do2, jnp.zeros_like(do2))
                    s = lax.dot_general(qm, kc, NT, preferred_element_type=f32) + b_ref[2 * pp + hh]
                    p = jnp.exp(s - lse)
                    dp = lax.dot_general(dom, vc, NT, preferred_element_type=f32)
                    ds = p * (dp - jnp.sum(p * dp, axis=1, keepdims=True))
                    db_ref[2 * pp + hh] += ds[:, KG - DB_COLS:]
                    dsb, pb = ds.astype(bf16), p.astype(bf16)
                    dqs.append(jnp.dot(dsb, kc, preferred_element_type=f32) * scale)
                    dks.append(lax.dot_general(dsb, q2, TN, preferred_element_type=f32))
                    dvs.append(lax.dot_general(pb, do2, TN, preferred_element_type=f32))
                dq_ref[:, cs] = jnp.where(first, dqs[0], dqs[1]).astype(bf16)
                dkc = jnp.where(first, dks[0], dks[1])
                dvc = jnp.where(first, dvs[0], dvs[1])
                for jj in range(3):
                    slot = (g + 1 + jj) % 3
                    if jj == 2:
                        acck[slot, :, cs] = dkc[jj * QG:(jj + 1) * QG]
                        accv[slot, :, cs] = dvc[jj * QG:(jj + 1) * QG]
                    else:
                        acck[slot, :, cs] += dkc[jj * QG:(jj + 1) * QG]
                        accv[slot, :, cs] += dvc[jj * QG:(jj + 1) * QG]

        done = (g + 1) % 3
        dk_ref[...] = acck[done].astype(bf16)
        dv_ref[...] = accv[done].astype(bf16)

    last = n_groups - 1
    width = 128 * BWD_PAIRS
    qspec = pl.BlockSpec((QG, width), lambda hp, g: (jnp.minimum(g, last), hp))
    kout = pl.BlockSpec((QG, width), lambda hp, g: (jnp.clip(g - 2, 0, last), hp))
    dbspec = pl.BlockSpec((2 * BWD_PAIRS, QG, DB_COLS), lambda hp, g: (hp, 0, 0))
    lspec = pl.BlockSpec((QG, 128), lambda hp, g: (jnp.minimum(g, last), 0))
    return _pcall_carry(
        body, comm, n_in=10, n_out=4, name=f"attn_bwd_{layer}", grid=(N_HEADS // (2 * BWD_PAIRS), n_groups + 2),
        out_shape=(jax.ShapeDtypeStruct((T, E_MIX), bf16),) * 3 + (jax.ShapeDtypeStruct((N_HEADS, QG, DB_COLS), f32),),
        in_specs=[qspec] + _key_specs(n_groups, BWD_PAIRS, 1) + _key_specs(n_groups, BWD_PAIRS, 2)
        + [qspec, _bias_spec(BWD_PAIRS), lspec],
        out_specs=(qspec, kout, kout, dbspec),
        scratch_shapes=[pltpu.VMEM((3, QG, width), f32), pltpu.VMEM((3, QG, width), f32)],
        compiler_params=_params("arbitrary", "arbitrary"))(h, h, h, h, h, h, h, dmix, bias, lse)


def _conv_bwd(h, w, dmix, layer):
    T = h.shape[0]
    tm = 512
    n_tiles = T // tm

    def body(bg_ref, cg_ref, u_ref, cgp_ref, up_ref, dy_ref, bgn_ref, dyn_ref, w_ref, dbg_ref, dcg_ref, du_ref, dw_ref):
        i = pl.program_id(0)

        @pl.when(i == 0)
        def _():
            dw_ref[...] = jnp.zeros_like(dw_ref)

        first = (i == 0).astype(f32)
        final = (i == n_tiles - 1).astype(f32)
        bg, cg, u = bg_ref[...].astype(f32), cg_ref[...].astype(f32), u_ref[...].astype(f32)
        dy = dy_ref[...].astype(f32)
        cu = cg * u
        p6 = _halo_rows(cgp_ref, 14) * _halo_rows(up_ref, 14) * (1.0 - first)
        p7 = _halo_rows(cgp_ref, 15) * _halo_rows(up_ref, 15) * (1.0 - first)
        conv, r1, r2 = _conv_taps(cu, p6, p7, w_ref)
        dbg_ref[...] = (dy * conv).astype(bf16)
        dc = dy * bg
        n0 = _halo_rows(dyn_ref, 0) * _halo_rows(bgn_ref, 0) * (1.0 - final)
        n1 = _halo_rows(dyn_ref, 1) * _halo_rows(bgn_ref, 1) * (1.0 - final)
        row = lax.broadcasted_iota(jnp.int32, dc.shape, 0)
        f1 = jnp.where(row == tm - 1, n0, pltpu.roll(dc, tm - 1, axis=0))
        f2 = jnp.where(row == tm - 2, n0, jnp.where(row == tm - 1, n1, pltpu.roll(dc, tm - 2, axis=0)))
        dcu = w_ref[2:3, :] * dc + w_ref[1:2, :] * f1 + w_ref[0:1, :] * f2
        dcg_ref[...] = (dcu * u).astype(bf16)
        du_ref[...] = (dcu * cg).astype(bf16)
        dw_ref[0:1, :] += jnp.sum(dc * r2, axis=0, keepdims=True)
        dw_ref[1:2, :] += jnp.sum(dc * r1, axis=0, keepdims=True)
        dw_ref[2:3, :] += jnp.sum(dc * cu, axis=0, keepdims=True)

    tile = lambda slab: pl.BlockSpec((tm, E_MIX), lambda i: (i, slab))
    prev = lambda slab: pl.BlockSpec((16, E_MIX), lambda i: (jnp.maximum(i * (tm // 16) - 1, 0), slab))
    nxt = lambda slab: pl.BlockSpec((16, E_MIX), lambda i: (jnp.minimum((i + 1) * (tm // 16), T // 16 - 1), slab))
    return _pcall(
        body, name=f"conv_bwd_{layer}", grid=(n_tiles,),
        out_shape=(jax.ShapeDtypeStruct((T, E_MIX), bf16),) * 3 + (jax.ShapeDtypeStruct((8, E_MIX), f32),),
        in_specs=[tile(0), tile(1), tile(2), prev(1), prev(2), tile(0), nxt(0), nxt(0),
                  pl.BlockSpec((CONV_W, E_MIX), lambda i: (0, 0))],
        out_specs=(tile(0), tile(0), tile(0), pl.BlockSpec((8, E_MIX), lambda i: (0, 0))),
        compiler_params=_params("arbitrary"))(h, h, h, h, h, dmix, h, dmix, w)


def _inproj_bwd_dx(da, db, dc, dhb, dxp, win, layer, comm=None):
    T = dxp.shape[0]
    tm = 512

    def body(da_ref, db_ref, dc_ref, dhb_ref, dxp_ref, w_ref, o_ref):
        dh = jnp.concatenate([da_ref[...], db_ref[...], dc_ref[...], dhb_ref[...]], axis=1)
        acc = dxp_ref[...]
        for j in range(N_CHIPS):
            acc = acc + lax.dot_general(dh[:, j * W_IN_COLS:(j + 1) * W_IN_COLS], w_ref[j], NT, preferred_element_type=f32)
        o_ref[...] = acc

    tile = lambda w: pl.BlockSpec((tm, w), lambda i: (i, 0))
    (dx,), carried = _pcall_carry(
        body, comm, n_in=6, n_out=1, name=f"inproj_bwd_dx_{layer}", grid=(T // tm,),
        out_shape=(jax.ShapeDtypeStruct((T, D_MODEL), f32),),
        in_specs=[tile(E_MIX), tile(E_MIX), tile(E_MIX), tile(E_MEM + E_BRANCH), tile(D_MODEL),
                  pl.BlockSpec((N_CHIPS, D_MODEL, W_IN_COLS), lambda i: (0, 0, 0), pipeline_mode=pl.Buffered(1))],
        out_specs=(tile(D_MODEL),),
        compiler_params=_params("arbitrary"))(da, db, dc, dhb, dxp, win)
    return dx, carried


def _inproj_bwd_dw(da, db, dc, dhb, xb, layer, comm=None):
    T = xb.shape[0]
    tm = 512
    n_tiles = T // tm

    def body(da_ref, db_ref, dc_ref, dhb_ref, x_ref, o_ref, acc, stage, sem):
        i = pl.program_id(0)

        @pl.when(i == 0)
        def _():
            acc[...] = jnp.zeros_like(acc)

        dh = jnp.concatenate([da_ref[...], db_ref[...], dc_ref[...], dhb_ref[...]], axis=1)
        xt = x_ref[...]
        for j in range(N_CHIPS):
            acc[j] += lax.dot_general(xt, dh[:, j * W_IN_COLS:(j + 1) * W_IN_COLS], TN, preferred_element_type=f32)

        @pl.when(i == n_tiles - 1)
        def _():
            for j in range(N_CHIPS):
                stage[...] = acc[j].astype(bf16)
                cp = pltpu.make_async_copy(stage, o_ref.at[j], sem)
                cp.start()
                cp.wait()

    tile = lambda w: pl.BlockSpec((tm, w), lambda i: (i, 0))
    (dw,), carried = _pcall_carry(
        body, comm, n_in=5, n_out=1, name=f"inproj_bwd_dw_{layer}", grid=(n_tiles,),
        out_shape=(jax.ShapeDtypeStruct((N_CHIPS, D_MODEL, W_IN_COLS), bf16),),
        in_specs=[tile(E_MIX), tile(E_MIX), tile(E_MIX), tile(E_MEM + E_BRANCH), tile(D_MODEL)],
        out_specs=(ANY,),
        scratch_shapes=[pltpu.VMEM((N_CHIPS, D_MODEL, W_IN_COLS), f32), pltpu.VMEM((D_MODEL, W_IN_COLS), bf16),
                        pltpu.SemaphoreType.DMA],
        compiler_params=_params("arbitrary"))(da, db, dc, dhb, xb)
    return dw, carried


def _kv_mem_bwd(memb, dkv, layer):
    def body(m_ref, d_ref, o_ref):
        o_ref[...] = lax.dot_general(m_ref[...], d_ref[...].astype(bf16), TN, preferred_element_type=f32).astype(bf16)

    return _pcall(body, name=f"kv_mem_bwd_{layer}", out_shape=jax.ShapeDtypeStruct((D_MODEL, 2 * E_MEM), bf16),
                  compiler_params=pltpu.CompilerParams(vmem_limit_bytes=VMEM_LIMIT))(memb, dkv)


def _to_bf16(a, name):
    rows, cols = a.shape
    br = 512

    def body(a_ref, o_ref):
        o_ref[...] = a_ref[...].astype(bf16)

    return _pcall(body, name=name, grid=(rows // br,), out_shape=jax.ShapeDtypeStruct((rows, cols), bf16),
                  in_specs=[pl.BlockSpec((br, cols), lambda i: (i, 0))], out_specs=pl.BlockSpec((br, cols), lambda i: (i, 0)),
                  compiler_params=_params("arbitrary"))(a)


def _adamw(w, g, m, v, name):
    shape = w.shape
    cols = shape[-1]
    rows = w.size // cols
    args = [a.reshape(rows, cols) for a in (w, g, m, v)]
    br = 256 if rows % 256 == 0 and rows > 256 else rows

    def body(w_ref, g_ref, m_ref, v_ref, d_ref, nm_ref, nv_ref):
        gg = g_ref[...]
        nm = ADAM_B1 * m_ref[...] + (1.0 - ADAM_B1) * gg
        nv = ADAM_B2 * v_ref[...] + (1.0 - ADAM_B2) * jnp.square(gg)
        m_hat = nm / (1.0 - ADAM_B1 ** ADAM_STEP)
        v_hat = nv / (1.0 - ADAM_B2 ** ADAM_STEP)
        d_ref[...] = -ADAM_LR * (m_hat / (jnp.sqrt(v_hat) + ADAM_EPS) + ADAM_WD * w_ref[...])
        nm_ref[...] = nm
        nv_ref[...] = nv

    spec = pl.BlockSpec((br, cols), lambda i: (i, 0))
    outs = _pcall(body, name=name, grid=(rows // br,), out_shape=(jax.ShapeDtypeStruct((rows, cols), f32),) * 3,
                  in_specs=[spec] * 4, out_specs=(spec,) * 3, compiler_params=_params("arbitrary"))(*args)
    return tuple(o.reshape(shape) for o in outs)


def kernel(x, mem, w_in, w_mem_kv, w_out, rel_bias, conv_w, ln_g, ln_b, loss_target, m_w_in, m_w_mem_kv, m_w_out, m_rel_bias, m_conv_w, m_ln_g, m_ln_b, v_w_in, v_w_mem_kv, v_w_out, v_rel_bias, v_conv_w, v_ln_g, v_ln_b):
    T = x.shape[1]
    x0 = x.reshape(T, D_MODEL)
    target = loss_target.reshape(T, D_MODEL)
    memb = mem.reshape(N_MEM, D_MODEL).astype(bf16)
    chip = 2 * lax.axis_index("x") + lax.axis_index("y")
    core = lax.axis_index("c")
    chip_arr = jnp.reshape(chip, (1,)).astype(jnp.int32)
    core_arr = jnp.reshape(core, (1,)).astype(jnp.int32)

    place = jnp.concatenate([chip_arr, core_arr])
    tables = jnp.pad(rel_bias, ((0, 0), (0, 0), (0, N_REL_PAD - N_REL)))

    shards = [w_in.astype(bf16), w_mem_kv.astype(bf16), w_out.astype(bf16)]
    *arrived, cw_g = _comm_call(_gather_ici(shards, 0, extra=conv_w), "gather_ici_0")
    gathered = {0: _comm_call(_gather_d2d(arrived), "gather_d2d_0")}
    conv_full = jnp.transpose(cw_g, (1, 2, 0, 3)).reshape(DEPTH // 2, CONV_W, E_MIX)

    xs, xbs, hs, mixes, kvs, xhats, rstds, biases, lses = [x0], [_to_bf16(x0, "cast_x")], [], [], [], [], [], {}, {}
    for layer in range(DEPTH):
        win, wkv, wout = gathered[layer]
        more = layer + 1 < DEPTH
        h, arrived = _inproj(xbs[layer], win, layer, _gather_ici(shards, layer + 1) if more else None)
        passing = _gather_d2d(list(arrived)) if more else None
        if layer % 2 == 0:
            biases[layer] = _bias_expand(tables[layer // 2], layer)
            mix, lses[layer], done = _attn_fwd(h, biases[layer], layer, passing)
        else:
            mix, done = _conv_fwd(h, conv_full[layer // 2], layer, passing)
        if more:
            gathered[layer + 1] = list(done)
        kv = _kv_mem(memb, wkv.reshape(D_MODEL, 2 * E_MEM))
        xn, xnb, xhat, rstd = _post_fwd(h, mix, kv, wout.reshape(E_BRANCH, D_MODEL), xs[layer], ln_g[layer][None, :],
                                        ln_b[layer][None, :], layer)
        xs.append(xn); xbs.append(xnb); hs.append(h); mixes.append(mix); kvs.append(kv); xhats.append(xhat); rstds.append(rstd)

    lsum, dx = _loss_head(xs[DEPTH], target)
    loss = lax.psum(lsum[0, 0], ("x", "y", "c")) * (0.5 / D_MODEL)

    dgs, dbs, dconvs, dtables = [None] * DEPTH, [None] * DEPTH, [None] * (DEPTH // 2), [None] * ((DEPTH + 1) // 2)
    finals = [None, None, None]
    above = None
    for layer in reversed(range(DEPTH)):
        h = hs[layer]
        win, wkv, wout = gathered[layer]
        (dxp, dhb, dmix, dkv, dwo, dgs[layer], dbs[layer]), from_sibling = _post_bwd(
            dx, xhats[layer], rstds[layer], ln_g[layer][None, :], h, mixes[layer], kvs[layer], wout.reshape(E_BRANCH, D_MODEL),
            layer, _sibling_exchange(above) if above else None)
        sums = [_pair_sum(g, r, core_arr, layer + 1) for g, r in zip(above, from_sibling)] if above else None
        scatter = _chip_scatter(sums) if above else None
        if layer % 2 == 0:
            (da, db, dc, dbias), from_chips = _attn_bwd(h, biases[layer], dmix, lses[layer], layer, scatter)
            dtables[layer // 2] = _bias_reduce(dbias, layer)
        else:
            da, db, dc, dconvs[layer // 2] = _conv_bwd(h, conv_full[layer // 2], dmix, layer)
        if layer > 0:
            dx, landed = _inproj_bwd_dx(da, db, dc, dhb, dxp, win, layer, scatter if layer % 2 == 1 else None)
            from_chips = landed if layer % 2 == 1 else from_chips
        share = None
        if above:
            finals = [_chip_sum(s, r, place, layer + 1, f) for s, r, f in zip(sums, from_chips, finals)]
            share = _sibling_share(finals, layer + 1)
        g_win, shared = _inproj_bwd_dw(da, db, dc, dhb, xbs[layer], layer, share)
        finals = list(shared) if above else finals
        above = [g_win, _kv_mem_bwd(memb, dkv, layer).reshape(N_CHIPS, W_KV_ROWS, 2 * E_MEM),
                 dwo.reshape(N_CHIPS, W_OUT_ROWS, D_MODEL)]
    from_sibling = _comm_call(_sibling_exchange(above), "sibling_exchange_0")
    sums = [_pair_sum(g, r, core_arr, 0) for g, r in zip(above, from_sibling)]
    dx, from_chips = _inproj_bwd_dx(da, db, dc, dhb, dxp, win, 0, _chip_scatter(sums))
    finals = [_chip_sum(s, r, place, 0, f) for s, r, f in zip(sums, from_chips, finals)]
    grad_w_in, grad_w_mem_kv, grad_w_out = _comm_call(_sibling_share(finals, 0), "sibling_share_0")
    grad_x = dx.reshape(1, T, D_MODEL)

    pad8 = lambda a: jnp.pad(a, ((0, 8 - a.shape[0]), (0, 0)))
    parts = dgs + dbs + dconvs + [pad8(t.reshape(-1, D_MODEL)) for t in dtables]
    small = _small_allreduce(jnp.concatenate(parts, axis=0))
    grad_ln_g = jnp.stack([small[8 * l] for l in range(DEPTH)])
    grad_ln_b = jnp.stack([small[8 * (DEPTH + l)] for l in range(DEPTH)])
    conv_all = jnp.stack([small[8 * (2 * DEPTH + a):8 * (2 * DEPTH + a) + CONV_W] for a in range(DEPTH // 2)])
    grad_conv_w = lax.dynamic_slice_in_dim(conv_all, chip * (E_MIX // N_CHIPS), E_MIX // N_CHIPS, axis=2)
    t0 = 8 * (2 * DEPTH + DEPTH // 2)
    grad_rel_bias = jnp.stack([small[t0 + 8 * a:t0 + 8 * a + 6].reshape(N_HEADS, N_REL_PAD)[:, :N_REL]
                               for a in range((DEPTH + 1) // 2)])

    grads = [grad_w_in, grad_w_mem_kv, grad_w_out, grad_rel_bias, grad_conv_w, grad_ln_g, grad_ln_b]
    weights = [w_in, w_mem_kv, w_out, rel_bias, conv_w, ln_g, ln_b]
    moms = [m_w_in, m_w_mem_kv, m_w_out, m_rel_bias, m_conv_w, m_ln_g, m_ln_b]
    vels = [v_w_in, v_w_mem_kv, v_w_out, v_rel_bias, v_conv_w, v_ln_g, v_ln_b]
    names = ["w_in", "w_mem_kv", "w_out", "rel_bias", "conv_w", "ln_g", "ln_b"]
    upd = [_adamw(w, g, m, v, f"adamw_{n}") for w, g, m, v, n in zip(weights, grads, moms, vels, names)]
    deltas, new_m, new_v = zip(*upd)
    return (loss, grad_x, *grads, *deltas, *new_m, *new_v)
```

```python
import functools
import math

import jax
import jax.numpy as jnp
from jax import lax
from jax.experimental import pallas as pl
from jax.experimental.pallas import tpu as pltpu

f32, bf16 = jnp.float32, jnp.bfloat16

D_MODEL = 1024
DEPTH = 4
CHUNK = 64
N_PREV = 8
N_HEADS = 16
HEAD_DIM = 64
E_MIX = 1024
REL_CLIP = 128
N_REL = 2 * REL_CLIP + 1
N_REL_PAD = 384
CONV_W = 3
N_MEM = 256
MEM_HEADS = 4
MEM_HEAD_DIM = 128
E_MEM = 512
E_BRANCH = E_MIX + E_MEM
N_IN = 3 * E_MIX + E_MEM + E_BRANCH
N_CHIPS = 4
W_IN_COLS = N_IN // N_CHIPS
W_KV_ROWS = D_MODEL // N_CHIPS
W_OUT_ROWS = E_BRANCH // N_CHIPS
DN_ALPHA = (2.0 * DEPTH) ** 0.25
LN_EPS = 1e-5
ADAM_LR, ADAM_B1, ADAM_B2, ADAM_EPS, ADAM_WD, ADAM_STEP = 0.001, 0.9, 0.999, 1e-08, 0.01, 10

QG = 4 * CHUNK
KG = QG + N_PREV * CHUNK
DB_COLS = KG // 2
NEG = -1e30
VMEM_LIMIT = 56 * 1024 * 1024

NT = (((1,), (1,)), ((), ()))
TN = (((0,), (0,)), ((), ()))
MESH = pl.DeviceIdType.MESH
ANY = pl.BlockSpec(memory_space=pl.ANY)


def _pcall(body, **kw):
    return pl.pallas_call(body, **kw)


def _params(*sem):
    return pltpu.CompilerParams(dimension_semantics=sem, vmem_limit_bytes=VMEM_LIMIT)


def _silu_parts(z):
    sig = 1.0 / (1.0 + jnp.exp(-z))
    return z * sig, sig


class _Comm:
    def __init__(self, inputs, out_shapes, aliases, n_sems, copies):
        self.inputs, self.out_shapes, self.aliases, self.n_sems, self.copies = inputs, out_shapes, aliases, n_sems, copies

    def start(self, cin, cout, send, recv):
        for cp in self.copies(cin, cout, send, recv)[0]:
            cp.start()

    def wait(self, cin, cout, send, recv):
        sends, recvs = self.copies(cin, cout, send, recv)
        for cp in recvs:
            cp.wait_recv()
        for cp in sends:
            cp.wait_send()


def _pcall_carry(body, comm, *, n_in, n_out, **kw):
    if comm is None:
        return lambda *args: (_pcall(body, **kw)(*args), ())
    grid = kw["grid"]
    k_in, k_out = len(comm.inputs), len(comm.out_shapes)

    def carried(*refs):
        ins, cin = refs[:n_in], refs[n_in:n_in + k_in]
        outs = refs[n_in + k_in:n_in + k_in + n_out]
        cout = refs[n_in + k_in + n_out:n_in + k_in + n_out + k_out]
        scratch, send, recv = refs[n_in + k_in + n_out + k_out:-2], refs[-2], refs[-1]
        ids = [pl.program_id(a) for a in range(len(grid))]
        first = functools.reduce(jnp.logical_and, [i == 0 for i in ids])
        last = functools.reduce(jnp.logical_and, [i == n - 1 for i, n in zip(ids, grid)])

        @pl.when(first)
        def _():
            comm.start(cin, cout, send, recv)

        body(*ins, *outs, *scratch)

        @pl.when(last)
        def _():
            comm.wait(cin, cout, send, recv)

    kw = dict(kw)
    kw["in_specs"] = list(kw["in_specs"]) + [ANY] * k_in
    kw["out_specs"] = tuple(kw["out_specs"]) + (ANY,) * k_out
    kw["out_shape"] = tuple(kw["out_shape"]) + tuple(comm.out_shapes)
    kw["scratch_shapes"] = list(kw.get("scratch_shapes", ())) + [pltpu.SemaphoreType.DMA((comm.n_sems,))] * 2
    aliases = dict(kw.get("input_output_aliases", {}))
    aliases.update({n_in + ci: n_out + co for ci, co in comm.aliases.items()})
    kw["input_output_aliases"] = aliases

    def run(*args):
        res = _pcall(carried, **kw)(*args, *comm.inputs)
        return res[:n_out], res[n_out:]

    return run


def _comm_call(comm, name):
    k_in = len(comm.inputs)

    def body(*refs):
        cin, cout, send, recv = refs[:k_in], refs[k_in:-2], refs[-2], refs[-1]
        comm.start(cin, cout, send, recv)
        comm.wait(cin, cout, send, recv)

    return _pcall(body, name=name, out_shape=tuple(comm.out_shapes), in_specs=[ANY] * k_in,
                  out_specs=(ANY,) * len(comm.out_shapes), input_output_aliases=dict(comm.aliases),
                  scratch_shapes=[pltpu.SemaphoreType.DMA((comm.n_sems,))] * 2)(*comm.inputs)


def _place():
    x, y, c = lax.axis_index("x"), lax.axis_index("y"), lax.axis_index("c")
    return x, y, c, 2 * x + y, (x, y, 1 - c), [(1 - x, y), (x, 1 - y), (1 - x, 1 - y)]


def _rcopy(send, recv, k, src, dst, to):
    return pltpu.make_async_remote_copy(src_ref=src, dst_ref=dst, send_sem=send.at[k], recv_sem=recv.at[k],
                                        device_id=to, device_id_type=MESH)


def _half(ref_rows, core):
    return pl.ds(core * (ref_rows // 2), ref_rows // 2)


def _gather_ici(shards, layer, extra=None):
    extras = [] if extra is None else [extra]
    n = len(shards)

    def copies(cin, cout, send, recv):
        x, y, c, me, sibling, chips = _place()
        sends, recvs = [], []
        for a in range(n):
            s, g = cin[a], cout[a]
            rows = s.shape[1]
            mine = _half(rows, c)
            sends.append(_rcopy(send, recv, 4 * a, s.at[layer], g.at[me], sibling))
            recvs.append(_rcopy(send, recv, 4 * a, s.at[layer], g.at[me], sibling))
            for p, (px, py) in enumerate(chips):
                sends.append(_rcopy(send, recv, 4 * a + 1 + p, s.at[layer, mine], g.at[me, mine], (px, py, c)))
                recvs.append(_rcopy(send, recv, 4 * a + 1 + p, s.at[layer, mine], g.at[2 * px + py, mine], (px, py, c)))
        for e in range(len(extras)):
            s, g = cin[n + e], cout[n + e]
            k = 4 * (n + e)
            sends.append(_rcopy(send, recv, k, s, g.at[me], sibling))
            recvs.append(_rcopy(send, recv, k, s, g.at[me], sibling))
            for p, (px, py) in enumerate(chips):
                sends.append(_rcopy(send, recv, k + 1 + p, s, g.at[me], (px, py, c)))
                recvs.append(_rcopy(send, recv, k + 1 + p, s, g.at[2 * px + py], (px, py, c)))
        return sends, recvs

    out_shapes = [jax.ShapeDtypeStruct((N_CHIPS,) + s.shape[1:], s.dtype) for s in shards]
    out_shapes += [jax.ShapeDtypeStruct((N_CHIPS,) + e.shape, e.dtype) for e in extras]
    return _Comm(list(shards) + extras, out_shapes, {}, 4 * (n + len(extras)), copies)


def _gather_d2d(gathered):
    n = len(gathered)

    def copies(cin, cout, send, recv):
        x, y, c, me, sibling, chips = _place()
        sends, recvs = [], []
        for a in range(n):
            g = cout[a]
            rows = g.shape[1]
            for p, (px, py) in enumerate(chips):
                mine, theirs = g.at[2 * px + py, _half(rows, c)], g.at[2 * px + py, _half(rows, 1 - c)]
                sends.append(_rcopy(send, recv, 3 * a + p, mine, mine, sibling))
                recvs.append(_rcopy(send, recv, 3 * a + p, theirs, theirs, sibling))
        return sends, recvs

    return _Comm(list(gathered), [jax.ShapeDtypeStruct(g.shape, g.dtype) for g in gathered],
                 {a: a for a in range(n)}, 3 * n, copies)


class _SemView:
    def __init__(self, ref, base):
        self.ref, self.base, self.at = ref, base, self

    def __getitem__(self, k):
        return self.ref.at[self.base + k]


def _both(a, b):
    ka, ma = len(a.inputs), len(a.out_shapes)

    def copies(cin, cout, send, recv):
        sa, ra = a.copies(cin[:ka], cout[:ma], send, recv)
        sb, rb = b.copies(cin[ka:], cout[ma:], _SemView(send, a.n_sems), _SemView(recv, a.n_sems))
        return sa + sb, ra + rb

    aliases = dict(a.aliases)
    aliases.update({ka + ci: ma + co for ci, co in b.aliases.items()})
    return _Comm(a.inputs + b.inputs, a.out_shapes + b.out_shapes, aliases, a.n_sems + b.n_sems, copies)


def _small_exchange(buf):
    def copies(cin, cout, send, recv):
        x, y, c, me, sibling, chips = _place()
        sends, recvs = [], []
        for r in range(1, 8):
            px, py, pc = x ^ ((r >> 2) & 1), y ^ ((r >> 1) & 1), c ^ (r & 1)
            sends.append(_rcopy(send, recv, r - 1, cin[0], cout[0].at[2 * me + c], (px, py, pc)))
            recvs.append(_rcopy(send, recv, r - 1, cin[0], cout[0].at[4 * px + 2 * py + pc], (px, py, pc)))
        return sends, recvs

    return _Comm([buf], [jax.ShapeDtypeStruct((8,) + buf.shape, buf.dtype)], {}, 7, copies)


def _small_sum(slots, buf, device_arr):
    rows, cols = buf.shape

    def body(d_ref, s_ref, b_ref, o_ref):
        d = pl.program_id(0)
        val = jnp.where(d == d_ref[0], b_ref[...], s_ref[...])

        @pl.when(d == 0)
        def _():
            o_ref[...] = val

        @pl.when(d > 0)
        def _():
            o_ref[...] += val

    return _pcall(
        body, name="small_sum", out_shape=jax.ShapeDtypeStruct((rows, cols), f32),
        grid_spec=pltpu.PrefetchScalarGridSpec(
            num_scalar_prefetch=1, grid=(8,),
            in_specs=[pl.BlockSpec((None, rows, cols), lambda d, d_ref: (jnp.where(d == d_ref[0], (d + 1) % 8, d), 0, 0)),
                      pl.BlockSpec((rows, cols), lambda d, d_ref: (0, 0))],
            out_specs=pl.BlockSpec((rows, cols), lambda d, d_ref: (0, 0))),
        compiler_params=_params("arbitrary"))(device_arr, slots, buf)


def _sibling_exchange(gs):
    def copies(cin, cout, send, recv):
        x, y, c, me, sibling, chips = _place()
        sends = [_rcopy(send, recv, a, g.at[:, _half(g.shape[1], 1 - c)], r, sibling) for a, (g, r) in enumerate(zip(cin, cout))]
        return sends, sends

    shapes = [jax.ShapeDtypeStruct((N_CHIPS, g.shape[1] // 2, g.shape[2]), g.dtype) for g in gs]
    return _Comm(list(gs), shapes, {}, len(gs), copies)


def _chip_scatter(ss):
    def copies(cin, cout, send, recv):
        x, y, c, me, sibling, chips = _place()
        sends = [_rcopy(send, recv, 3 * a + p, s.at[2 * px + py], r.at[p], (px, py, c))
                 for a, (s, r) in enumerate(zip(cin, cout)) for p, (px, py) in enumerate(chips)]
        return sends, sends

    shapes = [jax.ShapeDtypeStruct((3,) + s.shape[1:], s.dtype) for s in ss]
    return _Comm(list(ss), shapes, {}, 3 * len(ss), copies)


def _sibling_share(fs, layer):
    def copies(cin, cout, send, recv):
        x, y, c, me, sibling, chips = _place()
        sends, recvs = [], []
        for a, f in enumerate(cout):
            mine, theirs = f.at[layer, _half(f.shape[1], c)], f.at[layer, _half(f.shape[1], 1 - c)]
            sends.append(_rcopy(send, recv, a, mine, mine, sibling))
            recvs.append(_rcopy(send, recv, a, theirs, theirs, sibling))
        return sends, recvs

    return _Comm(list(fs), [jax.ShapeDtypeStruct(f.shape, f.dtype) for f in fs], {a: a for a in range(len(fs))},
                 len(fs), copies)


def _sum_rows(rows):
    return next(b for b in (256, 192, 128) if rows % b == 0)


def _pair_sum(g, r, core_arr, layer):
    _, rows, cols = r.shape
    br = _sum_rows(rows)
    nb = rows // br

    def body(c_ref, g_ref, r_ref, o_ref):
        o_ref[...] = (g_ref[...].astype(f32) + r_ref[...].astype(f32)).astype(bf16)

    return _pcall(
        body, name=f"pair_sum_{layer}", out_shape=jax.ShapeDtypeStruct(r.shape, bf16),
        grid_spec=pltpu.PrefetchScalarGridSpec(
            num_scalar_prefetch=1, grid=(N_CHIPS, nb),
            in_specs=[pl.BlockSpec((1, br, cols), lambda j, i, c_ref: (j, c_ref[0] * nb + i, 0)),
                      pl.BlockSpec((1, br, cols), lambda j, i, c_ref: (j, i, 0))],
            out_specs=pl.BlockSpec((1, br, cols), lambda j, i, c_ref: (j, i, 0))),
        compiler_params=_params("arbitrary", "arbitrary"))(core_arr, g, r)


def _chip_sum(s, r, place, layer, final):
    _, rows, cols = s.shape
    br = _sum_rows(rows)
    nb = rows // br

    def body(place_ref, s_ref, r_ref, *rest):
        o_ref = rest[-1]
        acc = s_ref[0].astype(f32)
        for p in range(3):
            acc = acc + r_ref[p].astype(f32)
        o_ref[...] = acc

    carried = [] if final is None else [final]
    return _pcall(
        body, name=f"chip_sum_{layer}", out_shape=jax.ShapeDtypeStruct((DEPTH, 2 * rows, cols), f32),
        grid_spec=pltpu.PrefetchScalarGridSpec(
            num_scalar_prefetch=1, grid=(nb,),
            in_specs=[pl.BlockSpec((1, br, cols), lambda i, place_ref: (place_ref[0], i, 0)),
                      pl.BlockSpec((3, br, cols), lambda i, place_ref: (0, i, 0))] + [ANY] * len(carried),
            out_specs=pl.BlockSpec((None, br, cols), lambda i, place_ref: (layer, place_ref[1] * nb + i, 0))),
        input_output_aliases={3: 0} if carried else {},
        compiler_params=_params("arbitrary"))(place, s, r, *carried)


def _inproj(xb, win, layer, comm=None):
    T = xb.shape[0]
    tm = 512

    def body(x_ref, w_ref, o_ref):
        xt = x_ref[...]
        for j in range(N_CHIPS):
            o_ref[:, j * W_IN_COLS:(j + 1) * W_IN_COLS] = jnp.dot(xt, w_ref[j], preferred_element_type=f32).astype(bf16)

    (h,), carried = _pcall_carry(
        body, comm, n_in=2, n_out=1, name=f"inproj_{layer}", grid=(T // tm,),
        out_shape=(jax.ShapeDtypeStruct((T, N_IN), bf16),),
        in_specs=[pl.BlockSpec((tm, D_MODEL), lambda i: (i, 0)),
                  pl.BlockSpec((N_CHIPS, D_MODEL, W_IN_COLS), lambda i: (0, 0, 0), pipeline_mode=pl.Buffered(1))],
        out_specs=(pl.BlockSpec((tm, N_IN), lambda i: (i, 0)),),
        compiler_params=_params("arbitrary"))(xb, win)
    return h, carried


def _rel_index_rows():
    j = lax.broadcasted_iota(jnp.int32, (N_REL_PAD, KG), 1)
    r = lax.broadcasted_iota(jnp.int32, (N_REL_PAD, KG), 0)
    off = jnp.where(j < KG - 2 * CHUNK, j, j - KG)
    idx = jnp.clip(N_PREV * CHUNK - off, -REL_CLIP, REL_CLIP) + REL_CLIP
    return (idx == r).astype(f32)


def _bias_expand(table_pad, layer):
    def body(t_ref, o_ref, row_scr):
        h = pl.program_id(0)

        @pl.when(h == 0)
        def _():
            row_scr[...] = jnp.dot(t_ref[...], _rel_index_rows(), precision=lax.Precision.HIGHEST,
                                   preferred_element_type=f32)

        q = lax.broadcasted_iota(jnp.int32, (QG, KG), 0)
        k = lax.broadcasted_iota(jnp.int32, (QG, KG), 1)
        band = (k // CHUNK >= q // CHUNK) & (k // CHUNK <= q // CHUNK + N_PREV)
        t = jnp.broadcast_to(row_scr[pl.ds(h, 1), :], (QG, KG))
        for b in range(8):
            t = jnp.where(((q >> b) & 1) == 1, pltpu.roll(t, 1 << b, axis=1), t)
        for v in range(3):
            o_ref[v] = jnp.where(band & (k >= (2 - v) * QG), t, NEG)

    return _pcall(body, name=f"bias_expand_{layer}", grid=(N_HEADS,),
                  out_shape=jax.ShapeDtypeStruct((3, N_HEADS, QG, KG), f32),
                  in_specs=[pl.BlockSpec((N_HEADS, N_REL_PAD), lambda h: (0, 0))],
                  out_specs=pl.BlockSpec((3, None, QG, KG), lambda h: (0, h, 0, 0)),
                  scratch_shapes=[pltpu.VMEM((N_HEADS, KG), f32)], compiler_params=_params("arbitrary"))(table_pad)


def _bias_reduce(dbias, layer):
    def body(d_ref, o_ref, row_scr):
        q = lax.broadcasted_iota(jnp.int32, (QG, DB_COLS), 0)
        k = lax.broadcasted_iota(jnp.int32, (QG, DB_COLS), 1)
        for h in range(N_HEADS):
            t = jnp.where(k > q, d_ref[h], 0.0)
            for b in range(8):
                t = jnp.where(((q >> b) & 1) == 1, pltpu.roll(t, DB_COLS - (1 << b), axis=1), t)
            row_scr[h:h + 1, :] = jnp.sum(t, axis=0, keepdims=True)
        r = lax.broadcasted_iota(jnp.int32, (N_REL_PAD, DB_COLS), 0)
        off = lax.broadcasted_iota(jnp.int32, (N_REL_PAD, DB_COLS), 1)
        own = (off >= 1) & (off < REL_CLIP + CHUNK)
        sel = jnp.where(own & (r == 2 * REL_CLIP - off), 1.0, 0.0) - jnp.where(own & (r == 2 * REL_CLIP), 1.0, 0.0)
        o_ref[...] = lax.dot_general(row_scr[...], sel, NT, precision=lax.Precision.HIGHEST, preferred_element_type=f32)

    return _pcall(body, name=f"bias_reduce_{layer}", out_shape=jax.ShapeDtypeStruct((N_HEADS, N_REL_PAD), f32),
                  scratch_shapes=[pltpu.VMEM((N_HEADS, DB_COLS), f32)],
                  compiler_params=pltpu.CompilerParams(vmem_limit_bytes=VMEM_LIMIT))(dbias)


FWD_PAIRS = 8
BWD_PAIRS = 4


def _key_specs(n_groups, npairs, slab):
    per_slab = E_MIX // (128 * npairs)
    return [pl.BlockSpec((QG, 128 * npairs), functools.partial(
        lambda hp, g, jj: (jnp.clip(g - 2 + jj, 0, n_groups - 1), slab * per_slab + hp), jj=jj)) for jj in range(3)]


def _bias_spec(npairs):
    return pl.BlockSpec((None, 2 * npairs, QG, KG), lambda hp, g: (jnp.minimum(g, 2), hp, 0, 0))


def _attn_fwd(h, bias, layer, comm=None):
    T = h.shape[0]
    n_groups = T // QG
    scale = 1.0 / math.sqrt(HEAD_DIM)

    def body(q_ref, k0, k1, k2, v0, v1, v2, b_ref, o_ref, lse_ref):
        lane = lax.broadcasted_iota(jnp.int32, (1, 128), 1)
        ones = jnp.ones((KG, 128), bf16)
        lse = jnp.zeros((QG, 128), f32)
        for pp in range(FWD_PAIRS):
            cs = slice(pp * 128, (pp + 1) * 128)
            q2 = q_ref[:, cs] * scale
            kc = jnp.concatenate([k0[:, cs], k1[:, cs], k2[:, cs]], axis=0)
            vc = jnp.concatenate([jnp.concatenate([v0[:, cs], v1[:, cs], v2[:, cs]], axis=0), ones], axis=1)
            outs = []
            for hh in range(2):
                qm = jnp.where(lane // HEAD_DIM == hh, q2, jnp.zeros_like(q2))
                s = lax.dot_general(qm, kc, NT, preferred_element_type=f32) + b_ref[2 * pp + hh]
                m = jnp.max(s, axis=1, keepdims=True)
                ol = jnp.dot(jnp.exp(s - m).astype(bf16), vc, preferred_element_type=f32)
                outs.append(ol[:, :128] / ol[:, 128:])
                lse = jnp.where(lane == 2 * pp + hh, m + jnp.log(ol[:, 128:]), lse)
            o_ref[:, cs] = jnp.where(lane // HEAD_DIM == 0, outs[0], outs[1]).astype(bf16)
        lse_ref[...] = lse

    (mix, lse), carried = _pcall_carry(
        body, comm, n_in=8, n_out=2, name=f"attn_fwd_{layer}", grid=(N_HEADS // (2 * FWD_PAIRS), n_groups),
        out_shape=(jax.ShapeDtypeStruct((T, E_MIX), bf16), jax.ShapeDtypeStruct((T, 128), f32)),
        in_specs=[pl.BlockSpec((QG, 128 * FWD_PAIRS), lambda hp, g: (g, hp))] + _key_specs(n_groups, FWD_PAIRS, 1)
        + _key_specs(n_groups, FWD_PAIRS, 2) + [_bias_spec(FWD_PAIRS)],
        out_specs=(pl.BlockSpec((QG, 128 * FWD_PAIRS), lambda hp, g: (g, hp)), pl.BlockSpec((QG, 128), lambda hp, g: (g, 0))),
        compiler_params=_params("arbitrary", "arbitrary"))(h, h, h, h, h, h, h, bias)
    return mix, lse, carried


def _halo_rows(ref, r):
    return ref[r:r + 1, :].astype(f32)


def _conv_taps(cu, p6, p7, w_ref):
    row = lax.broadcasted_iota(jnp.int32, cu.shape, 0)
    r1 = jnp.where(row == 0, p7, pltpu.roll(cu, 1, axis=0))
    r2 = jnp.where(row == 0, p6, jnp.where(row == 1, p7, pltpu.roll(cu, 2, axis=0)))
    return w_ref[2:3, :] * cu + w_ref[1:2, :] * r1 + w_ref[0:1, :] * r2, r1, r2


def _conv_fwd(h, w, layer, comm=None):
    T = h.shape[0]
    tm = 512

    def body(bg_ref, cg_ref, u_ref, cgp_ref, up_ref, w_ref, o_ref):
        first = (pl.program_id(0) == 0).astype(f32)
        cu = cg_ref[...].astype(f32) * u_ref[...].astype(f32)
        p6 = _halo_rows(cgp_ref, 14) * _halo_rows(up_ref, 14) * (1.0 - first)
        p7 = _halo_rows(cgp_ref, 15) * _halo_rows(up_ref, 15) * (1.0 - first)
        conv, _, _ = _conv_taps(cu, p6, p7, w_ref)
        o_ref[...] = (bg_ref[...].astype(f32) * conv).astype(bf16)

    prev = lambda slab: pl.BlockSpec((16, E_MIX), lambda i: (jnp.maximum(i * (tm // 16) - 1, 0), slab))
    (mix,), carried = _pcall_carry(
        body, comm, n_in=6, n_out=1, name=f"conv_fwd_{layer}", grid=(T // tm,),
        out_shape=(jax.ShapeDtypeStruct((T, E_MIX), bf16),),
        in_specs=[pl.BlockSpec((tm, E_MIX), lambda i: (i, 0)), pl.BlockSpec((tm, E_MIX), lambda i: (i, 1)),
                  pl.BlockSpec((tm, E_MIX), lambda i: (i, 2)), prev(1), prev(2),
                  pl.BlockSpec((CONV_W, E_MIX), lambda i: (0, 0))],
        out_specs=(pl.BlockSpec((tm, E_MIX), lambda i: (i, 0)),),
        compiler_params=_params("arbitrary"))(h, h, h, h, h, w)
    return mix, carried


def _kv_mem(memb, wkv):
    def body(m_ref, w_ref, o_ref):
        o_ref[...] = jnp.dot(m_ref[...], w_ref[...], preferred_element_type=f32).astype(bf16)

    return _pcall(body, name="kv_mem", out_shape=jax.ShapeDtypeStruct((N_MEM, 2 * E_MEM), bf16),
                  compiler_params=pltpu.CompilerParams(vmem_limit_bytes=VMEM_LIMIT))(memb, wkv)


def _mem_probs(qm_ref, kv_ref, hh):
    qh = qm_ref[:, hh * MEM_HEAD_DIM:(hh + 1) * MEM_HEAD_DIM]
    kh = kv_ref[:, hh * MEM_HEAD_DIM:(hh + 1) * MEM_HEAD_DIM]
    vh = kv_ref[:, E_MEM + hh * MEM_HEAD_DIM:E_MEM + (hh + 1) * MEM_HEAD_DIM]
    s = lax.dot_general(qh, kh, NT, preferred_element_type=f32) * (1.0 / math.sqrt(MEM_HEAD_DIM))
    e = jnp.exp(s - jnp.max(s, axis=1, keepdims=True))
    return e / jnp.sum(e, axis=1, keepdims=True), qh, kh, vh


def _h_tail_specs(tm):
    return [pl.BlockSpec((tm, E_MEM), functools.partial(lambda i, cb: (i, cb), cb=cb)) for cb in (6, 7, 8, 9)]


def _post_fwd(h, mix, kv, wout, x, g, b, layer, target=None):
    T = x.shape[0]
    tm = 512

    def body(qm_ref, z0, z1, z2, mix_ref, kv_ref, w_ref, x_ref, g_ref, b_ref, *rest):
        mem = jnp.concatenate(
            [jnp.dot(_mem_probs(qm_ref, kv_ref, hh)[0].astype(bf16), kv_ref[:, E_MEM + hh * MEM_HEAD_DIM:E_MEM + (hh + 1) * MEM_HEAD_DIM],
                     preferred_element_type=f32) for hh in range(MEM_HEADS)], axis=1)
        z = jnp.concatenate([z0[...], z1[...], z2[...]], axis=1).astype(f32)
        act, _ = _silu_parts(z)
        y = jnp.concatenate([mix_ref[...].astype(f32), mem], axis=1) * act
        out = jnp.dot(y.astype(bf16), w_ref[...], preferred_element_type=f32)
        r = DN_ALPHA * x_ref[...] + out
        mu = jnp.mean(r, axis=1, keepdims=True)
        var = jnp.mean(jnp.square(r - mu), axis=1, keepdims=True)
        rstd = lax.rsqrt(var + LN_EPS)
        xhat = (r - mu) * rstd
        xn = xhat * g_ref[...] + b_ref[...]
        if target is None:
            xn_ref, xb_ref, xh_ref, rs_ref = rest
            xn_ref[...] = xn
            xb_ref[...] = xn.astype(bf16)
        else:
            t_ref, l_ref, d_ref, xh_ref, rs_ref = rest

            @pl.when(pl.program_id(0) == 0)
            def _():
                l_ref[...] = jnp.zeros_like(l_ref)

            err = xn - t_ref[...]
            d_ref[...] = err * (1.0 / D_MODEL)
            l_ref[...] += jnp.sum(jnp.square(err))
        xh_ref[...] = xhat
        rs_ref[...] = rstd

    tile = lambda w: pl.BlockSpec((tm, w), lambda i: (i, 0))
    const = lambda r, c: pl.BlockSpec((r, c), lambda i: (0, 0))
    in_specs = _h_tail_specs(tm) + [tile(E_MIX), const(N_MEM, 2 * E_MEM), const(E_BRANCH, D_MODEL), tile(D_MODEL),
                                    const(1, D_MODEL), const(1, D_MODEL)]
    stats = (jax.ShapeDtypeStruct((T, D_MODEL), f32), jax.ShapeDtypeStruct((T, 1), f32))
    if target is None:
        return _pcall(
            body, name=f"post_fwd_{layer}", grid=(T // tm,),
            out_shape=(jax.ShapeDtypeStruct((T, D_MODEL), f32), jax.ShapeDtypeStruct((T, D_MODEL), bf16)) + stats,
            in_specs=in_specs, out_specs=(tile(D_MODEL), tile(D_MODEL), tile(D_MODEL), tile(1)),
            compiler_params=_params("arbitrary"))(h, h, h, h, mix, kv, wout, x, g, b)
    return _pcall(
        body, name=f"post_fwd_loss_{layer}", grid=(T // tm,),
        out_shape=(jax.ShapeDtypeStruct((8, 128), f32), jax.ShapeDtypeStruct((T, D_MODEL), f32)) + stats,
        in_specs=in_specs + [tile(D_MODEL)], out_specs=(const(8, 128), tile(D_MODEL), tile(D_MODEL), tile(1)),
        compiler_params=_params("arbitrary"))(h, h, h, h, mix, kv, wout, x, g, b, target)


def _post_bwd(dxn, xhat, rstd, g, h, mix, kv, wout, layer, comm=None):
    T = dxn.shape[0]
    tm = 256
    n_tiles = T // tm
    inv = 1.0 / math.sqrt(MEM_HEAD_DIM)

    def body(dxn_ref, xh_ref, rs_ref, g_ref, qm_ref, z0, z1, z2, mix_ref, kv_ref, w_ref,
             dxp_ref, dhb_ref, dmix_ref, dkv_ref, dwo_out, dg_ref, db_ref, dwo_ref):
        @pl.when(pl.program_id(0) == 0)
        def _():
            dkv_ref[...] = jnp.zeros_like(dkv_ref)
            dwo_ref[...] = jnp.zeros_like(dwo_ref)
            dg_ref[...] = jnp.zeros_like(dg_ref)
            db_ref[...] = jnp.zeros_like(db_ref)

        dy, xh = dxn_ref[...], xh_ref[...]
        dg_ref[0:1, :] += jnp.sum(dy * xh, axis=0, keepdims=True)
        db_ref[0:1, :] += jnp.sum(dy, axis=0, keepdims=True)
        gx = dy * g_ref[...]
        dr = rs_ref[...] * (gx - jnp.mean(gx, axis=1, keepdims=True) - xh * jnp.mean(gx * xh, axis=1, keepdims=True))
        dxp_ref[...] = DN_ALPHA * dr
        dob = dr.astype(bf16)

        probs = [_mem_probs(qm_ref, kv_ref, hh) for hh in range(MEM_HEADS)]
        mem = jnp.concatenate([jnp.dot(p.astype(bf16), vh, preferred_element_type=f32) for p, _, _, vh in probs], axis=1)
        z = jnp.concatenate([z0[...], z1[...], z2[...]], axis=1).astype(f32)
        act, sig = _silu_parts(z)
        cat = jnp.concatenate([mix_ref[...].astype(f32), mem], axis=1)
        yb = (cat * act).astype(bf16)
        dwo_ref[...] += lax.dot_general(yb, dob, TN, preferred_element_type=f32)
        dyv = lax.dot_general(dob, w_ref[...], NT, preferred_element_type=f32)
        dz = dyv * cat * (sig * (1.0 + z * (1.0 - sig)))
        dcat = dyv * act
        dmix_ref[...] = dcat[:, :E_MIX].astype(bf16)
        dqs = []
        for hh, (p, qh, kh, vh) in enumerate(probs):
            dmem = dcat[:, E_MIX + hh * MEM_HEAD_DIM:E_MIX + (hh + 1) * MEM_HEAD_DIM].astype(bf16)
            dp = lax.dot_general(dmem, vh, NT, preferred_element_type=f32)
            ds = (p * (dp - jnp.sum(p * dp, axis=1, keepdims=True))).astype(bf16)
            dqs.append(jnp.dot(ds, kh, preferred_element_type=f32) * inv)
            dkv_ref[:, hh * MEM_HEAD_DIM:(hh + 1) * MEM_HEAD_DIM] += lax.dot_general(ds, qh, TN, preferred_element_type=f32) * inv
            dkv_ref[:, E_MEM + hh * MEM_HEAD_DIM:E_MEM + (hh + 1) * MEM_HEAD_DIM] += lax.dot_general(
                p.astype(bf16), dmem, TN, preferred_element_type=f32)
        dhb_ref[...] = jnp.concatenate(dqs + [dz], axis=1).astype(bf16)

        @pl.when(pl.program_id(0) == n_tiles - 1)
        def _():
            dwo_out[...] = dwo_ref[...].astype(bf16)

    tile = lambda w: pl.BlockSpec((tm, w), lambda i: (i, 0))
    const = lambda r, c: pl.BlockSpec((r, c), lambda i: (0, 0))
    return _pcall_carry(
        body, comm, n_in=11, n_out=7, name=f"post_bwd_{layer}", grid=(n_tiles,),
        out_shape=(jax.ShapeDtypeStruct((T, D_MODEL), f32), jax.ShapeDtypeStruct((T, E_MEM + E_BRANCH), bf16),
                   jax.ShapeDtypeStruct((T, E_MIX), bf16), jax.ShapeDtypeStruct((N_MEM, 2 * E_MEM), f32),
                   jax.ShapeDtypeStruct((E_BRANCH, D_MODEL), bf16), jax.ShapeDtypeStruct((8, D_MODEL), f32),
                   jax.ShapeDtypeStruct((8, D_MODEL), f32)),
        in_specs=[tile(D_MODEL), tile(D_MODEL), tile(1), const(1, D_MODEL)] + _h_tail_specs(tm)
        + [tile(E_MIX), const(N_MEM, 2 * E_MEM), const(E_BRANCH, D_MODEL)],
        out_specs=(tile(D_MODEL), tile(E_MEM + E_BRANCH), tile(E_MIX), const(N_MEM, 2 * E_MEM),
                   const(E_BRANCH, D_MODEL), const(8, D_MODEL), const(8, D_MODEL)),
        scratch_shapes=[pltpu.VMEM((E_BRANCH, D_MODEL), f32)],
        compiler_params=_params("arbitrary"))(dxn, xhat, rstd, g, h, h, h, h, mix, kv, wout)


def _attn_bwd(h, bias, dmix, lse, layer, comm=None):
    T = h.shape[0]
    n_groups = T // QG
    scale = 1.0 / math.sqrt(HEAD_DIM)

    def body(q_ref, k0, k1, k2, v0, v1, v2, do_ref, b_ref, lse_ref, dq_ref, dk_ref, dv_ref, db_ref, acck, accv):
        g = pl.program_id(1)

        @pl.when(g == 0)
        def _():
            acck[...] = jnp.zeros_like(acck)
            accv[...] = jnp.zeros_like(accv)
            db_ref[...] = jnp.zeros_like(db_ref)

        @pl.when(g < n_groups)
        def _():
            lane = lax.broadcasted_iota(jnp.int32, (1, 128), 1)
            first = lane // HEAD_DIM == 0
            for pp in range(BWD_PAIRS):
                cs = slice(pp * 128, (pp + 1) * 128)
                do2 = do_ref[:, cs]
                q2 = q_ref[:, cs] * scale
                kc = jnp.concatenate([k0[:, cs], k1[:, cs], k2[:, cs]], axis=0)
                vc = jnp.concatenate([v0[:, cs], v1[:, cs], v2[:, cs]], axis=0)
                dqs, dks, dvs = [], [], []
                for hh in range(2):
                    hm = lane // HEAD_DIM == hh
                    head = (pl.program_id(0) * BWD_PAIRS + pp) * 2 + hh
                    lse = jnp.sum(jnp.where(lane == head, lse_ref[...], 0.0), axis=1, keepdims=True)
                    qm = jnp.where(hm, q2, jnp.zeros_like(q2))
                    dom = jnp.where(hm, do2, jnp.zeros_like(do2))
                    s = lax.dot_general(qm, kc, NT, preferred_element_type=f32) + b_ref[2 * pp + hh]
                    p = jnp.exp(s - lse)
                    dp = lax.dot_general(dom, vc, NT, preferred_element_type=f32)
                    ds = p * (dp - jnp.sum(p * dp, axis=1, keepdims=True))
                    db_ref[2 * pp + hh] += ds[:, KG - DB_COLS:]
                    dsb, pb = ds.astype(bf16), p.astype(bf16)
                    dqs.append(jnp.dot(dsb, kc, preferred_element_type=f32) * scale)
                    dks.append(lax.dot_general(dsb, q2, TN, preferred_element_type=f32))
                    dvs.append(lax.dot_general(pb, do2, TN, preferred_element_type=f32))
                dq_ref[:, cs] = jnp.where(first, dqs[0], dqs[1]).astype(bf16)
                dkc = jnp.where(first, dks[0], dks[1])
                dvc = jnp.where(first, dvs[0], dvs[1])
                for jj in range(3):
                    slot = (g + 1 + jj) % 3
                    if jj == 2:
                        acck[slot, :, cs] = dkc[jj * QG:(jj + 1) * QG]
                        accv[slot, :, cs] = dvc[jj * QG:(jj + 1) * QG]
                    else:
                        acck[slot, :, cs] += dkc[jj * QG:(jj + 1) * QG]
                        accv[slot, :, cs] += dvc[jj * QG:(jj + 1) * QG]

        done = (g + 1) % 3
        dk_ref[...] = acck[done].astype(bf16)
        dv_ref[...] = accv[done].astype(bf16)

    last = n_groups - 1
    width = 128 * BWD_PAIRS
    qspec = pl.BlockSpec((QG, width), lambda hp, g: (jnp.minimum(g, last), hp))
    kout = pl.BlockSpec((QG, width), lambda hp, g: (jnp.clip(g - 2, 0, last), hp))
    dbspec = pl.BlockSpec((2 * BWD_PAIRS, QG, DB_COLS), lambda hp, g: (hp, 0, 0))
    lspec = pl.BlockSpec((QG, 128), lambda hp, g: (jnp.minimum(g, last), 0))
    return _pcall_carry(
        body, comm, n_in=10, n_out=4, name=f"attn_bwd_{layer}", grid=(N_HEADS // (2 * BWD_PAIRS), n_groups + 2),
        out_shape=(jax.ShapeDtypeStruct((T, E_MIX), bf16),) * 3 + (jax.ShapeDtypeStruct((N_HEADS, QG, DB_COLS), f32),),
        in_specs=[qspec] + _key_specs(n_groups, BWD_PAIRS, 1) + _key_specs(n_groups, BWD_PAIRS, 2)
        + [qspec, _bias_spec(BWD_PAIRS), lspec],
        out_specs=(qspec, kout, kout, dbspec),
        scratch_shapes=[pltpu.VMEM((3, QG, width), f32), pltpu.VMEM((3, QG, width), f32)],
        compiler_params=_params("arbitrary", "arbitrary"))(h, h, h, h, h, h, h, dmix, bias, lse)


def _conv_bwd(h, w, dmix, layer):
    T = h.shape[0]
    tm = 512
    n_tiles = T // tm

    def body(bg_ref, cg_ref, u_ref, cgp_ref, up_ref, dy_ref, bgn_ref, dyn_ref, w_ref, dbg_ref, dcg_ref, du_ref, dw_ref):
        i = pl.program_id(0)

        @pl.when(i == 0)
        def _():
            dw_ref[...] = jnp.zeros_like(dw_ref)

        first = (i == 0).astype(f32)
        final = (i == n_tiles - 1).astype(f32)
        bg, cg, u = bg_ref[...].astype(f32), cg_ref[...].astype(f32), u_ref[...].astype(f32)
        dy = dy_ref[...].astype(f32)
        cu = cg * u
        p6 = _halo_rows(cgp_ref, 14) * _halo_rows(up_ref, 14) * (1.0 - first)
        p7 = _halo_rows(cgp_ref, 15) * _halo_rows(up_ref, 15) * (1.0 - first)
        conv, r1, r2 = _conv_taps(cu, p6, p7, w_ref)
        dbg_ref[...] = (dy * conv).astype(bf16)
        dc = dy * bg
        n0 = _halo_rows(dyn_ref, 0) * _halo_rows(bgn_ref, 0) * (1.0 - final)
        n1 = _halo_rows(dyn_ref, 1) * _halo_rows(bgn_ref, 1) * (1.0 - final)
        row = lax.broadcasted_iota(jnp.int32, dc.shape, 0)
        f1 = jnp.where(row == tm - 1, n0, pltpu.roll(dc, tm - 1, axis=0))
        f2 = jnp.where(row == tm - 2, n0, jnp.where(row == tm - 1, n1, pltpu.roll(dc, tm - 2, axis=0)))
        dcu = w_ref[2:3, :] * dc + w_ref[1:2, :] * f1 + w_ref[0:1, :] * f2
        dcg_ref[...] = (dcu * u).astype(bf16)
        du_ref[...] = (dcu * cg).astype(bf16)
        dw_ref[0:1, :] += jnp.sum(dc * r2, axis=0, keepdims=True)
        dw_ref[1:2, :] += jnp.sum(dc * r1, axis=0, keepdims=True)
        dw_ref[2:3, :] += jnp.sum(dc * cu, axis=0, keepdims=True)

    tile = lambda slab: pl.BlockSpec((tm, E_MIX), lambda i: (i, slab))
    prev = lambda slab: pl.BlockSpec((16, E_MIX), lambda i: (jnp.maximum(i * (tm // 16) - 1, 0), slab))
    nxt = lambda slab: pl.BlockSpec((16, E_MIX), lambda i: (jnp.minimum((i + 1) * (tm // 16), T // 16 - 1), slab))
    return _pcall(
        body, name=f"conv_bwd_{layer}", grid=(n_tiles,),
        out_shape=(jax.ShapeDtypeStruct((T, E_MIX), bf16),) * 3 + (jax.ShapeDtypeStruct((8, E_MIX), f32),),
        in_specs=[tile(0), tile(1), tile(2), prev(1), prev(2), tile(0), nxt(0), nxt(0),
                  pl.BlockSpec((CONV_W, E_MIX), lambda i: (0, 0))],
        out_specs=(tile(0), tile(0), tile(0), pl.BlockSpec((8, E_MIX), lambda i: (0, 0))),
        compiler_params=_params("arbitrary"))(h, h, h, h, h, dmix, h, dmix, w)


def _inproj_bwd_dx(da, db, dc, dhb, dxp, win, layer, comm=None):
    T = dxp.shape[0]
    tm = 512

    def body(da_ref, db_ref, dc_ref, dhb_ref, dxp_ref, w_ref, o_ref):
        dh = jnp.concatenate([da_ref[...], db_ref[...], dc_ref[...], dhb_ref[...]], axis=1)
        acc = dxp_ref[...]
        for j in range(N_CHIPS):
            acc = acc + lax.dot_general(dh[:, j * W_IN_COLS:(j + 1) * W_IN_COLS], w_ref[j], NT, preferred_element_type=f32)
        o_ref[...] = acc

    tile = lambda w: pl.BlockSpec((tm, w), lambda i: (i, 0))
    (dx,), carried = _pcall_carry(
        body, comm, n_in=6, n_out=1, name=f"inproj_bwd_dx_{layer}", grid=(T // tm,),
        out_shape=(jax.ShapeDtypeStruct((T, D_MODEL), f32),),
        in_specs=[tile(E_MIX), tile(E_MIX), tile(E_MIX), tile(E_MEM + E_BRANCH), tile(D_MODEL),
                  pl.BlockSpec((N_CHIPS, D_MODEL, W_IN_COLS), lambda i: (0, 0, 0), pipeline_mode=pl.Buffered(1))],
        out_specs=(tile(D_MODEL),),
        compiler_params=_params("arbitrary"))(da, db, dc, dhb, dxp, win)
    return dx, carried


def _inproj_bwd_dw(da, db, dc, dhb, xb, layer, comm=None):
    T = xb.shape[0]
    tm = 512
    n_tiles = T // tm

    def body(da_ref, db_ref, dc_ref, dhb_ref, x_ref, o_ref, acc, stage, sem):
        i = pl.program_id(0)

        @pl.when(i == 0)
        def _():
            acc[...] = jnp.zeros_like(acc)

        dh = jnp.concatenate([da_ref[...], db_ref[...], dc_ref[...], dhb_ref[...]], axis=1)
        xt = x_ref[...]
        for j in range(N_CHIPS):
            acc[j] += lax.dot_general(xt, dh[:, j * W_IN_COLS:(j + 1) * W_IN_COLS], TN, preferred_element_type=f32)

        @pl.when(i == n_tiles - 1)
        def _():
            for j in range(N_CHIPS):
                stage[...] = acc[j].astype(bf16)
                cp = pltpu.make_async_copy(stage, o_ref.at[j], sem)
                cp.start()
                cp.wait()

    tile = lambda w: pl.BlockSpec((tm, w), lambda i: (i, 0))
    (dw,), carried = _pcall_carry(
        body, comm, n_in=5, n_out=1, name=f"inproj_bwd_dw_{layer}", grid=(n_tiles,),
        out_shape=(jax.ShapeDtypeStruct((N_CHIPS, D_MODEL, W_IN_COLS), bf16),),
        in_specs=[tile(E_MIX), tile(E_MIX), tile(E_MIX), tile(E_MEM + E_BRANCH), tile(D_MODEL)],
        out_specs=(ANY,),
        scratch_shapes=[pltpu.VMEM((N_CHIPS, D_MODEL, W_IN_COLS), f32), pltpu.VMEM((D_MODEL, W_IN_COLS), bf16),
                        pltpu.SemaphoreType.DMA],
        compiler_params=_params("arbitrary"))(da, db, dc, dhb, xb)
    return dw, carried


def _kv_mem_bwd(memb, dkv, layer):
    def body(m_ref, d_ref, o_ref):
        o_ref[...] = lax.dot_general(m_ref[...], d_ref[...].astype(bf16), TN, preferred_element_type=f32).astype(bf16)

    return _pcall(body, name=f"kv_mem_bwd_{layer}", out_shape=jax.ShapeDtypeStruct((D_MODEL, 2 * E_MEM), bf16),
                  compiler_params=pltpu.CompilerParams(vmem_limit_bytes=VMEM_LIMIT))(memb, dkv)


def _to_bf16(a, name):
    rows, cols = a.shape
    br = 512

    def body(a_ref, o_ref):
        o_ref[...] = a_ref[...].astype(bf16)

    return _pcall(body, name=name, grid=(rows // br,), out_shape=jax.ShapeDtypeStruct((rows, cols), bf16),
                  in_specs=[pl.BlockSpec((br, cols), lambda i: (i, 0))], out_specs=pl.BlockSpec((br, cols), lambda i: (i, 0)),
                  compiler_params=_params("arbitrary"))(a)


def _adamw(w, g, m, v, name):
    shape = w.shape
    cols = shape[-1]
    rows = w.size // cols
    args = [a.reshape(rows, cols) for a in (w, g, m, v)]
    br = 256 if rows % 256 == 0 and rows > 256 else rows

    def body(w_ref, g_ref, m_ref, v_ref, go_ref, d_ref, nm_ref, nv_ref):
        gg = g_ref[...]
        nm = ADAM_B1 * m_ref[...] + (1.0 - ADAM_B1) * gg
        nv = ADAM_B2 * v_ref[...] + (1.0 - ADAM_B2) * jnp.square(gg)
        m_hat = nm / (1.0 - ADAM_B1 ** ADAM_STEP)
        v_hat = nv / (1.0 - ADAM_B2 ** ADAM_STEP)
        go_ref[...] = gg
        d_ref[...] = -ADAM_LR * (m_hat / (jnp.sqrt(v_hat) + ADAM_EPS) + ADAM_WD * w_ref[...])
        nm_ref[...] = nm
        nv_ref[...] = nv

    spec = pl.BlockSpec((br, cols), lambda i: (i, 0))
    outs = _pcall(body, name=name, grid=(rows // br,), out_shape=(jax.ShapeDtypeStruct((rows, cols), f32),) * 4,
                  in_specs=[spec] * 4, out_specs=(spec,) * 4, compiler_params=_params("arbitrary"))(*args)
    return tuple(o.reshape(shape) for o in outs)


def kernel(x, mem, w_in, w_mem_kv, w_out, rel_bias, conv_w, ln_g, ln_b, loss_target, m_w_in, m_w_mem_kv, m_w_out, m_rel_bias, m_conv_w, m_ln_g, m_ln_b, v_w_in, v_w_mem_kv, v_w_out, v_rel_bias, v_conv_w, v_ln_g, v_ln_b):
    T = x.shape[1]
    x0 = x.reshape(T, D_MODEL)
    target = loss_target.reshape(T, D_MODEL)
    memb = mem.reshape(N_MEM, D_MODEL).astype(bf16)
    chip = 2 * lax.axis_index("x") + lax.axis_index("y")
    core = lax.axis_index("c")
    chip_arr = jnp.reshape(chip, (1,)).astype(jnp.int32)
    core_arr = jnp.reshape(core, (1,)).astype(jnp.int32)

    place = jnp.concatenate([chip_arr, core_arr])
    tables = jnp.pad(rel_bias, ((0, 0), (0, 0), (0, N_REL_PAD - N_REL)))

    shards = [w_in.astype(bf16), w_mem_kv.astype(bf16), w_out.astype(bf16)]
    *arrived, cw_g = _comm_call(_gather_ici(shards, 0, extra=conv_w), "gather_ici_0")
    gathered = {0: _comm_call(_gather_d2d(arrived), "gather_d2d_0")}
    conv_full = jnp.transpose(cw_g, (1, 2, 0, 3)).reshape(DEPTH // 2, CONV_W, E_MIX)

    xs, xbs, hs, mixes, kvs, xhats, rstds, biases, lses = [x0], [_to_bf16(x0, "cast_x")], [], [], [], [], [], {}, {}
    for layer in range(DEPTH):
        win, wkv, wout = gathered[layer]
        more = layer + 1 < DEPTH
        h, arrived = _inproj(xbs[layer], win, layer, _gather_ici(shards, layer + 1) if more else None)
        passing = _gather_d2d(list(arrived)) if more else None
        if layer % 2 == 0:
            biases[layer] = _bias_expand(tables[layer // 2], layer)
            mix, lses[layer], done = _attn_fwd(h, biases[layer], layer, passing)
        else:
            mix, done = _conv_fwd(h, conv_full[layer // 2], layer, passing)
        if more:
            gathered[layer + 1] = list(done)
        kv = _kv_mem(memb, wkv.reshape(D_MODEL, 2 * E_MEM))
        first, second, xhat, rstd = _post_fwd(h, mix, kv, wout.reshape(E_BRANCH, D_MODEL), xs[layer], ln_g[layer][None, :],
                                              ln_b[layer][None, :], layer, None if more else target)
        if more:
            xs.append(first); xbs.append(second)
        else:
            lsum, dx = first, second
        hs.append(h); mixes.append(mix); kvs.append(kv); xhats.append(xhat); rstds.append(rstd)

    loss = lax.psum(lsum[0, 0], ("x", "y", "c")) * (0.5 / D_MODEL)

    dgs, dbs, dconvs, dtables = [None] * DEPTH, [None] * DEPTH, [None] * (DEPTH // 2), [None] * ((DEPTH + 1) // 2)
    finals = [None, None, None]
    above = None
    for layer in reversed(range(DEPTH)):
        h = hs[layer]
        win, wkv, wout = gathered[layer]
        (dxp, dhb, dmix, dkv, dwo, dgs[layer], dbs[layer]), from_sibling = _post_bwd(
            dx, xhats[layer], rstds[layer], ln_g[layer][None, :], h, mixes[layer], kvs[layer], wout.reshape(E_BRANCH, D_MODEL),
            layer, _sibling_exchange(above) if above else None)
        sums = [_pair_sum(g, r, core_arr, layer + 1) for g, r in zip(above, from_sibling)] if above else None
        scatter = _chip_scatter(sums) if above else None
        if layer % 2 == 0:
            (da, db, dc, dbias), from_chips = _attn_bwd(h, biases[layer], dmix, lses[layer], layer, scatter)
            dtables[layer // 2] = _bias_reduce(dbias, layer)
        else:
            da, db, dc, dconvs[layer // 2] = _conv_bwd(h, conv_full[layer // 2], dmix, layer)
        if layer > 0:
            dx, landed = _inproj_bwd_dx(da, db, dc, dhb, dxp, win, layer, scatter if layer % 2 == 1 else None)
            from_chips = landed if layer % 2 == 1 else from_chips
        share = None
        if above:
            finals = [_chip_sum(s, r, place, layer + 1, f) for s, r, f in zip(sums, from_chips, finals)]
            share = _sibling_share(finals, layer + 1)
        if layer == 0:
            pad8 = lambda a: jnp.pad(a, ((0, 8 - a.shape[0]), (0, 0)))
            small_mine = jnp.concatenate(dgs + dbs + dconvs + [pad8(t.reshape(-1, D_MODEL)) for t in dtables], axis=0)
            share = _both(share, _small_exchange(small_mine))
        g_win, shared = _inproj_bwd_dw(da, db, dc, dhb, xbs[layer], layer, share)
        if layer == 0:
            *shared, small_slots = shared
        finals = list(shared) if above else finals
        above = [g_win, _kv_mem_bwd(memb, dkv, layer).reshape(N_CHIPS, W_KV_ROWS, 2 * E_MEM),
                 dwo.reshape(N_CHIPS, W_OUT_ROWS, D_MODEL)]
    from_sibling = _comm_call(_sibling_exchange(above), "sibling_exchange_0")
    sums = [_pair_sum(g, r, core_arr, 0) for g, r in zip(above, from_sibling)]
    dx, from_chips = _inproj_bwd_dx(da, db, dc, dhb, dxp, win, 0, _chip_scatter(sums))
    finals = [_chip_sum(s, r, place, 0, f) for s, r, f in zip(sums, from_chips, finals)]
    grad_w_in, grad_w_mem_kv, grad_w_out = _comm_call(_sibling_share(finals, 0), "sibling_share_0")
    grad_x = dx.reshape(1, T, D_MODEL)

    device_arr = jnp.reshape(2 * chip + core, (1,)).astype(jnp.int32)
    small = _small_sum(small_slots, small_mine, device_arr)
    grad_ln_g = jnp.stack([small[8 * l] for l in range(DEPTH)])
    grad_ln_b = jnp.stack([small[8 * (DEPTH + l)] for l in range(DEPTH)])
    conv_all = jnp.stack([small[8 * (2 * DEPTH + a):8 * (2 * DEPTH + a) + CONV_W] for a in range(DEPTH // 2)])
    grad_conv_w = lax.dynamic_slice_in_dim(conv_all, chip * (E_MIX // N_CHIPS), E_MIX // N_CHIPS, axis=2)
    t0 = 8 * (2 * DEPTH + DEPTH // 2)
    grad_rel_bias = jnp.stack([small[t0 + 8 * a:t0 + 8 * a + 6].reshape(N_HEADS, N_REL_PAD)[:, :N_REL]
                               for a in range((DEPTH + 1) // 2)])

    grads = [grad_w_in, grad_w_mem_kv, grad_w_out, grad_rel_bias, grad_conv_w, grad_ln_g, grad_ln_b]
    weights = [w_in, w_mem_kv, w_out, rel_bias, conv_w, ln_g, ln_b]
    moms = [m_w_in, m_w_mem_kv, m_w_out, m_rel_bias, m_conv_w, m_ln_g, m_ln_b]
    vels = [v_w_in, v_w_mem_kv, v_w_out, v_rel_bias, v_conv_w, v_ln_g, v_ln_b]
    names = ["w_in", "w_mem_kv", "w_out", "rel_bias", "conv_w", "ln_g", "ln_b"]
    upd = [_adamw(w, g, m, v, f"adamw_{n}") for w, g, m, v, n in zip(weights, grads, moms, vels, names)]
    grads, deltas, new_m, new_v = zip(*upd)
    return (loss, grad_x, *grads, *deltas, *new_m, *new_v)
```

```python
import functools
import math

import jax
import jax.numpy as jnp
from jax import lax
from jax.experimental import pallas as pl
from jax.experimental.pallas import tpu as pltpu

f32, bf16 = jnp.float32, jnp.bfloat16

D_MODEL = 1024
DEPTH = 4
CHUNK = 64
N_PREV = 8
N_HEADS = 16
HEAD_DIM = 64
E_MIX = 1024
REL_CLIP = 128
N_REL = 2 * REL_CLIP + 1
N_REL_PAD = 384
CONV_W = 3
N_MEM = 256
MEM_HEADS = 4
MEM_HEAD_DIM = 128
E_MEM = 512
E_BRANCH = E_MIX + E_MEM
N_IN = 3 * E_MIX + E_MEM + E_BRANCH
N_CHIPS = 4
W_IN_COLS = N_IN // N_CHIPS
W_KV_ROWS = D_MODEL // N_CHIPS
W_OUT_ROWS = E_BRANCH // N_CHIPS
DN_ALPHA = (2.0 * DEPTH) ** 0.25
LN_EPS = 1e-5
ADAM_LR, ADAM_B1, ADAM_B2, ADAM_EPS, ADAM_WD, ADAM_STEP = 0.001, 0.9, 0.999, 1e-08, 0.01, 10

QG = 4 * CHUNK
KG = QG + N_PREV * CHUNK
DB_COLS = KG // 2
NEG = -1e30
VMEM_LIMIT = 56 * 1024 * 1024

NT = (((1,), (1,)), ((), ()))
TN = (((0,), (0,)), ((), ()))
MESH = pl.DeviceIdType.MESH
ANY = pl.BlockSpec(memory_space=pl.ANY)


def _pcall(body, **kw):
    return pl.pallas_call(body, **kw)


def _params(*sem):
    return pltpu.CompilerParams(dimension_semantics=sem, vmem_limit_bytes=VMEM_LIMIT)


def _silu_parts(z):
    sig = 1.0 / (1.0 + jnp.exp(-z))
    return z * sig, sig


class _Comm:
    def __init__(self, inputs, out_shapes, aliases, n_sems, copies):
        self.inputs, self.out_shapes, self.aliases, self.n_sems, self.copies = inputs, out_shapes, aliases, n_sems, copies

    def start(self, cin, cout, send, recv):
        for cp in self.copies(cin, cout, send, recv)[0]:
            cp.start()

    def wait(self, cin, cout, send, recv):
        sends, recvs = self.copies(cin, cout, send, recv)
        for cp in recvs:
            cp.wait_recv()
        for cp in sends:
            cp.wait_send()


def _pcall_carry(body, comm, *, n_in, n_out, **kw):
    if comm is None:
        return lambda *args: (_pcall(body, **kw)(*args), ())
    grid = kw["grid"]
    k_in, k_out = len(comm.inputs), len(comm.out_shapes)

    def carried(*refs):
        ins, cin = refs[:n_in], refs[n_in:n_in + k_in]
        outs = refs[n_in + k_in:n_in + k_in + n_out]
        cout = refs[n_in + k_in + n_out:n_in + k_in + n_out + k_out]
        scratch, send, recv = refs[n_in + k_in + n_out + k_out:-2], refs[-2], refs[-1]
        ids = [pl.program_id(a) for a in range(len(grid))]
        first = functools.reduce(jnp.logical_and, [i == 0 for i in ids])
        last = functools.reduce(jnp.logical_and, [i == n - 1 for i, n in zip(ids, grid)])

        @pl.when(first)
        def _():
            comm.start(cin, cout, send, recv)

        body(*ins, *outs, *scratch)

        @pl.when(last)
        def _():
            comm.wait(cin, cout, send, recv)

    kw = dict(kw)
    kw["in_specs"] = list(kw["in_specs"]) + [ANY] * k_in
    kw["out_specs"] = tuple(kw["out_specs"]) + (ANY,) * k_out
    kw["out_shape"] = tuple(kw["out_shape"]) + tuple(comm.out_shapes)
    kw["scratch_shapes"] = list(kw.get("scratch_shapes", ())) + [pltpu.SemaphoreType.DMA((comm.n_sems,))] * 2
    aliases = dict(kw.get("input_output_aliases", {}))
    aliases.update({n_in + ci: n_out + co for ci, co in comm.aliases.items()})
    kw["input_output_aliases"] = aliases

    def run(*args):
        res = _pcall(carried, **kw)(*args, *comm.inputs)
        return res[:n_out], res[n_out:]

    return run


def _comm_call(comm, name):
    k_in = len(comm.inputs)

    def body(*refs):
        cin, cout, send, recv = refs[:k_in], refs[k_in:-2], refs[-2], refs[-1]
        comm.start(cin, cout, send, recv)
        comm.wait(cin, cout, send, recv)

    return _pcall(body, name=name, out_shape=tuple(comm.out_shapes), in_specs=[ANY] * k_in,
                  out_specs=(ANY,) * len(comm.out_shapes), input_output_aliases=dict(comm.aliases),
                  scratch_shapes=[pltpu.SemaphoreType.DMA((comm.n_sems,))] * 2)(*comm.inputs)


def _place():
    x, y, c = lax.axis_index("x"), lax.axis_index("y"), lax.axis_index("c")
    return x, y, c, 2 * x + y, (x, y, 1 - c), [(1 - x, y), (x, 1 - y), (1 - x, 1 - y)]


def _rcopy(send, recv, k, src, dst, to):
    return pltpu.make_async_remote_copy(src_ref=src, dst_ref=dst, send_sem=send.at[k], recv_sem=recv.at[k],
                                        device_id=to, device_id_type=MESH)


def _half(ref_rows, core):
    return pl.ds(core * (ref_rows // 2), ref_rows // 2)


def _gather_ici(shards, layer, extra=None):
    extras = [] if extra is None else [extra]
    n = len(shards)

    def copies(cin, cout, send, recv):
        x, y, c, me, sibling, chips = _place()
        sends, recvs = [], []
        for a in range(n):
            s, g = cin[a], cout[a]
            rows = s.shape[1]
            mine = _half(rows, c)
            sends.append(_rcopy(send, recv, 4 * a, s.at[layer], g.at[me], sibling))
            recvs.append(_rcopy(send, recv, 4 * a, s.at[layer], g.at[me], sibling))
            for p, (px, py) in enumerate(chips):
                sends.append(_rcopy(send, recv, 4 * a + 1 + p, s.at[layer, mine], g.at[me, mine], (px, py, c)))
                recvs.append(_rcopy(send, recv, 4 * a + 1 + p, s.at[layer, mine], g.at[2 * px + py, mine], (px, py, c)))
        for e in range(len(extras)):
            s, g = cin[n + e], cout[n + e]
            k = 4 * (n + e)
            sends.append(_rcopy(send, recv, k, s, g.at[me], sibling))
            recvs.append(_rcopy(send, recv, k, s, g.at[me], sibling))
            for p, (px, py) in enumerate(chips):
                sends.append(_rcopy(send, recv, k + 1 + p, s, g.at[me], (px, py, c)))
                recvs.append(_rcopy(send, recv, k + 1 + p, s, g.at[2 * px + py], (px, py, c)))
        return sends, recvs

    out_shapes = [jax.ShapeDtypeStruct((N_CHIPS,) + s.shape[1:], s.dtype) for s in shards]
    out_shapes += [jax.ShapeDtypeStruct((N_CHIPS,) + e.shape, e.dtype) for e in extras]
    return _Comm(list(shards) + extras, out_shapes, {}, 4 * (n + len(extras)), copies)


def _gather_d2d(gathered):
    n = len(gathered)

    def copies(cin, cout, send, recv):
        x, y, c, me, sibling, chips = _place()
        sends, recvs = [], []
        for a in range(n):
            g = cout[a]
            rows = g.shape[1]
            for p, (px, py) in enumerate(chips):
                mine, theirs = g.at[2 * px + py, _half(rows, c)], g.at[2 * px + py, _half(rows, 1 - c)]
                sends.append(_rcopy(send, recv, 3 * a + p, mine, mine, sibling))
                recvs.append(_rcopy(send, recv, 3 * a + p, theirs, theirs, sibling))
        return sends, recvs

    return _Comm(list(gathered), [jax.ShapeDtypeStruct(g.shape, g.dtype) for g in gathered],
                 {a: a for a in range(n)}, 3 * n, copies)


class _SemView:
    def __init__(self, ref, base):
        self.ref, self.base, self.at = ref, base, self

    def __getitem__(self, k):
        return self.ref.at[self.base + k]


def _both(a, b):
    ka, ma = len(a.inputs), len(a.out_shapes)

    def copies(cin, cout, send, recv):
        sa, ra = a.copies(cin[:ka], cout[:ma], send, recv)
        sb, rb = b.copies(cin[ka:], cout[ma:], _SemView(send, a.n_sems), _SemView(recv, a.n_sems))
        return sa + sb, ra + rb

    aliases = dict(a.aliases)
    aliases.update({ka + ci: ma + co for ci, co in b.aliases.items()})
    return _Comm(a.inputs + b.inputs, a.out_shapes + b.out_shapes, aliases, a.n_sems + b.n_sems, copies)


def _small_exchange(buf):
    def copies(cin, cout, send, recv):
        x, y, c, me, sibling, chips = _place()
        sends, recvs = [], []
        for r in range(1, 8):
            px, py, pc = x ^ ((r >> 2) & 1), y ^ ((r >> 1) & 1), c ^ (r & 1)
            sends.append(_rcopy(send, recv, r - 1, cin[0], cout[0].at[2 * me + c], (px, py, pc)))
            recvs.append(_rcopy(send, recv, r - 1, cin[0], cout[0].at[4 * px + 2 * py + pc], (px, py, pc)))
        return sends, recvs

    return _Comm([buf], [jax.ShapeDtypeStruct((8,) + buf.shape, buf.dtype)], {}, 7, copies)


def _small_sum(slots, buf, device_arr):
    rows, cols = buf.shape

    def body(d_ref, s_ref, b_ref, o_ref):
        d = pl.program_id(0)
        val = jnp.where(d == d_ref[0], b_ref[...], s_ref[...])

        @pl.when(d == 0)
        def _():
            o_ref[...] = val

        @pl.when(d > 0)
        def _():
            o_ref[...] += val

    return _pcall(
        body, name="small_sum", out_shape=jax.ShapeDtypeStruct((rows, cols), f32),
        grid_spec=pltpu.PrefetchScalarGridSpec(
            num_scalar_prefetch=1, grid=(8,),
            in_specs=[pl.BlockSpec((None, rows, cols), lambda d, d_ref: (jnp.where(d == d_ref[0], (d + 1) % 8, d), 0, 0)),
                      pl.BlockSpec((rows, cols), lambda d, d_ref: (0, 0))],
            out_specs=pl.BlockSpec((rows, cols), lambda d, d_ref: (0, 0))),
        compiler_params=_params("arbitrary"))(device_arr, slots, buf)


def _sibling_exchange(gs):
    def copies(cin, cout, send, recv):
        x, y, c, me, sibling, chips = _place()
        sends = [_rcopy(send, recv, a, g.at[:, _half(g.shape[1], 1 - c)], r, sibling) for a, (g, r) in enumerate(zip(cin, cout))]
        return sends, sends

    shapes = [jax.ShapeDtypeStruct((N_CHIPS, g.shape[1] // 2, g.shape[2]), g.dtype) for g in gs]
    return _Comm(list(gs), shapes, {}, len(gs), copies)


def _chip_scatter(ss):
    def copies(cin, cout, send, recv):
        x, y, c, me, sibling, chips = _place()
        sends = [_rcopy(send, recv, 3 * a + p, s.at[2 * px + py], r.at[p], (px, py, c))
                 for a, (s, r) in enumerate(zip(cin, cout)) for p, (px, py) in enumerate(chips)]
        return sends, sends

    shapes = [jax.ShapeDtypeStruct((3,) + s.shape[1:], s.dtype) for s in ss]
    return _Comm(list(ss), shapes, {}, 3 * len(ss), copies)


def _sibling_share(fs, layer):
    def copies(cin, cout, send, recv):
        x, y, c, me, sibling, chips = _place()
        sends, recvs = [], []
        for a, f in enumerate(cout):
            mine, theirs = f.at[layer, _half(f.shape[1], c)], f.at[layer, _half(f.shape[1], 1 - c)]
            sends.append(_rcopy(send, recv, a, mine, mine, sibling))
            recvs.append(_rcopy(send, recv, a, theirs, theirs, sibling))
        return sends, recvs

    return _Comm(list(fs), [jax.ShapeDtypeStruct(f.shape, f.dtype) for f in fs], {a: a for a in range(len(fs))},
                 len(fs), copies)


def _sum_rows(rows):
    return next(b for b in (256, 192, 128) if rows % b == 0)


def _pair_sum(g, r, core_arr, layer):
    _, rows, cols = r.shape
    br = _sum_rows(rows)
    nb = rows // br

    def body(c_ref, g_ref, r_ref, o_ref):
        o_ref[...] = (g_ref[...].astype(f32) + r_ref[...].astype(f32)).astype(bf16)

    return _pcall(
        body, name=f"pair_sum_{layer}", out_shape=jax.ShapeDtypeStruct(r.shape, bf16),
        grid_spec=pltpu.PrefetchScalarGridSpec(
            num_scalar_prefetch=1, grid=(N_CHIPS, nb),
            in_specs=[pl.BlockSpec((1, br, cols), lambda j, i, c_ref: (j, c_ref[0] * nb + i, 0)),
                      pl.BlockSpec((1, br, cols), lambda j, i, c_ref: (j, i, 0))],
            out_specs=pl.BlockSpec((1, br, cols), lambda j, i, c_ref: (j, i, 0))),
        compiler_params=_params("arbitrary", "arbitrary"))(core_arr, g, r)


def _chip_sum(s, r, place, layer, final):
    _, rows, cols = s.shape
    br = _sum_rows(rows)
    nb = rows // br

    def body(place_ref, s_ref, r_ref, *rest):
        o_ref = rest[-1]
        acc = s_ref[0].astype(f32)
        for p in range(3):
            acc = acc + r_ref[p].astype(f32)
        o_ref[...] = acc

    carried = [] if final is None else [final]
    return _pcall(
        body, name=f"chip_sum_{layer}", out_shape=jax.ShapeDtypeStruct((DEPTH, 2 * rows, cols), f32),
        grid_spec=pltpu.PrefetchScalarGridSpec(
            num_scalar_prefetch=1, grid=(nb,),
            in_specs=[pl.BlockSpec((1, br, cols), lambda i, place_ref: (place_ref[0], i, 0)),
                      pl.BlockSpec((3, br, cols), lambda i, place_ref: (0, i, 0))] + [ANY] * len(carried),
            out_specs=pl.BlockSpec((None, br, cols), lambda i, place_ref: (layer, place_ref[1] * nb + i, 0))),
        input_output_aliases={3: 0} if carried else {},
        compiler_params=_params("arbitrary"))(place, s, r, *carried)


def _inproj(xb, win, layer, comm=None):
    T = xb.shape[0]
    tm = 512

    def body(x_ref, w_ref, o_ref):
        xt = x_ref[...]
        for j in range(N_CHIPS):
            o_ref[:, j * W_IN_COLS:(j + 1) * W_IN_COLS] = jnp.dot(xt, w_ref[j], preferred_element_type=f32).astype(bf16)

    (h,), carried = _pcall_carry(
        body, comm, n_in=2, n_out=1, name=f"inproj_{layer}", grid=(T // tm,),
        out_shape=(jax.ShapeDtypeStruct((T, N_IN), bf16),),
        in_specs=[pl.BlockSpec((tm, D_MODEL), lambda i: (i, 0)),
                  pl.BlockSpec((N_CHIPS, D_MODEL, W_IN_COLS), lambda i: (0, 0, 0), pipeline_mode=pl.Buffered(1))],
        out_specs=(pl.BlockSpec((tm, N_IN), lambda i: (i, 0)),),
        compiler_params=_params("arbitrary"))(xb, win)
    return h, carried


def _rel_index_rows():
    j = lax.broadcasted_iota(jnp.int32, (N_REL_PAD, KG), 1)
    r = lax.broadcasted_iota(jnp.int32, (N_REL_PAD, KG), 0)
    off = jnp.where(j < KG - 2 * CHUNK, j, j - KG)
    idx = jnp.clip(N_PREV * CHUNK - off, -REL_CLIP, REL_CLIP) + REL_CLIP
    return (idx == r).astype(f32)


def _bias_expand(table_pad, layer):
    def body(t_ref, o_ref, row_scr):
        h = pl.program_id(0)

        @pl.when(h == 0)
        def _():
            row_scr[...] = jnp.dot(t_ref[...], _rel_index_rows(), precision=lax.Precision.HIGHEST,
                                   preferred_element_type=f32)

        q = lax.broadcasted_iota(jnp.int32, (QG, KG), 0)
        k = lax.broadcasted_iota(jnp.int32, (QG, KG), 1)
        band = (k // CHUNK >= q // CHUNK) & (k // CHUNK <= q // CHUNK + N_PREV)
        t = jnp.broadcast_to(row_scr[pl.ds(h, 1), :], (QG, KG))
        for b in range(8):
            t = jnp.where(((q >> b) & 1) == 1, pltpu.roll(t, 1 << b, axis=1), t)
        for v in range(3):
            o_ref[v] = jnp.where(band & (k >= (2 - v) * QG), t, NEG)

    return _pcall(body, name=f"bias_expand_{layer}", grid=(N_HEADS,),
                  out_shape=jax.ShapeDtypeStruct((3, N_HEADS, QG, KG), f32),
                  in_specs=[pl.BlockSpec((N_HEADS, N_REL_PAD), lambda h: (0, 0))],
                  out_specs=pl.BlockSpec((3, None, QG, KG), lambda h: (0, h, 0, 0)),
                  scratch_shapes=[pltpu.VMEM((N_HEADS, KG), f32)], compiler_params=_params("arbitrary"))(table_pad)


def _bias_reduce(dbias, layer):
    def body(d_ref, o_ref, row_scr):
        q = lax.broadcasted_iota(jnp.int32, (QG, DB_COLS), 0)
        k = lax.broadcasted_iota(jnp.int32, (QG, DB_COLS), 1)
        for h in range(N_HEADS):
            t = jnp.where(k > q, d_ref[h], 0.0)
            for b in range(8):
                t = jnp.where(((q >> b) & 1) == 1, pltpu.roll(t, DB_COLS - (1 << b), axis=1), t)
            row_scr[h:h + 1, :] = jnp.sum(t, axis=0, keepdims=True)
        r = lax.broadcasted_iota(jnp.int32, (N_REL_PAD, DB_COLS), 0)
        off = lax.broadcasted_iota(jnp.int32, (N_REL_PAD, DB_COLS), 1)
        own = (off >= 1) & (off < REL_CLIP + CHUNK)
        sel = jnp.where(own & (r == 2 * REL_CLIP - off), 1.0, 0.0) - jnp.where(own & (r == 2 * REL_CLIP), 1.0, 0.0)
        o_ref[...] = lax.dot_general(row_scr[...], sel, NT, precision=lax.Precision.HIGHEST, preferred_element_type=f32)

    return _pcall(body, name=f"bias_reduce_{layer}", out_shape=jax.ShapeDtypeStruct((N_HEADS, N_REL_PAD), f32),
                  scratch_shapes=[pltpu.VMEM((N_HEADS, DB_COLS), f32)],
                  compiler_params=pltpu.CompilerParams(vmem_limit_bytes=VMEM_LIMIT))(dbias)


FWD_PAIRS = 8
BWD_PAIRS = 4


def _key_specs(n_groups, npairs, slab):
    per_slab = E_MIX // (128 * npairs)
    return [pl.BlockSpec((QG, 128 * npairs), functools.partial(
        lambda hp, g, jj: (jnp.clip(g - 2 + jj, 0, n_groups - 1), slab * per_slab + hp), jj=jj)) for jj in range(3)]


def _bias_spec(npairs):
    return pl.BlockSpec((None, 2 * npairs, QG, KG), lambda hp, g: (jnp.minimum(g, 2), hp, 0, 0))


def _attn_fwd(h, bias, layer, comm=None):
    T = h.shape[0]
    n_groups = T // QG
    scale = 1.0 / math.sqrt(HEAD_DIM)

    def body(q_ref, k0, k1, k2, v0, v1, v2, b_ref, o_ref, lse_ref):
        lane = lax.broadcasted_iota(jnp.int32, (1, 128), 1)
        ones = jnp.ones((KG, 128), bf16)
        lse = jnp.zeros((QG, 128), f32)
        for pp in range(FWD_PAIRS):
            cs = slice(pp * 128, (pp + 1) * 128)
            q2 = q_ref[:, cs] * scale
            kc = jnp.concatenate([k0[:, cs], k1[:, cs], k2[:, cs]], axis=0)
            vc = jnp.concatenate([jnp.concatenate([v0[:, cs], v1[:, cs], v2[:, cs]], axis=0), ones], axis=1)
            outs = []
            for hh in range(2):
                qm = jnp.where(lane // HEAD_DIM == hh, q2, jnp.zeros_like(q2))
                s = lax.dot_general(qm, kc, NT, preferred_element_type=f32) + b_ref[2 * pp + hh]
                m = jnp.max(s, axis=1, keepdims=True)
                ol = jnp.dot(jnp.exp(s - m).astype(bf16), vc, preferred_element_type=f32)
                outs.append(ol[:, :128] / ol[:, 128:])
                lse = jnp.where(lane == 2 * pp + hh, m + jnp.log(ol[:, 128:]), lse)
            o_ref[:, cs] = jnp.where(lane // HEAD_DIM == 0, outs[0], outs[1]).astype(bf16)
        lse_ref[...] = lse

    (mix, lse), carried = _pcall_carry(
        body, comm, n_in=8, n_out=2, name=f"attn_fwd_{layer}", grid=(N_HEADS // (2 * FWD_PAIRS), n_groups),
        out_shape=(jax.ShapeDtypeStruct((T, E_MIX), bf16), jax.ShapeDtypeStruct((T, 128), f32)),
        in_specs=[pl.BlockSpec((QG, 128 * FWD_PAIRS), lambda hp, g: (g, hp))] + _key_specs(n_groups, FWD_PAIRS, 1)
        + _key_specs(n_groups, FWD_PAIRS, 2) + [_bias_spec(FWD_PAIRS)],
        out_specs=(pl.BlockSpec((QG, 128 * FWD_PAIRS), lambda hp, g: (g, hp)), pl.BlockSpec((QG, 128), lambda hp, g: (g, 0))),
        compiler_params=_params("arbitrary", "arbitrary"))(h, h, h, h, h, h, h, bias)
    return mix, lse, carried


def _halo_rows(ref, r):
    return ref[r:r + 1, :].astype(f32)


def _conv_taps(cu, p6, p7, w_ref):
    row = lax.broadcasted_iota(jnp.int32, cu.shape, 0)
    r1 = jnp.where(row == 0, p7, pltpu.roll(cu, 1, axis=0))
    r2 = jnp.where(row == 0, p6, jnp.where(row == 1, p7, pltpu.roll(cu, 2, axis=0)))
    return w_ref[2:3, :] * cu + w_ref[1:2, :] * r1 + w_ref[0:1, :] * r2, r1, r2


def _conv_fwd(h, w, layer, comm=None):
    T = h.shape[0]
    tm = 512

    def body(bg_ref, cg_ref, u_ref, cgp_ref, up_ref, w_ref, o_ref):
        first = (pl.program_id(0) == 0).astype(f32)
        cu = cg_ref[...].astype(f32) * u_ref[...].astype(f32)
        p6 = _halo_rows(cgp_ref, 14) * _halo_rows(up_ref, 14) * (1.0 - first)
        p7 = _halo_rows(cgp_ref, 15) * _halo_rows(up_ref, 15) * (1.0 - first)
        conv, _, _ = _conv_taps(cu, p6, p7, w_ref)
        o_ref[...] = (bg_ref[...].astype(f32) * conv).astype(bf16)

    prev = lambda slab: pl.BlockSpec((16, E_MIX), lambda i: (jnp.maximum(i * (tm // 16) - 1, 0), slab))
    (mix,), carried = _pcall_carry(
        body, comm, n_in=6, n_out=1, name=f"conv_fwd_{layer}", grid=(T // tm,),
        out_shape=(jax.ShapeDtypeStruct((T, E_MIX), bf16),),
        in_specs=[pl.BlockSpec((tm, E_MIX), lambda i: (i, 0)), pl.BlockSpec((tm, E_MIX), lambda i: (i, 1)),
                  pl.BlockSpec((tm, E_MIX), lambda i: (i, 2)), prev(1), prev(2),
                  pl.BlockSpec((CONV_W, E_MIX), lambda i: (0, 0))],
        out_specs=(pl.BlockSpec((tm, E_MIX), lambda i: (i, 0)),),
        compiler_params=_params("arbitrary"))(h, h, h, h, h, w)
    return mix, carried


def _kv_mem(memb, wkv):
    def body(m_ref, w_ref, o_ref):
        o_ref[...] = jnp.dot(m_ref[...], w_ref[...], preferred_element_type=f32).astype(bf16)

    return _pcall(body, name="kv_mem", out_shape=jax.ShapeDtypeStruct((N_MEM, 2 * E_MEM), bf16),
                  compiler_params=pltpu.CompilerParams(vmem_limit_bytes=VMEM_LIMIT))(memb, wkv)


def _mem_probs(qm_ref, kv_ref, hh):
    qh = qm_ref[:, hh * MEM_HEAD_DIM:(hh + 1) * MEM_HEAD_DIM]
    kh = kv_ref[:, hh * MEM_HEAD_DIM:(hh + 1) * MEM_HEAD_DIM]
    vh = kv_ref[:, E_MEM + hh * MEM_HEAD_DIM:E_MEM + (hh + 1) * MEM_HEAD_DIM]
    s = lax.dot_general(qh, kh, NT, preferred_element_type=f32) * (1.0 / math.sqrt(MEM_HEAD_DIM))
    e = jnp.exp(s - jnp.max(s, axis=1, keepdims=True))
    return e / jnp.sum(e, axis=1, keepdims=True), qh, kh, vh


def _h_tail_specs(tm):
    return [pl.BlockSpec((tm, E_MEM), functools.partial(lambda i, cb: (i, cb), cb=cb)) for cb in (6, 7, 8, 9)]


def _post_fwd(h, mix, kv, wout, x, g, b, layer, target=None):
    T = x.shape[0]
    tm = 512

    def body(qm_ref, z0, z1, z2, mix_ref, kv_ref, w_ref, x_ref, g_ref, b_ref, *rest):
        mem = jnp.concatenate(
            [jnp.dot(_mem_probs(qm_ref, kv_ref, hh)[0].astype(bf16), kv_ref[:, E_MEM + hh * MEM_HEAD_DIM:E_MEM + (hh + 1) * MEM_HEAD_DIM],
                     preferred_element_type=f32) for hh in range(MEM_HEADS)], axis=1)
        z = jnp.concatenate([z0[...], z1[...], z2[...]], axis=1)
        one = jnp.ones((), bf16)
        y = jnp.concatenate([mix_ref[...], mem.astype(bf16)], axis=1) * (z * (one / (one + jnp.exp(-z))))
        out = jnp.dot(y, w_ref[...], preferred_element_type=f32)
        r = DN_ALPHA * x_ref[...] + out
        mu = jnp.mean(r, axis=1, keepdims=True)
        var = jnp.mean(jnp.square(r - mu), axis=1, keepdims=True)
        rstd = lax.rsqrt(var + LN_EPS)
        xhat = (r - mu) * rstd
        xn = xhat * g_ref[...] + b_ref[...]
        if target is None:
            xn_ref, xb_ref, xh_ref, rs_ref = rest
            xn_ref[...] = xn
            xb_ref[...] = xn.astype(bf16)
        else:
            t_ref, l_ref, d_ref, xh_ref, rs_ref = rest

            @pl.when(pl.program_id(0) == 0)
            def _():
                l_ref[...] = jnp.zeros_like(l_ref)

            err = xn - t_ref[...]
            d_ref[...] = err * (1.0 / D_MODEL)
            l_ref[...] += jnp.sum(jnp.square(err))
        xh_ref[...] = xhat
        rs_ref[...] = rstd

    tile = lambda w: pl.BlockSpec((tm, w), lambda i: (i, 0))
    const = lambda r, c: pl.BlockSpec((r, c), lambda i: (0, 0))
    in_specs = _h_tail_specs(tm) + [tile(E_MIX), const(N_MEM, 2 * E_MEM), const(E_BRANCH, D_MODEL), tile(D_MODEL),
                                    const(1, D_MODEL), const(1, D_MODEL)]
    stats = (jax.ShapeDtypeStruct((T, D_MODEL), f32), jax.ShapeDtypeStruct((T, 1), f32))
    if target is None:
        return _pcall(
            body, name=f"post_fwd_{layer}", grid=(T // tm,),
            out_shape=(jax.ShapeDtypeStruct((T, D_MODEL), f32), jax.ShapeDtypeStruct((T, D_MODEL), bf16)) + stats,
            in_specs=in_specs, out_specs=(tile(D_MODEL), tile(D_MODEL), tile(D_MODEL), tile(1)),
            compiler_params=_params("arbitrary"))(h, h, h, h, mix, kv, wout, x, g, b)
    return _pcall(
        body, name=f"post_fwd_loss_{layer}", grid=(T // tm,),
        out_shape=(jax.ShapeDtypeStruct((8, 128), f32), jax.ShapeDtypeStruct((T, D_MODEL), f32)) + stats,
        in_specs=in_specs + [tile(D_MODEL)], out_specs=(const(8, 128), tile(D_MODEL), tile(D_MODEL), tile(1)),
        compiler_params=_params("arbitrary"))(h, h, h, h, mix, kv, wout, x, g, b, target)


def _post_bwd(dxn, xhat, rstd, g, h, mix, kv, wout, layer, comm=None):
    T = dxn.shape[0]
    tm = 256
    n_tiles = T // tm
    inv = 1.0 / math.sqrt(MEM_HEAD_DIM)

    def body(dxn_ref, xh_ref, rs_ref, g_ref, qm_ref, z0, z1, z2, mix_ref, kv_ref, w_ref,
             dxp_ref, dhb_ref, dmix_ref, dkv_ref, dwo_out, dg_ref, db_ref, dwo_ref):
        @pl.when(pl.program_id(0) == 0)
        def _():
            dkv_ref[...] = jnp.zeros_like(dkv_ref)
            dwo_ref[...] = jnp.zeros_like(dwo_ref)
            dg_ref[...] = jnp.zeros_like(dg_ref)
            db_ref[...] = jnp.zeros_like(db_ref)

        dy, xh = dxn_ref[...], xh_ref[...]
        dg_ref[0:1, :] += jnp.sum(dy * xh, axis=0, keepdims=True)
        db_ref[0:1, :] += jnp.sum(dy, axis=0, keepdims=True)
        gx = dy * g_ref[...]
        dr = rs_ref[...] * (gx - jnp.mean(gx, axis=1, keepdims=True) - xh * jnp.mean(gx * xh, axis=1, keepdims=True))
        dxp_ref[...] = DN_ALPHA * dr
        dob = dr.astype(bf16)

        probs = [_mem_probs(qm_ref, kv_ref, hh) for hh in range(MEM_HEADS)]
        mem = jnp.concatenate([jnp.dot(p.astype(bf16), vh, preferred_element_type=f32) for p, _, _, vh in probs], axis=1)
        z = jnp.concatenate([z0[...], z1[...], z2[...]], axis=1).astype(f32)
        act, sig = _silu_parts(z)
        cat = jnp.concatenate([mix_ref[...].astype(f32), mem], axis=1)
        yb = (cat * act).astype(bf16)
        dwo_ref[...] += lax.dot_general(yb, dob, TN, preferred_element_type=f32)
        dyv = lax.dot_general(dob, w_ref[...], NT, preferred_element_type=f32)
        dz = dyv * cat * (sig * (1.0 + z * (1.0 - sig)))
        dcat = dyv * act
        dmix_ref[...] = dcat[:, :E_MIX].astype(bf16)
        dqs = []
        for hh, (p, qh, kh, vh) in enumerate(probs):
            dmem = dcat[:, E_MIX + hh * MEM_HEAD_DIM:E_MIX + (hh + 1) * MEM_HEAD_DIM].astype(bf16)
            dp = lax.dot_general(dmem, vh, NT, preferred_element_type=f32)
            ds = (p * (dp - jnp.sum(p * dp, axis=1, keepdims=True))).astype(bf16)
            dqs.append(jnp.dot(ds, kh, preferred_element_type=f32) * inv)
            dkv_ref[:, hh * MEM_HEAD_DIM:(hh + 1) * MEM_HEAD_DIM] += lax.dot_general(ds, qh, TN, preferred_element_type=f32) * inv
            dkv_ref[:, E_MEM + hh * MEM_HEAD_DIM:E_MEM + (hh + 1) * MEM_HEAD_DIM] += lax.dot_general(
                p.astype(bf16), dmem, TN, preferred_element_type=f32)
        dhb_ref[...] = jnp.concatenate(dqs + [dz], axis=1).astype(bf16)

        @pl.when(pl.program_id(0) == n_tiles - 1)
        def _():
            dwo_out[...] = dwo_ref[...].astype(bf16)

    tile = lambda w: pl.BlockSpec((tm, w), lambda i: (i, 0))
    const = lambda r, c: pl.BlockSpec((r, c), lambda i: (0, 0))
    return _pcall_carry(
        body, comm, n_in=11, n_out=7, name=f"post_bwd_{layer}", grid=(n_tiles,),
        out_shape=(jax.ShapeDtypeStruct((T, D_MODEL), f32), jax.ShapeDtypeStruct((T, E_MEM + E_BRANCH), bf16),
                   jax.ShapeDtypeStruct((T, E_MIX), bf16), jax.ShapeDtypeStruct((N_MEM, 2 * E_MEM), f32),
                   jax.ShapeDtypeStruct((E_BRANCH, D_MODEL), bf16), jax.ShapeDtypeStruct((8, D_MODEL), f32),
                   jax.ShapeDtypeStruct((8, D_MODEL), f32)),
        in_specs=[tile(D_MODEL), tile(D_MODEL), tile(1), const(1, D_MODEL)] + _h_tail_specs(tm)
        + [tile(E_MIX), const(N_MEM, 2 * E_MEM), const(E_BRANCH, D_MODEL)],
        out_specs=(tile(D_MODEL), tile(E_MEM + E_BRANCH), tile(E_MIX), const(N_MEM, 2 * E_MEM),
                   const(E_BRANCH, D_MODEL), const(8, D_MODEL), const(8, D_MODEL)),
        scratch_shapes=[pltpu.VMEM((E_BRANCH, D_MODEL), f32)],
        compiler_params=_params("arbitrary"))(dxn, xhat, rstd, g, h, h, h, h, mix, kv, wout)


def _attn_bwd(h, bias, dmix, lse, layer, comm=None):
    T = h.shape[0]
    n_groups = T // QG
    scale = 1.0 / math.sqrt(HEAD_DIM)

    def body(q_ref, k0, k1, k2, v0, v1, v2, do_ref, b_ref, lse_ref, dq_ref, dk_ref, dv_ref, db_ref, acck, accv):
        g = pl.program_id(1)

        @pl.when(g == 0)
        def _():
            acck[...] = jnp.zeros_like(acck)
            accv[...] = jnp.zeros_like(accv)
            db_ref[...] = jnp.zeros_like(db_ref)

        @pl.when(g < n_groups)
        def _():
            lane = lax.broadcasted_iota(jnp.int32, (1, 128), 1)
            first = lane // HEAD_DIM == 0
            for pp in range(BWD_PAIRS):
                cs = slice(pp * 128, (pp + 1) * 128)
                do2 = do_ref[:, cs]
                q2 = q_ref[:, cs] * scale
                kc = jnp.concatenate([k0[:, cs], k1[:, cs], k2[:, cs]], axis=0)
                vc = jnp.concatenate([v0[:, cs], v1[:, cs], v2[:, cs]], axis=0)
                q2t, do2t = q2.T, do2.T
                dqs, dks, dvs = [], [], []
                for hh in range(2):
                    hm = lane // HEAD_DIM == hh
                    head = (pl.program_id(0) * BWD_PAIRS + pp) * 2 + hh
                    lse = jnp.sum(jnp.where(lane == head, lse_ref[...], 0.0), axis=1, keepdims=True)
                    qm = jnp.where(hm, q2, jnp.zeros_like(q2))
                    dom = jnp.where(hm, do2, jnp.zeros_like(do2))
                    s = lax.dot_general(qm, kc, NT, preferred_element_type=f32) + b_ref[2 * pp + hh]
                    p = jnp.exp(s - lse)
                    dp = lax.dot_general(dom, vc, NT, preferred_element_type=f32)
                    ds = p * (dp - jnp.sum(p * dp, axis=1, keepdims=True))
                    db_ref[2 * pp + hh] += ds[:, KG - DB_COLS:]
                    dsb, pb = ds.astype(bf16), p.astype(bf16)
                    dqs.append(jnp.dot(dsb, kc, preferred_element_type=f32) * scale)
                    dks.append(jnp.dot(q2t[hh * HEAD_DIM:(hh + 1) * HEAD_DIM], dsb, preferred_element_type=f32))
                    dvs.append(jnp.dot(do2t[hh * HEAD_DIM:(hh + 1) * HEAD_DIM], pb, preferred_element_type=f32))
                dq_ref[:, cs] = jnp.where(first, dqs[0], dqs[1]).astype(bf16)
                dkc = jnp.concatenate(dks, axis=0).T
                dvc = jnp.concatenate(dvs, axis=0).T
                for jj in range(3):
                    slot = (g + 1 + jj) % 3
                    if jj == 2:
                        acck[slot, :, cs] = dkc[jj * QG:(jj + 1) * QG]
                        accv[slot, :, cs] = dvc[jj * QG:(jj + 1) * QG]
                    else:
                        acck[slot, :, cs] += dkc[jj * QG:(jj + 1) * QG]
                        accv[slot, :, cs] += dvc[jj * QG:(jj + 1) * QG]

        done = (g + 1) % 3
        dk_ref[...] = acck[done].astype(bf16)
        dv_ref[...] = accv[done].astype(bf16)

    last = n_groups - 1
    width = 128 * BWD_PAIRS
    qspec = pl.BlockSpec((QG, width), lambda hp, g: (jnp.minimum(g, last), hp))
    kout = pl.BlockSpec((QG, width), lambda hp, g: (jnp.clip(g - 2, 0, last), hp))
    dbspec = pl.BlockSpec((2 * BWD_PAIRS, QG, DB_COLS), lambda hp, g: (hp, 0, 0))
    lspec = pl.BlockSpec((QG, 128), lambda hp, g: (jnp.minimum(g, last), 0))
    return _pcall_carry(
        body, comm, n_in=10, n_out=4, name=f"attn_bwd_{layer}", grid=(N_HEADS // (2 * BWD_PAIRS), n_groups + 2),
        out_shape=(jax.ShapeDtypeStruct((T, E_MIX), bf16),) * 3 + (jax.ShapeDtypeStruct((N_HEADS, QG, DB_COLS), f32),),
        in_specs=[qspec] + _key_specs(n_groups, BWD_PAIRS, 1) + _key_specs(n_groups, BWD_PAIRS, 2)
        + [qspec, _bias_spec(BWD_PAIRS), lspec],
        out_specs=(qspec, kout, kout, dbspec),
        scratch_shapes=[pltpu.VMEM((3, QG, width), f32), pltpu.VMEM((3, QG, width), f32)],
        compiler_params=_params("arbitrary", "arbitrary"))(h, h, h, h, h, h, h, dmix, bias, lse)


def _conv_bwd(h, w, dmix, layer):
    T = h.shape[0]
    tm = 512
    n_tiles = T // tm

    def body(bg_ref, cg_ref, u_ref, cgp_ref, up_ref, dy_ref, bgn_ref, dyn_ref, w_ref, dbg_ref, dcg_ref, du_ref, dw_ref):
        i = pl.program_id(0)

        @pl.when(i == 0)
        def _():
            dw_ref[...] = jnp.zeros_like(dw_ref)

        first = (i == 0).astype(f32)
        final = (i == n_tiles - 1).astype(f32)
        bg, cg, u = bg_ref[...].astype(f32), cg_ref[...].astype(f32), u_ref[...].astype(f32)
        dy = dy_ref[...].astype(f32)
        cu = cg * u
        p6 = _halo_rows(cgp_ref, 14) * _halo_rows(up_ref, 14) * (1.0 - first)
        p7 = _halo_rows(cgp_ref, 15) * _halo_rows(up_ref, 15) * (1.0 - first)
        conv, r1, r2 = _conv_taps(cu, p6, p7, w_ref)
        dbg_ref[...] = (dy * conv).astype(bf16)
        dc = dy * bg
        n0 = _halo_rows(dyn_ref, 0) * _halo_rows(bgn_ref, 0) * (1.0 - final)
        n1 = _halo_rows(dyn_ref, 1) * _halo_rows(bgn_ref, 1) * (1.0 - final)
        row = lax.broadcasted_iota(jnp.int32, dc.shape, 0)
        f1 = jnp.where(row == tm - 1, n0, pltpu.roll(dc, tm - 1, axis=0))
        f2 = jnp.where(row == tm - 2, n0, jnp.where(row == tm - 1, n1, pltpu.roll(dc, tm - 2, axis=0)))
        dcu = w_ref[2:3, :] * dc + w_ref[1:2, :] * f1 + w_ref[0:1, :] * f2
        dcg_ref[...] = (dcu * u).astype(bf16)
        du_ref[...] = (dcu * cg).astype(bf16)
        dw_ref[0:1, :] += jnp.sum(dc * r2, axis=0, keepdims=True)
        dw_ref[1:2, :] += jnp.sum(dc * r1, axis=0, keepdims=True)
        dw_ref[2:3, :] += jnp.sum(dc * cu, axis=0, keepdims=True)

    tile = lambda slab: pl.BlockSpec((tm, E_MIX), lambda i: (i, slab))
    prev = lambda slab: pl.BlockSpec((16, E_MIX), lambda i: (jnp.maximum(i * (tm // 16) - 1, 0), slab))
    nxt = lambda slab: pl.BlockSpec((16, E_MIX), lambda i: (jnp.minimum((i + 1) * (tm // 16), T // 16 - 1), slab))
    return _pcall(
        body, name=f"conv_bwd_{layer}", grid=(n_tiles,),
        out_shape=(jax.ShapeDtypeStruct((T, E_MIX), bf16),) * 3 + (jax.ShapeDtypeStruct((8, E_MIX), f32),),
        in_specs=[tile(0), tile(1), tile(2), prev(1), prev(2), tile(0), nxt(0), nxt(0),
                  pl.BlockSpec((CONV_W, E_MIX), lambda i: (0, 0))],
        out_specs=(tile(0), tile(0), tile(0), pl.BlockSpec((8, E_MIX), lambda i: (0, 0))),
        compiler_params=_params("arbitrary"))(h, h, h, h, h, dmix, h, dmix, w)


def _inproj_bwd_dx(da, db, dc, dhb, dxp, win, layer, comm=None):
    T = dxp.shape[0]
    tm = 512

    def body(da_ref, db_ref, dc_ref, dhb_ref, dxp_ref, w_ref, o_ref):
        dh = jnp.concatenate([da_ref[...], db_ref[...], dc_ref[...], dhb_ref[...]], axis=1)
        acc = dxp_ref[...]
        for j in range(N_CHIPS):
            acc = acc + lax.dot_general(dh[:, j * W_IN_COLS:(j + 1) * W_IN_COLS], w_ref[j], NT, preferred_element_type=f32)
        o_ref[...] = acc

    tile = lambda w: pl.BlockSpec((tm, w), lambda i: (i, 0))
    (dx,), carried = _pcall_carry(
        body, comm, n_in=6, n_out=1, name=f"inproj_bwd_dx_{layer}", grid=(T // tm,),
        out_shape=(jax.ShapeDtypeStruct((T, D_MODEL), f32),),
        in_specs=[tile(E_MIX), tile(E_MIX), tile(E_MIX), tile(E_MEM + E_BRANCH), tile(D_MODEL),
                  pl.BlockSpec((N_CHIPS, D_MODEL, W_IN_COLS), lambda i: (0, 0, 0), pipeline_mode=pl.Buffered(1))],
        out_specs=(tile(D_MODEL),),
        compiler_params=_params("arbitrary"))(da, db, dc, dhb, dxp, win)
    return dx, carried


def _dh_pieces():
    pieces, col = [], 0
    for src, width in enumerate((E_MIX, E_MIX, E_MIX, E_MEM + E_BRANCH)):
        lo = 0
        while lo < width:
            j, c0 = divmod(col + lo, W_IN_COLS)
            n = min(width - lo, W_IN_COLS - c0)
            pieces.append((src, lo, lo + n, j, c0, c0 + n))
            lo += n
        col += width
    return pieces


def _inproj_bwd_dw(da, db, dc, dhb, xb, layer, comm=None):
    T = xb.shape[0]
    tm = 1024 if T % 1024 == 0 else 512
    n_tiles = T // tm

    def body(da_ref, db_ref, dc_ref, dhb_ref, x_ref, o_ref, acc, stage, sem):
        i = pl.program_id(0)

        @pl.when(i == 0)
        def _():
            acc[...] = jnp.zeros_like(acc)

        srcs = (da_ref, db_ref, dc_ref, dhb_ref)
        xt = x_ref[...]
        for s, s0, s1, j, c0, c1 in _dh_pieces():
            acc[j, :, c0:c1] += lax.dot_general(xt, srcs[s][:, s0:s1], TN, preferred_element_type=f32)

        @pl.when(i == n_tiles - 1)
        def _():
            for j in range(N_CHIPS):
                stage[...] = acc[j].astype(bf16)
                cp = pltpu.make_async_copy(stage, o_ref.at[j], sem)
                cp.start()
                cp.wait()

    tile = lambda w: pl.BlockSpec((tm, w), lambda i: (i, 0))
    (dw,), carried = _pcall_carry(
        body, comm, n_in=5, n_out=1, name=f"inproj_bwd_dw_{layer}", grid=(n_tiles,),
        out_shape=(jax.ShapeDtypeStruct((N_CHIPS, D_MODEL, W_IN_COLS), bf16),),
        in_specs=[tile(E_MIX), tile(E_MIX), tile(E_MIX), tile(E_MEM + E_BRANCH), tile(D_MODEL)],
        out_specs=(ANY,),
        scratch_shapes=[pltpu.VMEM((N_CHIPS, D_MODEL, W_IN_COLS), f32), pltpu.VMEM((D_MODEL, W_IN_COLS), bf16),
                        pltpu.SemaphoreType.DMA],
        compiler_params=_params("arbitrary"))(da, db, dc, dhb, xb)
    return dw, carried


def _kv_mem_bwd(memb, dkv, layer):
    def body(m_ref, d_ref, o_ref):
        o_ref[...] = lax.dot_general(m_ref[...], d_ref[...].astype(bf16), TN, preferred_element_type=f32).astype(bf16)

    return _pcall(body, name=f"kv_mem_bwd_{layer}", out_shape=jax.ShapeDtypeStruct((D_MODEL, 2 * E_MEM), bf16),
                  compiler_params=pltpu.CompilerParams(vmem_limit_bytes=VMEM_LIMIT))(memb, dkv)


def _to_bf16(a, name):
    rows, cols = a.shape
    br = 512

    def body(a_ref, o_ref):
        o_ref[...] = a_ref[...].astype(bf16)

    return _pcall(body, name=name, grid=(rows // br,), out_shape=jax.ShapeDtypeStruct((rows, cols), bf16),
                  in_specs=[pl.BlockSpec((br, cols), lambda i: (i, 0))], out_specs=pl.BlockSpec((br, cols), lambda i: (i, 0)),
                  compiler_params=_params("arbitrary"))(a)


def _adamw(w, g, m, v, name):
    shape = w.shape
    cols = shape[-1]
    rows = w.size // cols
    args = [a.reshape(rows, cols) for a in (w, g, m, v)]
    br = 256 if rows % 256 == 0 and rows > 256 else rows

    def body(w_ref, g_ref, m_ref, v_ref, go_ref, d_ref, nm_ref, nv_ref):
        gg = g_ref[...]
        nm = ADAM_B1 * m_ref[...] + (1.0 - ADAM_B1) * gg
        nv = ADAM_B2 * v_ref[...] + (1.0 - ADAM_B2) * jnp.square(gg)
        m_hat = nm / (1.0 - ADAM_B1 ** ADAM_STEP)
        v_hat = nv / (1.0 - ADAM_B2 ** ADAM_STEP)
        go_ref[...] = gg
        d_ref[...] = -ADAM_LR * (m_hat / (jnp.sqrt(v_hat) + ADAM_EPS) + ADAM_WD * w_ref[...])
        nm_ref[...] = nm
        nv_ref[...] = nv

    spec = pl.BlockSpec((br, cols), lambda i: (i, 0))
    outs = _pcall(body, name=name, grid=(rows // br,), out_shape=(jax.ShapeDtypeStruct((rows, cols), f32),) * 4,
                  in_specs=[spec] * 4, out_specs=(spec,) * 4, compiler_params=_params("arbitrary"))(*args)
    return tuple(o.reshape(shape) for o in outs)


def kernel(x, mem, w_in, w_mem_kv, w_out, rel_bias, conv_w, ln_g, ln_b, loss_target, m_w_in, m_w_mem_kv, m_w_out, m_rel_bias, m_conv_w, m_ln_g, m_ln_b, v_w_in, v_w_mem_kv, v_w_out, v_rel_bias, v_conv_w, v_ln_g, v_ln_b):
    T = x.shape[1]
    x0 = x.reshape(T, D_MODEL)
    target = loss_target.reshape(T, D_MODEL)
    memb = mem.reshape(N_MEM, D_MODEL).astype(bf16)
    chip = 2 * lax.axis_index("x") + lax.axis_index("y")
    core = lax.axis_index("c")
    chip_arr = jnp.reshape(chip, (1,)).astype(jnp.int32)
    core_arr = jnp.reshape(core, (1,)).astype(jnp.int32)

    place = jnp.concatenate([chip_arr, core_arr])
    tables = jnp.pad(rel_bias, ((0, 0), (0, 0), (0, N_REL_PAD - N_REL)))

    shards = [w_in.astype(bf16), w_mem_kv.astype(bf16), w_out.astype(bf16)]
    *arrived, cw_g = _comm_call(_gather_ici(shards, 0, extra=conv_w), "gather_ici_0")
    gathered = {0: _comm_call(_gather_d2d(arrived), "gather_d2d_0")}
    conv_full = jnp.transpose(cw_g, (1, 2, 0, 3)).reshape(DEPTH // 2, CONV_W, E_MIX)

    xs, xbs, hs, mixes, kvs, xhats, rstds, biases, lses = [x0], [_to_bf16(x0, "cast_x")], [], [], [], [], [], {}, {}
    for layer in range(DEPTH):
        win, wkv, wout = gathered[layer]
        more = layer + 1 < DEPTH
        h, arrived = _inproj(xbs[layer], win, layer, _gather_ici(shards, layer + 1) if more else None)
        passing = _gather_d2d(list(arrived)) if more else None
        if layer % 2 == 0:
            biases[layer] = _bias_expand(tables[layer // 2], layer)
            mix, lses[layer], done = _attn_fwd(h, biases[layer], layer, passing)
        else:
            mix, done = _conv_fwd(h, conv_full[layer // 2], layer, passing)
        if more:
            gathered[layer + 1] = list(done)
        kv = _kv_mem(memb, wkv.reshape(D_MODEL, 2 * E_MEM))
        first, second, xhat, rstd = _post_fwd(h, mix, kv, wout.reshape(E_BRANCH, D_MODEL), xs[layer], ln_g[layer][None, :],
                                              ln_b[layer][None, :], layer, None if more else target)
        if more:
            xs.append(first); xbs.append(second)
        else:
            lsum, dx = first, second
        hs.append(h); mixes.append(mix); kvs.append(kv); xhats.append(xhat); rstds.append(rstd)

    loss = lax.psum(lsum[0, 0], ("x", "y", "c")) * (0.5 / D_MODEL)

    dgs, dbs, dconvs, dtables = [None] * DEPTH, [None] * DEPTH, [None] * (DEPTH // 2), [None] * ((DEPTH + 1) // 2)
    finals = [None, None, None]
    above = None
    for layer in reversed(range(DEPTH)):
        h = hs[layer]
        win, wkv, wout = gathered[layer]
        (dxp, dhb, dmix, dkv, dwo, dgs[layer], dbs[layer]), from_sibling = _post_bwd(
            dx, xhats[layer], rstds[layer], ln_g[layer][None, :], h, mixes[layer], kvs[layer], wout.reshape(E_BRANCH, D_MODEL),
            layer, _sibling_exchange(above) if above else None)
        sums = [_pair_sum(g, r, core_arr, layer + 1) for g, r in zip(above, from_sibling)] if above else None
        scatter = _chip_scatter(sums) if above else None
        if layer % 2 == 0:
            (da, db, dc, dbias), from_chips = _attn_bwd(h, biases[layer], dmix, lses[layer], layer, scatter)
            dtables[layer // 2] = _bias_reduce(dbias, layer)
        else:
            da, db, dc, dconvs[layer // 2] = _conv_bwd(h, conv_full[layer // 2], dmix, layer)
        if layer > 0:
            dx, landed = _inproj_bwd_dx(da, db, dc, dhb, dxp, win, layer, scatter if layer % 2 == 1 else None)
            from_chips = landed if layer % 2 == 1 else from_chips
        share = None
        if above:
            finals = [_chip_sum(s, r, place, layer + 1, f) for s, r, f in zip(sums, from_chips, finals)]
            share = _sibling_share(finals, layer + 1)
        if layer == 0:
            pad8 = lambda a: jnp.pad(a, ((0, 8 - a.shape[0]), (0, 0)))
            small_mine = jnp.concatenate(dgs + dbs + dconvs + [pad8(t.reshape(-1, D_MODEL)) for t in dtables], axis=0)
            share = _both(share, _small_exchange(small_mine))
        g_win, shared = _inproj_bwd_dw(da, db, dc, dhb, xbs[layer], layer, share)
        if layer == 0:
            *shared, small_slots = shared
        finals = list(shared) if above else finals
        above = [g_win, _kv_mem_bwd(memb, dkv, layer).reshape(N_CHIPS, W_KV_ROWS, 2 * E_MEM),
                 dwo.reshape(N_CHIPS, W_OUT_ROWS, D_MODEL)]
    from_sibling = _comm_call(_sibling_exchange(above), "sibling_exchange_0")
    sums = [_pair_sum(g, r, core_arr, 0) for g, r in zip(above, from_sibling)]
    dx, from_chips = _inproj_bwd_dx(da, db, dc, dhb, dxp, win, 0, _chip_scatter(sums))
    finals = [_chip_sum(s, r, place, 0, f) for s, r, f in zip(sums, from_chips, finals)]
    grad_w_in, grad_w_mem_kv, grad_w_out = _comm_call(_sibling_share(finals, 0), "sibling_share_0")
    grad_x = dx.reshape(1, T, D_MODEL)

    device_arr = jnp.reshape(2 * chip + core, (1,)).astype(jnp.int32)
    small = _small_sum(small_slots, small_mine, device_arr)
    grad_ln_g = jnp.stack([small[8 * l] for l in range(DEPTH)])
    grad_ln_b = jnp.stack([small[8 * (DEPTH + l)] for l in range(DEPTH)])
    conv_all = jnp.stack([small[8 * (2 * DEPTH + a):8 * (2 * DEPTH + a) + CONV_W] for a in range(DEPTH // 2)])
    grad_conv_w = lax.dynamic_slice_in_dim(conv_all, chip * (E_MIX // N_CHIPS), E_MIX // N_CHIPS, axis=2)
    t0 = 8 * (2 * DEPTH + DEPTH // 2)
    grad_rel_bias = jnp.stack([small[t0 + 8 * a:t0 + 8 * a + 6].reshape(N_HEADS, N_REL_PAD)[:, :N_REL]
                               for a in range((DEPTH + 1) // 2)])

    grads = [grad_w_in, grad_w_mem_kv, grad_w_out, grad_rel_bias, grad_conv_w, grad_ln_g, grad_ln_b]
    weights = [w_in, w_mem_kv, w_out, rel_bias, conv_w, ln_g, ln_b]
    moms = [m_w_in, m_w_mem_kv, m_w_out, m_rel_bias, m_conv_w, m_ln_g, m_ln_b]
    vels = [v_w_in, v_w_mem_kv, v_w_out, v_rel_bias, v_conv_w, v_ln_g, v_ln_b]
    names = ["w_in", "w_mem_kv", "w_out", "rel_bias", "conv_w", "ln_g", "ln_b"]
    upd = [_adamw(w, g, m, v, f"adamw_{n}") for w, g, m, v, n in zip(weights, grads, moms, vels, names)]
    grads, deltas, new_m, new_v = zip(*upd)
    return (loss, grad_x, *grads, *deltas, *new_m, *new_v)
```

```python
import functools
import math

import jax
import jax.numpy as jnp
from jax import lax
from jax.experimental import pallas as pl
from jax.experimental.pallas import tpu as pltpu

f32, bf16 = jnp.float32, jnp.bfloat16

D_MODEL = 1024
DEPTH = 4
CHUNK = 64
N_PREV = 8
N_HEADS = 16
HEAD_DIM = 64
E_MIX = 1024
REL_CLIP = 128
N_REL = 2 * REL_CLIP + 1
N_REL_PAD = 384
CONV_W = 3
N_MEM = 256
MEM_HEADS = 4
MEM_HEAD_DIM = 128
E_MEM = 512
E_BRANCH = E_MIX + E_MEM
N_IN = 3 * E_MIX + E_MEM + E_BRANCH
N_CHIPS = 4
W_IN_COLS = N_IN // N_CHIPS
W_KV_ROWS = D_MODEL // N_CHIPS
W_OUT_ROWS = E_BRANCH // N_CHIPS
DN_ALPHA = (2.0 * DEPTH) ** 0.25
LN_EPS = 1e-5
ADAM_LR, ADAM_B1, ADAM_B2, ADAM_EPS, ADAM_WD, ADAM_STEP = 0.001, 0.9, 0.999, 1e-08, 0.01, 10

QG = 4 * CHUNK
KG = QG + N_PREV * CHUNK
DB_COLS = KG // 2
NEG = -1e30
VMEM_LIMIT = 56 * 1024 * 1024

NT = (((1,), (1,)), ((), ()))
TN = (((0,), (0,)), ((), ()))
MESH = pl.DeviceIdType.MESH
ANY = pl.BlockSpec(memory_space=pl.ANY)


def _pcall(body, **kw):
    return pl.pallas_call(body, **kw)


def _params(*sem):
    return pltpu.CompilerParams(dimension_semantics=sem, vmem_limit_bytes=VMEM_LIMIT)


def _silu_parts(z):
    sig = 1.0 / (1.0 + jnp.exp(-z))
    return z * sig, sig


class _Comm:
    def __init__(self, inputs, out_shapes, aliases, n_sems, copies):
        self.inputs, self.out_shapes, self.aliases, self.n_sems, self.copies = inputs, out_shapes, aliases, n_sems, copies

    def start(self, cin, cout, send, recv):
        for cp in self.copies(cin, cout, send, recv)[0]:
            cp.start()

    def wait(self, cin, cout, send, recv):
        sends, recvs = self.copies(cin, cout, send, recv)
        for cp in recvs:
            cp.wait_recv()
        for cp in sends:
            cp.wait_send()


def _pcall_carry(body, comm, *, n_in, n_out, **kw):
    if comm is None:
        return lambda *args: (_pcall(body, **kw)(*args), ())
    grid = kw["grid"]
    k_in, k_out = len(comm.inputs), len(comm.out_shapes)

    def carried(*refs):
        ins, cin = refs[:n_in], refs[n_in:n_in + k_in]
        outs = refs[n_in + k_in:n_in + k_in + n_out]
        cout = refs[n_in + k_in + n_out:n_in + k_in + n_out + k_out]
        scratch, send, recv = refs[n_in + k_in + n_out + k_out:-2], refs[-2], refs[-1]
        ids = [pl.program_id(a) for a in range(len(grid))]
        first = functools.reduce(jnp.logical_and, [i == 0 for i in ids])
        last = functools.reduce(jnp.logical_and, [i == n - 1 for i, n in zip(ids, grid)])

        @pl.when(first)
        def _():
            comm.start(cin, cout, send, recv)

        body(*ins, *outs, *scratch)

        @pl.when(last)
        def _():
            comm.wait(cin, cout, send, recv)

    kw = dict(kw)
    kw["in_specs"] = list(kw["in_specs"]) + [ANY] * k_in
    kw["out_specs"] = tuple(kw["out_specs"]) + (ANY,) * k_out
    kw["out_shape"] = tuple(kw["out_shape"]) + tuple(comm.out_shapes)
    kw["scratch_shapes"] = list(kw.get("scratch_shapes", ())) + [pltpu.SemaphoreType.DMA((comm.n_sems,))] * 2
    aliases = dict(kw.get("input_output_aliases", {}))
    aliases.update({n_in + ci: n_out + co for ci, co in comm.aliases.items()})
    kw["input_output_aliases"] = aliases

    def run(*args):
        res = _pcall(carried, **kw)(*args, *comm.inputs)
        return res[:n_out], res[n_out:]

    return run


def _comm_call(comm, name):
    k_in = len(comm.inputs)

    def body(*refs):
        cin, cout, send, recv = refs[:k_in], refs[k_in:-2], refs[-2], refs[-1]
        comm.start(cin, cout, send, recv)
        comm.wait(cin, cout, send, recv)

    return _pcall(body, name=name, out_shape=tuple(comm.out_shapes), in_specs=[ANY] * k_in,
                  out_specs=(ANY,) * len(comm.out_shapes), input_output_aliases=dict(comm.aliases),
                  scratch_shapes=[pltpu.SemaphoreType.DMA((comm.n_sems,))] * 2)(*comm.inputs)


def _place():
    x, y, c = lax.axis_index("x"), lax.axis_index("y"), lax.axis_index("c")
    return x, y, c, 2 * x + y, (x, y, 1 - c), [(1 - x, y), (x, 1 - y), (1 - x, 1 - y)]


def _rcopy(send, recv, k, src, dst, to):
    return pltpu.make_async_remote_copy(src_ref=src, dst_ref=dst, send_sem=send.at[k], recv_sem=recv.at[k],
                                        device_id=to, device_id_type=MESH)


def _half(ref_rows, core):
    return pl.ds(core * (ref_rows // 2), ref_rows // 2)


def _gather_ici(shards, layer, extra=None):
    extras = [] if extra is None else [extra]
    n = len(shards)

    def copies(cin, cout, send, recv):
        x, y, c, me, sibling, chips = _place()
        sends, recvs = [], []
        for a in range(n):
            s, g = cin[a], cout[a]
            rows = s.shape[1]
            mine = _half(rows, c)
            sends.append(_rcopy(send, recv, 4 * a, s.at[layer], g.at[me], sibling))
            recvs.append(_rcopy(send, recv, 4 * a, s.at[layer], g.at[me], sibling))
            for p, (px, py) in enumerate(chips):
                sends.append(_rcopy(send, recv, 4 * a + 1 + p, s.at[layer, mine], g.at[me, mine], (px, py, c)))
                recvs.append(_rcopy(send, recv, 4 * a + 1 + p, s.at[layer, mine], g.at[2 * px + py, mine], (px, py, c)))
        for e in range(len(extras)):
            s, g = cin[n + e], cout[n + e]
            k = 4 * (n + e)
            sends.append(_rcopy(send, recv, k, s, g.at[me], sibling))
            recvs.append(_rcopy(send, recv, k, s, g.at[me], sibling))
            for p, (px, py) in enumerate(chips):
                sends.append(_rcopy(send, recv, k + 1 + p, s, g.at[me], (px, py, c)))
                recvs.append(_rcopy(send, recv, k + 1 + p, s, g.at[2 * px + py], (px, py, c)))
        return sends, recvs

    out_shapes = [jax.ShapeDtypeStruct((N_CHIPS,) + s.shape[1:], s.dtype) for s in shards]
    out_shapes += [jax.ShapeDtypeStruct((N_CHIPS,) + e.shape, e.dtype) for e in extras]
    return _Comm(list(shards) + extras, out_shapes, {}, 4 * (n + len(extras)), copies)


def _gather_d2d(gathered):
    n = len(gathered)

    def copies(cin, cout, send, recv):
        x, y, c, me, sibling, chips = _place()
        sends, recvs = [], []
        for a in range(n):
            g = cout[a]
            rows = g.shape[1]
            for p, (px, py) in enumerate(chips):
                mine, theirs = g.at[2 * px + py, _half(rows, c)], g.at[2 * px + py, _half(rows, 1 - c)]
                sends.append(_rcopy(send, recv, 3 * a + p, mine, mine, sibling))
                recvs.append(_rcopy(send, recv, 3 * a + p, theirs, theirs, sibling))
        return sends, recvs

    return _Comm(list(gathered), [jax.ShapeDtypeStruct(g.shape, g.dtype) for g in gathered],
                 {a: a for a in range(n)}, 3 * n, copies)


class _SemView:
    def __init__(self, ref, base):
        self.ref, self.base, self.at = ref, base, self

    def __getitem__(self, k):
        return self.ref.at[self.base + k]


def _both(a, b):
    ka, ma = len(a.inputs), len(a.out_shapes)

    def copies(cin, cout, send, recv):
        sa, ra = a.copies(cin[:ka], cout[:ma], send, recv)
        sb, rb = b.copies(cin[ka:], cout[ma:], _SemView(send, a.n_sems), _SemView(recv, a.n_sems))
        return sa + sb, ra + rb

    aliases = dict(a.aliases)
    aliases.update({ka + ci: ma + co for ci, co in b.aliases.items()})
    return _Comm(a.inputs + b.inputs, a.out_shapes + b.out_shapes, aliases, a.n_sems + b.n_sems, copies)


def _small_exchange(buf):
    def copies(cin, cout, send, recv):
        x, y, c, me, sibling, chips = _place()
        sends, recvs = [], []
        for r in range(1, 8):
            px, py, pc = x ^ ((r >> 2) & 1), y ^ ((r >> 1) & 1), c ^ (r & 1)
            sends.append(_rcopy(send, recv, r - 1, cin[0], cout[0].at[2 * me + c], (px, py, pc)))
            recvs.append(_rcopy(send, recv, r - 1, cin[0], cout[0].at[4 * px + 2 * py + pc], (px, py, pc)))
        return sends, recvs

    return _Comm([buf], [jax.ShapeDtypeStruct((8,) + buf.shape, buf.dtype)], {}, 7, copies)


def _small_sum(slots, buf, device_arr):
    rows, cols = buf.shape

    def body(d_ref, s_ref, b_ref, o_ref):
        d = pl.program_id(0)
        val = jnp.where(d == d_ref[0], b_ref[...], s_ref[...])

        @pl.when(d == 0)
        def _():
            o_ref[...] = val

        @pl.when(d > 0)
        def _():
            o_ref[...] += val

    return _pcall(
        body, name="small_sum", out_shape=jax.ShapeDtypeStruct((rows, cols), f32),
        grid_spec=pltpu.PrefetchScalarGridSpec(
            num_scalar_prefetch=1, grid=(8,),
            in_specs=[pl.BlockSpec((None, rows, cols), lambda d, d_ref: (jnp.where(d == d_ref[0], (d + 1) % 8, d), 0, 0)),
                      pl.BlockSpec((rows, cols), lambda d, d_ref: (0, 0))],
            out_specs=pl.BlockSpec((rows, cols), lambda d, d_ref: (0, 0))),
        compiler_params=_params("arbitrary"))(device_arr, slots, buf)


def _sibling_exchange(gs):
    def copies(cin, cout, send, recv):
        x, y, c, me, sibling, chips = _place()
        sends = [_rcopy(send, recv, a, g.at[:, _half(g.shape[1], 1 - c)], r, sibling) for a, (g, r) in enumerate(zip(cin, cout))]
        return sends, sends

    shapes = [jax.ShapeDtypeStruct((N_CHIPS, g.shape[1] // 2, g.shape[2]), g.dtype) for g in gs]
    return _Comm(list(gs), shapes, {}, len(gs), copies)


def _chip_scatter(ss):
    def copies(cin, cout, send, recv):
        x, y, c, me, sibling, chips = _place()
        sends = [_rcopy(send, recv, 3 * a + p, s.at[2 * px + py], r.at[p], (px, py, c))
                 for a, (s, r) in enumerate(zip(cin, cout)) for p, (px, py) in enumerate(chips)]
        return sends, sends

    shapes = [jax.ShapeDtypeStruct((3,) + s.shape[1:], s.dtype) for s in ss]
    return _Comm(list(ss), shapes, {}, 3 * len(ss), copies)


def _sibling_share(fs, layer):
    def copies(cin, cout, send, recv):
        x, y, c, me, sibling, chips = _place()
        sends, recvs = [], []
        for a, f in enumerate(cout):
            mine, theirs = f.at[layer, _half(f.shape[1], c)], f.at[layer, _half(f.shape[1], 1 - c)]
            sends.append(_rcopy(send, recv, a, mine, mine, sibling))
            recvs.append(_rcopy(send, recv, a, theirs, theirs, sibling))
        return sends, recvs

    return _Comm(list(fs), [jax.ShapeDtypeStruct(f.shape, f.dtype) for f in fs], {a: a for a in range(len(fs))},
                 len(fs), copies)


def _sum_rows(rows):
    return next(b for b in (256, 192, 128) if rows % b == 0)


def _pair_sum(g, r, core_arr, layer):
    _, rows, cols = r.shape
    br = _sum_rows(rows)
    nb = rows // br

    def body(c_ref, g_ref, r_ref, o_ref):
        o_ref[...] = (g_ref[...].astype(f32) + r_ref[...].astype(f32)).astype(bf16)

    return _pcall(
        body, name=f"pair_sum_{layer}", out_shape=jax.ShapeDtypeStruct(r.shape, bf16),
        grid_spec=pltpu.PrefetchScalarGridSpec(
            num_scalar_prefetch=1, grid=(N_CHIPS, nb),
            in_specs=[pl.BlockSpec((1, br, cols), lambda j, i, c_ref: (j, c_ref[0] * nb + i, 0)),
                      pl.BlockSpec((1, br, cols), lambda j, i, c_ref: (j, i, 0))],
            out_specs=pl.BlockSpec((1, br, cols), lambda j, i, c_ref: (j, i, 0))),
        compiler_params=_params("arbitrary", "arbitrary"))(core_arr, g, r)


def _chip_sum(s, r, place, layer, final):
    _, rows, cols = s.shape
    br = _sum_rows(rows)
    nb = rows // br

    def body(place_ref, s_ref, r_ref, *rest):
        o_ref = rest[-1]
        acc = s_ref[0].astype(f32)
        for p in range(3):
            acc = acc + r_ref[p].astype(f32)
        o_ref[...] = acc

    carried = [] if final is None else [final]
    return _pcall(
        body, name=f"chip_sum_{layer}", out_shape=jax.ShapeDtypeStruct((DEPTH, 2 * rows, cols), f32),
        grid_spec=pltpu.PrefetchScalarGridSpec(
            num_scalar_prefetch=1, grid=(nb,),
            in_specs=[pl.BlockSpec((1, br, cols), lambda i, place_ref: (place_ref[0], i, 0)),
                      pl.BlockSpec((3, br, cols), lambda i, place_ref: (0, i, 0))] + [ANY] * len(carried),
            out_specs=pl.BlockSpec((None, br, cols), lambda i, place_ref: (layer, place_ref[1] * nb + i, 0))),
        input_output_aliases={3: 0} if carried else {},
        compiler_params=_params("arbitrary"))(place, s, r, *carried)


def _inproj(xin, win, layer, comm=None):
    T = xin.shape[0]
    tm = 512
    cast = xin.dtype != bf16

    def body(x_ref, w_ref, o_ref, *xb_ref):
        xt = x_ref[...].astype(bf16)
        if cast:
            xb_ref[0][...] = xt
        for j in range(N_CHIPS):
            o_ref[:, j * W_IN_COLS:(j + 1) * W_IN_COLS] = jnp.dot(xt, w_ref[j], preferred_element_type=f32).astype(bf16)

    tile = pl.BlockSpec((tm, D_MODEL), lambda i: (i, 0))
    outs, carried = _pcall_carry(
        body, comm, n_in=2, n_out=1 + cast, name=f"inproj_{layer}", grid=(T // tm,),
        out_shape=(jax.ShapeDtypeStruct((T, N_IN), bf16),) + ((jax.ShapeDtypeStruct((T, D_MODEL), bf16),) if cast else ()),
        in_specs=[tile, pl.BlockSpec((N_CHIPS, D_MODEL, W_IN_COLS), lambda i: (0, 0, 0), pipeline_mode=pl.Buffered(1))],
        out_specs=(pl.BlockSpec((tm, N_IN), lambda i: (i, 0)),) + ((tile,) if cast else ()),
        compiler_params=_params("arbitrary"))(xin, win)
    return outs[0], (outs[1] if cast else xin), carried


def _rel_index_rows():
    j = lax.broadcasted_iota(jnp.int32, (N_REL_PAD, KG), 1)
    r = lax.broadcasted_iota(jnp.int32, (N_REL_PAD, KG), 0)
    off = jnp.where(j < KG - 2 * CHUNK, j, j - KG)
    idx = jnp.clip(N_PREV * CHUNK - off, -REL_CLIP, REL_CLIP) + REL_CLIP
    return (idx == r).astype(f32)


def _bias_expand(table_pad, layer, comm=None):
    def body(t_ref, o_ref, row_scr):
        h = pl.program_id(0)

        @pl.when(h == 0)
        def _():
            row_scr[...] = jnp.dot(t_ref[...], _rel_index_rows(), precision=lax.Precision.HIGHEST,
                                   preferred_element_type=f32)

        q = lax.broadcasted_iota(jnp.int32, (QG, KG), 0)
        k = lax.broadcasted_iota(jnp.int32, (QG, KG), 1)
        band = (k // CHUNK >= q // CHUNK) & (k // CHUNK <= q // CHUNK + N_PREV)
        t = jnp.broadcast_to(row_scr[pl.ds(h, 1), :], (QG, KG))
        for b in range(8):
            t = jnp.where(((q >> b) & 1) == 1, pltpu.roll(t, 1 << b, axis=1), t)
        for v in range(3):
            o_ref[v] = jnp.where(band & (k >= (2 - v) * QG), t, NEG)

    (bias,), carried = _pcall_carry(
        body, comm, n_in=1, n_out=1, name=f"bias_expand_{layer}", grid=(N_HEADS,),
        out_shape=(jax.ShapeDtypeStruct((3, N_HEADS, QG, KG), f32),),
        in_specs=[pl.BlockSpec((N_HEADS, N_REL_PAD), lambda h: (0, 0))],
        out_specs=(pl.BlockSpec((3, None, QG, KG), lambda h: (0, h, 0, 0)),),
        scratch_shapes=[pltpu.VMEM((N_HEADS, KG), f32)], compiler_params=_params("arbitrary"))(table_pad)
    return bias, carried


def _bias_reduce(dbias, layer):
    def body(d_ref, o_ref, row_scr):
        q = lax.broadcasted_iota(jnp.int32, (QG, DB_COLS), 0)
        k = lax.broadcasted_iota(jnp.int32, (QG, DB_COLS), 1)
        for h in range(N_HEADS):
            t = jnp.where(k > q, d_ref[h], 0.0)
            for b in range(8):
                t = jnp.where(((q >> b) & 1) == 1, pltpu.roll(t, DB_COLS - (1 << b), axis=1), t)
            row_scr[h:h + 1, :] = jnp.sum(t, axis=0, keepdims=True)
        r = lax.broadcasted_iota(jnp.int32, (N_REL_PAD, DB_COLS), 0)
        off = lax.broadcasted_iota(jnp.int32, (N_REL_PAD, DB_COLS), 1)
        own = (off >= 1) & (off < REL_CLIP + CHUNK)
        sel = jnp.where(own & (r == 2 * REL_CLIP - off), 1.0, 0.0) - jnp.where(own & (r == 2 * REL_CLIP), 1.0, 0.0)
        o_ref[...] = lax.dot_general(row_scr[...], sel, NT, precision=lax.Precision.HIGHEST, preferred_element_type=f32)

    return _pcall(body, name=f"bias_reduce_{layer}", out_shape=jax.ShapeDtypeStruct((N_HEADS, N_REL_PAD), f32),
                  scratch_shapes=[pltpu.VMEM((N_HEADS, DB_COLS), f32)],
                  compiler_params=pltpu.CompilerParams(vmem_limit_bytes=VMEM_LIMIT))(dbias)


FWD_PAIRS = 8
BWD_PAIRS = 4


def _key_specs(n_groups, npairs, slab):
    per_slab = E_MIX // (128 * npairs)
    return [pl.BlockSpec((QG, 128 * npairs), functools.partial(
        lambda hp, g, jj: (jnp.clip(g - 2 + jj, 0, n_groups - 1), slab * per_slab + hp), jj=jj)) for jj in range(3)]


def _bias_spec(npairs):
    return pl.BlockSpec((None, 2 * npairs, QG, KG), lambda hp, g: (jnp.minimum(g, 2), hp, 0, 0))


def _attn_fwd(h, bias, layer, comm=None):
    T = h.shape[0]
    n_groups = T // QG
    scale = 1.0 / math.sqrt(HEAD_DIM)

    def body(q_ref, k0, k1, k2, v0, v1, v2, b_ref, o_ref, lse_ref):
        lane = lax.broadcasted_iota(jnp.int32, (1, 128), 1)
        ones = jnp.ones((KG, 128), bf16)
        lse = jnp.zeros((QG, 128), f32)
        for pp in range(FWD_PAIRS):
            cs = slice(pp * 128, (pp + 1) * 128)
            q2 = q_ref[:, cs] * scale
            kc = jnp.concatenate([k0[:, cs], k1[:, cs], k2[:, cs]], axis=0)
            vc = jnp.concatenate([jnp.concatenate([v0[:, cs], v1[:, cs], v2[:, cs]], axis=0), ones], axis=1)
            outs = []
            for hh in range(2):
                qm = jnp.where(lane // HEAD_DIM == hh, q2, jnp.zeros_like(q2))
                s = lax.dot_general(qm, kc, NT, preferred_element_type=f32) + b_ref[2 * pp + hh]
                m = jnp.max(s, axis=1, keepdims=True)
                ol = jnp.dot(jnp.exp(s - m).astype(bf16), vc, preferred_element_type=f32)
                outs.append(ol[:, :128] / ol[:, 128:])
                lse = jnp.where(lane == 2 * pp + hh, m + jnp.log(ol[:, 128:]), lse)
            o_ref[:, cs] = jnp.where(lane // HEAD_DIM == 0, outs[0], outs[1]).astype(bf16)
        lse_ref[...] = lse

    (mix, lse), carried = _pcall_carry(
        body, comm, n_in=8, n_out=2, name=f"attn_fwd_{layer}", grid=(N_HEADS // (2 * FWD_PAIRS), n_groups),
        out_shape=(jax.ShapeDtypeStruct((T, E_MIX), bf16), jax.ShapeDtypeStruct((T, 128), f32)),
        in_specs=[pl.BlockSpec((QG, 128 * FWD_PAIRS), lambda hp, g: (g, hp))] + _key_specs(n_groups, FWD_PAIRS, 1)
        + _key_specs(n_groups, FWD_PAIRS, 2) + [_bias_spec(FWD_PAIRS)],
        out_specs=(pl.BlockSpec((QG, 128 * FWD_PAIRS), lambda hp, g: (g, hp)), pl.BlockSpec((QG, 128), lambda hp, g: (g, 0))),
        compiler_params=_params("arbitrary", "arbitrary"))(h, h, h, h, h, h, h, bias)
    return mix, lse, carried


def _halo_rows(ref, r):
    return ref[r:r + 1, :].astype(f32)


def _conv_taps(cu, p6, p7, w_ref):
    row = lax.broadcasted_iota(jnp.int32, cu.shape, 0)
    r1 = jnp.where(row == 0, p7, pltpu.roll(cu, 1, axis=0))
    r2 = jnp.where(row == 0, p6, jnp.where(row == 1, p7, pltpu.roll(cu, 2, axis=0)))
    return w_ref[2:3, :] * cu + w_ref[1:2, :] * r1 + w_ref[0:1, :] * r2, r1, r2


def _conv_fwd(h, w, layer, comm=None):
    T = h.shape[0]
    tm = 512

    def body(bg_ref, cg_ref, u_ref, cgp_ref, up_ref, w_ref, o_ref):
        first = (pl.program_id(0) == 0).astype(f32)
        cu = cg_ref[...].astype(f32) * u_ref[...].astype(f32)
        p6 = _halo_rows(cgp_ref, 14) * _halo_rows(up_ref, 14) * (1.0 - first)
        p7 = _halo_rows(cgp_ref, 15) * _halo_rows(up_ref, 15) * (1.0 - first)
        conv, _, _ = _conv_taps(cu, p6, p7, w_ref)
        o_ref[...] = (bg_ref[...].astype(f32) * conv).astype(bf16)

    prev = lambda slab: pl.BlockSpec((16, E_MIX), lambda i: (jnp.maximum(i * (tm // 16) - 1, 0), slab))
    (mix,), carried = _pcall_carry(
        body, comm, n_in=6, n_out=1, name=f"conv_fwd_{layer}", grid=(T // tm,),
        out_shape=(jax.ShapeDtypeStruct((T, E_MIX), bf16),),
        in_specs=[pl.BlockSpec((tm, E_MIX), lambda i: (i, 0)), pl.BlockSpec((tm, E_MIX), lambda i: (i, 1)),
                  pl.BlockSpec((tm, E_MIX), lambda i: (i, 2)), prev(1), prev(2),
                  pl.BlockSpec((CONV_W, E_MIX), lambda i: (0, 0))],
        out_specs=(pl.BlockSpec((tm, E_MIX), lambda i: (i, 0)),),
        compiler_params=_params("arbitrary"))(h, h, h, h, h, w)
    return mix, carried


def _kv_mem(memb, wkv):
    def body(m_ref, w_ref, o_ref):
        o_ref[...] = jnp.dot(m_ref[...], w_ref[...], preferred_element_type=f32).astype(bf16)

    return _pcall(body, name="kv_mem", out_shape=jax.ShapeDtypeStruct((N_MEM, 2 * E_MEM), bf16),
                  compiler_params=pltpu.CompilerParams(vmem_limit_bytes=VMEM_LIMIT))(memb, wkv)


def _mem_probs(qm_ref, kv_ref, hh):
    qh = qm_ref[:, hh * MEM_HEAD_DIM:(hh + 1) * MEM_HEAD_DIM]
    kh = kv_ref[:, hh * MEM_HEAD_DIM:(hh + 1) * MEM_HEAD_DIM]
    vh = kv_ref[:, E_MEM + hh * MEM_HEAD_DIM:E_MEM + (hh + 1) * MEM_HEAD_DIM]
    s = lax.dot_general(qh, kh, NT, preferred_element_type=f32) * (1.0 / math.sqrt(MEM_HEAD_DIM))
    e = jnp.exp(s - jnp.max(s, axis=1, keepdims=True))
    return e / jnp.sum(e, axis=1, keepdims=True), qh, kh, vh


def _h_tail_specs(tm):
    return [pl.BlockSpec((tm, E_MEM), functools.partial(lambda i, cb: (i, cb), cb=cb)) for cb in (6, 7, 8, 9)]


def _post_fwd(h, mix, kv, wout, x, g, b, layer, target=None):
    T = x.shape[0]
    tm = 512

    def body(qm_ref, z0, z1, z2, mix_ref, kv_ref, w_ref, x_ref, g_ref, b_ref, *rest):
        mem = jnp.concatenate(
            [jnp.dot(_mem_probs(qm_ref, kv_ref, hh)[0].astype(bf16), kv_ref[:, E_MEM + hh * MEM_HEAD_DIM:E_MEM + (hh + 1) * MEM_HEAD_DIM],
                     preferred_element_type=f32) for hh in range(MEM_HEADS)], axis=1)
        z = jnp.concatenate([z0[...], z1[...], z2[...]], axis=1)
        one = jnp.ones((), bf16)
        y = jnp.concatenate([mix_ref[...], mem.astype(bf16)], axis=1) * (z * (one / (one + jnp.exp(-z))))
        out = jnp.dot(y, w_ref[...], preferred_element_type=f32)
        r = DN_ALPHA * x_ref[...] + out
        mu = jnp.mean(r, axis=1, keepdims=True)
        var = jnp.mean(jnp.square(r - mu), axis=1, keepdims=True)
        rstd = lax.rsqrt(var + LN_EPS)
        xhat = (r - mu) * rstd
        xn = xhat * g_ref[...] + b_ref[...]
        if target is None:
            xn_ref, xb_ref, xh_ref, rs_ref = rest
            xn_ref[...] = xn
            xb_ref[...] = xn.astype(bf16)
        else:
            t_ref, l_ref, d_ref, xh_ref, rs_ref = rest

            @pl.when(pl.program_id(0) == 0)
            def _():
                l_ref[...] = jnp.zeros_like(l_ref)

            err = xn - t_ref[...]
            d_ref[...] = err * (1.0 / D_MODEL)
            l_ref[...] += jnp.sum(jnp.square(err))
        xh_ref[...] = xhat
        rs_ref[...] = rstd

    tile = lambda w: pl.BlockSpec((tm, w), lambda i: (i, 0))
    const = lambda r, c: pl.BlockSpec((r, c), lambda i: (0, 0))
    in_specs = _h_tail_specs(tm) + [tile(E_MIX), const(N_MEM, 2 * E_MEM), const(E_BRANCH, D_MODEL), tile(D_MODEL),
                                    const(1, D_MODEL), const(1, D_MODEL)]
    stats = (jax.ShapeDtypeStruct((T, D_MODEL), f32), jax.ShapeDtypeStruct((T, 1), f32))
    if target is None:
        return _pcall(
            body, name=f"post_fwd_{layer}", grid=(T // tm,),
            out_shape=(jax.ShapeDtypeStruct((T, D_MODEL), f32), jax.ShapeDtypeStruct((T, D_MODEL), bf16)) + stats,
            in_specs=in_specs, out_specs=(tile(D_MODEL), tile(D_MODEL), tile(D_MODEL), tile(1)),
            compiler_params=_params("arbitrary"))(h, h, h, h, mix, kv, wout, x, g, b)
    return _pcall(
        body, name=f"post_fwd_loss_{layer}", grid=(T // tm,),
        out_shape=(jax.ShapeDtypeStruct((8, 128), f32), jax.ShapeDtypeStruct((T, D_MODEL), f32)) + stats,
        in_specs=in_specs + [tile(D_MODEL)], out_specs=(const(8, 128), tile(D_MODEL), tile(D_MODEL), tile(1)),
        compiler_params=_params("arbitrary"))(h, h, h, h, mix, kv, wout, x, g, b, target)


def _post_bwd(dxn, xhat, rstd, g, h, mix, kv, wout, layer, comm=None):
    T = dxn.shape[0]
    tm = 256
    n_tiles = T // tm
    inv = 1.0 / math.sqrt(MEM_HEAD_DIM)

    def body(dxn_ref, xh_ref, rs_ref, g_ref, qm_ref, z0, z1, z2, mix_ref, kv_ref, w_ref,
             dxp_ref, dhb_ref, dmix_ref, dkv_ref, dwo_out, dg_ref, db_ref, dwo_ref):
        @pl.when(pl.program_id(0) == 0)
        def _():
            dkv_ref[...] = jnp.zeros_like(dkv_ref)
            dwo_ref[...] = jnp.zeros_like(dwo_ref)
            dg_ref[...] = jnp.zeros_like(dg_ref)
            db_ref[...] = jnp.zeros_like(db_ref)

        dy, xh = dxn_ref[...], xh_ref[...]
        dg_ref[0:1, :] += jnp.sum(dy * xh, axis=0, keepdims=True)
        db_ref[0:1, :] += jnp.sum(dy, axis=0, keepdims=True)
        gx = dy * g_ref[...]
        dr = rs_ref[...] * (gx - jnp.mean(gx, axis=1, keepdims=True) - xh * jnp.mean(gx * xh, axis=1, keepdims=True))
        dxp_ref[...] = DN_ALPHA * dr
        dob = dr.astype(bf16)

        probs = [_mem_probs(qm_ref, kv_ref, hh) for hh in range(MEM_HEADS)]
        mem = jnp.concatenate([jnp.dot(p.astype(bf16), vh, preferred_element_type=f32) for p, _, _, vh in probs], axis=1)
        z = jnp.concatenate([z0[...], z1[...], z2[...]], axis=1).astype(f32)
        act, sig = _silu_parts(z)
        cat = jnp.concatenate([mix_ref[...].astype(f32), mem], axis=1)
        yb = (cat * act).astype(bf16)
        dwo_ref[...] += lax.dot_general(yb, dob, TN, preferred_element_type=f32)
        dyv = lax.dot_general(dob, w_ref[...], NT, preferred_element_type=f32)
        dz = dyv * cat * (sig * (1.0 + z * (1.0 - sig)))
        dcat = dyv * act
        dmix_ref[...] = dcat[:, :E_MIX].astype(bf16)
        dqs = []
        for hh, (p, qh, kh, vh) in enumerate(probs):
            dmem = dcat[:, E_MIX + hh * MEM_HEAD_DIM:E_MIX + (hh + 1) * MEM_HEAD_DIM].astype(bf16)
            dp = lax.dot_general(dmem, vh, NT, preferred_element_type=f32)
            ds = (p * (dp - jnp.sum(p * dp, axis=1, keepdims=True))).astype(bf16)
            dqs.append(jnp.dot(ds, kh, preferred_element_type=f32) * inv)
            dkv_ref[:, hh * MEM_HEAD_DIM:(hh + 1) * MEM_HEAD_DIM] += lax.dot_general(ds, qh, TN, preferred_element_type=f32) * inv
            dkv_ref[:, E_MEM + hh * MEM_HEAD_DIM:E_MEM + (hh + 1) * MEM_HEAD_DIM] += lax.dot_general(
                p.astype(bf16), dmem, TN, preferred_element_type=f32)
        dhb_ref[...] = jnp.concatenate(dqs + [dz], axis=1).astype(bf16)

        @pl.when(pl.program_id(0) == n_tiles - 1)
        def _():
            dwo_out[...] = dwo_ref[...].astype(bf16)

    tile = lambda w: pl.BlockSpec((tm, w), lambda i: (i, 0))
    const = lambda r, c: pl.BlockSpec((r, c), lambda i: (0, 0))
    return _pcall_carry(
        body, comm, n_in=11, n_out=7, name=f"post_bwd_{layer}", grid=(n_tiles,),
        out_shape=(jax.ShapeDtypeStruct((T, D_MODEL), f32), jax.ShapeDtypeStruct((T, E_MEM + E_BRANCH), bf16),
                   jax.ShapeDtypeStruct((T, E_MIX), bf16), jax.ShapeDtypeStruct((N_MEM, 2 * E_MEM), f32),
                   jax.ShapeDtypeStruct((E_BRANCH, D_MODEL), bf16), jax.ShapeDtypeStruct((8, D_MODEL), f32),
                   jax.ShapeDtypeStruct((8, D_MODEL), f32)),
        in_specs=[tile(D_MODEL), tile(D_MODEL), tile(1), const(1, D_MODEL)] + _h_tail_specs(tm)
        + [tile(E_MIX), const(N_MEM, 2 * E_MEM), const(E_BRANCH, D_MODEL)],
        out_specs=(tile(D_MODEL), tile(E_MEM + E_BRANCH), tile(E_MIX), const(N_MEM, 2 * E_MEM),
                   const(E_BRANCH, D_MODEL), const(8, D_MODEL), const(8, D_MODEL)),
        scratch_shapes=[pltpu.VMEM((E_BRANCH, D_MODEL), f32)],
        compiler_params=_params("arbitrary"))(dxn, xhat, rstd, g, h, h, h, h, mix, kv, wout)


def _attn_bwd(h, bias, dmix, lse, layer, comm=None):
    T = h.shape[0]
    n_groups = T // QG
    scale = 1.0 / math.sqrt(HEAD_DIM)

    def body(q_ref, k0, k1, k2, v0, v1, v2, do_ref, b_ref, lse_ref, dq_ref, dk_ref, dv_ref, db_ref, acck, accv):
        g = pl.program_id(1)

        @pl.when(g == 0)
        def _():
            acck[...] = jnp.zeros_like(acck)
            accv[...] = jnp.zeros_like(accv)
            db_ref[...] = jnp.zeros_like(db_ref)

        @pl.when(g < n_groups)
        def _():
            lane = lax.broadcasted_iota(jnp.int32, (1, 128), 1)
            first = lane // HEAD_DIM == 0
            for pp in range(BWD_PAIRS):
                cs = slice(pp * 128, (pp + 1) * 128)
                do2 = do_ref[:, cs]
                q2 = q_ref[:, cs] * scale
                kc = jnp.concatenate([k0[:, cs], k1[:, cs], k2[:, cs]], axis=0)
                vc = jnp.concatenate([v0[:, cs], v1[:, cs], v2[:, cs]], axis=0)
                q2t, do2t = q2.T, do2.T
                dqs, dks, dvs = [], [], []
                for hh in range(2):
                    hm = lane // HEAD_DIM == hh
                    head = (pl.program_id(0) * BWD_PAIRS + pp) * 2 + hh
                    lse = jnp.sum(jnp.where(lane == head, lse_ref[...], 0.0), axis=1, keepdims=True)
                    qm = jnp.where(hm, q2, jnp.zeros_like(q2))
                    dom = jnp.where(hm, do2, jnp.zeros_like(do2))
                    s = lax.dot_general(qm, kc, NT, preferred_element_type=f32) + b_ref[2 * pp + hh]
                    p = jnp.exp(s - lse)
                    dp = lax.dot_general(dom, vc, NT, preferred_element_type=f32)
                    ds = p * (dp - jnp.sum(p * dp, axis=1, keepdims=True))
                    db_ref[2 * pp + hh] += ds[:, KG - DB_COLS:]
                    dsb, pb = ds.astype(bf16), p.astype(bf16)
                    dqs.append(jnp.dot(dsb, kc, preferred_element_type=f32) * scale)
                    dks.append(jnp.dot(q2t[hh * HEAD_DIM:(hh + 1) * HEAD_DIM], dsb, preferred_element_type=f32))
                    dvs.append(jnp.dot(do2t[hh * HEAD_DIM:(hh + 1) * HEAD_DIM], pb, preferred_element_type=f32))
                dq_ref[:, cs] = jnp.where(first, dqs[0], dqs[1]).astype(bf16)
                dkc = jnp.concatenate(dks, axis=0).T
                dvc = jnp.concatenate(dvs, axis=0).T
                for jj in range(3):
                    slot = (g + 1 + jj) % 3
                    if jj == 2:
                        acck[slot, :, cs] = dkc[jj * QG:(jj + 1) * QG]
                        accv[slot, :, cs] = dvc[jj * QG:(jj + 1) * QG]
                    else:
                        acck[slot, :, cs] += dkc[jj * QG:(jj + 1) * QG]
                        accv[slot, :, cs] += dvc[jj * QG:(jj + 1) * QG]

        done = (g + 1) % 3
        dk_ref[...] = acck[done].astype(bf16)
        dv_ref[...] = accv[done].astype(bf16)

    last = n_groups - 1
    width = 128 * BWD_PAIRS
    qspec = pl.BlockSpec((QG, width), lambda hp, g: (jnp.minimum(g, last), hp))
    kout = pl.BlockSpec((QG, width), lambda hp, g: (jnp.clip(g - 2, 0, last), hp))
    dbspec = pl.BlockSpec((2 * BWD_PAIRS, QG, DB_COLS), lambda hp, g: (hp, 0, 0))
    lspec = pl.BlockSpec((QG, 128), lambda hp, g: (jnp.minimum(g, last), 0))
    return _pcall_carry(
        body, comm, n_in=10, n_out=4, name=f"attn_bwd_{layer}", grid=(N_HEADS // (2 * BWD_PAIRS), n_groups + 2),
        out_shape=(jax.ShapeDtypeStruct((T, E_MIX), bf16),) * 3 + (jax.ShapeDtypeStruct((N_HEADS, QG, DB_COLS), f32),),
        in_specs=[qspec] + _key_specs(n_groups, BWD_PAIRS, 1) + _key_specs(n_groups, BWD_PAIRS, 2)
        + [qspec, _bias_spec(BWD_PAIRS), lspec],
        out_specs=(qspec, kout, kout, dbspec),
        scratch_shapes=[pltpu.VMEM((3, QG, width), f32), pltpu.VMEM((3, QG, width), f32)],
        compiler_params=_params("arbitrary", "arbitrary"))(h, h, h, h, h, h, h, dmix, bias, lse)


def _conv_bwd(h, w, dmix, layer):
    T = h.shape[0]
    tm = 512
    n_tiles = T // tm

    def body(bg_ref, cg_ref, u_ref, cgp_ref, up_ref, dy_ref, bgn_ref, dyn_ref, w_ref, dbg_ref, dcg_ref, du_ref, dw_ref):
        i = pl.program_id(0)

        @pl.when(i == 0)
        def _():
            dw_ref[...] = jnp.zeros_like(dw_ref)

        first = (i == 0).astype(f32)
        final = (i == n_tiles - 1).astype(f32)
        bg, cg, u = bg_ref[...].astype(f32), cg_ref[...].astype(f32), u_ref[...].astype(f32)
        dy = dy_ref[...].astype(f32)
        cu = cg * u
        p6 = _halo_rows(cgp_ref, 14) * _halo_rows(up_ref, 14) * (1.0 - first)
        p7 = _halo_rows(cgp_ref, 15) * _halo_rows(up_ref, 15) * (1.0 - first)
        conv, r1, r2 = _conv_taps(cu, p6, p7, w_ref)
        dbg_ref[...] = (dy * conv).astype(bf16)
        dc = dy * bg
        n0 = _halo_rows(dyn_ref, 0) * _halo_rows(bgn_ref, 0) * (1.0 - final)
        n1 = _halo_rows(dyn_ref, 1) * _halo_rows(bgn_ref, 1) * (1.0 - final)
        row = lax.broadcasted_iota(jnp.int32, dc.shape, 0)
        f1 = jnp.where(row == tm - 1, n0, pltpu.roll(dc, tm - 1, axis=0))
        f2 = jnp.where(row == tm - 2, n0, jnp.where(row == tm - 1, n1, pltpu.roll(dc, tm - 2, axis=0)))
        dcu = w_ref[2:3, :] * dc + w_ref[1:2, :] * f1 + w_ref[0:1, :] * f2
        dcg_ref[...] = (dcu * u).astype(bf16)
        du_ref[...] = (dcu * cg).astype(bf16)
        dw_ref[0:1, :] += jnp.sum(dc * r2, axis=0, keepdims=True)
        dw_ref[1:2, :] += jnp.sum(dc * r1, axis=0, keepdims=True)
        dw_ref[2:3, :] += jnp.sum(dc * cu, axis=0, keepdims=True)

    tile = lambda slab: pl.BlockSpec((tm, E_MIX), lambda i: (i, slab))
    prev = lambda slab: pl.BlockSpec((16, E_MIX), lambda i: (jnp.maximum(i * (tm // 16) - 1, 0), slab))
    nxt = lambda slab: pl.BlockSpec((16, E_MIX), lambda i: (jnp.minimum((i + 1) * (tm // 16), T // 16 - 1), slab))
    return _pcall(
        body, name=f"conv_bwd_{layer}", grid=(n_tiles,),
        out_shape=(jax.ShapeDtypeStruct((T, E_MIX), bf16),) * 3 + (jax.ShapeDtypeStruct((8, E_MIX), f32),),
        in_specs=[tile(0), tile(1), tile(2), prev(1), prev(2), tile(0), nxt(0), nxt(0),
                  pl.BlockSpec((CONV_W, E_MIX), lambda i: (0, 0))],
        out_specs=(tile(0), tile(0), tile(0), pl.BlockSpec((8, E_MIX), lambda i: (0, 0))),
        compiler_params=_params("arbitrary"))(h, h, h, h, h, dmix, h, dmix, w)


def _inproj_bwd_dx(da, db, dc, dhb, dxp, win, layer, comm=None):
    T = dxp.shape[0]
    tm = 512

    def body(da_ref, db_ref, dc_ref, dhb_ref, dxp_ref, w_ref, o_ref):
        dh = jnp.concatenate([da_ref[...], db_ref[...], dc_ref[...], dhb_ref[...]], axis=1)
        acc = dxp_ref[...]
        for j in range(N_CHIPS):
            acc = acc + lax.dot_general(dh[:, j * W_IN_COLS:(j + 1) * W_IN_COLS], w_ref[j], NT, preferred_element_type=f32)
        o_ref[...] = acc

    tile = lambda w: pl.BlockSpec((tm, w), lambda i: (i, 0))
    (dx,), carried = _pcall_carry(
        body, comm, n_in=6, n_out=1, name=f"inproj_bwd_dx_{layer}", grid=(T // tm,),
        out_shape=(jax.ShapeDtypeStruct((T, D_MODEL), f32),),
        in_specs=[tile(E_MIX), tile(E_MIX), tile(E_MIX), tile(E_MEM + E_BRANCH), tile(D_MODEL),
                  pl.BlockSpec((N_CHIPS, D_MODEL, W_IN_COLS), lambda i: (0, 0, 0), pipeline_mode=pl.Buffered(1))],
        out_specs=(tile(D_MODEL),),
        compiler_params=_params("arbitrary"))(da, db, dc, dhb, dxp, win)
    return dx, carried


def _dh_pieces():
    pieces, col = [], 0
    for src, width in enumerate((E_MIX, E_MIX, E_MIX, E_MEM + E_BRANCH)):
        lo = 0
        while lo < width:
            j, c0 = divmod(col + lo, W_IN_COLS)
            n = min(width - lo, W_IN_COLS - c0)
            pieces.append((src, lo, lo + n, j, c0, c0 + n))
            lo += n
        col += width
    return pieces


def _inproj_bwd_dw(da, db, dc, dhb, xb, layer, comm=None):
    T = xb.shape[0]
    tm = 1024 if T % 1024 == 0 else 512
    n_tiles = T // tm

    def body(da_ref, db_ref, dc_ref, dhb_ref, x_ref, o_ref, acc, stage, sem):
        i = pl.program_id(0)

        @pl.when(i == 0)
        def _():
            acc[...] = jnp.zeros_like(acc)

        srcs = (da_ref, db_ref, dc_ref, dhb_ref)
        xt = x_ref[...]
        for s, s0, s1, j, c0, c1 in _dh_pieces():
            acc[j, :, c0:c1] += lax.dot_general(xt, srcs[s][:, s0:s1], TN, preferred_element_type=f32)

        @pl.when(i == n_tiles - 1)
        def _():
            for j in range(N_CHIPS):
                stage[...] = acc[j].astype(bf16)
                cp = pltpu.make_async_copy(stage, o_ref.at[j], sem)
                cp.start()
                cp.wait()

    tile = lambda w: pl.BlockSpec((tm, w), lambda i: (i, 0))
    (dw,), carried = _pcall_carry(
        body, comm, n_in=5, n_out=1, name=f"inproj_bwd_dw_{layer}", grid=(n_tiles,),
        out_shape=(jax.ShapeDtypeStruct((N_CHIPS, D_MODEL, W_IN_COLS), bf16),),
        in_specs=[tile(E_MIX), tile(E_MIX), tile(E_MIX), tile(E_MEM + E_BRANCH), tile(D_MODEL)],
        out_specs=(ANY,),
        scratch_shapes=[pltpu.VMEM((N_CHIPS, D_MODEL, W_IN_COLS), f32), pltpu.VMEM((D_MODEL, W_IN_COLS), bf16),
                        pltpu.SemaphoreType.DMA],
        compiler_params=_params("arbitrary"))(da, db, dc, dhb, xb)
    return dw, carried


def _kv_mem_bwd(memb, dkv, layer):
    def body(m_ref, d_ref, o_ref):
        o_ref[...] = lax.dot_general(m_ref[...], d_ref[...].astype(bf16), TN, preferred_element_type=f32).astype(bf16)

    return _pcall(body, name=f"kv_mem_bwd_{layer}", out_shape=jax.ShapeDtypeStruct((D_MODEL, 2 * E_MEM), bf16),
                  compiler_params=pltpu.CompilerParams(vmem_limit_bytes=VMEM_LIMIT))(memb, dkv)


def _adamw(w, g, m, v, name):
    shape = w.shape
    cols = shape[-1]
    rows = w.size // cols
    args = [a.reshape(rows, cols) for a in (w, g, m, v)]
    br = 256 if rows % 256 == 0 and rows > 256 else rows

    def body(w_ref, g_ref, m_ref, v_ref, go_ref, d_ref, nm_ref, nv_ref):
        gg = g_ref[...]
        nm = ADAM_B1 * m_ref[...] + (1.0 - ADAM_B1) * gg
        nv = ADAM_B2 * v_ref[...] + (1.0 - ADAM_B2) * jnp.square(gg)
        m_hat = nm / (1.0 - ADAM_B1 ** ADAM_STEP)
        v_hat = nv / (1.0 - ADAM_B2 ** ADAM_STEP)
        go_ref[...] = gg
        d_ref[...] = -ADAM_LR * (m_hat / (jnp.sqrt(v_hat) + ADAM_EPS) + ADAM_WD * w_ref[...])
        nm_ref[...] = nm
        nv_ref[...] = nv

    spec = pl.BlockSpec((br, cols), lambda i: (i, 0))
    outs = _pcall(body, name=name, grid=(rows // br,), out_shape=(jax.ShapeDtypeStruct((rows, cols), f32),) * 4,
                  in_specs=[spec] * 4, out_specs=(spec,) * 4, compiler_params=_params("arbitrary"))(*args)
    return tuple(o.reshape(shape) for o in outs)


def kernel(x, mem, w_in, w_mem_kv, w_out, rel_bias, conv_w, ln_g, ln_b, loss_target, m_w_in, m_w_mem_kv, m_w_out, m_rel_bias, m_conv_w, m_ln_g, m_ln_b, v_w_in, v_w_mem_kv, v_w_out, v_rel_bias, v_conv_w, v_ln_g, v_ln_b):
    T = x.shape[1]
    x0 = x.reshape(T, D_MODEL)
    target = loss_target.reshape(T, D_MODEL)
    memb = mem.reshape(N_MEM, D_MODEL).astype(bf16)
    chip = 2 * lax.axis_index("x") + lax.axis_index("y")
    core = lax.axis_index("c")
    chip_arr = jnp.reshape(chip, (1,)).astype(jnp.int32)
    core_arr = jnp.reshape(core, (1,)).astype(jnp.int32)

    place = jnp.concatenate([chip_arr, core_arr])
    tables = jnp.pad(rel_bias, ((0, 0), (0, 0), (0, N_REL_PAD - N_REL)))

    shards = [w_in.astype(bf16), w_mem_kv.astype(bf16), w_out.astype(bf16)]
    biases = {}
    biases[0], arrived_in = _bias_expand(tables[0], 0, _gather_ici(shards[:1], 0))
    biases[2], (*arrived_rest, cw_g) = _bias_expand(tables[1], 2, _gather_ici(shards[1:], 0, extra=conv_w))
    gathered = {0: _comm_call(_gather_d2d(list(arrived_in) + arrived_rest), "gather_d2d_0")}
    conv_full = jnp.transpose(cw_g, (1, 2, 0, 3)).reshape(DEPTH // 2, CONV_W, E_MIX)

    xs, xbs, hs, mixes, kvs, xhats, rstds, lses = [x0], [x0], [], [], [], [], [], {}
    for layer in range(DEPTH):
        win, wkv, wout = gathered[layer]
        more = layer + 1 < DEPTH
        h, xbs[layer], arrived = _inproj(xbs[layer], win, layer, _gather_ici(shards, layer + 1) if more else None)
        passing = _gather_d2d(list(arrived)) if more else None
        if layer % 2 == 0:
            mix, lses[layer], done = _attn_fwd(h, biases[layer], layer, passing)
        else:
            mix, done = _conv_fwd(h, conv_full[layer // 2], layer, passing)
        if more:
            gathered[layer + 1] = list(done)
        kv = _kv_mem(memb, wkv.reshape(D_MODEL, 2 * E_MEM))
        first, second, xhat, rstd = _post_fwd(h, mix, kv, wout.reshape(E_BRANCH, D_MODEL), xs[layer], ln_g[layer][None, :],
                                              ln_b[layer][None, :], layer, None if more else target)
        if more:
            xs.append(first); xbs.append(second)
        else:
            lsum, dx = first, second
        hs.append(h); mixes.append(mix); kvs.append(kv); xhats.append(xhat); rstds.append(rstd)

    loss = lax.psum(lsum[0, 0], ("x", "y", "c")) * (0.5 / D_MODEL)

    dgs, dbs, dconvs, dtables = [None] * DEPTH, [None] * DEPTH, [None] * (DEPTH // 2), [None] * ((DEPTH + 1) // 2)
    finals = [None, None, None]
    above = None
    for layer in reversed(range(DEPTH)):
        h = hs[layer]
        win, wkv, wout = gathered[layer]
        (dxp, dhb, dmix, dkv, dwo, dgs[layer], dbs[layer]), from_sibling = _post_bwd(
            dx, xhats[layer], rstds[layer], ln_g[layer][None, :], h, mixes[layer], kvs[layer], wout.reshape(E_BRANCH, D_MODEL),
            layer, _sibling_exchange(above) if above else None)
        sums = [_pair_sum(g, r, core_arr, layer + 1) for g, r in zip(above, from_sibling)] if above else None
        scatter = _chip_scatter(sums) if above else None
        if layer % 2 == 0:
            (da, db, dc, dbias), from_chips = _attn_bwd(h, biases[layer], dmix, lses[layer], layer, scatter)
            dtables[layer // 2] = _bias_reduce(dbias, layer)
        else:
            da, db, dc, dconvs[layer // 2] = _conv_bwd(h, conv_full[layer // 2], dmix, layer)
        if layer > 0:
            dx, landed = _inproj_bwd_dx(da, db, dc, dhb, dxp, win, layer, scatter if layer % 2 == 1 else None)
            from_chips = landed if layer % 2 == 1 else from_chips
        share = None
        if above:
            finals = [_chip_sum(s, r, place, layer + 1, f) for s, r, f in zip(sums, from_chips, finals)]
            share = _sibling_share(finals, layer + 1)
        if layer == 0:
            pad8 = lambda a: jnp.pad(a, ((0, 8 - a.shape[0]), (0, 0)))
            small_mine = jnp.concatenate(dgs + dbs + dconvs + [pad8(t.reshape(-1, D_MODEL)) for t in dtables], axis=0)
            share = _both(share, _small_exchange(small_mine))
        g_win, shared = _inproj_bwd_dw(da, db, dc, dhb, xbs[layer], layer, share)
        if layer == 0:
            *shared, small_slots = shared
        finals = list(shared) if above else finals
        above = [g_win, _kv_mem_bwd(memb, dkv, layer).reshape(N_CHIPS, W_KV_ROWS, 2 * E_MEM),
                 dwo.reshape(N_CHIPS, W_OUT_ROWS, D_MODEL)]
    from_sibling = _comm_call(_sibling_exchange(above), "sibling_exchange_0")
    sums = [_pair_sum(g, r, core_arr, 0) for g, r in zip(above, from_sibling)]
    dx, from_chips = _inproj_bwd_dx(da, db, dc, dhb, dxp, win, 0, _chip_scatter(sums))
    finals = [_chip_sum(s, r, place, 0, f) for s, r, f in zip(sums, from_chips, finals)]
    grad_w_in, grad_w_mem_kv, grad_w_out = _comm_call(_sibling_share(finals, 0), "sibling_share_0")
    grad_x = dx.reshape(1, T, D_MODEL)

    device_arr = jnp.reshape(2 * chip + core, (1,)).astype(jnp.int32)
    small = _small_sum(small_slots, small_mine, device_arr)
    grad_ln_g = jnp.stack([small[8 * l] for l in range(DEPTH)])
    grad_ln_b = jnp.stack([small[8 * (DEPTH + l)] for l in range(DEPTH)])
    conv_all = jnp.stack([small[8 * (2 * DEPTH + a):8 * (2 * DEPTH + a) + CONV_W] for a in range(DEPTH // 2)])
    grad_conv_w = lax.dynamic_slice_in_dim(conv_all, chip * (E_MIX // N_CHIPS), E_MIX // N_CHIPS, axis=2)
    t0 = 8 * (2 * DEPTH + DEPTH // 2)
    grad_rel_bias = jnp.stack([small[t0 + 8 * a:t0 + 8 * a + 6].reshape(N_HEADS, N_REL_PAD)[:, :N_REL]
                               for a in range((DEPTH + 1) // 2)])

    grads = [grad_w_in, grad_w_mem_kv, grad_w_out, grad_rel_bias, grad_conv_w, grad_ln_g, grad_ln_b]
    weights = [w_in, w_mem_kv, w_out, rel_bias, conv_w, ln_g, ln_b]
    moms = [m_w_in, m_w_mem_kv, m_w_out, m_rel_bias, m_conv_w, m_ln_g, m_ln_b]
    vels = [v_w_in, v_w_mem_kv, v_w_out, v_rel_bias, v_conv_w, v_ln_g, v_ln_b]
    names = ["w_in", "w_mem_kv", "w_out", "rel_bias", "conv_w", "ln_g", "ln_b"]
    upd = [_adamw(w, g, m, v, f"adamw_{n}") for w, g, m, v, n in zip(weights, grads, moms, vels, names)]
    grads, deltas, new_m, new_v = zip(*upd)
    return (loss, grad_x, *grads, *deltas, *new_m, *new_v)
```

```python
import functools
import math

import jax
import jax.numpy as jnp
from jax import lax
from jax.experimental import pallas as pl
from jax.experimental.pallas import tpu as pltpu

f32, bf16 = jnp.float32, jnp.bfloat16

D_MODEL = 1024
DEPTH = 4
CHUNK = 64
N_PREV = 8
N_HEADS = 16
HEAD_DIM = 64
E_MIX = 1024
REL_CLIP = 128
N_REL = 2 * REL_CLIP + 1
N_REL_PAD = 384
CONV_W = 3
N_MEM = 256
MEM_HEADS = 4
MEM_HEAD_DIM = 128
E_MEM = 512
E_BRANCH = E_MIX + E_MEM
N_IN = 3 * E_MIX + E_MEM + E_BRANCH
N_CHIPS = 4
W_IN_COLS = N_IN // N_CHIPS
W_KV_ROWS = D_MODEL // N_CHIPS
W_OUT_ROWS = E_BRANCH // N_CHIPS
DN_ALPHA = (2.0 * DEPTH) ** 0.25
LN_EPS = 1e-5
ADAM_LR, ADAM_B1, ADAM_B2, ADAM_EPS, ADAM_WD, ADAM_STEP = 0.001, 0.9, 0.999, 1e-08, 0.01, 10

QG = 4 * CHUNK
KG = QG + N_PREV * CHUNK
DB_COLS = KG // 2
NEG = -1e30
VMEM_LIMIT = 56 * 1024 * 1024

NT = (((1,), (1,)), ((), ()))
TN = (((0,), (0,)), ((), ()))
MESH = pl.DeviceIdType.MESH
ANY = pl.BlockSpec(memory_space=pl.ANY)


def _pcall(body, **kw):
    return pl.pallas_call(body, **kw)


def _params(*sem):
    return pltpu.CompilerParams(dimension_semantics=sem, vmem_limit_bytes=VMEM_LIMIT)


def _silu_parts(z):
    sig = 1.0 / (1.0 + jnp.exp(-z))
    return z * sig, sig


class _Comm:
    def __init__(self, inputs, out_shapes, aliases, n_sems, copies):
        self.inputs, self.out_shapes, self.aliases, self.n_sems, self.copies = inputs, out_shapes, aliases, n_sems, copies

    def start(self, cin, cout, send, recv):
        for cp in self.copies(cin, cout, send, recv)[0]:
            cp.start()

    def wait(self, cin, cout, send, recv):
        sends, recvs = self.copies(cin, cout, send, recv)
        for cp in recvs:
            cp.wait_recv()
        for cp in sends:
            cp.wait_send()


def _pcall_carry(body, comm, *, n_in, n_out, **kw):
    if comm is None:
        return lambda *args: (_pcall(body, **kw)(*args), ())
    grid = kw["grid"]
    k_in, k_out = len(comm.inputs), len(comm.out_shapes)

    def carried(*refs):
        ins, cin = refs[:n_in], refs[n_in:n_in + k_in]
        outs = refs[n_in + k_in:n_in + k_in + n_out]
        cout = refs[n_in + k_in + n_out:n_in + k_in + n_out + k_out]
        scratch, send, recv = refs[n_in + k_in + n_out + k_out:-2], refs[-2], refs[-1]
        ids = [pl.program_id(a) for a in range(len(grid))]
        first = functools.reduce(jnp.logical_and, [i == 0 for i in ids])
        last = functools.reduce(jnp.logical_and, [i == n - 1 for i, n in zip(ids, grid)])

        @pl.when(first)
        def _():
            comm.start(cin, cout, send, recv)

        body(*ins, *outs, *scratch)

        @pl.when(last)
        def _():
            comm.wait(cin, cout, send, recv)

    kw = dict(kw)
    kw["in_specs"] = list(kw["in_specs"]) + [ANY] * k_in
    kw["out_specs"] = tuple(kw["out_specs"]) + (ANY,) * k_out
    kw["out_shape"] = tuple(kw["out_shape"]) + tuple(comm.out_shapes)
    kw["scratch_shapes"] = list(kw.get("scratch_shapes", ())) + [pltpu.SemaphoreType.DMA((comm.n_sems,))] * 2
    aliases = dict(kw.get("input_output_aliases", {}))
    aliases.update({n_in + ci: n_out + co for ci, co in comm.aliases.items()})
    kw["input_output_aliases"] = aliases

    def run(*args):
        res = _pcall(carried, **kw)(*args, *comm.inputs)
        return res[:n_out], res[n_out:]

    return run


def _comm_call(comm, name):
    k_in = len(comm.inputs)

    def body(*refs):
        cin, cout, send, recv = refs[:k_in], refs[k_in:-2], refs[-2], refs[-1]
        comm.start(cin, cout, send, recv)
        comm.wait(cin, cout, send, recv)

    return _pcall(body, name=name, out_shape=tuple(comm.out_shapes), in_specs=[ANY] * k_in,
                  out_specs=(ANY,) * len(comm.out_shapes), input_output_aliases=dict(comm.aliases),
                  scratch_shapes=[pltpu.SemaphoreType.DMA((comm.n_sems,))] * 2)(*comm.inputs)


def _place():
    x, y, c = lax.axis_index("x"), lax.axis_index("y"), lax.axis_index("c")
    return x, y, c, 2 * x + y, (x, y, 1 - c), [(1 - x, y), (x, 1 - y), (1 - x, 1 - y)]


def _rcopy(send, recv, k, src, dst, to):
    return pltpu.make_async_remote_copy(src_ref=src, dst_ref=dst, send_sem=send.at[k], recv_sem=recv.at[k],
                                        device_id=to, device_id_type=MESH)


def _half(ref_rows, core):
    return pl.ds(core * (ref_rows // 2), ref_rows // 2)


def _gather_ici(shards, layer, extra=None):
    extras = [] if extra is None else [extra]
    n = len(shards)

    def copies(cin, cout, send, recv):
        x, y, c, me, sibling, chips = _place()
        sends, recvs = [], []
        for a in range(n):
            s, g = cin[a], cout[a]
            rows = s.shape[1]
            mine = _half(rows, c)
            sends.append(_rcopy(send, recv, 4 * a, s.at[layer], g.at[me], sibling))
            recvs.append(_rcopy(send, recv, 4 * a, s.at[layer], g.at[me], sibling))
            for p, (px, py) in enumerate(chips):
                sends.append(_rcopy(send, recv, 4 * a + 1 + p, s.at[layer, mine], g.at[me, mine], (px, py, c)))
                recvs.append(_rcopy(send, recv, 4 * a + 1 + p, s.at[layer, mine], g.at[2 * px + py, mine], (px, py, c)))
        for e in range(len(extras)):
            s, g = cin[n + e], cout[n + e]
            k = 4 * (n + e)
            sends.append(_rcopy(send, recv, k, s, g.at[me], sibling))
            recvs.append(_rcopy(send, recv, k, s, g.at[me], sibling))
            for p, (px, py) in enumerate(chips):
                sends.append(_rcopy(send, recv, k + 1 + p, s, g.at[me], (px, py, c)))
                recvs.append(_rcopy(send, recv, k + 1 + p, s, g.at[2 * px + py], (px, py, c)))
        return sends, recvs

    out_shapes = [jax.ShapeDtypeStruct((N_CHIPS,) + s.shape[1:], s.dtype) for s in shards]
    out_shapes += [jax.ShapeDtypeStruct((N_CHIPS,) + e.shape, e.dtype) for e in extras]
    return _Comm(list(shards) + extras, out_shapes, {}, 4 * (n + len(extras)), copies)


def _gather_d2d(gathered):
    n = len(gathered)

    def copies(cin, cout, send, recv):
        x, y, c, me, sibling, chips = _place()
        sends, recvs = [], []
        for a in range(n):
            g = cout[a]
            rows = g.shape[1]
            for p, (px, py) in enumerate(chips):
                mine, theirs = g.at[2 * px + py, _half(rows, c)], g.at[2 * px + py, _half(rows, 1 - c)]
                sends.append(_rcopy(send, recv, 3 * a + p, mine, mine, sibling))
                recvs.append(_rcopy(send, recv, 3 * a + p, theirs, theirs, sibling))
        return sends, recvs

    return _Comm(list(gathered), [jax.ShapeDtypeStruct(g.shape, g.dtype) for g in gathered],
                 {a: a for a in range(n)}, 3 * n, copies)


class _SemView:
    def __init__(self, ref, base):
        self.ref, self.base, self.at = ref, base, self

    def __getitem__(self, k):
        return self.ref.at[self.base + k]


def _both(a, b):
    ka, ma = len(a.inputs), len(a.out_shapes)

    def copies(cin, cout, send, recv):
        sa, ra = a.copies(cin[:ka], cout[:ma], send, recv)
        sb, rb = b.copies(cin[ka:], cout[ma:], _SemView(send, a.n_sems), _SemView(recv, a.n_sems))
        return sa + sb, ra + rb

    aliases = dict(a.aliases)
    aliases.update({ka + ci: ma + co for ci, co in b.aliases.items()})
    return _Comm(a.inputs + b.inputs, a.out_shapes + b.out_shapes, aliases, a.n_sems + b.n_sems, copies)


def _small_exchange(buf):
    def copies(cin, cout, send, recv):
        x, y, c, me, sibling, chips = _place()
        sends, recvs = [], []
        for r in range(1, 8):
            px, py, pc = x ^ ((r >> 2) & 1), y ^ ((r >> 1) & 1), c ^ (r & 1)
            sends.append(_rcopy(send, recv, r - 1, cin[0], cout[0].at[2 * me + c], (px, py, pc)))
            recvs.append(_rcopy(send, recv, r - 1, cin[0], cout[0].at[4 * px + 2 * py + pc], (px, py, pc)))
        return sends, recvs

    return _Comm([buf], [jax.ShapeDtypeStruct((8,) + buf.shape, buf.dtype)], {}, 7, copies)


def _small_sum(slots, buf, device_arr):
    rows, cols = buf.shape

    def body(d_ref, s_ref, b_ref, o_ref):
        d = pl.program_id(0)
        val = jnp.where(d == d_ref[0], b_ref[...], s_ref[...])

        @pl.when(d == 0)
        def _():
            o_ref[...] = val

        @pl.when(d > 0)
        def _():
            o_ref[...] += val

    return _pcall(
        body, name="small_sum", out_shape=jax.ShapeDtypeStruct((rows, cols), f32),
        grid_spec=pltpu.PrefetchScalarGridSpec(
            num_scalar_prefetch=1, grid=(8,),
            in_specs=[pl.BlockSpec((None, rows, cols), lambda d, d_ref: (jnp.where(d == d_ref[0], (d + 1) % 8, d), 0, 0)),
                      pl.BlockSpec((rows, cols), lambda d, d_ref: (0, 0))],
            out_specs=pl.BlockSpec((rows, cols), lambda d, d_ref: (0, 0))),
        compiler_params=_params("arbitrary"))(device_arr, slots, buf)


def _sibling_exchange(gs):
    def copies(cin, cout, send, recv):
        x, y, c, me, sibling, chips = _place()
        sends = [_rcopy(send, recv, a, g.at[:, _half(g.shape[1], 1 - c)], r, sibling) for a, (g, r) in enumerate(zip(cin, cout))]
        return sends, sends

    shapes = [jax.ShapeDtypeStruct((N_CHIPS, g.shape[1] // 2, g.shape[2]), g.dtype) for g in gs]
    return _Comm(list(gs), shapes, {}, len(gs), copies)


def _chip_scatter(ss):
    def copies(cin, cout, send, recv):
        x, y, c, me, sibling, chips = _place()
        sends = [_rcopy(send, recv, 3 * a + p, s.at[2 * px + py], r.at[p], (px, py, c))
                 for a, (s, r) in enumerate(zip(cin, cout)) for p, (px, py) in enumerate(chips)]
        return sends, sends

    shapes = [jax.ShapeDtypeStruct((3,) + s.shape[1:], s.dtype) for s in ss]
    return _Comm(list(ss), shapes, {}, 3 * len(ss), copies)


def _sibling_share(fs, layer):
    def copies(cin, cout, send, recv):
        x, y, c, me, sibling, chips = _place()
        sends, recvs = [], []
        for a, f in enumerate(cout):
            mine, theirs = f.at[layer, _half(f.shape[1], c)], f.at[layer, _half(f.shape[1], 1 - c)]
            sends.append(_rcopy(send, recv, a, mine, mine, sibling))
            recvs.append(_rcopy(send, recv, a, theirs, theirs, sibling))
        return sends, recvs

    return _Comm(list(fs), [jax.ShapeDtypeStruct(f.shape, f.dtype) for f in fs], {a: a for a in range(len(fs))},
                 len(fs), copies)


def _sum_rows(rows):
    return next(b for b in (256, 192, 128) if rows % b == 0)


def _pair_sum(g, r, core_arr, layer):
    _, rows, cols = r.shape
    br = _sum_rows(rows)
    nb = rows // br

    def body(c_ref, g_ref, r_ref, o_ref):
        o_ref[...] = (g_ref[...].astype(f32) + r_ref[...].astype(f32)).astype(bf16)

    return _pcall(
        body, name=f"pair_sum_{layer}", out_shape=jax.ShapeDtypeStruct(r.shape, bf16),
        grid_spec=pltpu.PrefetchScalarGridSpec(
            num_scalar_prefetch=1, grid=(N_CHIPS, nb),
            in_specs=[pl.BlockSpec((1, br, cols), lambda j, i, c_ref: (j, c_ref[0] * nb + i, 0)),
                      pl.BlockSpec((1, br, cols), lambda j, i, c_ref: (j, i, 0))],
            out_specs=pl.BlockSpec((1, br, cols), lambda j, i, c_ref: (j, i, 0))),
        compiler_params=_params("arbitrary", "arbitrary"))(core_arr, g, r)


def _chip_sum(s, r, place, layer, final):
    _, rows, cols = s.shape
    br = _sum_rows(rows)
    nb = rows // br

    def body(place_ref, s_ref, r_ref, *rest):
        o_ref = rest[-1]
        acc = s_ref[0].astype(f32)
        for p in range(3):
            acc = acc + r_ref[p].astype(f32)
        o_ref[...] = acc

    carried = [] if final is None else [final]
    return _pcall(
        body, name=f"chip_sum_{layer}", out_shape=jax.ShapeDtypeStruct((DEPTH, 2 * rows, cols), f32),
        grid_spec=pltpu.PrefetchScalarGridSpec(
            num_scalar_prefetch=1, grid=(nb,),
            in_specs=[pl.BlockSpec((1, br, cols), lambda i, place_ref: (place_ref[0], i, 0)),
                      pl.BlockSpec((3, br, cols), lambda i, place_ref: (0, i, 0))] + [ANY] * len(carried),
            out_specs=pl.BlockSpec((None, br, cols), lambda i, place_ref: (layer, place_ref[1] * nb + i, 0))),
        input_output_aliases={3: 0} if carried else {},
        compiler_params=_params("arbitrary"))(place, s, r, *carried)


def _inproj(xin, win, layer, comm=None):
    T = xin.shape[0]
    tm = 512
    cast = xin.dtype != bf16

    def body(x_ref, w_ref, o_ref, *xb_ref):
        xt = x_ref[...].astype(bf16)
        if cast:
            xb_ref[0][...] = xt
        for j in range(N_CHIPS):
            o_ref[:, j * W_IN_COLS:(j + 1) * W_IN_COLS] = jnp.dot(xt, w_ref[j], preferred_element_type=f32).astype(bf16)

    tile = pl.BlockSpec((tm, D_MODEL), lambda i: (i, 0))
    outs, carried = _pcall_carry(
        body, comm, n_in=2, n_out=1 + cast, name=f"inproj_{layer}", grid=(T // tm,),
        out_shape=(jax.ShapeDtypeStruct((T, N_IN), bf16),) + ((jax.ShapeDtypeStruct((T, D_MODEL), bf16),) if cast else ()),
        in_specs=[tile, pl.BlockSpec((N_CHIPS, D_MODEL, W_IN_COLS), lambda i: (0, 0, 0), pipeline_mode=pl.Buffered(1))],
        out_specs=(pl.BlockSpec((tm, N_IN), lambda i: (i, 0)),) + ((tile,) if cast else ()),
        compiler_params=_params("arbitrary"))(xin, win)
    return outs[0], (outs[1] if cast else xin), carried


def _rel_index_rows():
    j = lax.broadcasted_iota(jnp.int32, (N_REL_PAD, KG), 1)
    r = lax.broadcasted_iota(jnp.int32, (N_REL_PAD, KG), 0)
    off = jnp.where(j < KG - 2 * CHUNK, j, j - KG)
    idx = jnp.clip(N_PREV * CHUNK - off, -REL_CLIP, REL_CLIP) + REL_CLIP
    return (idx == r).astype(f32)


def _bias_expand(table_pad, layer, comm=None):
    def body(t_ref, o_ref, row_scr):
        h = pl.program_id(0)

        @pl.when(h == 0)
        def _():
            row_scr[...] = jnp.dot(t_ref[...], _rel_index_rows(), precision=lax.Precision.HIGHEST,
                                   preferred_element_type=f32)

        q = lax.broadcasted_iota(jnp.int32, (QG, KG), 0)
        k = lax.broadcasted_iota(jnp.int32, (QG, KG), 1)
        band = (k // CHUNK >= q // CHUNK) & (k // CHUNK <= q // CHUNK + N_PREV)
        t = jnp.broadcast_to(row_scr[pl.ds(h, 1), :], (QG, KG))
        for b in range(8):
            t = jnp.where(((q >> b) & 1) == 1, pltpu.roll(t, 1 << b, axis=1), t)
        for v in range(3):
            o_ref[v] = jnp.where(band & (k >= (2 - v) * QG), t, NEG)

    (bias,), carried = _pcall_carry(
        body, comm, n_in=1, n_out=1, name=f"bias_expand_{layer}", grid=(N_HEADS,),
        out_shape=(jax.ShapeDtypeStruct((3, N_HEADS, QG, KG), f32),),
        in_specs=[pl.BlockSpec((N_HEADS, N_REL_PAD), lambda h: (0, 0))],
        out_specs=(pl.BlockSpec((3, None, QG, KG), lambda h: (0, h, 0, 0)),),
        scratch_shapes=[pltpu.VMEM((N_HEADS, KG), f32)], compiler_params=_params("arbitrary"))(table_pad)
    return bias, carried


def _bias_reduce(dbias, layer):
    def body(d_ref, o_ref, row_scr):
        q = lax.broadcasted_iota(jnp.int32, (QG, DB_COLS), 0)
        k = lax.broadcasted_iota(jnp.int32, (QG, DB_COLS), 1)
        for h in range(N_HEADS):
            t = jnp.where(k > q, d_ref[h], 0.0)
            for b in range(8):
                t = jnp.where(((q >> b) & 1) == 1, pltpu.roll(t, DB_COLS - (1 << b), axis=1), t)
            row_scr[h:h + 1, :] = jnp.sum(t, axis=0, keepdims=True)
        r = lax.broadcasted_iota(jnp.int32, (N_REL_PAD, DB_COLS), 0)
        off = lax.broadcasted_iota(jnp.int32, (N_REL_PAD, DB_COLS), 1)
        own = (off >= 1) & (off < REL_CLIP + CHUNK)
        sel = jnp.where(own & (r == 2 * REL_CLIP - off), 1.0, 0.0) - jnp.where(own & (r == 2 * REL_CLIP), 1.0, 0.0)
        o_ref[...] = lax.dot_general(row_scr[...], sel, NT, precision=lax.Precision.HIGHEST, preferred_element_type=f32)

    return _pcall(body, name=f"bias_reduce_{layer}", out_shape=jax.ShapeDtypeStruct((N_HEADS, N_REL_PAD), f32),
                  scratch_shapes=[pltpu.VMEM((N_HEADS, DB_COLS), f32)],
                  compiler_params=pltpu.CompilerParams(vmem_limit_bytes=VMEM_LIMIT))(dbias)


FWD_PAIRS = 8
BWD_PAIRS = 4


def _key_specs(n_groups, npairs, slab):
    per_slab = E_MIX // (128 * npairs)
    return [pl.BlockSpec((QG, 128 * npairs), functools.partial(
        lambda hp, g, jj: (jnp.clip(g - 2 + jj, 0, n_groups - 1), slab * per_slab + hp), jj=jj)) for jj in range(3)]


def _bias_spec(npairs):
    return pl.BlockSpec((None, 2 * npairs, QG, KG), lambda hp, g: (jnp.minimum(g, 2), hp, 0, 0))


def _attn_fwd(h, bias, layer, comm=None):
    T = h.shape[0]
    n_groups = T // QG
    scale = 1.0 / math.sqrt(HEAD_DIM)

    def body(q_ref, k0, k1, k2, v0, v1, v2, b_ref, o_ref, lse_ref):
        lane = lax.broadcasted_iota(jnp.int32, (1, 128), 1)
        ones = jnp.ones((KG, 128), bf16)
        lse = jnp.zeros((QG, 128), f32)
        for pp in range(FWD_PAIRS):
            cs = slice(pp * 128, (pp + 1) * 128)
            q2 = q_ref[:, cs] * scale
            kc = jnp.concatenate([k0[:, cs], k1[:, cs], k2[:, cs]], axis=0)
            vc = jnp.concatenate([jnp.concatenate([v0[:, cs], v1[:, cs], v2[:, cs]], axis=0), ones], axis=1)
            outs = []
            for hh in range(2):
                qm = jnp.where(lane // HEAD_DIM == hh, q2, jnp.zeros_like(q2))
                s = lax.dot_general(qm, kc, NT, preferred_element_type=f32) + b_ref[2 * pp + hh]
                m = jnp.max(s, axis=1, keepdims=True)
                ol = jnp.dot(jnp.exp(s - m).astype(bf16), vc, preferred_element_type=f32)
                outs.append(ol[:, :128] / ol[:, 128:])
                lse = jnp.where(lane == 2 * pp + hh, m + jnp.log(ol[:, 128:]), lse)
            o_ref[:, cs] = jnp.where(lane // HEAD_DIM == 0, outs[0], outs[1]).astype(bf16)
        lse_ref[...] = lse

    (mix, lse), carried = _pcall_carry(
        body, comm, n_in=8, n_out=2, name=f"attn_fwd_{layer}", grid=(N_HEADS // (2 * FWD_PAIRS), n_groups),
        out_shape=(jax.ShapeDtypeStruct((T, E_MIX), bf16), jax.ShapeDtypeStruct((T, 128), f32)),
        in_specs=[pl.BlockSpec((QG, 128 * FWD_PAIRS), lambda hp, g: (g, hp))] + _key_specs(n_groups, FWD_PAIRS, 1)
        + _key_specs(n_groups, FWD_PAIRS, 2) + [_bias_spec(FWD_PAIRS)],
        out_specs=(pl.BlockSpec((QG, 128 * FWD_PAIRS), lambda hp, g: (g, hp)), pl.BlockSpec((QG, 128), lambda hp, g: (g, 0))),
        compiler_params=_params("arbitrary", "arbitrary"))(h, h, h, h, h, h, h, bias)
    return mix, lse, carried


def _halo_rows(ref, r):
    return ref[r:r + 1, :].astype(f32)


def _conv_taps(cu, p6, p7, w_ref):
    row = lax.broadcasted_iota(jnp.int32, cu.shape, 0)
    r1 = jnp.where(row == 0, p7, pltpu.roll(cu, 1, axis=0))
    r2 = jnp.where(row == 0, p6, jnp.where(row == 1, p7, pltpu.roll(cu, 2, axis=0)))
    return w_ref[2:3, :] * cu + w_ref[1:2, :] * r1 + w_ref[0:1, :] * r2, r1, r2


def _conv_fwd(h, w, layer, comm=None):
    T = h.shape[0]
    tm = 512

    def body(bg_ref, cg_ref, u_ref, cgp_ref, up_ref, w_ref, o_ref):
        first = (pl.program_id(0) == 0).astype(f32)
        cu = cg_ref[...].astype(f32) * u_ref[...].astype(f32)
        p6 = _halo_rows(cgp_ref, 14) * _halo_rows(up_ref, 14) * (1.0 - first)
        p7 = _halo_rows(cgp_ref, 15) * _halo_rows(up_ref, 15) * (1.0 - first)
        conv, _, _ = _conv_taps(cu, p6, p7, w_ref)
        o_ref[...] = (bg_ref[...].astype(f32) * conv).astype(bf16)

    prev = lambda slab: pl.BlockSpec((16, E_MIX), lambda i: (jnp.maximum(i * (tm // 16) - 1, 0), slab))
    (mix,), carried = _pcall_carry(
        body, comm, n_in=6, n_out=1, name=f"conv_fwd_{layer}", grid=(T // tm,),
        out_shape=(jax.ShapeDtypeStruct((T, E_MIX), bf16),),
        in_specs=[pl.BlockSpec((tm, E_MIX), lambda i: (i, 0)), pl.BlockSpec((tm, E_MIX), lambda i: (i, 1)),
                  pl.BlockSpec((tm, E_MIX), lambda i: (i, 2)), prev(1), prev(2),
                  pl.BlockSpec((CONV_W, E_MIX), lambda i: (0, 0))],
        out_specs=(pl.BlockSpec((tm, E_MIX), lambda i: (i, 0)),),
        compiler_params=_params("arbitrary"))(h, h, h, h, h, w)
    return mix, carried


def _kv_mem(memb, wkv):
    def body(m_ref, w_ref, o_ref):
        o_ref[...] = jnp.dot(m_ref[...], w_ref[...], preferred_element_type=f32).astype(bf16)

    return _pcall(body, name="kv_mem", out_shape=jax.ShapeDtypeStruct((N_MEM, 2 * E_MEM), bf16),
                  compiler_params=pltpu.CompilerParams(vmem_limit_bytes=VMEM_LIMIT))(memb, wkv)


def _mem_probs(qm_ref, kv_ref, hh):
    qh = qm_ref[:, hh * MEM_HEAD_DIM:(hh + 1) * MEM_HEAD_DIM]
    kh = kv_ref[:, hh * MEM_HEAD_DIM:(hh + 1) * MEM_HEAD_DIM]
    vh = kv_ref[:, E_MEM + hh * MEM_HEAD_DIM:E_MEM + (hh + 1) * MEM_HEAD_DIM]
    s = lax.dot_general(qh, kh, NT, preferred_element_type=f32) * (1.0 / math.sqrt(MEM_HEAD_DIM))
    e = jnp.exp(s - jnp.max(s, axis=1, keepdims=True))
    return e / jnp.sum(e, axis=1, keepdims=True), qh, kh, vh


def _h_tail_specs(tm):
    return [pl.BlockSpec((tm, E_MEM), functools.partial(lambda i, cb: (i, cb), cb=cb)) for cb in (6, 7, 8, 9)]


def _ln_bwd(dy, xhat, rstd, g, dob_ref, dxp_ref, dg_ref, db_ref):
    dg_ref[0:1, :] += jnp.sum(dy * xhat, axis=0, keepdims=True)
    db_ref[0:1, :] += jnp.sum(dy, axis=0, keepdims=True)
    gx = dy * g
    dr = rstd * (gx - jnp.mean(gx, axis=1, keepdims=True) - xhat * jnp.mean(gx * xhat, axis=1, keepdims=True))
    dxp_ref[...] = DN_ALPHA * dr
    dob_ref[...] = dr.astype(bf16)


def _post_fwd(h, mix, kv, wout, x, g, b, layer, target=None):
    T = x.shape[0]
    tm = 512

    def body(qm_ref, z0, z1, z2, mix_ref, kv_ref, w_ref, x_ref, g_ref, b_ref, *rest):
        mem = jnp.concatenate(
            [jnp.dot(_mem_probs(qm_ref, kv_ref, hh)[0].astype(bf16), kv_ref[:, E_MEM + hh * MEM_HEAD_DIM:E_MEM + (hh + 1) * MEM_HEAD_DIM],
                     preferred_element_type=f32) for hh in range(MEM_HEADS)], axis=1)
        z = jnp.concatenate([z0[...], z1[...], z2[...]], axis=1)
        one = jnp.ones((), bf16)
        y = jnp.concatenate([mix_ref[...], mem.astype(bf16)], axis=1) * (z * (one / (one + jnp.exp(-z))))
        out = jnp.dot(y, w_ref[...], preferred_element_type=f32)
        r = DN_ALPHA * x_ref[...] + out
        mu = jnp.mean(r, axis=1, keepdims=True)
        var = jnp.mean(jnp.square(r - mu), axis=1, keepdims=True)
        rstd = lax.rsqrt(var + LN_EPS)
        xhat = (r - mu) * rstd
        xn = xhat * g_ref[...] + b_ref[...]
        if target is None:
            xn_ref, xb_ref, xh_ref, rs_ref = rest
            xn_ref[...] = xn
            xb_ref[...] = xn.astype(bf16)
            xh_ref[...] = xhat
            rs_ref[...] = rstd
        else:
            t_ref, l_ref, dob_ref, dxp_ref, dg_ref, db_ref = rest

            @pl.when(pl.program_id(0) == 0)
            def _():
                l_ref[...] = jnp.zeros_like(l_ref)
                dg_ref[...] = jnp.zeros_like(dg_ref)
                db_ref[...] = jnp.zeros_like(db_ref)

            err = xn - t_ref[...]
            l_ref[...] += jnp.sum(jnp.square(err))
            _ln_bwd(err * (1.0 / D_MODEL), xhat, rstd, g_ref[...], dob_ref, dxp_ref, dg_ref, db_ref)

    tile = lambda w: pl.BlockSpec((tm, w), lambda i: (i, 0))
    const = lambda r, c: pl.BlockSpec((r, c), lambda i: (0, 0))
    in_specs = _h_tail_specs(tm) + [tile(E_MIX), const(N_MEM, 2 * E_MEM), const(E_BRANCH, D_MODEL), tile(D_MODEL),
                                    const(1, D_MODEL), const(1, D_MODEL)]
    if target is None:
        return _pcall(
            body, name=f"post_fwd_{layer}", grid=(T // tm,),
            out_shape=(jax.ShapeDtypeStruct((T, D_MODEL), f32), jax.ShapeDtypeStruct((T, D_MODEL), bf16),
                       jax.ShapeDtypeStruct((T, D_MODEL), f32), jax.ShapeDtypeStruct((T, 1), f32)),
            in_specs=in_specs, out_specs=(tile(D_MODEL), tile(D_MODEL), tile(D_MODEL), tile(1)),
            compiler_params=_params("arbitrary"))(h, h, h, h, mix, kv, wout, x, g, b)
    return _pcall(
        body, name=f"post_fwd_loss_{layer}", grid=(T // tm,),
        out_shape=(jax.ShapeDtypeStruct((8, 128), f32), jax.ShapeDtypeStruct((T, D_MODEL), bf16),
                   jax.ShapeDtypeStruct((T, D_MODEL), f32), jax.ShapeDtypeStruct((8, D_MODEL), f32),
                   jax.ShapeDtypeStruct((8, D_MODEL), f32)),
        in_specs=in_specs + [tile(D_MODEL)],
        out_specs=(const(8, 128), tile(D_MODEL), tile(D_MODEL), const(8, D_MODEL), const(8, D_MODEL)),
        compiler_params=_params("arbitrary"))(h, h, h, h, mix, kv, wout, x, g, b, target)


def _post_bwd(dob, h, mix, kv, wout, layer, comm=None):
    T = dob.shape[0]
    tm = 512
    n_tiles = T // tm
    inv = 1.0 / math.sqrt(MEM_HEAD_DIM)

    def body(dob_ref, qm_ref, z0, z1, z2, mix_ref, kv_ref, w_ref, dhb_ref, dmix_ref, dkv_ref, dwo_out, dwo_ref):
        @pl.when(pl.program_id(0) == 0)
        def _():
            dkv_ref[...] = jnp.zeros_like(dkv_ref)
            dwo_ref[...] = jnp.zeros_like(dwo_ref)

        dob = dob_ref[...]
        probs = [_mem_probs(qm_ref, kv_ref, hh) for hh in range(MEM_HEADS)]
        mem = jnp.concatenate([jnp.dot(p.astype(bf16), vh, preferred_element_type=f32) for p, _, _, vh in probs], axis=1)
        z = jnp.concatenate([z0[...], z1[...], z2[...]], axis=1).astype(f32)
        act, sig = _silu_parts(z)
        cat = jnp.concatenate([mix_ref[...].astype(f32), mem], axis=1)
        yb = (cat * act).astype(bf16)
        dwo_ref[...] += lax.dot_general(yb, dob, TN, preferred_element_type=f32)
        dyv = lax.dot_general(dob, w_ref[...], NT, preferred_element_type=f32)
        dz = dyv * cat * (sig * (1.0 + z * (1.0 - sig)))
        dcat = dyv * act
        dmix_ref[...] = dcat[:, :E_MIX].astype(bf16)
        dqs = []
        for hh, (p, qh, kh, vh) in enumerate(probs):
            dmem = dcat[:, E_MIX + hh * MEM_HEAD_DIM:E_MIX + (hh + 1) * MEM_HEAD_DIM].astype(bf16)
            dp = lax.dot_general(dmem, vh, NT, preferred_element_type=f32)
            ds = (p * (dp - jnp.sum(p * dp, axis=1, keepdims=True))).astype(bf16)
            dqs.append(jnp.dot(ds, kh, preferred_element_type=f32) * inv)
            dkv_ref[:, hh * MEM_HEAD_DIM:(hh + 1) * MEM_HEAD_DIM] += lax.dot_general(ds, qh, TN, preferred_element_type=f32) * inv
            dkv_ref[:, E_MEM + hh * MEM_HEAD_DIM:E_MEM + (hh + 1) * MEM_HEAD_DIM] += lax.dot_general(
                p.astype(bf16), dmem, TN, preferred_element_type=f32)
        dhb_ref[...] = jnp.concatenate(dqs + [dz], axis=1).astype(bf16)

        @pl.when(pl.program_id(0) == n_tiles - 1)
        def _():
            dwo_out[...] = dwo_ref[...].astype(bf16)

    tile = lambda w: pl.BlockSpec((tm, w), lambda i: (i, 0))
    const = lambda r, c: pl.BlockSpec((r, c), lambda i: (0, 0))
    resident = lambda r, c: pl.BlockSpec((r, c), lambda i: (0, 0), pipeline_mode=pl.Buffered(1))
    return _pcall_carry(
        body, comm, n_in=8, n_out=4, name=f"post_bwd_{layer}", grid=(n_tiles,),
        out_shape=(jax.ShapeDtypeStruct((T, E_MEM + E_BRANCH), bf16), jax.ShapeDtypeStruct((T, E_MIX), bf16),
                   jax.ShapeDtypeStruct((N_MEM, 2 * E_MEM), f32), jax.ShapeDtypeStruct((E_BRANCH, D_MODEL), bf16)),
        in_specs=[tile(D_MODEL)] + _h_tail_specs(tm) + [tile(E_MIX), resident(N_MEM, 2 * E_MEM), resident(E_BRANCH, D_MODEL)],
        out_specs=(tile(E_MEM + E_BRANCH), tile(E_MIX), const(N_MEM, 2 * E_MEM), const(E_BRANCH, D_MODEL)),
        scratch_shapes=[pltpu.VMEM((E_BRANCH, D_MODEL), f32)],
        compiler_params=_params("arbitrary"))(dob, h, h, h, h, mix, kv, wout)


def _attn_bwd(h, bias, dmix, lse, layer, comm=None):
    T = h.shape[0]
    n_groups = T // QG
    scale = 1.0 / math.sqrt(HEAD_DIM)

    def body(q_ref, k0, k1, k2, v0, v1, v2, do_ref, b_ref, lse_ref, dq_ref, dk_ref, dv_ref, db_ref, acck, accv):
        g = pl.program_id(1)

        @pl.when(g == 0)
        def _():
            acck[...] = jnp.zeros_like(acck)
            accv[...] = jnp.zeros_like(accv)
            db_ref[...] = jnp.zeros_like(db_ref)

        @pl.when(g < n_groups)
        def _():
            lane = lax.broadcasted_iota(jnp.int32, (1, 128), 1)
            first = lane // HEAD_DIM == 0
            for pp in range(BWD_PAIRS):
                cs = slice(pp * 128, (pp + 1) * 128)
                do2 = do_ref[:, cs]
                q2 = q_ref[:, cs] * scale
                kc = jnp.concatenate([k0[:, cs], k1[:, cs], k2[:, cs]], axis=0)
                vc = jnp.concatenate([v0[:, cs], v1[:, cs], v2[:, cs]], axis=0)
                q2t, do2t = q2.T, do2.T
                dqs, dks, dvs = [], [], []
                for hh in range(2):
                    hm = lane // HEAD_DIM == hh
                    head = (pl.program_id(0) * BWD_PAIRS + pp) * 2 + hh
                    lse = jnp.sum(jnp.where(lane == head, lse_ref[...], 0.0), axis=1, keepdims=True)
                    qm = jnp.where(hm, q2, jnp.zeros_like(q2))
                    dom = jnp.where(hm, do2, jnp.zeros_like(do2))
                    s = lax.dot_general(qm, kc, NT, preferred_element_type=f32) + b_ref[2 * pp + hh]
                    p = jnp.exp(s - lse)
                    dp = lax.dot_general(dom, vc, NT, preferred_element_type=f32)
                    ds = p * (dp - jnp.sum(p * dp, axis=1, keepdims=True))
                    db_ref[2 * pp + hh] += ds[:, KG - DB_COLS:]
                    dsb, pb = ds.astype(bf16), p.astype(bf16)
                    dqs.append(jnp.dot(dsb, kc, preferred_element_type=f32) * scale)
                    dks.append(jnp.dot(q2t[hh * HEAD_DIM:(hh + 1) * HEAD_DIM], dsb, preferred_element_type=f32))
                    dvs.append(jnp.dot(do2t[hh * HEAD_DIM:(hh + 1) * HEAD_DIM], pb, preferred_element_type=f32))
                dq_ref[:, cs] = jnp.where(first, dqs[0], dqs[1]).astype(bf16)
                dkc = jnp.concatenate(dks, axis=0).T
                dvc = jnp.concatenate(dvs, axis=0).T
                for jj in range(3):
                    slot = (g + 1 + jj) % 3
                    if jj == 2:
                        acck[slot, :, cs] = dkc[jj * QG:(jj + 1) * QG]
                        accv[slot, :, cs] = dvc[jj * QG:(jj + 1) * QG]
                    else:
                        acck[slot, :, cs] += dkc[jj * QG:(jj + 1) * QG]
                        accv[slot, :, cs] += dvc[jj * QG:(jj + 1) * QG]

        done = (g + 1) % 3
        dk_ref[...] = acck[done].astype(bf16)
        dv_ref[...] = accv[done].astype(bf16)

    last = n_groups - 1
    width = 128 * BWD_PAIRS
    qspec = pl.BlockSpec((QG, width), lambda hp, g: (jnp.minimum(g, last), hp))
    kout = pl.BlockSpec((QG, width), lambda hp, g: (jnp.clip(g - 2, 0, last), hp))
    dbspec = pl.BlockSpec((2 * BWD_PAIRS, QG, DB_COLS), lambda hp, g: (hp, 0, 0))
    lspec = pl.BlockSpec((QG, 128), lambda hp, g: (jnp.minimum(g, last), 0))
    return _pcall_carry(
        body, comm, n_in=10, n_out=4, name=f"attn_bwd_{layer}", grid=(N_HEADS // (2 * BWD_PAIRS), n_groups + 2),
        out_shape=(jax.ShapeDtypeStruct((T, E_MIX), bf16),) * 3 + (jax.ShapeDtypeStruct((N_HEADS, QG, DB_COLS), f32),),
        in_specs=[qspec] + _key_specs(n_groups, BWD_PAIRS, 1) + _key_specs(n_groups, BWD_PAIRS, 2)
        + [qspec, _bias_spec(BWD_PAIRS), lspec],
        out_specs=(qspec, kout, kout, dbspec),
        scratch_shapes=[pltpu.VMEM((3, QG, width), f32), pltpu.VMEM((3, QG, width), f32)],
        compiler_params=_params("arbitrary", "arbitrary"))(h, h, h, h, h, h, h, dmix, bias, lse)


def _conv_bwd(h, w, dmix, layer):
    T = h.shape[0]
    tm = 512
    n_tiles = T // tm

    def body(bg_ref, cg_ref, u_ref, cgp_ref, up_ref, dy_ref, bgn_ref, dyn_ref, w_ref, dbg_ref, dcg_ref, du_ref, dw_ref):
        i = pl.program_id(0)

        @pl.when(i == 0)
        def _():
            dw_ref[...] = jnp.zeros_like(dw_ref)

        first = (i == 0).astype(f32)
        final = (i == n_tiles - 1).astype(f32)
        bg, cg, u = bg_ref[...].astype(f32), cg_ref[...].astype(f32), u_ref[...].astype(f32)
        dy = dy_ref[...].astype(f32)
        cu = cg * u
        p6 = _halo_rows(cgp_ref, 14) * _halo_rows(up_ref, 14) * (1.0 - first)
        p7 = _halo_rows(cgp_ref, 15) * _halo_rows(up_ref, 15) * (1.0 - first)
        conv, r1, r2 = _conv_taps(cu, p6, p7, w_ref)
        dbg_ref[...] = (dy * conv).astype(bf16)
        dc = dy * bg
        n0 = _halo_rows(dyn_ref, 0) * _halo_rows(bgn_ref, 0) * (1.0 - final)
        n1 = _halo_rows(dyn_ref, 1) * _halo_rows(bgn_ref, 1) * (1.0 - final)
        row = lax.broadcasted_iota(jnp.int32, dc.shape, 0)
        f1 = jnp.where(row == tm - 1, n0, pltpu.roll(dc, tm - 1, axis=0))
        f2 = jnp.where(row == tm - 2, n0, jnp.where(row == tm - 1, n1, pltpu.roll(dc, tm - 2, axis=0)))
        dcu = w_ref[2:3, :] * dc + w_ref[1:2, :] * f1 + w_ref[0:1, :] * f2
        dcg_ref[...] = (dcu * u).astype(bf16)
        du_ref[...] = (dcu * cg).astype(bf16)
        dw_ref[0:1, :] += jnp.sum(dc * r2, axis=0, keepdims=True)
        dw_ref[1:2, :] += jnp.sum(dc * r1, axis=0, keepdims=True)
        dw_ref[2:3, :] += jnp.sum(dc * cu, axis=0, keepdims=True)

    tile = lambda slab: pl.BlockSpec((tm, E_MIX), lambda i: (i, slab))
    prev = lambda slab: pl.BlockSpec((16, E_MIX), lambda i: (jnp.maximum(i * (tm // 16) - 1, 0), slab))
    nxt = lambda slab: pl.BlockSpec((16, E_MIX), lambda i: (jnp.minimum((i + 1) * (tm // 16), T // 16 - 1), slab))
    return _pcall(
        body, name=f"conv_bwd_{layer}", grid=(n_tiles,),
        out_shape=(jax.ShapeDtypeStruct((T, E_MIX), bf16),) * 3 + (jax.ShapeDtypeStruct((8, E_MIX), f32),),
        in_specs=[tile(0), tile(1), tile(2), prev(1), prev(2), tile(0), nxt(0), nxt(0),
                  pl.BlockSpec((CONV_W, E_MIX), lambda i: (0, 0))],
        out_specs=(tile(0), tile(0), tile(0), pl.BlockSpec((8, E_MIX), lambda i: (0, 0))),
        compiler_params=_params("arbitrary"))(h, h, h, h, h, dmix, h, dmix, w)


def _inproj_bwd_dx(da, db, dc, dhb, dxp, win, layer, below=None, comm=None):
    T = dxp.shape[0]
    tm = 512

    def body(da_ref, db_ref, dc_ref, dhb_ref, dxp_ref, w_ref, *rest):
        dh = jnp.concatenate([da_ref[...], db_ref[...], dc_ref[...], dhb_ref[...]], axis=1)
        acc = dxp_ref[...]
        for j in range(N_CHIPS):
            acc = acc + lax.dot_general(dh[:, j * W_IN_COLS:(j + 1) * W_IN_COLS], w_ref[j], NT, preferred_element_type=f32)
        if below is None:
            rest[0][...] = acc
        else:
            xh_ref, rs_ref, g_ref, dob_ref, dxo_ref, dg_ref, db_out = rest

            @pl.when(pl.program_id(0) == 0)
            def _():
                dg_ref[...] = jnp.zeros_like(dg_ref)
                db_out[...] = jnp.zeros_like(db_out)

            _ln_bwd(acc, xh_ref[...], rs_ref[...], g_ref[...], dob_ref, dxo_ref, dg_ref, db_out)

    tile = lambda w: pl.BlockSpec((tm, w), lambda i: (i, 0))
    const = lambda r, c: pl.BlockSpec((r, c), lambda i: (0, 0))
    in_specs = [tile(E_MIX), tile(E_MIX), tile(E_MIX), tile(E_MEM + E_BRANCH), tile(D_MODEL),
                pl.BlockSpec((N_CHIPS, D_MODEL, W_IN_COLS), lambda i: (0, 0, 0), pipeline_mode=pl.Buffered(1))]
    if below is None:
        return _pcall_carry(
            body, comm, n_in=6, n_out=1, name=f"inproj_bwd_dx_{layer}", grid=(T // tm,),
            out_shape=(jax.ShapeDtypeStruct((T, D_MODEL), f32),), in_specs=in_specs, out_specs=(tile(D_MODEL),),
            compiler_params=_params("arbitrary"))(da, db, dc, dhb, dxp, win)
    return _pcall_carry(
        body, comm, n_in=9, n_out=4, name=f"inproj_bwd_dx_{layer}", grid=(T // tm,),
        out_shape=(jax.ShapeDtypeStruct((T, D_MODEL), bf16), jax.ShapeDtypeStruct((T, D_MODEL), f32),
                   jax.ShapeDtypeStruct((8, D_MODEL), f32), jax.ShapeDtypeStruct((8, D_MODEL), f32)),
        in_specs=in_specs + [tile(D_MODEL), tile(1), const(1, D_MODEL)],
        out_specs=(tile(D_MODEL), tile(D_MODEL), const(8, D_MODEL), const(8, D_MODEL)),
        compiler_params=_params("arbitrary"))(da, db, dc, dhb, dxp, win, *below)


def _dh_pieces():
    pieces, col = [], 0
    for src, width in enumerate((E_MIX, E_MIX, E_MIX, E_MEM + E_BRANCH)):
        lo = 0
        while lo < width:
            j, c0 = divmod(col + lo, W_IN_COLS)
            n = min(width - lo, W_IN_COLS - c0)
            pieces.append((src, lo, lo + n, j, c0, c0 + n))
            lo += n
        col += width
    return pieces


def _inproj_bwd_dw(da, db, dc, dhb, xb, layer, comm=None):
    T = xb.shape[0]
    tm = 1024 if T % 1024 == 0 else 512
    n_tiles = T // tm

    def body(da_ref, db_ref, dc_ref, dhb_ref, x_ref, o_ref, acc, stage, sem):
        i = pl.program_id(0)

        @pl.when(i == 0)
        def _():
            acc[...] = jnp.zeros_like(acc)

        srcs = (da_ref, db_ref, dc_ref, dhb_ref)
        xt = x_ref[...]
        for s, s0, s1, j, c0, c1 in _dh_pieces():
            acc[j, :, c0:c1] += lax.dot_general(xt, srcs[s][:, s0:s1], TN, preferred_element_type=f32)

        @pl.when(i == n_tiles - 1)
        def _():
            for j in range(N_CHIPS):
                stage[...] = acc[j].astype(bf16)
                cp = pltpu.make_async_copy(stage, o_ref.at[j], sem)
                cp.start()
                cp.wait()

    tile = lambda w: pl.BlockSpec((tm, w), lambda i: (i, 0))
    (dw,), carried = _pcall_carry(
        body, comm, n_in=5, n_out=1, name=f"inproj_bwd_dw_{layer}", grid=(n_tiles,),
        out_shape=(jax.ShapeDtypeStruct((N_CHIPS, D_MODEL, W_IN_COLS), bf16),),
        in_specs=[tile(E_MIX), tile(E_MIX), tile(E_MIX), tile(E_MEM + E_BRANCH), tile(D_MODEL)],
        out_specs=(ANY,),
        scratch_shapes=[pltpu.VMEM((N_CHIPS, D_MODEL, W_IN_COLS), f32), pltpu.VMEM((D_MODEL, W_IN_COLS), bf16),
                        pltpu.SemaphoreType.DMA],
        compiler_params=_params("arbitrary"))(da, db, dc, dhb, xb)
    return dw, carried


def _kv_mem_bwd(memb, dkv, layer):
    def body(m_ref, d_ref, o_ref):
        o_ref[...] = lax.dot_general(m_ref[...], d_ref[...].astype(bf16), TN, preferred_element_type=f32).astype(bf16)

    return _pcall(body, name=f"kv_mem_bwd_{layer}", out_shape=jax.ShapeDtypeStruct((D_MODEL, 2 * E_MEM), bf16),
                  compiler_params=pltpu.CompilerParams(vmem_limit_bytes=VMEM_LIMIT))(memb, dkv)


def _adamw(w, g, m, v, name):
    shape = w.shape
    cols = shape[-1]
    rows = w.size // cols
    args = [a.reshape(rows, cols) for a in (w, g, m, v)]
    br = 256 if rows % 256 == 0 and rows > 256 else rows

    def body(w_ref, g_ref, m_ref, v_ref, go_ref, d_ref, nm_ref, nv_ref):
        gg = g_ref[...]
        nm = ADAM_B1 * m_ref[...] + (1.0 - ADAM_B1) * gg
        nv = ADAM_B2 * v_ref[...] + (1.0 - ADAM_B2) * jnp.square(gg)
        m_hat = nm / (1.0 - ADAM_B1 ** ADAM_STEP)
        v_hat = nv / (1.0 - ADAM_B2 ** ADAM_STEP)
        go_ref[...] = gg
        d_ref[...] = -ADAM_LR * (m_hat / (jnp.sqrt(v_hat) + ADAM_EPS) + ADAM_WD * w_ref[...])
        nm_ref[...] = nm
        nv_ref[...] = nv

    spec = pl.BlockSpec((br, cols), lambda i: (i, 0))
    outs = _pcall(body, name=name, grid=(rows // br,), out_shape=(jax.ShapeDtypeStruct((rows, cols), f32),) * 4,
                  in_specs=[spec] * 4, out_specs=(spec,) * 4, compiler_params=_params("arbitrary"))(*args)
    return tuple(o.reshape(shape) for o in outs)


def kernel(x, mem, w_in, w_mem_kv, w_out, rel_bias, conv_w, ln_g, ln_b, loss_target, m_w_in, m_w_mem_kv, m_w_out, m_rel_bias, m_conv_w, m_ln_g, m_ln_b, v_w_in, v_w_mem_kv, v_w_out, v_rel_bias, v_conv_w, v_ln_g, v_ln_b):
    T = x.shape[1]
    x0 = x.reshape(T, D_MODEL)
    target = loss_target.reshape(T, D_MODEL)
    memb = mem.reshape(N_MEM, D_MODEL).astype(bf16)
    chip = 2 * lax.axis_index("x") + lax.axis_index("y")
    core = lax.axis_index("c")
    chip_arr = jnp.reshape(chip, (1,)).astype(jnp.int32)
    core_arr = jnp.reshape(core, (1,)).astype(jnp.int32)

    place = jnp.concatenate([chip_arr, core_arr])
    tables = jnp.pad(rel_bias, ((0, 0), (0, 0), (0, N_REL_PAD - N_REL)))

    shards = [w_in.astype(bf16), w_mem_kv.astype(bf16), w_out.astype(bf16)]
    biases = {}
    biases[0], arrived_in = _bias_expand(tables[0], 0, _gather_ici(shards[:1], 0))
    biases[2], (*arrived_rest, cw_g) = _bias_expand(tables[1], 2, _gather_ici(shards[1:], 0, extra=conv_w))
    gathered = {0: _comm_call(_gather_d2d(list(arrived_in) + arrived_rest), "gather_d2d_0")}
    conv_full = jnp.transpose(cw_g, (1, 2, 0, 3)).reshape(DEPTH // 2, CONV_W, E_MIX)

    xs, xbs, hs, mixes, kvs, xhats, rstds, lses = [x0], [x0], [], [], [], [], [], {}
    for layer in range(DEPTH):
        win, wkv, wout = gathered[layer]
        more = layer + 1 < DEPTH
        h, xbs[layer], arrived = _inproj(xbs[layer], win, layer, _gather_ici(shards, layer + 1) if more else None)
        passing = _gather_d2d(list(arrived)) if more else None
        if layer % 2 == 0:
            mix, lses[layer], done = _attn_fwd(h, biases[layer], layer, passing)
        else:
            mix, done = _conv_fwd(h, conv_full[layer // 2], layer, passing)
        if more:
            gathered[layer + 1] = list(done)
        kv = _kv_mem(memb, wkv.reshape(D_MODEL, 2 * E_MEM))
        result = _post_fwd(h, mix, kv, wout.reshape(E_BRANCH, D_MODEL), xs[layer], ln_g[layer][None, :],
                           ln_b[layer][None, :], layer, None if more else target)
        if more:
            xn, xnb, xhat, rstd = result
            xs.append(xn); xbs.append(xnb); xhats.append(xhat); rstds.append(rstd)
        hs.append(h); mixes.append(mix); kvs.append(kv)

    dgs, dbs, dconvs, dtables = [None] * DEPTH, [None] * DEPTH, [None] * (DEPTH // 2), [None] * ((DEPTH + 1) // 2)
    lsum, dob, dxp, dgs[DEPTH - 1], dbs[DEPTH - 1] = result
    loss = lax.psum(lsum[0, 0], ("x", "y", "c")) * (0.5 / D_MODEL)
    finals = [None, None, None]
    above = None
    for layer in reversed(range(DEPTH)):
        h = hs[layer]
        win, wkv, wout = gathered[layer]
        (dhb, dmix, dkv, dwo), from_sibling = _post_bwd(
            dob, h, mixes[layer], kvs[layer], wout.reshape(E_BRANCH, D_MODEL), layer,
            _sibling_exchange(above) if above else None)
        sums = [_pair_sum(g, r, core_arr, layer + 1) for g, r in zip(above, from_sibling)] if above else None
        scatter = _chip_scatter(sums) if above else None
        if layer % 2 == 0:
            (da, db, dc, dbias), from_chips = _attn_bwd(h, biases[layer], dmix, lses[layer], layer, scatter)
            dtables[layer // 2] = _bias_reduce(dbias, layer)
        else:
            da, db, dc, dconvs[layer // 2] = _conv_bwd(h, conv_full[layer // 2], dmix, layer)
        if layer > 0:
            (dob_below, dxp_below, dgs[layer - 1], dbs[layer - 1]), landed = _inproj_bwd_dx(
                da, db, dc, dhb, dxp, win, layer, (xhats[layer - 1], rstds[layer - 1], ln_g[layer - 1][None, :]),
                scatter if layer % 2 == 1 else None)
            from_chips = landed if layer % 2 == 1 else from_chips
        share = None
        if above:
            finals = [_chip_sum(s, r, place, layer + 1, f) for s, r, f in zip(sums, from_chips, finals)]
            share = _sibling_share(finals, layer + 1)
        if layer == 0:
            pad8 = lambda a: jnp.pad(a, ((0, 8 - a.shape[0]), (0, 0)))
            small_mine = jnp.concatenate(dgs + dbs + dconvs + [pad8(t.reshape(-1, D_MODEL)) for t in dtables], axis=0)
            share = _both(share, _small_exchange(small_mine))
        g_win, shared = _inproj_bwd_dw(da, db, dc, dhb, xbs[layer], layer, share)
        if layer == 0:
            *shared, small_slots = shared
        finals = list(shared) if above else finals
        above = [g_win, _kv_mem_bwd(memb, dkv, layer).reshape(N_CHIPS, W_KV_ROWS, 2 * E_MEM),
                 dwo.reshape(N_CHIPS, W_OUT_ROWS, D_MODEL)]
        if layer > 0:
            dob, dxp = dob_below, dxp_below
    from_sibling = _comm_call(_sibling_exchange(above), "sibling_exchange_0")
    sums = [_pair_sum(g, r, core_arr, 0) for g, r in zip(above, from_sibling)]
    (dx,), from_chips = _inproj_bwd_dx(da, db, dc, dhb, dxp, win, 0, None, _chip_scatter(sums))
    finals = [_chip_sum(s, r, place, 0, f) for s, r, f in zip(sums, from_chips, finals)]
    grad_w_in, grad_w_mem_kv, grad_w_out = _comm_call(_sibling_share(finals, 0), "sibling_share_0")
    grad_x = dx.reshape(1, T, D_MODEL)

    device_arr = jnp.reshape(2 * chip + core, (1,)).astype(jnp.int32)
    small = _small_sum(small_slots, small_mine, device_arr)
    grad_ln_g = jnp.stack([small[8 * l] for l in range(DEPTH)])
    grad_ln_b = jnp.stack([small[8 * (DEPTH + l)] for l in range(DEPTH)])
    conv_all = jnp.stack([small[8 * (2 * DEPTH + a):8 * (2 * DEPTH + a) + CONV_W] for a in range(DEPTH // 2)])
    grad_conv_w = lax.dynamic_slice_in_dim(conv_all, chip * (E_MIX // N_CHIPS), E_MIX // N_CHIPS, axis=2)
    t0 = 8 * (2 * DEPTH + DEPTH // 2)
    grad_rel_bias = jnp.stack([small[t0 + 8 * a:t0 + 8 * a + 6].reshape(N_HEADS, N_REL_PAD)[:, :N_REL]
                               for a in range((DEPTH + 1) // 2)])

    grads = [grad_w_in, grad_w_mem_kv, grad_w_out, grad_rel_bias, grad_conv_w, grad_ln_g, grad_ln_b]
    weights = [w_in, w_mem_kv, w_out, rel_bias, conv_w, ln_g, ln_b]
    moms = [m_w_in, m_w_mem_kv, m_w_out, m_rel_bias, m_conv_w, m_ln_g, m_ln_b]
    vels = [v_w_in, v_w_mem_kv, v_w_out, v_rel_bias, v_conv_w, v_ln_g, v_ln_b]
    names = ["w_in", "w_mem_kv", "w_out", "rel_bias", "conv_w", "ln_g", "ln_b"]
    upd = [_adamw(w, g, m, v, f"adamw_{n}") for w, g, m, v, n in zip(weights, grads, moms, vels, names)]
    grads, deltas, new_m, new_v = zip(*upd)
    return (loss, grad_x, *grads, *deltas, *new_m, *new_v)
```

```python
import functools
import math

import jax
import jax.numpy as jnp
from jax import lax
from jax.experimental import pallas as pl
from jax.experimental.pallas import tpu as pltpu

f32, bf16 = jnp.float32, jnp.bfloat16

D_MODEL = 1024
DEPTH = 4
CHUNK = 64
N_PREV = 8
N_HEADS = 16
HEAD_DIM = 64
E_MIX = 1024
REL_CLIP = 128
N_REL = 2 * REL_CLIP + 1
N_REL_PAD = 384
CONV_W = 3
N_MEM = 256
MEM_HEADS = 4
MEM_HEAD_DIM = 128
E_MEM = 512
E_BRANCH = E_MIX + E_MEM
N_IN = 3 * E_MIX + E_MEM + E_BRANCH
N_CHIPS = 4
W_IN_COLS = N_IN // N_CHIPS
W_KV_ROWS = D_MODEL // N_CHIPS
W_OUT_ROWS = E_BRANCH // N_CHIPS
DN_ALPHA = (2.0 * DEPTH) ** 0.25
LN_EPS = 1e-5
ADAM_LR, ADAM_B1, ADAM_B2, ADAM_EPS, ADAM_WD, ADAM_STEP = 0.001, 0.9, 0.999, 1e-08, 0.01, 10

QG = 4 * CHUNK
KG = QG + N_PREV * CHUNK
DB_COLS = KG // 2
NEG = -1e30
VMEM_LIMIT = 56 * 1024 * 1024

NT = (((1,), (1,)), ((), ()))
TN = (((0,), (0,)), ((), ()))
MESH = pl.DeviceIdType.MESH
ANY = pl.BlockSpec(memory_space=pl.ANY)


def _pcall(body, **kw):
    return pl.pallas_call(body, **kw)


def _params(*sem):
    return pltpu.CompilerParams(dimension_semantics=sem, vmem_limit_bytes=VMEM_LIMIT)


def _silu_parts(z):
    sig = 1.0 / (1.0 + jnp.exp(-z))
    return z * sig, sig


class _Comm:
    def __init__(self, inputs, out_shapes, aliases, n_sems, copies):
        self.inputs, self.out_shapes, self.aliases, self.n_sems, self.copies = inputs, out_shapes, aliases, n_sems, copies

    def start(self, cin, cout, send, recv):
        for cp in self.copies(cin, cout, send, recv)[0]:
            cp.start()

    def wait(self, cin, cout, send, recv):
        sends, recvs = self.copies(cin, cout, send, recv)
        for cp in recvs:
            cp.wait_recv()
        for cp in sends:
            cp.wait_send()


def _pcall_carry(body, comm, *, n_in, n_out, **kw):
    if comm is None:
        return lambda *args: (_pcall(body, **kw)(*args), ())
    grid = kw["grid"]
    k_in, k_out = len(comm.inputs), len(comm.out_shapes)

    def carried(*refs):
        ins, cin = refs[:n_in], refs[n_in:n_in + k_in]
        outs = refs[n_in + k_in:n_in + k_in + n_out]
        cout = refs[n_in + k_in + n_out:n_in + k_in + n_out + k_out]
        scratch, send, recv = refs[n_in + k_in + n_out + k_out:-2], refs[-2], refs[-1]
        ids = [pl.program_id(a) for a in range(len(grid))]
        first = functools.reduce(jnp.logical_and, [i == 0 for i in ids])
        last = functools.reduce(jnp.logical_and, [i == n - 1 for i, n in zip(ids, grid)])

        @pl.when(first)
        def _():
            comm.start(cin, cout, send, recv)

        body(*ins, *outs, *scratch)

        @pl.when(last)
        def _():
            comm.wait(cin, cout, send, recv)

    kw = dict(kw)
    kw["in_specs"] = list(kw["in_specs"]) + [ANY] * k_in
    kw["out_specs"] = tuple(kw["out_specs"]) + (ANY,) * k_out
    kw["out_shape"] = tuple(kw["out_shape"]) + tuple(comm.out_shapes)
    kw["scratch_shapes"] = list(kw.get("scratch_shapes", ())) + [pltpu.SemaphoreType.DMA((comm.n_sems,))] * 2
    aliases = dict(kw.get("input_output_aliases", {}))
    aliases.update({n_in + ci: n_out + co for ci, co in comm.aliases.items()})
    kw["input_output_aliases"] = aliases

    def run(*args):
        res = _pcall(carried, **kw)(*args, *comm.inputs)
        return res[:n_out], res[n_out:]

    return run


def _comm_call(comm, name):
    k_in = len(comm.inputs)

    def body(*refs):
        cin, cout, send, recv = refs[:k_in], refs[k_in:-2], refs[-2], refs[-1]
        comm.start(cin, cout, send, recv)
        comm.wait(cin, cout, send, recv)

    return _pcall(body, name=name, out_shape=tuple(comm.out_shapes), in_specs=[ANY] * k_in,
                  out_specs=(ANY,) * len(comm.out_shapes), input_output_aliases=dict(comm.aliases),
                  scratch_shapes=[pltpu.SemaphoreType.DMA((comm.n_sems,))] * 2)(*comm.inputs)


def _place():
    x, y, c = lax.axis_index("x"), lax.axis_index("y"), lax.axis_index("c")
    return x, y, c, 2 * x + y, (x, y, 1 - c), [(1 - x, y), (x, 1 - y), (1 - x, 1 - y)]


def _rcopy(send, recv, k, src, dst, to):
    return pltpu.make_async_remote_copy(src_ref=src, dst_ref=dst, send_sem=send.at[k], recv_sem=recv.at[k],
                                        device_id=to, device_id_type=MESH)


def _half(ref_rows, core):
    return pl.ds(core * (ref_rows // 2), ref_rows // 2)


def _gather_ici(shards, layer, extra=None):
    extras = [] if extra is None else [extra]
    n = len(shards)

    def copies(cin, cout, send, recv):
        x, y, c, me, sibling, chips = _place()
        sends, recvs = [], []
        for a in range(n):
            s, g = cin[a], cout[a]
            rows = s.shape[1]
            mine = _half(rows, c)
            sends.append(_rcopy(send, recv, 4 * a, s.at[layer], g.at[me], sibling))
            recvs.append(_rcopy(send, recv, 4 * a, s.at[layer], g.at[me], sibling))
            for p, (px, py) in enumerate(chips):
                sends.append(_rcopy(send, recv, 4 * a + 1 + p, s.at[layer, mine], g.at[me, mine], (px, py, c)))
                recvs.append(_rcopy(send, recv, 4 * a + 1 + p, s.at[layer, mine], g.at[2 * px + py, mine], (px, py, c)))
        for e in range(len(extras)):
            s, g = cin[n + e], cout[n + e]
            k = 4 * (n + e)
            sends.append(_rcopy(send, recv, k, s, g.at[me], sibling))
            recvs.append(_rcopy(send, recv, k, s, g.at[me], sibling))
            for p, (px, py) in enumerate(chips):
                sends.append(_rcopy(send, recv, k + 1 + p, s, g.at[me], (px, py, c)))
                recvs.append(_rcopy(send, recv, k + 1 + p, s, g.at[2 * px + py], (px, py, c)))
        return sends, recvs

    out_shapes = [jax.ShapeDtypeStruct((N_CHIPS,) + s.shape[1:], s.dtype) for s in shards]
    out_shapes += [jax.ShapeDtypeStruct((N_CHIPS,) + e.shape, e.dtype) for e in extras]
    return _Comm(list(shards) + extras, out_shapes, {}, 4 * (n + len(extras)), copies)


def _gather_d2d(gathered):
    n = len(gathered)

    def copies(cin, cout, send, recv):
        x, y, c, me, sibling, chips = _place()
        sends, recvs = [], []
        for a in range(n):
            g = cout[a]
            rows = g.shape[1]
            for p, (px, py) in enumerate(chips):
                mine, theirs = g.at[2 * px + py, _half(rows, c)], g.at[2 * px + py, _half(rows, 1 - c)]
                sends.append(_rcopy(send, recv, 3 * a + p, mine, mine, sibling))
                recvs.append(_rcopy(send, recv, 3 * a + p, theirs, theirs, sibling))
        return sends, recvs

    return _Comm(list(gathered), [jax.ShapeDtypeStruct(g.shape, g.dtype) for g in gathered],
                 {a: a for a in range(n)}, 3 * n, copies)


class _SemView:
    def __init__(self, ref, base):
        self.ref, self.base, self.at = ref, base, self

    def __getitem__(self, k):
        return self.ref.at[self.base + k]


def _both(a, b):
    ka, ma = len(a.inputs), len(a.out_shapes)

    def copies(cin, cout, send, recv):
        sa, ra = a.copies(cin[:ka], cout[:ma], send, recv)
        sb, rb = b.copies(cin[ka:], cout[ma:], _SemView(send, a.n_sems), _SemView(recv, a.n_sems))
        return sa + sb, ra + rb

    aliases = dict(a.aliases)
    aliases.update({ka + ci: ma + co for ci, co in b.aliases.items()})
    return _Comm(a.inputs + b.inputs, a.out_shapes + b.out_shapes, aliases, a.n_sems + b.n_sems, copies)


def _small_exchange(buf):
    def copies(cin, cout, send, recv):
        x, y, c, me, sibling, chips = _place()
        sends, recvs = [], []
        for r in range(1, 8):
            px, py, pc = x ^ ((r >> 2) & 1), y ^ ((r >> 1) & 1), c ^ (r & 1)
            sends.append(_rcopy(send, recv, r - 1, cin[0], cout[0].at[2 * me + c], (px, py, pc)))
            recvs.append(_rcopy(send, recv, r - 1, cin[0], cout[0].at[4 * px + 2 * py + pc], (px, py, pc)))
        return sends, recvs

    return _Comm([buf], [jax.ShapeDtypeStruct((8,) + buf.shape, buf.dtype)], {}, 7, copies)


def _small_sum(slots, buf, device_arr):
    rows, cols = buf.shape

    def body(d_ref, s_ref, b_ref, o_ref):
        d = pl.program_id(0)
        val = jnp.where(d == d_ref[0], b_ref[...], s_ref[...])

        @pl.when(d == 0)
        def _():
            o_ref[...] = val

        @pl.when(d > 0)
        def _():
            o_ref[...] += val

    return _pcall(
        body, name="small_sum", out_shape=jax.ShapeDtypeStruct((rows, cols), f32),
        grid_spec=pltpu.PrefetchScalarGridSpec(
            num_scalar_prefetch=1, grid=(8,),
            in_specs=[pl.BlockSpec((None, rows, cols), lambda d, d_ref: (jnp.where(d == d_ref[0], (d + 1) % 8, d), 0, 0)),
                      pl.BlockSpec((rows, cols), lambda d, d_ref: (0, 0))],
            out_specs=pl.BlockSpec((rows, cols), lambda d, d_ref: (0, 0))),
        compiler_params=_params("arbitrary"))(device_arr, slots, buf)


def _sibling_exchange(gs):
    def copies(cin, cout, send, recv):
        x, y, c, me, sibling, chips = _place()
        sends = [_rcopy(send, recv, a, g.at[:, _half(g.shape[1], 1 - c)], r, sibling) for a, (g, r) in enumerate(zip(cin, cout))]
        return sends, sends

    shapes = [jax.ShapeDtypeStruct((N_CHIPS, g.shape[1] // 2, g.shape[2]), g.dtype) for g in gs]
    return _Comm(list(gs), shapes, {}, len(gs), copies)


def _chip_scatter(ss):
    def copies(cin, cout, send, recv):
        x, y, c, me, sibling, chips = _place()
        sends = [_rcopy(send, recv, 3 * a + p, s.at[2 * px + py], r.at[p], (px, py, c))
                 for a, (s, r) in enumerate(zip(cin, cout)) for p, (px, py) in enumerate(chips)]
        return sends, sends

    shapes = [jax.ShapeDtypeStruct((3,) + s.shape[1:], s.dtype) for s in ss]
    return _Comm(list(ss), shapes, {}, 3 * len(ss), copies)


def _sibling_share(fs, layer):
    def copies(cin, cout, send, recv):
        x, y, c, me, sibling, chips = _place()
        sends, recvs = [], []
        for a, f in enumerate(cout):
            mine, theirs = f.at[layer, _half(f.shape[1], c)], f.at[layer, _half(f.shape[1], 1 - c)]
            sends.append(_rcopy(send, recv, a, mine, mine, sibling))
            recvs.append(_rcopy(send, recv, a, theirs, theirs, sibling))
        return sends, recvs

    return _Comm(list(fs), [jax.ShapeDtypeStruct(f.shape, f.dtype) for f in fs], {a: a for a in range(len(fs))},
                 len(fs), copies)


def _sum_rows(rows):
    return next(b for b in (256, 192, 128) if rows % b == 0)


def _pair_sum(g, r, core_arr, layer):
    _, rows, cols = r.shape
    br = _sum_rows(rows)
    nb = rows // br

    def body(c_ref, g_ref, r_ref, o_ref):
        o_ref[...] = (g_ref[...].astype(f32) + r_ref[...].astype(f32)).astype(bf16)

    return _pcall(
        body, name=f"pair_sum_{layer}", out_shape=jax.ShapeDtypeStruct(r.shape, bf16),
        grid_spec=pltpu.PrefetchScalarGridSpec(
            num_scalar_prefetch=1, grid=(N_CHIPS, nb),
            in_specs=[pl.BlockSpec((1, br, cols), lambda j, i, c_ref: (j, c_ref[0] * nb + i, 0)),
                      pl.BlockSpec((1, br, cols), lambda j, i, c_ref: (j, i, 0))],
            out_specs=pl.BlockSpec((1, br, cols), lambda j, i, c_ref: (j, i, 0))),
        compiler_params=_params("arbitrary", "arbitrary"))(core_arr, g, r)


def _chip_sum(s, r, place, layer, final):
    _, rows, cols = s.shape
    br = _sum_rows(rows)
    nb = rows // br

    def body(place_ref, s_ref, r_ref, *rest):
        o_ref = rest[-1]
        acc = s_ref[0].astype(f32)
        for p in range(3):
            acc = acc + r_ref[p].astype(f32)
        o_ref[...] = acc

    carried = [] if final is None else [final]
    return _pcall(
        body, name=f"chip_sum_{layer}", out_shape=jax.ShapeDtypeStruct((DEPTH, 2 * rows, cols), f32),
        grid_spec=pltpu.PrefetchScalarGridSpec(
            num_scalar_prefetch=1, grid=(nb,),
            in_specs=[pl.BlockSpec((1, br, cols), lambda i, place_ref: (place_ref[0], i, 0)),
                      pl.BlockSpec((3, br, cols), lambda i, place_ref: (0, i, 0))] + [ANY] * len(carried),
            out_specs=pl.BlockSpec((None, br, cols), lambda i, place_ref: (layer, place_ref[1] * nb + i, 0))),
        input_output_aliases={3: 0} if carried else {},
        compiler_params=_params("arbitrary"))(place, s, r, *carried)


def _inproj(xin, win, layer, comm=None, conv_w=None):
    T = xin.shape[0]
    tm = 512
    cast = xin.dtype != bf16
    conv = conv_w is not None

    def body(x_ref, w_ref, *rest):
        rest = list(rest)
        cw_ref = rest.pop(0) if conv else None
        o_ref = rest.pop(0)
        xt = x_ref[...].astype(bf16)
        if cast:
            rest.pop(0)[...] = xt
        for j in range(N_CHIPS):
            o_ref[:, j * W_IN_COLS:(j + 1) * W_IN_COLS] = jnp.dot(xt, w_ref[j], preferred_element_type=f32).astype(bf16)
        if conv:
            mix_ref, halo = rest

            @pl.when(pl.program_id(0) == 0)
            def _():
                halo[...] = jnp.zeros_like(halo)

            bg, cg, u = (o_ref[:, s * E_MIX:(s + 1) * E_MIX].astype(f32) for s in range(3))
            cu = cg * u
            out, _, _ = _conv_taps(cu, halo[6:7, :], halo[7:8, :], cw_ref)
            mix_ref[...] = (bg * out).astype(bf16)
            halo[...] = cu[tm - 8:, :]

    tile = pl.BlockSpec((tm, D_MODEL), lambda i: (i, 0))
    outs, carried = _pcall_carry(
        body, comm, n_in=2 + conv, n_out=1 + cast + conv, name=f"inproj_{layer}", grid=(T // tm,),
        out_shape=(jax.ShapeDtypeStruct((T, N_IN), bf16),) + (jax.ShapeDtypeStruct((T, D_MODEL), bf16),) * (cast + conv),
        in_specs=[tile, pl.BlockSpec((N_CHIPS, D_MODEL, W_IN_COLS), lambda i: (0, 0, 0), pipeline_mode=pl.Buffered(1))]
        + [pl.BlockSpec((CONV_W, E_MIX), lambda i: (0, 0))] * conv,
        out_specs=(pl.BlockSpec((tm, N_IN), lambda i: (i, 0)),) + (tile,) * (cast + conv),
        scratch_shapes=[pltpu.VMEM((8, E_MIX), f32)] * conv,
        compiler_params=_params("arbitrary"))(xin, win, *([conv_w] if conv else []))
    return outs[0], (outs[1] if cast else xin), (outs[-1] if conv else None), carried


def _rel_index_rows():
    j = lax.broadcasted_iota(jnp.int32, (N_REL_PAD, KG), 1)
    r = lax.broadcasted_iota(jnp.int32, (N_REL_PAD, KG), 0)
    off = jnp.where(j < KG - 2 * CHUNK, j, j - KG)
    idx = jnp.clip(N_PREV * CHUNK - off, -REL_CLIP, REL_CLIP) + REL_CLIP
    return (idx == r).astype(f32)


def _bias_expand(table_pad, layer, comm=None):
    def body(t_ref, o_ref, row_scr):
        h = pl.program_id(0)

        @pl.when(h == 0)
        def _():
            row_scr[...] = jnp.dot(t_ref[...], _rel_index_rows(), precision=lax.Precision.HIGHEST,
                                   preferred_element_type=f32)

        q = lax.broadcasted_iota(jnp.int32, (QG, KG), 0)
        k = lax.broadcasted_iota(jnp.int32, (QG, KG), 1)
        band = (k // CHUNK >= q // CHUNK) & (k // CHUNK <= q // CHUNK + N_PREV)
        t = jnp.broadcast_to(row_scr[pl.ds(h, 1), :], (QG, KG))
        for b in range(8):
            t = jnp.where(((q >> b) & 1) == 1, pltpu.roll(t, 1 << b, axis=1), t)
        for v in range(3):
            o_ref[v] = jnp.where(band & (k >= (2 - v) * QG), t, NEG)

    (bias,), carried = _pcall_carry(
        body, comm, n_in=1, n_out=1, name=f"bias_expand_{layer}", grid=(N_HEADS,),
        out_shape=(jax.ShapeDtypeStruct((3, N_HEADS, QG, KG), f32),),
        in_specs=[pl.BlockSpec((N_HEADS, N_REL_PAD), lambda h: (0, 0))],
        out_specs=(pl.BlockSpec((3, None, QG, KG), lambda h: (0, h, 0, 0)),),
        scratch_shapes=[pltpu.VMEM((N_HEADS, KG), f32)], compiler_params=_params("arbitrary"))(table_pad)
    return bias, carried


def _bias_reduce(dbias, layer):
    def body(d_ref, o_ref, row_scr):
        q = lax.broadcasted_iota(jnp.int32, (QG, DB_COLS), 0)
        k = lax.broadcasted_iota(jnp.int32, (QG, DB_COLS), 1)
        for h in range(N_HEADS):
            t = jnp.where(k > q, d_ref[h], 0.0)
            for b in range(8):
                t = jnp.where(((q >> b) & 1) == 1, pltpu.roll(t, DB_COLS - (1 << b), axis=1), t)
            row_scr[h:h + 1, :] = jnp.sum(t, axis=0, keepdims=True)
        r = lax.broadcasted_iota(jnp.int32, (N_REL_PAD, DB_COLS), 0)
        off = lax.broadcasted_iota(jnp.int32, (N_REL_PAD, DB_COLS), 1)
        own = (off >= 1) & (off < REL_CLIP + CHUNK)
        sel = jnp.where(own & (r == 2 * REL_CLIP - off), 1.0, 0.0) - jnp.where(own & (r == 2 * REL_CLIP), 1.0, 0.0)
        o_ref[...] = lax.dot_general(row_scr[...], sel, NT, precision=lax.Precision.HIGHEST, preferred_element_type=f32)

    return _pcall(body, name=f"bias_reduce_{layer}", out_shape=jax.ShapeDtypeStruct((N_HEADS, N_REL_PAD), f32),
                  scratch_shapes=[pltpu.VMEM((N_HEADS, DB_COLS), f32)],
                  compiler_params=pltpu.CompilerParams(vmem_limit_bytes=VMEM_LIMIT))(dbias)


FWD_PAIRS = 8
BWD_PAIRS = 4


def _key_specs(n_groups, npairs, slab):
    per_slab = E_MIX // (128 * npairs)
    return [pl.BlockSpec((QG, 128 * npairs), functools.partial(
        lambda hp, g, jj: (jnp.clip(g - 2 + jj, 0, n_groups - 1), slab * per_slab + hp), jj=jj)) for jj in range(3)]


def _bias_spec(npairs):
    return pl.BlockSpec((None, 2 * npairs, QG, KG), lambda hp, g: (jnp.minimum(g, 2), hp, 0, 0))


def _attn_fwd(h, bias, layer, comm=None):
    T = h.shape[0]
    n_groups = T // QG
    scale = 1.0 / math.sqrt(HEAD_DIM)

    def body(q_ref, k0, k1, k2, v0, v1, v2, b_ref, o_ref, lse_ref):
        lane = lax.broadcasted_iota(jnp.int32, (1, 128), 1)
        ones = jnp.ones((KG, 128), bf16)
        lse = jnp.zeros((QG, 128), f32)
        for pp in range(FWD_PAIRS):
            cs = slice(pp * 128, (pp + 1) * 128)
            q2 = q_ref[:, cs] * scale
            kc = jnp.concatenate([k0[:, cs], k1[:, cs], k2[:, cs]], axis=0)
            vc = jnp.concatenate([jnp.concatenate([v0[:, cs], v1[:, cs], v2[:, cs]], axis=0), ones], axis=1)
            outs = []
            for hh in range(2):
                qm = jnp.where(lane // HEAD_DIM == hh, q2, jnp.zeros_like(q2))
                s = lax.dot_general(qm, kc, NT, preferred_element_type=f32) + b_ref[2 * pp + hh]
                m = jnp.max(s, axis=1, keepdims=True)
                ol = jnp.dot(jnp.exp(s - m).astype(bf16), vc, preferred_element_type=f32)
                outs.append(ol[:, :128] / ol[:, 128:])
                lse = jnp.where(lane == 2 * pp + hh, m + jnp.log(ol[:, 128:]), lse)
            o_ref[:, cs] = jnp.where(lane // HEAD_DIM == 0, outs[0], outs[1]).astype(bf16)
        lse_ref[...] = lse

    (mix, lse), carried = _pcall_carry(
        body, comm, n_in=8, n_out=2, name=f"attn_fwd_{layer}", grid=(N_HEADS // (2 * FWD_PAIRS), n_groups),
        out_shape=(jax.ShapeDtypeStruct((T, E_MIX), bf16), jax.ShapeDtypeStruct((T, 128), f32)),
        in_specs=[pl.BlockSpec((QG, 128 * FWD_PAIRS), lambda hp, g: (g, hp))] + _key_specs(n_groups, FWD_PAIRS, 1)
        + _key_specs(n_groups, FWD_PAIRS, 2) + [_bias_spec(FWD_PAIRS)],
        out_specs=(pl.BlockSpec((QG, 128 * FWD_PAIRS), lambda hp, g: (g, hp)), pl.BlockSpec((QG, 128), lambda hp, g: (g, 0))),
        compiler_params=_params("arbitrary", "arbitrary"))(h, h, h, h, h, h, h, bias)
    return mix, lse, carried


def _halo_rows(ref, r):
    return ref[r:r + 1, :].astype(f32)


def _conv_taps(cu, p6, p7, w_ref):
    row = lax.broadcasted_iota(jnp.int32, cu.shape, 0)
    r1 = jnp.where(row == 0, p7, pltpu.roll(cu, 1, axis=0))
    r2 = jnp.where(row == 0, p6, jnp.where(row == 1, p7, pltpu.roll(cu, 2, axis=0)))
    return w_ref[2:3, :] * cu + w_ref[1:2, :] * r1 + w_ref[0:1, :] * r2, r1, r2


def _kv_mem(memb, wkv):
    def body(m_ref, w_ref, o_ref):
        o_ref[...] = jnp.dot(m_ref[...], w_ref[...], preferred_element_type=f32).astype(bf16)

    return _pcall(body, name="kv_mem", out_shape=jax.ShapeDtypeStruct((N_MEM, 2 * E_MEM), bf16),
                  compiler_params=pltpu.CompilerParams(vmem_limit_bytes=VMEM_LIMIT))(memb, wkv)


def _mem_probs(qm_ref, kv_ref, hh):
    qh = qm_ref[:, hh * MEM_HEAD_DIM:(hh + 1) * MEM_HEAD_DIM]
    kh = kv_ref[:, hh * MEM_HEAD_DIM:(hh + 1) * MEM_HEAD_DIM]
    vh = kv_ref[:, E_MEM + hh * MEM_HEAD_DIM:E_MEM + (hh + 1) * MEM_HEAD_DIM]
    s = lax.dot_general(qh, kh, NT, preferred_element_type=f32) * (1.0 / math.sqrt(MEM_HEAD_DIM))
    e = jnp.exp(s - jnp.max(s, axis=1, keepdims=True))
    return e / jnp.sum(e, axis=1, keepdims=True), qh, kh, vh


def _h_tail_specs(tm):
    return [pl.BlockSpec((tm, E_MEM), functools.partial(lambda i, cb: (i, cb), cb=cb)) for cb in (6, 7, 8, 9)]


def _ln_bwd(dy, xhat, rstd, g, dob_ref, dxp_ref, dg_ref, db_ref):
    dg_ref[0:1, :] += jnp.sum(dy * xhat, axis=0, keepdims=True)
    db_ref[0:1, :] += jnp.sum(dy, axis=0, keepdims=True)
    gx = dy * g
    dr = rstd * (gx - jnp.mean(gx, axis=1, keepdims=True) - xhat * jnp.mean(gx * xhat, axis=1, keepdims=True))
    dxp_ref[...] = DN_ALPHA * dr
    dob_ref[...] = dr.astype(bf16)


def _post_fwd(h, mix, kv, wout, x, g, b, layer, target=None, comm=None):
    T = x.shape[0]
    tm = 1024 if target is None and T % 1024 == 0 else 512

    def body(qm_ref, z0, z1, z2, mix_ref, kv_ref, w_ref, x_ref, g_ref, b_ref, *rest):
        mem = jnp.concatenate(
            [jnp.dot(_mem_probs(qm_ref, kv_ref, hh)[0].astype(bf16), kv_ref[:, E_MEM + hh * MEM_HEAD_DIM:E_MEM + (hh + 1) * MEM_HEAD_DIM],
                     preferred_element_type=f32) for hh in range(MEM_HEADS)], axis=1)
        z = jnp.concatenate([z0[...], z1[...], z2[...]], axis=1)
        one = jnp.ones((), bf16)
        y = jnp.concatenate([mix_ref[...], mem.astype(bf16)], axis=1) * (z * (one / (one + jnp.exp(-z))))
        out = jnp.dot(y, w_ref[...], preferred_element_type=f32)
        r = DN_ALPHA * x_ref[...] + out
        mu = jnp.mean(r, axis=1, keepdims=True)
        var = jnp.mean(jnp.square(r - mu), axis=1, keepdims=True)
        rstd = lax.rsqrt(var + LN_EPS)
        xhat = (r - mu) * rstd
        xn = xhat * g_ref[...] + b_ref[...]
        if target is None:
            xn_ref, xb_ref, xh_ref, rs_ref = rest
            xn_ref[...] = xn
            xb_ref[...] = xn.astype(bf16)
            xh_ref[...] = xhat
            rs_ref[...] = rstd
        else:
            t_ref, l_ref, dob_ref, dxp_ref, dg_ref, db_ref = rest

            @pl.when(pl.program_id(0) == 0)
            def _():
                l_ref[...] = jnp.zeros_like(l_ref)
                dg_ref[...] = jnp.zeros_like(dg_ref)
                db_ref[...] = jnp.zeros_like(db_ref)

            err = xn - t_ref[...]
            l_ref[...] += jnp.sum(jnp.square(err))
            _ln_bwd(err * (1.0 / D_MODEL), xhat, rstd, g_ref[...], dob_ref, dxp_ref, dg_ref, db_ref)

    tile = lambda w: pl.BlockSpec((tm, w), lambda i: (i, 0))
    const = lambda r, c: pl.BlockSpec((r, c), lambda i: (0, 0))
    resident = lambda r, c: pl.BlockSpec((r, c), lambda i: (0, 0), pipeline_mode=pl.Buffered(1))
    in_specs = _h_tail_specs(tm) + [tile(E_MIX), resident(N_MEM, 2 * E_MEM), resident(E_BRANCH, D_MODEL), tile(D_MODEL),
                                    const(1, D_MODEL), const(1, D_MODEL)]
    if target is None:
        return _pcall_carry(
            body, comm, n_in=10, n_out=4, name=f"post_fwd_{layer}", grid=(T // tm,),
            out_shape=(jax.ShapeDtypeStruct((T, D_MODEL), f32), jax.ShapeDtypeStruct((T, D_MODEL), bf16),
                       jax.ShapeDtypeStruct((T, D_MODEL), f32), jax.ShapeDtypeStruct((T, 1), f32)),
            in_specs=in_specs, out_specs=(tile(D_MODEL), tile(D_MODEL), tile(D_MODEL), tile(1)),
            compiler_params=_params("arbitrary"))(h, h, h, h, mix, kv, wout, x, g, b)
    return _pcall(
        body, name=f"post_fwd_loss_{layer}", grid=(T // tm,),
        out_shape=(jax.ShapeDtypeStruct((8, 128), f32), jax.ShapeDtypeStruct((T, D_MODEL), bf16),
                   jax.ShapeDtypeStruct((T, D_MODEL), f32), jax.ShapeDtypeStruct((8, D_MODEL), f32),
                   jax.ShapeDtypeStruct((8, D_MODEL), f32)),
        in_specs=in_specs + [tile(D_MODEL)],
        out_specs=(const(8, 128), tile(D_MODEL), tile(D_MODEL), const(8, D_MODEL), const(8, D_MODEL)),
        compiler_params=_params("arbitrary"))(h, h, h, h, mix, kv, wout, x, g, b, target)


def _post_bwd(dob, h, mix, kv, wout, layer, comm=None):
    T = dob.shape[0]
    tm = 512
    n_tiles = T // tm
    inv = 1.0 / math.sqrt(MEM_HEAD_DIM)

    def body(dob_ref, qm_ref, z0, z1, z2, mix_ref, kv_ref, w_ref, dhb_ref, dmix_ref, dkv_ref, dwo_out, dwo_ref):
        @pl.when(pl.program_id(0) == 0)
        def _():
            dkv_ref[...] = jnp.zeros_like(dkv_ref)
            dwo_ref[...] = jnp.zeros_like(dwo_ref)

        dob = dob_ref[...]
        probs = [_mem_probs(qm_ref, kv_ref, hh) for hh in range(MEM_HEADS)]
        mem = jnp.concatenate([jnp.dot(p.astype(bf16), vh, preferred_element_type=f32) for p, _, _, vh in probs], axis=1)
        z = jnp.concatenate([z0[...], z1[...], z2[...]], axis=1).astype(f32)
        act, sig = _silu_parts(z)
        cat = jnp.concatenate([mix_ref[...].astype(f32), mem], axis=1)
        yb = (cat * act).astype(bf16)
        dwo_ref[...] += lax.dot_general(yb, dob, TN, preferred_element_type=f32)
        dyv = lax.dot_general(dob, w_ref[...], NT, preferred_element_type=f32)
        dz = dyv * cat * (sig * (1.0 + z * (1.0 - sig)))
        dcat = dyv * act
        dmix_ref[...] = dcat[:, :E_MIX].astype(bf16)
        dqs = []
        for hh, (p, qh, kh, vh) in enumerate(probs):
            dmem = dcat[:, E_MIX + hh * MEM_HEAD_DIM:E_MIX + (hh + 1) * MEM_HEAD_DIM].astype(bf16)
            dp = lax.dot_general(dmem, vh, NT, preferred_element_type=f32)
            ds = (p * (dp - jnp.sum(p * dp, axis=1, keepdims=True))).astype(bf16)
            dqs.append(jnp.dot(ds, kh, preferred_element_type=f32) * inv)
            dkv_ref[:, hh * MEM_HEAD_DIM:(hh + 1) * MEM_HEAD_DIM] += lax.dot_general(ds, qh, TN, preferred_element_type=f32) * inv
            dkv_ref[:, E_MEM + hh * MEM_HEAD_DIM:E_MEM + (hh + 1) * MEM_HEAD_DIM] += lax.dot_general(
                p.astype(bf16), dmem, TN, preferred_element_type=f32)
        dhb_ref[...] = jnp.concatenate(dqs + [dz], axis=1).astype(bf16)

        @pl.when(pl.program_id(0) == n_tiles - 1)
        def _():
            dwo_out[...] = dwo_ref[...].astype(bf16)

    tile = lambda w: pl.BlockSpec((tm, w), lambda i: (i, 0))
    const = lambda r, c: pl.BlockSpec((r, c), lambda i: (0, 0))
    resident = lambda r, c: pl.BlockSpec((r, c), lambda i: (0, 0), pipeline_mode=pl.Buffered(1))
    return _pcall_carry(
        body, comm, n_in=8, n_out=4, name=f"post_bwd_{layer}", grid=(n_tiles,),
        out_shape=(jax.ShapeDtypeStruct((T, E_MEM + E_BRANCH), bf16), jax.ShapeDtypeStruct((T, E_MIX), bf16),
                   jax.ShapeDtypeStruct((N_MEM, 2 * E_MEM), f32), jax.ShapeDtypeStruct((E_BRANCH, D_MODEL), bf16)),
        in_specs=[tile(D_MODEL)] + _h_tail_specs(tm) + [tile(E_MIX), resident(N_MEM, 2 * E_MEM), resident(E_BRANCH, D_MODEL)],
        out_specs=(tile(E_MEM + E_BRANCH), tile(E_MIX), const(N_MEM, 2 * E_MEM), const(E_BRANCH, D_MODEL)),
        scratch_shapes=[pltpu.VMEM((E_BRANCH, D_MODEL), f32)],
        compiler_params=_params("arbitrary"))(dob, h, h, h, h, mix, kv, wout)


def _attn_bwd(h, bias, dmix, lse, layer, comm=None):
    T = h.shape[0]
    n_groups = T // QG
    scale = 1.0 / math.sqrt(HEAD_DIM)

    def body(q_ref, k0, k1, k2, v0, v1, v2, do_ref, b_ref, lse_ref, dq_ref, dk_ref, dv_ref, db_ref, acck, accv):
        g = pl.program_id(1)

        @pl.when(g == 0)
        def _():
            acck[...] = jnp.zeros_like(acck)
            accv[...] = jnp.zeros_like(accv)
            db_ref[...] = jnp.zeros_like(db_ref)

        @pl.when(g < n_groups)
        def _():
            lane = lax.broadcasted_iota(jnp.int32, (1, 128), 1)
            first = lane // HEAD_DIM == 0
            for pp in range(BWD_PAIRS):
                cs = slice(pp * 128, (pp + 1) * 128)
                do2 = do_ref[:, cs]
                q2 = q_ref[:, cs] * scale
                kc = jnp.concatenate([k0[:, cs], k1[:, cs], k2[:, cs]], axis=0)
                vc = jnp.concatenate([v0[:, cs], v1[:, cs], v2[:, cs]], axis=0)
                q2t, do2t = q2.T, do2.T
                dqs, dks, dvs = [], [], []
                for hh in range(2):
                    hm = lane // HEAD_DIM == hh
                    head = (pl.program_id(0) * BWD_PAIRS + pp) * 2 + hh
                    lse = jnp.sum(jnp.where(lane == head, lse_ref[...], 0.0), axis=1, keepdims=True)
                    qm = jnp.where(hm, q2, jnp.zeros_like(q2))
                    dom = jnp.where(hm, do2, jnp.zeros_like(do2))
                    s = lax.dot_general(qm, kc, NT, preferred_element_type=f32) + b_ref[2 * pp + hh]
                    p = jnp.exp(s - lse)
                    dp = lax.dot_general(dom, vc, NT, preferred_element_type=f32)
                    ds = p * (dp - jnp.sum(p * dp, axis=1, keepdims=True))
                    db_ref[2 * pp + hh] += ds[:, KG - DB_COLS:]
                    dsb, pb = ds.astype(bf16), p.astype(bf16)
                    dqs.append(jnp.dot(dsb, kc, preferred_element_type=f32) * scale)
                    dks.append(jnp.dot(q2t[hh * HEAD_DIM:(hh + 1) * HEAD_DIM], dsb, preferred_element_type=f32))
                    dvs.append(jnp.dot(do2t[hh * HEAD_DIM:(hh + 1) * HEAD_DIM], pb, preferred_element_type=f32))
                dq_ref[:, cs] = jnp.where(first, dqs[0], dqs[1]).astype(bf16)
                dkc = jnp.concatenate(dks, axis=0).T
                dvc = jnp.concatenate(dvs, axis=0).T
                for jj in range(3):
                    slot = (g + 1 + jj) % 3
                    if jj == 2:
                        acck[slot, :, cs] = dkc[jj * QG:(jj + 1) * QG]
                        accv[slot, :, cs] = dvc[jj * QG:(jj + 1) * QG]
                    else:
                        acck[slot, :, cs] += dkc[jj * QG:(jj + 1) * QG]
                        accv[slot, :, cs] += dvc[jj * QG:(jj + 1) * QG]

        done = (g + 1) % 3
        dk_ref[...] = acck[done].astype(bf16)
        dv_ref[...] = accv[done].astype(bf16)

    last = n_groups - 1
    width = 128 * BWD_PAIRS
    qspec = pl.BlockSpec((QG, width), lambda hp, g: (jnp.minimum(g, last), hp))
    kout = pl.BlockSpec((QG, width), lambda hp, g: (jnp.clip(g - 2, 0, last), hp))
    dbspec = pl.BlockSpec((2 * BWD_PAIRS, QG, DB_COLS), lambda hp, g: (hp, 0, 0))
    lspec = pl.BlockSpec((QG, 128), lambda hp, g: (jnp.minimum(g, last), 0))
    return _pcall_carry(
        body, comm, n_in=10, n_out=4, name=f"attn_bwd_{layer}", grid=(N_HEADS // (2 * BWD_PAIRS), n_groups + 2),
        out_shape=(jax.ShapeDtypeStruct((T, E_MIX), bf16),) * 3 + (jax.ShapeDtypeStruct((N_HEADS, QG, DB_COLS), f32),),
        in_specs=[qspec] + _key_specs(n_groups, BWD_PAIRS, 1) + _key_specs(n_groups, BWD_PAIRS, 2)
        + [qspec, _bias_spec(BWD_PAIRS), lspec],
        out_specs=(qspec, kout, kout, dbspec),
        scratch_shapes=[pltpu.VMEM((3, QG, width), f32), pltpu.VMEM((3, QG, width), f32)],
        compiler_params=_params("arbitrary", "arbitrary"))(h, h, h, h, h, h, h, dmix, bias, lse)


def _conv_bwd(h, w, dmix, layer):
    T = h.shape[0]
    tm = 512
    n_tiles = T // tm

    def body(bg_ref, cg_ref, u_ref, cgp_ref, up_ref, dy_ref, bgn_ref, dyn_ref, w_ref, dbg_ref, dcg_ref, du_ref, dw_ref):
        i = pl.program_id(0)

        @pl.when(i == 0)
        def _():
            dw_ref[...] = jnp.zeros_like(dw_ref)

        first = (i == 0).astype(f32)
        final = (i == n_tiles - 1).astype(f32)
        bg, cg, u = bg_ref[...].astype(f32), cg_ref[...].astype(f32), u_ref[...].astype(f32)
        dy = dy_ref[...].astype(f32)
        cu = cg * u
        p6 = _halo_rows(cgp_ref, 14) * _halo_rows(up_ref, 14) * (1.0 - first)
        p7 = _halo_rows(cgp_ref, 15) * _halo_rows(up_ref, 15) * (1.0 - first)
        conv, r1, r2 = _conv_taps(cu, p6, p7, w_ref)
        dbg_ref[...] = (dy * conv).astype(bf16)
        dc = dy * bg
        n0 = _halo_rows(dyn_ref, 0) * _halo_rows(bgn_ref, 0) * (1.0 - final)
        n1 = _halo_rows(dyn_ref, 1) * _halo_rows(bgn_ref, 1) * (1.0 - final)
        row = lax.broadcasted_iota(jnp.int32, dc.shape, 0)
        f1 = jnp.where(row == tm - 1, n0, pltpu.roll(dc, tm - 1, axis=0))
        f2 = jnp.where(row == tm - 2, n0, jnp.where(row == tm - 1, n1, pltpu.roll(dc, tm - 2, axis=0)))
        dcu = w_ref[2:3, :] * dc + w_ref[1:2, :] * f1 + w_ref[0:1, :] * f2
        dcg_ref[...] = (dcu * u).astype(bf16)
        du_ref[...] = (dcu * cg).astype(bf16)
        dw_ref[0:1, :] += jnp.sum(dc * r2, axis=0, keepdims=True)
        dw_ref[1:2, :] += jnp.sum(dc * r1, axis=0, keepdims=True)
        dw_ref[2:3, :] += jnp.sum(dc * cu, axis=0, keepdims=True)

    tile = lambda slab: pl.BlockSpec((tm, E_MIX), lambda i: (i, slab))
    prev = lambda slab: pl.BlockSpec((16, E_MIX), lambda i: (jnp.maximum(i * (tm // 16) - 1, 0), slab))
    nxt = lambda slab: pl.BlockSpec((16, E_MIX), lambda i: (jnp.minimum((i + 1) * (tm // 16), T // 16 - 1), slab))
    return _pcall(
        body, name=f"conv_bwd_{layer}", grid=(n_tiles,),
        out_shape=(jax.ShapeDtypeStruct((T, E_MIX), bf16),) * 3 + (jax.ShapeDtypeStruct((8, E_MIX), f32),),
        in_specs=[tile(0), tile(1), tile(2), prev(1), prev(2), tile(0), nxt(0), nxt(0),
                  pl.BlockSpec((CONV_W, E_MIX), lambda i: (0, 0))],
        out_specs=(tile(0), tile(0), tile(0), pl.BlockSpec((8, E_MIX), lambda i: (0, 0))),
        compiler_params=_params("arbitrary"))(h, h, h, h, h, dmix, h, dmix, w)


def _inproj_bwd_dx(da, db, dc, dhb, dxp, win, layer, below=None, comm=None):
    T = dxp.shape[0]
    tm = 512

    def body(da_ref, db_ref, dc_ref, dhb_ref, dxp_ref, w_ref, *rest):
        dh = jnp.concatenate([da_ref[...], db_ref[...], dc_ref[...], dhb_ref[...]], axis=1)
        acc = dxp_ref[...]
        for j in range(N_CHIPS):
            acc = acc + lax.dot_general(dh[:, j * W_IN_COLS:(j + 1) * W_IN_COLS], w_ref[j], NT, preferred_element_type=f32)
        if below is None:
            rest[0][...] = acc
        else:
            xh_ref, rs_ref, g_ref, dob_ref, dxo_ref, dg_ref, db_out = rest

            @pl.when(pl.program_id(0) == 0)
            def _():
                dg_ref[...] = jnp.zeros_like(dg_ref)
                db_out[...] = jnp.zeros_like(db_out)

            _ln_bwd(acc, xh_ref[...], rs_ref[...], g_ref[...], dob_ref, dxo_ref, dg_ref, db_out)

    tile = lambda w: pl.BlockSpec((tm, w), lambda i: (i, 0))
    const = lambda r, c: pl.BlockSpec((r, c), lambda i: (0, 0))
    in_specs = [tile(E_MIX), tile(E_MIX), tile(E_MIX), tile(E_MEM + E_BRANCH), tile(D_MODEL),
                pl.BlockSpec((N_CHIPS, D_MODEL, W_IN_COLS), lambda i: (0, 0, 0), pipeline_mode=pl.Buffered(1))]
    if below is None:
        return _pcall_carry(
            body, comm, n_in=6, n_out=1, name=f"inproj_bwd_dx_{layer}", grid=(T // tm,),
            out_shape=(jax.ShapeDtypeStruct((T, D_MODEL), f32),), in_specs=in_specs, out_specs=(tile(D_MODEL),),
            compiler_params=_params("arbitrary"))(da, db, dc, dhb, dxp, win)
    return _pcall_carry(
        body, comm, n_in=9, n_out=4, name=f"inproj_bwd_dx_{layer}", grid=(T // tm,),
        out_shape=(jax.ShapeDtypeStruct((T, D_MODEL), bf16), jax.ShapeDtypeStruct((T, D_MODEL), f32),
                   jax.ShapeDtypeStruct((8, D_MODEL), f32), jax.ShapeDtypeStruct((8, D_MODEL), f32)),
        in_specs=in_specs + [tile(D_MODEL), tile(1), const(1, D_MODEL)],
        out_specs=(tile(D_MODEL), tile(D_MODEL), const(8, D_MODEL), const(8, D_MODEL)),
        compiler_params=_params("arbitrary"))(da, db, dc, dhb, dxp, win, *below)


def _dh_pieces():
    pieces, col = [], 0
    for src, width in enumerate((E_MIX, E_MIX, E_MIX, E_MEM + E_BRANCH)):
        lo = 0
        while lo < width:
            j, c0 = divmod(col + lo, W_IN_COLS)
            n = min(width - lo, W_IN_COLS - c0)
            pieces.append((src, lo, lo + n, j, c0, c0 + n))
            lo += n
        col += width
    return pieces


def _inproj_bwd_dw(da, db, dc, dhb, xb, layer, comm=None):
    T = xb.shape[0]
    tm = 1024 if T % 1024 == 0 else 512
    n_tiles = T // tm

    def body(da_ref, db_ref, dc_ref, dhb_ref, x_ref, o_ref, acc, stage, sem):
        i = pl.program_id(0)

        @pl.when(i == 0)
        def _():
            acc[...] = jnp.zeros_like(acc)

        srcs = (da_ref, db_ref, dc_ref, dhb_ref)
        xt = x_ref[...]
        for s, s0, s1, j, c0, c1 in _dh_pieces():
            acc[j, :, c0:c1] += lax.dot_general(xt, srcs[s][:, s0:s1], TN, preferred_element_type=f32)

        @pl.when(i == n_tiles - 1)
        def _():
            for j in range(N_CHIPS):
                stage[...] = acc[j].astype(bf16)
                cp = pltpu.make_async_copy(stage, o_ref.at[j], sem)
                cp.start()
                cp.wait()

    tile = lambda w: pl.BlockSpec((tm, w), lambda i: (i, 0))
    (dw,), carried = _pcall_carry(
        body, comm, n_in=5, n_out=1, name=f"inproj_bwd_dw_{layer}", grid=(n_tiles,),
        out_shape=(jax.ShapeDtypeStruct((N_CHIPS, D_MODEL, W_IN_COLS), bf16),),
        in_specs=[tile(E_MIX), tile(E_MIX), tile(E_MIX), tile(E_MEM + E_BRANCH), tile(D_MODEL)],
        out_specs=(ANY,),
        scratch_shapes=[pltpu.VMEM((N_CHIPS, D_MODEL, W_IN_COLS), f32), pltpu.VMEM((D_MODEL, W_IN_COLS), bf16),
                        pltpu.SemaphoreType.DMA],
        compiler_params=_params("arbitrary"))(da, db, dc, dhb, xb)
    return dw, carried


def _kv_mem_bwd(memb, dkv, layer):
    def body(m_ref, d_ref, o_ref):
        o_ref[...] = lax.dot_general(m_ref[...], d_ref[...].astype(bf16), TN, preferred_element_type=f32).astype(bf16)

    return _pcall(body, name=f"kv_mem_bwd_{layer}", out_shape=jax.ShapeDtypeStruct((D_MODEL, 2 * E_MEM), bf16),
                  compiler_params=pltpu.CompilerParams(vmem_limit_bytes=VMEM_LIMIT))(memb, dkv)


def _adamw(w, g, m, v, name):
    shape = w.shape
    cols = shape[-1]
    rows = w.size // cols
    args = [a.reshape(rows, cols) for a in (w, g, m, v)]
    br = 256 if rows % 256 == 0 and rows > 256 else rows

    def body(w_ref, g_ref, m_ref, v_ref, go_ref, d_ref, nm_ref, nv_ref):
        gg = g_ref[...]
        nm = ADAM_B1 * m_ref[...] + (1.0 - ADAM_B1) * gg
        nv = ADAM_B2 * v_ref[...] + (1.0 - ADAM_B2) * jnp.square(gg)
        m_hat = nm / (1.0 - ADAM_B1 ** ADAM_STEP)
        v_hat = nv / (1.0 - ADAM_B2 ** ADAM_STEP)
        go_ref[...] = gg
        d_ref[...] = -ADAM_LR * (m_hat / (jnp.sqrt(v_hat) + ADAM_EPS) + ADAM_WD * w_ref[...])
        nm_ref[...] = nm
        nv_ref[...] = nv

    spec = pl.BlockSpec((br, cols), lambda i: (i, 0))
    outs = _pcall(body, name=name, grid=(rows // br,), out_shape=(jax.ShapeDtypeStruct((rows, cols), f32),) * 4,
                  in_specs=[spec] * 4, out_specs=(spec,) * 4, compiler_params=_params("arbitrary"))(*args)
    return tuple(o.reshape(shape) for o in outs)


def kernel(x, mem, w_in, w_mem_kv, w_out, rel_bias, conv_w, ln_g, ln_b, loss_target, m_w_in, m_w_mem_kv, m_w_out, m_rel_bias, m_conv_w, m_ln_g, m_ln_b, v_w_in, v_w_mem_kv, v_w_out, v_rel_bias, v_conv_w, v_ln_g, v_ln_b):
    T = x.shape[1]
    x0 = x.reshape(T, D_MODEL)
    target = loss_target.reshape(T, D_MODEL)
    memb = mem.reshape(N_MEM, D_MODEL).astype(bf16)
    chip = 2 * lax.axis_index("x") + lax.axis_index("y")
    core = lax.axis_index("c")
    chip_arr = jnp.reshape(chip, (1,)).astype(jnp.int32)
    core_arr = jnp.reshape(core, (1,)).astype(jnp.int32)

    place = jnp.concatenate([chip_arr, core_arr])
    tables = jnp.pad(rel_bias, ((0, 0), (0, 0), (0, N_REL_PAD - N_REL)))

    shards = [w_in.astype(bf16), w_mem_kv.astype(bf16), w_out.astype(bf16)]
    biases = {}
    biases[0], arrived_in = _bias_expand(tables[0], 0, _gather_ici(shards[:1], 0))
    biases[2], (*arrived_rest, cw_g) = _bias_expand(tables[1], 2, _gather_ici(shards[1:], 0, extra=conv_w))
    gathered = {0: _comm_call(_gather_d2d(list(arrived_in) + arrived_rest), "gather_d2d_0")}
    conv_full = jnp.transpose(cw_g, (1, 2, 0, 3)).reshape(DEPTH // 2, CONV_W, E_MIX)

    xs, xbs, hs, mixes, kvs, xhats, rstds, lses = [x0], [x0], [], [], [], [], [], {}
    for layer in range(DEPTH):
        win, wkv, wout = gathered[layer]
        more = layer + 1 < DEPTH
        attention = layer % 2 == 0
        h, xbs[layer], mix, arrived = _inproj(xbs[layer], win, layer, _gather_ici(shards, layer + 1) if more else None,
                                              None if attention else conv_full[layer // 2])
        passing = _gather_d2d(list(arrived)) if more else None
        if attention:
            mix, lses[layer], done = _attn_fwd(h, biases[layer], layer, passing)
        kv = _kv_mem(memb, wkv.reshape(D_MODEL, 2 * E_MEM))
        result = _post_fwd(h, mix, kv, wout.reshape(E_BRANCH, D_MODEL), xs[layer], ln_g[layer][None, :],
                           ln_b[layer][None, :], layer, None if more else target, None if attention else passing)
        if more:
            (xn, xnb, xhat, rstd), done = (result[0], done) if attention else result
            xs.append(xn); xbs.append(xnb); xhats.append(xhat); rstds.append(rstd)
            gathered[layer + 1] = list(done)
        hs.append(h); mixes.append(mix); kvs.append(kv)

    dgs, dbs, dconvs, dtables = [None] * DEPTH, [None] * DEPTH, [None] * (DEPTH // 2), [None] * ((DEPTH + 1) // 2)
    lsum, dob, dxp, dgs[DEPTH - 1], dbs[DEPTH - 1] = result
    loss = lax.psum(lsum[0, 0], ("x", "y", "c")) * (0.5 / D_MODEL)
    finals = [None, None, None]
    above = None
    for layer in reversed(range(DEPTH)):
        h = hs[layer]
        win, wkv, wout = gathered[layer]
        (dhb, dmix, dkv, dwo), from_sibling = _post_bwd(
            dob, h, mixes[layer], kvs[layer], wout.reshape(E_BRANCH, D_MODEL), layer,
            _sibling_exchange(above) if above else None)
        sums = [_pair_sum(g, r, core_arr, layer + 1) for g, r in zip(above, from_sibling)] if above else None
        scatter = _chip_scatter(sums) if above else None
        if layer % 2 == 0:
            (da, db, dc, dbias), from_chips = _attn_bwd(h, biases[layer], dmix, lses[layer], layer, scatter)
            dtables[layer // 2] = _bias_reduce(dbias, layer)
        else:
            da, db, dc, dconvs[layer // 2] = _conv_bwd(h, conv_full[layer // 2], dmix, layer)
        if layer > 0:
            (dob_below, dxp_below, dgs[layer - 1], dbs[layer - 1]), landed = _inproj_bwd_dx(
                da, db, dc, dhb, dxp, win, layer, (xhats[layer - 1], rstds[layer - 1], ln_g[layer - 1][None, :]),
                scatter if layer % 2 == 1 else None)
            from_chips = landed if layer % 2 == 1 else from_chips
        share = None
        if above:
            finals = [_chip_sum(s, r, place, layer + 1, f) for s, r, f in zip(sums, from_chips, finals)]
            share = _sibling_share(finals, layer + 1)
        if layer == 0:
            pad8 = lambda a: jnp.pad(a, ((0, 8 - a.shape[0]), (0, 0)))
            small_mine = jnp.concatenate(dgs + dbs + dconvs + [pad8(t.reshape(-1, D_MODEL)) for t in dtables], axis=0)
            share = _both(share, _small_exchange(small_mine))
        g_win, shared = _inproj_bwd_dw(da, db, dc, dhb, xbs[layer], layer, share)
        if layer == 0:
            *shared, small_slots = shared
        finals = list(shared) if above else finals
        above = [g_win, _kv_mem_bwd(memb, dkv, layer).reshape(N_CHIPS, W_KV_ROWS, 2 * E_MEM),
                 dwo.reshape(N_CHIPS, W_OUT_ROWS, D_MODEL)]
        if layer > 0:
            dob, dxp = dob_below, dxp_below
    from_sibling = _comm_call(_sibling_exchange(above), "sibling_exchange_0")
    sums = [_pair_sum(g, r, core_arr, 0) for g, r in zip(above, from_sibling)]
    (dx,), from_chips = _inproj_bwd_dx(da, db, dc, dhb, dxp, win, 0, None, _chip_scatter(sums))
    finals = [_chip_sum(s, r, place, 0, f) for s, r, f in zip(sums, from_chips, finals)]
    grad_w_in, grad_w_mem_kv, grad_w_out = _comm_call(_sibling_share(finals, 0), "sibling_share_0")
    grad_x = dx.reshape(1, T, D_MODEL)

    device_arr = jnp.reshape(2 * chip + core, (1,)).astype(jnp.int32)
    small = _small_sum(small_slots, small_mine, device_arr)
    grad_ln_g = jnp.stack([small[8 * l] for l in range(DEPTH)])
    grad_ln_b = jnp.stack([small[8 * (DEPTH + l)] for l in range(DEPTH)])
    conv_all = jnp.stack([small[8 * (2 * DEPTH + a):8 * (2 * DEPTH + a) + CONV_W] for a in range(DEPTH // 2)])
    grad_conv_w = lax.dynamic_slice_in_dim(conv_all, chip * (E_MIX // N_CHIPS), E_MIX // N_CHIPS, axis=2)
    t0 = 8 * (2 * DEPTH + DEPTH // 2)
    grad_rel_bias = jnp.stack([small[t0 + 8 * a:t0 + 8 * a + 6].reshape(N_HEADS, N_REL_PAD)[:, :N_REL]
                               for a in range((DEPTH + 1) // 2)])

    grads = [grad_w_in, grad_w_mem_kv, grad_w_out, grad_rel_bias, grad_conv_w, grad_ln_g, grad_ln_b]
    weights = [w_in, w_mem_kv, w_out, rel_bias, conv_w, ln_g, ln_b]
    moms = [m_w_in, m_w_mem_kv, m_w_out, m_rel_bias, m_conv_w, m_ln_g, m_ln_b]
    vels = [v_w_in, v_w_mem_kv, v_w_out, v_rel_bias, v_conv_w, v_ln_g, v_ln_b]
    names = ["w_in", "w_mem_kv", "w_out", "rel_bias", "conv_w", "ln_g", "ln_b"]
    upd = [_adamw(w, g, m, v, f"adamw_{n}") for w, g, m, v, n in zip(weights, grads, moms, vels, names)]
    grads, deltas, new_m, new_v = zip(*upd)
    return (loss, grad_x, *grads, *deltas, *new_m, *new_v)
```

```python
import functools
import math

import jax
import jax.numpy as jnp
from jax import lax
from jax.experimental import pallas as pl
from jax.experimental.pallas import tpu as pltpu

f32, bf16 = jnp.float32, jnp.bfloat16

D_MODEL = 1024
DEPTH = 4
CHUNK = 64
N_PREV = 8
N_HEADS = 16
HEAD_DIM = 64
E_MIX = 1024
REL_CLIP = 128
N_REL = 2 * REL_CLIP + 1
N_REL_PAD = 384
CONV_W = 3
N_MEM = 256
MEM_HEADS = 4
MEM_HEAD_DIM = 128
E_MEM = 512
E_BRANCH = E_MIX + E_MEM
N_IN = 3 * E_MIX + E_MEM + E_BRANCH
N_CHIPS = 4
W_IN_COLS = N_IN // N_CHIPS
W_KV_ROWS = D_MODEL // N_CHIPS
W_OUT_ROWS = E_BRANCH // N_CHIPS
DN_ALPHA = (2.0 * DEPTH) ** 0.25
LN_EPS = 1e-5
ADAM_LR, ADAM_B1, ADAM_B2, ADAM_EPS, ADAM_WD, ADAM_STEP = 0.001, 0.9, 0.999, 1e-08, 0.01, 10

QG = 4 * CHUNK
KG = QG + N_PREV * CHUNK
DB_COLS = KG // 2
NEG = -1e30
VMEM_LIMIT = 56 * 1024 * 1024

NT = (((1,), (1,)), ((), ()))
TN = (((0,), (0,)), ((), ()))
MESH = pl.DeviceIdType.MESH
ANY = pl.BlockSpec(memory_space=pl.ANY)


def _pcall(body, **kw):
    return pl.pallas_call(body, **kw)


def _params(*sem):
    return pltpu.CompilerParams(dimension_semantics=sem, vmem_limit_bytes=VMEM_LIMIT)


def _silu_parts(z):
    sig = 1.0 / (1.0 + jnp.exp(-z))
    return z * sig, sig


class _Comm:
    def __init__(self, inputs, out_shapes, aliases, n_sems, copies):
        self.inputs, self.out_shapes, self.aliases, self.n_sems, self.copies = inputs, out_shapes, aliases, n_sems, copies

    def start(self, cin, cout, send, recv):
        for cp in self.copies(cin, cout, send, recv)[0]:
            cp.start()

    def wait(self, cin, cout, send, recv):
        sends, recvs = self.copies(cin, cout, send, recv)
        for cp in recvs:
            cp.wait_recv()
        for cp in sends:
            cp.wait_send()


def _pcall_carry(body, comm, *, n_in, n_out, **kw):
    if comm is None:
        return lambda *args: (_pcall(body, **kw)(*args), ())
    grid = kw["grid"]
    k_in, k_out = len(comm.inputs), len(comm.out_shapes)

    def carried(*refs):
        ins, cin = refs[:n_in], refs[n_in:n_in + k_in]
        outs = refs[n_in + k_in:n_in + k_in + n_out]
        cout = refs[n_in + k_in + n_out:n_in + k_in + n_out + k_out]
        scratch, send, recv = refs[n_in + k_in + n_out + k_out:-2], refs[-2], refs[-1]
        ids = [pl.program_id(a) for a in range(len(grid))]
        first = functools.reduce(jnp.logical_and, [i == 0 for i in ids])
        last = functools.reduce(jnp.logical_and, [i == n - 1 for i, n in zip(ids, grid)])

        @pl.when(first)
        def _():
            comm.start(cin, cout, send, recv)

        body(*ins, *outs, *scratch)

        @pl.when(last)
        def _():
            comm.wait(cin, cout, send, recv)

    kw = dict(kw)
    kw["in_specs"] = list(kw["in_specs"]) + [ANY] * k_in
    kw["out_specs"] = tuple(kw["out_specs"]) + (ANY,) * k_out
    kw["out_shape"] = tuple(kw["out_shape"]) + tuple(comm.out_shapes)
    kw["scratch_shapes"] = list(kw.get("scratch_shapes", ())) + [pltpu.SemaphoreType.DMA((comm.n_sems,))] * 2
    aliases = dict(kw.get("input_output_aliases", {}))
    aliases.update({n_in + ci: n_out + co for ci, co in comm.aliases.items()})
    kw["input_output_aliases"] = aliases

    def run(*args):
        res = _pcall(carried, **kw)(*args, *comm.inputs)
        return res[:n_out], res[n_out:]

    return run


def _comm_call(comm, name):
    k_in = len(comm.inputs)

    def body(*refs):
        cin, cout, send, recv = refs[:k_in], refs[k_in:-2], refs[-2], refs[-1]
        comm.start(cin, cout, send, recv)
        comm.wait(cin, cout, send, recv)

    return _pcall(body, name=name, out_shape=tuple(comm.out_shapes), in_specs=[ANY] * k_in,
                  out_specs=(ANY,) * len(comm.out_shapes), input_output_aliases=dict(comm.aliases),
                  scratch_shapes=[pltpu.SemaphoreType.DMA((comm.n_sems,))] * 2)(*comm.inputs)


def _place():
    x, y, c = lax.axis_index("x"), lax.axis_index("y"), lax.axis_index("c")
    return x, y, c, 2 * x + y, (x, y, 1 - c), [(1 - x, y), (x, 1 - y), (1 - x, 1 - y)]


def _rcopy(send, recv, k, src, dst, to):
    return pltpu.make_async_remote_copy(src_ref=src, dst_ref=dst, send_sem=send.at[k], recv_sem=recv.at[k],
                                        device_id=to, device_id_type=MESH)


def _half(ref_rows, core):
    return pl.ds(core * (ref_rows // 2), ref_rows // 2)


def _gather_ici(shards, layer, extra=None, peers=(0, 1, 2), own=True, into=None):
    extras = [] if extra is None else [extra]
    n = len(shards)

    def copies(cin, cout, send, recv):
        x, y, c, me, sibling, chips = _place()
        sends, recvs = [], []
        for a in range(n + len(extras)):
            s, g = cin[a], cout[a]
            whole = a >= n
            src = s if whole else s.at[layer]
            if own:
                sends.append(_rcopy(send, recv, 4 * a, src, g.at[me], sibling))
                recvs.append(_rcopy(send, recv, 4 * a, src, g.at[me], sibling))
            for p in peers:
                px, py = chips[p]
                if whole:
                    sends.append(_rcopy(send, recv, 4 * a + 1 + p, src, g.at[me], (px, py, c)))
                    recvs.append(_rcopy(send, recv, 4 * a + 1 + p, src, g.at[2 * px + py], (px, py, c)))
                else:
                    mine = _half(s.shape[1], c)
                    sends.append(_rcopy(send, recv, 4 * a + 1 + p, s.at[layer, mine], g.at[me, mine], (px, py, c)))
                    recvs.append(_rcopy(send, recv, 4 * a + 1 + p, s.at[layer, mine], g.at[2 * px + py, mine], (px, py, c)))
        return sends, recvs

    sources = list(shards) + extras
    if into is None:
        out_shapes = [jax.ShapeDtypeStruct((N_CHIPS,) + s.shape[1:], s.dtype) for s in shards]
        out_shapes += [jax.ShapeDtypeStruct((N_CHIPS,) + e.shape, e.dtype) for e in extras]
        return _Comm(sources, out_shapes, {}, 4 * len(sources), copies)
    return _Comm(sources + list(into), [jax.ShapeDtypeStruct(g.shape, g.dtype) for g in into],
                 {len(sources) + a: a for a in range(len(into))}, 4 * len(sources), copies)


def _gather_d2d(gathered):
    n = len(gathered)

    def copies(cin, cout, send, recv):
        x, y, c, me, sibling, chips = _place()
        sends, recvs = [], []
        for a in range(n):
            g = cout[a]
            rows = g.shape[1]
            for p, (px, py) in enumerate(chips):
                mine, theirs = g.at[2 * px + py, _half(rows, c)], g.at[2 * px + py, _half(rows, 1 - c)]
                sends.append(_rcopy(send, recv, 3 * a + p, mine, mine, sibling))
                recvs.append(_rcopy(send, recv, 3 * a + p, theirs, theirs, sibling))
        return sends, recvs

    return _Comm(list(gathered), [jax.ShapeDtypeStruct(g.shape, g.dtype) for g in gathered],
                 {a: a for a in range(n)}, 3 * n, copies)


class _SemView:
    def __init__(self, ref, base):
        self.ref, self.base, self.at = ref, base, self

    def __getitem__(self, k):
        return self.ref.at[self.base + k]


def _both(a, b):
    ka, ma = len(a.inputs), len(a.out_shapes)

    def copies(cin, cout, send, recv):
        sa, ra = a.copies(cin[:ka], cout[:ma], send, recv)
        sb, rb = b.copies(cin[ka:], cout[ma:], _SemView(send, a.n_sems), _SemView(recv, a.n_sems))
        return sa + sb, ra + rb

    aliases = dict(a.aliases)
    aliases.update({ka + ci: ma + co for ci, co in b.aliases.items()})
    return _Comm(a.inputs + b.inputs, a.out_shapes + b.out_shapes, aliases, a.n_sems + b.n_sems, copies)


def _small_exchange(buf):
    def copies(cin, cout, send, recv):
        x, y, c, me, sibling, chips = _place()
        sends, recvs = [], []
        for r in range(1, 8):
            px, py, pc = x ^ ((r >> 2) & 1), y ^ ((r >> 1) & 1), c ^ (r & 1)
            sends.append(_rcopy(send, recv, r - 1, cin[0], cout[0].at[2 * me + c], (px, py, pc)))
            recvs.append(_rcopy(send, recv, r - 1, cin[0], cout[0].at[4 * px + 2 * py + pc], (px, py, pc)))
        return sends, recvs

    return _Comm([buf], [jax.ShapeDtypeStruct((8,) + buf.shape, buf.dtype)], {}, 7, copies)


def _small_sum(slots, buf, device_arr):
    rows, cols = buf.shape

    def body(d_ref, s_ref, b_ref, o_ref):
        d = pl.program_id(0)
        val = jnp.where(d == d_ref[0], b_ref[...], s_ref[...])

        @pl.when(d == 0)
        def _():
            o_ref[...] = val

        @pl.when(d > 0)
        def _():
            o_ref[...] += val

    return _pcall(
        body, name="small_sum", out_shape=jax.ShapeDtypeStruct((rows, cols), f32),
        grid_spec=pltpu.PrefetchScalarGridSpec(
            num_scalar_prefetch=1, grid=(8,),
            in_specs=[pl.BlockSpec((None, rows, cols), lambda d, d_ref: (jnp.where(d == d_ref[0], (d + 1) % 8, d), 0, 0)),
                      pl.BlockSpec((rows, cols), lambda d, d_ref: (0, 0))],
            out_specs=pl.BlockSpec((rows, cols), lambda d, d_ref: (0, 0))),
        compiler_params=_params("arbitrary"))(device_arr, slots, buf)


def _sibling_exchange(gs):
    def copies(cin, cout, send, recv):
        x, y, c, me, sibling, chips = _place()
        sends = [_rcopy(send, recv, a, g.at[:, _half(g.shape[1], 1 - c)], r, sibling) for a, (g, r) in enumerate(zip(cin, cout))]
        return sends, sends

    shapes = [jax.ShapeDtypeStruct((N_CHIPS, g.shape[1] // 2, g.shape[2]), g.dtype) for g in gs]
    return _Comm(list(gs), shapes, {}, len(gs), copies)


def _chip_scatter(ss):
    def copies(cin, cout, send, recv):
        x, y, c, me, sibling, chips = _place()
        sends = [_rcopy(send, recv, 3 * a + p, s.at[2 * px + py], r.at[p], (px, py, c))
                 for a, (s, r) in enumerate(zip(cin, cout)) for p, (px, py) in enumerate(chips)]
        return sends, sends

    shapes = [jax.ShapeDtypeStruct((3,) + s.shape[1:], s.dtype) for s in ss]
    return _Comm(list(ss), shapes, {}, 3 * len(ss), copies)


def _sibling_share(fs, layer):
    def copies(cin, cout, send, recv):
        x, y, c, me, sibling, chips = _place()
        sends, recvs = [], []
        for a, f in enumerate(cout):
            mine, theirs = f.at[layer, _half(f.shape[1], c)], f.at[layer, _half(f.shape[1], 1 - c)]
            sends.append(_rcopy(send, recv, a, mine, mine, sibling))
            recvs.append(_rcopy(send, recv, a, theirs, theirs, sibling))
        return sends, recvs

    return _Comm(list(fs), [jax.ShapeDtypeStruct(f.shape, f.dtype) for f in fs], {a: a for a in range(len(fs))},
                 len(fs), copies)


def _sum_rows(rows):
    return next(b for b in (256, 192, 128) if rows % b == 0)


def _pair_sum(g, r, core_arr, layer):
    _, rows, cols = r.shape
    br = _sum_rows(rows)
    nb = rows // br

    def body(c_ref, g_ref, r_ref, o_ref):
        o_ref[...] = (g_ref[...].astype(f32) + r_ref[...].astype(f32)).astype(bf16)

    return _pcall(
        body, name=f"pair_sum_{layer}", out_shape=jax.ShapeDtypeStruct(r.shape, bf16),
        grid_spec=pltpu.PrefetchScalarGridSpec(
            num_scalar_prefetch=1, grid=(N_CHIPS, nb),
            in_specs=[pl.BlockSpec((1, br, cols), lambda j, i, c_ref: (j, c_ref[0] * nb + i, 0)),
                      pl.BlockSpec((1, br, cols), lambda j, i, c_ref: (j, i, 0))],
            out_specs=pl.BlockSpec((1, br, cols), lambda j, i, c_ref: (j, i, 0))),
        compiler_params=_params("arbitrary", "arbitrary"))(core_arr, g, r)


def _chip_sum(s, r, place, layer, final):
    _, rows, cols = s.shape
    br = _sum_rows(rows)
    nb = rows // br

    def body(place_ref, s_ref, r_ref, *rest):
        o_ref = rest[-1]
        acc = s_ref[0].astype(f32)
        for p in range(3):
            acc = acc + r_ref[p].astype(f32)
        o_ref[...] = acc

    carried = [] if final is None else [final]
    return _pcall(
        body, name=f"chip_sum_{layer}", out_shape=jax.ShapeDtypeStruct((DEPTH, 2 * rows, cols), f32),
        grid_spec=pltpu.PrefetchScalarGridSpec(
            num_scalar_prefetch=1, grid=(nb,),
            in_specs=[pl.BlockSpec((1, br, cols), lambda i, place_ref: (place_ref[0], i, 0)),
                      pl.BlockSpec((3, br, cols), lambda i, place_ref: (0, i, 0))] + [ANY] * len(carried),
            out_specs=pl.BlockSpec((None, br, cols), lambda i, place_ref: (layer, place_ref[1] * nb + i, 0))),
        input_output_aliases={3: 0} if carried else {},
        compiler_params=_params("arbitrary"))(place, s, r, *carried)


def _inproj(xin, win, layer, comm=None, conv_w=None):
    T = xin.shape[0]
    tm = 512
    cast = xin.dtype != bf16
    conv = conv_w is not None

    def body(x_ref, w_ref, *rest):
        rest = list(rest)
        cw_ref = rest.pop(0) if conv else None
        o_ref = rest.pop(0)
        xt = x_ref[...].astype(bf16)
        if cast:
            rest.pop(0)[...] = xt
        for j in range(N_CHIPS):
            o_ref[:, j * W_IN_COLS:(j + 1) * W_IN_COLS] = jnp.dot(xt, w_ref[j], preferred_element_type=f32).astype(bf16)
        if conv:
            mix_ref, halo = rest

            @pl.when(pl.program_id(0) == 0)
            def _():
                halo[...] = jnp.zeros_like(halo)

            bg, cg, u = (o_ref[:, s * E_MIX:(s + 1) * E_MIX].astype(f32) for s in range(3))
            cu = cg * u
            out, _, _ = _conv_taps(cu, halo[6:7, :], halo[7:8, :], cw_ref)
            mix_ref[...] = (bg * out).astype(bf16)
            halo[...] = cu[tm - 8:, :]

    tile = pl.BlockSpec((tm, D_MODEL), lambda i: (i, 0))
    outs, carried = _pcall_carry(
        body, comm, n_in=2 + conv, n_out=1 + cast + conv, name=f"inproj_{layer}", grid=(T // tm,),
        out_shape=(jax.ShapeDtypeStruct((T, N_IN), bf16),) + (jax.ShapeDtypeStruct((T, D_MODEL), bf16),) * (cast + conv),
        in_specs=[tile, pl.BlockSpec((N_CHIPS, D_MODEL, W_IN_COLS), lambda i: (0, 0, 0), pipeline_mode=pl.Buffered(1))]
        + [pl.BlockSpec((CONV_W, E_MIX), lambda i: (0, 0))] * conv,
        out_specs=(pl.BlockSpec((tm, N_IN), lambda i: (i, 0)),) + (tile,) * (cast + conv),
        scratch_shapes=[pltpu.VMEM((8, E_MIX), f32)] * conv,
        compiler_params=_params("arbitrary"))(xin, win, *([conv_w] if conv else []))
    return outs[0], (outs[1] if cast else xin), (outs[-1] if conv else None), carried


def _rel_index_rows():
    j = lax.broadcasted_iota(jnp.int32, (N_REL_PAD, KG), 1)
    r = lax.broadcasted_iota(jnp.int32, (N_REL_PAD, KG), 0)
    off = jnp.where(j < KG - 2 * CHUNK, j, j - KG)
    idx = jnp.clip(N_PREV * CHUNK - off, -REL_CLIP, REL_CLIP) + REL_CLIP
    return (idx == r).astype(f32)


def _bias_expand(table_pad, layer, comm=None):
    def body(t_ref, o_ref, row_scr):
        h = pl.program_id(0)

        @pl.when(h == 0)
        def _():
            row_scr[...] = jnp.dot(t_ref[...], _rel_index_rows(), precision=lax.Precision.HIGHEST,
                                   preferred_element_type=f32)

        q = lax.broadcasted_iota(jnp.int32, (QG, KG), 0)
        k = lax.broadcasted_iota(jnp.int32, (QG, KG), 1)
        band = (k // CHUNK >= q // CHUNK) & (k // CHUNK <= q // CHUNK + N_PREV)
        t = jnp.broadcast_to(row_scr[pl.ds(h, 1), :], (QG, KG))
        for b in range(8):
            t = jnp.where(((q >> b) & 1) == 1, pltpu.roll(t, 1 << b, axis=1), t)
        for v in range(3):
            o_ref[v] = jnp.where(band & (k >= (2 - v) * QG), t, NEG)

    (bias,), carried = _pcall_carry(
        body, comm, n_in=1, n_out=1, name=f"bias_expand_{layer}", grid=(N_HEADS,),
        out_shape=(jax.ShapeDtypeStruct((3, N_HEADS, QG, KG), f32),),
        in_specs=[pl.BlockSpec((N_HEADS, N_REL_PAD), lambda h: (0, 0))],
        out_specs=(pl.BlockSpec((3, None, QG, KG), lambda h: (0, h, 0, 0)),),
        scratch_shapes=[pltpu.VMEM((N_HEADS, KG), f32)], compiler_params=_params("arbitrary"))(table_pad)
    return bias, carried


def _bias_reduce(dbias, layer):
    def body(d_ref, o_ref, row_scr):
        q = lax.broadcasted_iota(jnp.int32, (QG, DB_COLS), 0)
        k = lax.broadcasted_iota(jnp.int32, (QG, DB_COLS), 1)
        for h in range(N_HEADS):
            t = jnp.where(k > q, d_ref[h], 0.0)
            for b in range(8):
                t = jnp.where(((q >> b) & 1) == 1, pltpu.roll(t, DB_COLS - (1 << b), axis=1), t)
            row_scr[h:h + 1, :] = jnp.sum(t, axis=0, keepdims=True)
        r = lax.broadcasted_iota(jnp.int32, (N_REL_PAD, DB_COLS), 0)
        off = lax.broadcasted_iota(jnp.int32, (N_REL_PAD, DB_COLS), 1)
        own = (off >= 1) & (off < REL_CLIP + CHUNK)
        sel = jnp.where(own & (r == 2 * REL_CLIP - off), 1.0, 0.0) - jnp.where(own & (r == 2 * REL_CLIP), 1.0, 0.0)
        o_ref[...] = lax.dot_general(row_scr[...], sel, NT, precision=lax.Precision.HIGHEST, preferred_element_type=f32)

    return _pcall(body, name=f"bias_reduce_{layer}", out_shape=jax.ShapeDtypeStruct((N_HEADS, N_REL_PAD), f32),
                  scratch_shapes=[pltpu.VMEM((N_HEADS, DB_COLS), f32)],
                  compiler_params=pltpu.CompilerParams(vmem_limit_bytes=VMEM_LIMIT))(dbias)


FWD_PAIRS = 8
BWD_PAIRS = 4


def _key_specs(n_groups, npairs, slab):
    per_slab = E_MIX // (128 * npairs)
    return [pl.BlockSpec((QG, 128 * npairs), functools.partial(
        lambda hp, g, jj: (jnp.clip(g - 2 + jj, 0, n_groups - 1), slab * per_slab + hp), jj=jj)) for jj in range(3)]


def _bias_spec(npairs):
    return pl.BlockSpec((None, 2 * npairs, QG, KG), lambda hp, g: (jnp.minimum(g, 2), hp, 0, 0))


def _attn_fwd(h, bias, layer, comm=None):
    T = h.shape[0]
    n_groups = T // QG
    scale = 1.0 / math.sqrt(HEAD_DIM)

    def body(q_ref, k0, k1, k2, v0, v1, v2, b_ref, o_ref, lse_ref):
        lane = lax.broadcasted_iota(jnp.int32, (1, 128), 1)
        ones = jnp.ones((KG, 128), bf16)
        lse = jnp.zeros((QG, 128), f32)
        for pp in range(FWD_PAIRS):
            cs = slice(pp * 128, (pp + 1) * 128)
            q2 = q_ref[:, cs] * scale
            kc = jnp.concatenate([k0[:, cs], k1[:, cs], k2[:, cs]], axis=0)
            vc = jnp.concatenate([jnp.concatenate([v0[:, cs], v1[:, cs], v2[:, cs]], axis=0), ones], axis=1)
            outs = []
            for hh in range(2):
                qm = jnp.where(lane // HEAD_DIM == hh, q2, jnp.zeros_like(q2))
                s = lax.dot_general(qm, kc, NT, preferred_element_type=f32) + b_ref[2 * pp + hh]
                m = jnp.max(s, axis=1, keepdims=True)
                ol = jnp.dot(jnp.exp(s - m).astype(bf16), vc, preferred_element_type=f32)
                outs.append(ol[:, :128] / ol[:, 128:])
                lse = jnp.where(lane == 2 * pp + hh, m + jnp.log(ol[:, 128:]), lse)
            o_ref[:, cs] = jnp.where(lane // HEAD_DIM == 0, outs[0], outs[1]).astype(bf16)
        lse_ref[...] = lse

    (mix, lse), carried = _pcall_carry(
        body, comm, n_in=8, n_out=2, name=f"attn_fwd_{layer}", grid=(N_HEADS // (2 * FWD_PAIRS), n_groups),
        out_shape=(jax.ShapeDtypeStruct((T, E_MIX), bf16), jax.ShapeDtypeStruct((T, 128), f32)),
        in_specs=[pl.BlockSpec((QG, 128 * FWD_PAIRS), lambda hp, g: (g, hp))] + _key_specs(n_groups, FWD_PAIRS, 1)
        + _key_specs(n_groups, FWD_PAIRS, 2) + [_bias_spec(FWD_PAIRS)],
        out_specs=(pl.BlockSpec((QG, 128 * FWD_PAIRS), lambda hp, g: (g, hp)), pl.BlockSpec((QG, 128), lambda hp, g: (g, 0))),
        compiler_params=_params("arbitrary", "arbitrary"))(h, h, h, h, h, h, h, bias)
    return mix, lse, carried


def _halo_rows(ref, r):
    return ref[r:r + 1, :].astype(f32)


def _conv_taps(cu, p6, p7, w_ref):
    row = lax.broadcasted_iota(jnp.int32, cu.shape, 0)
    r1 = jnp.where(row == 0, p7, pltpu.roll(cu, 1, axis=0))
    r2 = jnp.where(row == 0, p6, jnp.where(row == 1, p7, pltpu.roll(cu, 2, axis=0)))
    return w_ref[2:3, :] * cu + w_ref[1:2, :] * r1 + w_ref[0:1, :] * r2, r1, r2


def _kv_mem(memb, wkv):
    def body(m_ref, w_ref, o_ref):
        o_ref[...] = jnp.dot(m_ref[...], w_ref[...], preferred_element_type=f32).astype(bf16)

    return _pcall(body, name="kv_mem", out_shape=jax.ShapeDtypeStruct((N_MEM, 2 * E_MEM), bf16),
                  compiler_params=pltpu.CompilerParams(vmem_limit_bytes=VMEM_LIMIT))(memb, wkv)


def _mem_probs(qm_ref, kv_ref, hh):
    qh = qm_ref[:, hh * MEM_HEAD_DIM:(hh + 1) * MEM_HEAD_DIM]
    kh = kv_ref[:, hh * MEM_HEAD_DIM:(hh + 1) * MEM_HEAD_DIM]
    vh = kv_ref[:, E_MEM + hh * MEM_HEAD_DIM:E_MEM + (hh + 1) * MEM_HEAD_DIM]
    s = lax.dot_general(qh, kh, NT, preferred_element_type=f32) * (1.0 / math.sqrt(MEM_HEAD_DIM))
    e = jnp.exp(s - jnp.max(s, axis=1, keepdims=True))
    return e / jnp.sum(e, axis=1, keepdims=True), qh, kh, vh


def _h_tail_specs(tm):
    return [pl.BlockSpec((tm, E_MEM), functools.partial(lambda i, cb: (i, cb), cb=cb)) for cb in (6, 7, 8, 9)]


def _ln_bwd(dy, xhat, rstd, g, dob_ref, dxp_ref, dg_ref, db_ref):
    dg_ref[0:1, :] += jnp.sum(dy * xhat, axis=0, keepdims=True)
    db_ref[0:1, :] += jnp.sum(dy, axis=0, keepdims=True)
    gx = dy * g
    dr = rstd * (gx - jnp.mean(gx, axis=1, keepdims=True) - xhat * jnp.mean(gx * xhat, axis=1, keepdims=True))
    dxp_ref[...] = DN_ALPHA * dr
    dob_ref[...] = dr.astype(bf16)


def _post_fwd(h, mix, kv, wout, x, g, b, layer, target=None, comm=None):
    T = x.shape[0]
    tm = 512

    def body(qm_ref, z0, z1, z2, mix_ref, kv_ref, w_ref, x_ref, g_ref, b_ref, *rest):
        *rest, ps_ref, ms_ref = rest
        mems = []
        for hh in range(MEM_HEADS):
            p, _, _, vh = _mem_probs(qm_ref, kv_ref, hh)
            pb = p.astype(bf16)
            ps_ref[:, hh * N_MEM:(hh + 1) * N_MEM] = pb
            mems.append(jnp.dot(pb, vh, preferred_element_type=f32))
        memb = jnp.concatenate(mems, axis=1).astype(bf16)
        ms_ref[...] = memb
        z = jnp.concatenate([z0[...], z1[...], z2[...]], axis=1)
        one = jnp.ones((), bf16)
        y = jnp.concatenate([mix_ref[...], memb], axis=1) * (z * (one / (one + jnp.exp(-z))))
        out = jnp.dot(y, w_ref[...], preferred_element_type=f32)
        r = DN_ALPHA * x_ref[...] + out
        mu = jnp.mean(r, axis=1, keepdims=True)
        var = jnp.mean(jnp.square(r - mu), axis=1, keepdims=True)
        rstd = lax.rsqrt(var + LN_EPS)
        xhat = (r - mu) * rstd
        xn = xhat * g_ref[...] + b_ref[...]
        if target is None:
            xn_ref, xb_ref, xh_ref, rs_ref = rest
            xn_ref[...] = xn
            xb_ref[...] = xn.astype(bf16)
            xh_ref[...] = xhat
            rs_ref[...] = rstd
        else:
            t_ref, l_ref, dob_ref, dxp_ref, dg_ref, db_ref = rest

            @pl.when(pl.program_id(0) == 0)
            def _():
                l_ref[...] = jnp.zeros_like(l_ref)
                dg_ref[...] = jnp.zeros_like(dg_ref)
                db_ref[...] = jnp.zeros_like(db_ref)

            err = xn - t_ref[...]
            l_ref[...] += jnp.sum(jnp.square(err))
            _ln_bwd(err * (1.0 / D_MODEL), xhat, rstd, g_ref[...], dob_ref, dxp_ref, dg_ref, db_ref)

    tile = lambda w: pl.BlockSpec((tm, w), lambda i: (i, 0))
    const = lambda r, c: pl.BlockSpec((r, c), lambda i: (0, 0))
    resident = lambda r, c: pl.BlockSpec((r, c), lambda i: (0, 0), pipeline_mode=pl.Buffered(1))
    in_specs = _h_tail_specs(tm) + [tile(E_MIX), resident(N_MEM, 2 * E_MEM), resident(E_BRANCH, D_MODEL), tile(D_MODEL),
                                    const(1, D_MODEL), const(1, D_MODEL)]
    kept = (jax.ShapeDtypeStruct((T, MEM_HEADS * N_MEM), bf16), jax.ShapeDtypeStruct((T, E_MEM), bf16))
    kept_specs = (tile(MEM_HEADS * N_MEM), tile(E_MEM))
    if target is None:
        return _pcall_carry(
            body, comm, n_in=10, n_out=6, name=f"post_fwd_{layer}", grid=(T // tm,),
            out_shape=(jax.ShapeDtypeStruct((T, D_MODEL), f32), jax.ShapeDtypeStruct((T, D_MODEL), bf16),
                       jax.ShapeDtypeStruct((T, D_MODEL), f32), jax.ShapeDtypeStruct((T, 1), f32)) + kept,
            in_specs=in_specs, out_specs=(tile(D_MODEL), tile(D_MODEL), tile(D_MODEL), tile(1)) + kept_specs,
            compiler_params=_params("arbitrary"))(h, h, h, h, mix, kv, wout, x, g, b)
    return _pcall(
        body, name=f"post_fwd_loss_{layer}", grid=(T // tm,),
        out_shape=(jax.ShapeDtypeStruct((8, 128), f32), jax.ShapeDtypeStruct((T, D_MODEL), bf16),
                   jax.ShapeDtypeStruct((T, D_MODEL), f32), jax.ShapeDtypeStruct((8, D_MODEL), f32),
                   jax.ShapeDtypeStruct((8, D_MODEL), f32)) + kept,
        in_specs=in_specs + [tile(D_MODEL)],
        out_specs=(const(8, 128), tile(D_MODEL), tile(D_MODEL), const(8, D_MODEL), const(8, D_MODEL)) + kept_specs,
        compiler_params=_params("arbitrary"))(h, h, h, h, mix, kv, wout, x, g, b, target)


def _post_bwd(dob, h, mix, kv, wout, kept, layer, comm=None):
    T = dob.shape[0]
    tm = 512
    n_tiles = T // tm
    inv = 1.0 / math.sqrt(MEM_HEAD_DIM)

    def body(dob_ref, qm_ref, z0, z1, z2, mix_ref, kv_ref, w_ref, ps_ref, ms_ref,
             dhb_ref, dmix_ref, dkv_ref, dwo_out, dwo_ref):
        @pl.when(pl.program_id(0) == 0)
        def _():
            dkv_ref[...] = jnp.zeros_like(dkv_ref)
            dwo_ref[...] = jnp.zeros_like(dwo_ref)

        dob = dob_ref[...]
        z = jnp.concatenate([z0[...], z1[...], z2[...]], axis=1).astype(f32)
        act, sig = _silu_parts(z)
        cat = jnp.concatenate([mix_ref[...].astype(f32), ms_ref[...].astype(f32)], axis=1)
        yb = (cat * act).astype(bf16)
        dwo_ref[...] += lax.dot_general(yb, dob, TN, preferred_element_type=f32)
        dyv = lax.dot_general(dob, w_ref[...], NT, preferred_element_type=f32)
        dz = dyv * cat * (sig * (1.0 + z * (1.0 - sig)))
        dcat = dyv * act
        dmix_ref[...] = dcat[:, :E_MIX].astype(bf16)
        dqs = []
        for hh in range(MEM_HEADS):
            cols = slice(hh * MEM_HEAD_DIM, (hh + 1) * MEM_HEAD_DIM)
            qh, kh, vh = qm_ref[:, cols], kv_ref[:, cols], kv_ref[:, E_MEM + hh * MEM_HEAD_DIM:E_MEM + (hh + 1) * MEM_HEAD_DIM]
            pb = ps_ref[:, hh * N_MEM:(hh + 1) * N_MEM]
            p = pb.astype(f32)
            dmem = dcat[:, E_MIX + hh * MEM_HEAD_DIM:E_MIX + (hh + 1) * MEM_HEAD_DIM].astype(bf16)
            dp = lax.dot_general(dmem, vh, NT, preferred_element_type=f32)
            ds = (p * (dp - jnp.sum(p * dp, axis=1, keepdims=True))).astype(bf16)
            dqs.append(jnp.dot(ds, kh, preferred_element_type=f32) * inv)
            dkv_ref[:, cols] += lax.dot_general(ds, qh, TN, preferred_element_type=f32) * inv
            dkv_ref[:, E_MEM + hh * MEM_HEAD_DIM:E_MEM + (hh + 1) * MEM_HEAD_DIM] += lax.dot_general(
                pb, dmem, TN, preferred_element_type=f32)
        dhb_ref[...] = jnp.concatenate(dqs + [dz], axis=1).astype(bf16)

        @pl.when(pl.program_id(0) == n_tiles - 1)
        def _():
            dwo_out[...] = dwo_ref[...].astype(bf16)

    tile = lambda w: pl.BlockSpec((tm, w), lambda i: (i, 0))
    const = lambda r, c: pl.BlockSpec((r, c), lambda i: (0, 0))
    resident = lambda r, c: pl.BlockSpec((r, c), lambda i: (0, 0), pipeline_mode=pl.Buffered(1))
    return _pcall_carry(
        body, comm, n_in=10, n_out=4, name=f"post_bwd_{layer}", grid=(n_tiles,),
        out_shape=(jax.ShapeDtypeStruct((T, E_MEM + E_BRANCH), bf16), jax.ShapeDtypeStruct((T, E_MIX), bf16),
                   jax.ShapeDtypeStruct((N_MEM, 2 * E_MEM), f32), jax.ShapeDtypeStruct((E_BRANCH, D_MODEL), bf16)),
        in_specs=[tile(D_MODEL)] + _h_tail_specs(tm) + [tile(E_MIX), resident(N_MEM, 2 * E_MEM), resident(E_BRANCH, D_MODEL),
                                                        tile(MEM_HEADS * N_MEM), tile(E_MEM)],
        out_specs=(tile(E_MEM + E_BRANCH), tile(E_MIX), const(N_MEM, 2 * E_MEM), const(E_BRANCH, D_MODEL)),
        scratch_shapes=[pltpu.VMEM((E_BRANCH, D_MODEL), f32)],
        compiler_params=_params("arbitrary"))(dob, h, h, h, h, mix, kv, wout, *kept)


def _attn_bwd(h, bias, dmix, lse, layer, comm=None):
    T = h.shape[0]
    n_groups = T // QG
    scale = 1.0 / math.sqrt(HEAD_DIM)

    def body(q_ref, k0, k1, k2, v0, v1, v2, do_ref, b_ref, lse_ref, dq_ref, dk_ref, dv_ref, db_ref, acck, accv):
        g = pl.program_id(1)

        @pl.when(g == 0)
        def _():
            acck[...] = jnp.zeros_like(acck)
            accv[...] = jnp.zeros_like(accv)
            db_ref[...] = jnp.zeros_like(db_ref)

        @pl.when(g < n_groups)
        def _():
            lane = lax.broadcasted_iota(jnp.int32, (1, 128), 1)
            first = lane // HEAD_DIM == 0
            for pp in range(BWD_PAIRS):
                cs = slice(pp * 128, (pp + 1) * 128)
                do2 = do_ref[:, cs]
                q2 = q_ref[:, cs] * scale
                kc = jnp.concatenate([k0[:, cs], k1[:, cs], k2[:, cs]], axis=0)
                vc = jnp.concatenate([v0[:, cs], v1[:, cs], v2[:, cs]], axis=0)
                q2t, do2t = q2.T, do2.T
                dqs, dks, dvs = [], [], []
                for hh in range(2):
                    hm = lane // HEAD_DIM == hh
                    head = (pl.program_id(0) * BWD_PAIRS + pp) * 2 + hh
                    lse = jnp.sum(jnp.where(lane == head, lse_ref[...], 0.0), axis=1, keepdims=True)
                    qm = jnp.where(hm, q2, jnp.zeros_like(q2))
                    dom = jnp.where(hm, do2, jnp.zeros_like(do2))
                    s = lax.dot_general(qm, kc, NT, preferred_element_type=f32) + b_ref[2 * pp + hh]
                    p = jnp.exp(s - lse)
                    dp = lax.dot_general(dom, vc, NT, preferred_element_type=f32)
                    ds = p * (dp - jnp.sum(p * dp, axis=1, keepdims=True))
                    db_ref[2 * pp + hh] += ds[:, KG - DB_COLS:]
                    dsb, pb = ds.astype(bf16), p.astype(bf16)
                    dqs.append(jnp.dot(dsb, kc, preferred_element_type=f32) * scale)
                    dks.append(jnp.dot(q2t[hh * HEAD_DIM:(hh + 1) * HEAD_DIM], dsb, preferred_element_type=f32))
                    dvs.append(jnp.dot(do2t[hh * HEAD_DIM:(hh + 1) * HEAD_DIM], pb, preferred_element_type=f32))
                dq_ref[:, cs] = jnp.where(first, dqs[0], dqs[1]).astype(bf16)
                dkc = jnp.concatenate(dks, axis=0).T
                dvc = jnp.concatenate(dvs, axis=0).T
                for jj in range(3):
                    slot = (g + 1 + jj) % 3
                    if jj == 2:
                        acck[slot, :, cs] = dkc[jj * QG:(jj + 1) * QG]
                        accv[slot, :, cs] = dvc[jj * QG:(jj + 1) * QG]
                    else:
                        acck[slot, :, cs] += dkc[jj * QG:(jj + 1) * QG]
                        accv[slot, :, cs] += dvc[jj * QG:(jj + 1) * QG]

        done = (g + 1) % 3
        dk_ref[...] = acck[done].astype(bf16)
        dv_ref[...] = accv[done].astype(bf16)

    last = n_groups - 1
    width = 128 * BWD_PAIRS
    qspec = pl.BlockSpec((QG, width), lambda hp, g: (jnp.minimum(g, last), hp))
    kout = pl.BlockSpec((QG, width), lambda hp, g: (jnp.clip(g - 2, 0, last), hp))
    dbspec = pl.BlockSpec((2 * BWD_PAIRS, QG, DB_COLS), lambda hp, g: (hp, 0, 0))
    lspec = pl.BlockSpec((QG, 128), lambda hp, g: (jnp.minimum(g, last), 0))
    return _pcall_carry(
        body, comm, n_in=10, n_out=4, name=f"attn_bwd_{layer}", grid=(N_HEADS // (2 * BWD_PAIRS), n_groups + 2),
        out_shape=(jax.ShapeDtypeStruct((T, E_MIX), bf16),) * 3 + (jax.ShapeDtypeStruct((N_HEADS, QG, DB_COLS), f32),),
        in_specs=[qspec] + _key_specs(n_groups, BWD_PAIRS, 1) + _key_specs(n_groups, BWD_PAIRS, 2)
        + [qspec, _bias_spec(BWD_PAIRS), lspec],
        out_specs=(qspec, kout, kout, dbspec),
        scratch_shapes=[pltpu.VMEM((3, QG, width), f32), pltpu.VMEM((3, QG, width), f32)],
        compiler_params=_params("arbitrary", "arbitrary"))(h, h, h, h, h, h, h, dmix, bias, lse)


def _conv_bwd(h, w, dmix, layer):
    T = h.shape[0]
    tm = 512
    n_tiles = T // tm

    def body(bg_ref, cg_ref, u_ref, cgp_ref, up_ref, dy_ref, bgn_ref, dyn_ref, w_ref, dbg_ref, dcg_ref, du_ref, dw_ref):
        i = pl.program_id(0)

        @pl.when(i == 0)
        def _():
            dw_ref[...] = jnp.zeros_like(dw_ref)

        first = (i == 0).astype(f32)
        final = (i == n_tiles - 1).astype(f32)
        bg, cg, u = bg_ref[...].astype(f32), cg_ref[...].astype(f32), u_ref[...].astype(f32)
        dy = dy_ref[...].astype(f32)
        cu = cg * u
        p6 = _halo_rows(cgp_ref, 14) * _halo_rows(up_ref, 14) * (1.0 - first)
        p7 = _halo_rows(cgp_ref, 15) * _halo_rows(up_ref, 15) * (1.0 - first)
        conv, r1, r2 = _conv_taps(cu, p6, p7, w_ref)
        dbg_ref[...] = (dy * conv).astype(bf16)
        dc = dy * bg
        n0 = _halo_rows(dyn_ref, 0) * _halo_rows(bgn_ref, 0) * (1.0 - final)
        n1 = _halo_rows(dyn_ref, 1) * _halo_rows(bgn_ref, 1) * (1.0 - final)
        row = lax.broadcasted_iota(jnp.int32, dc.shape, 0)
        f1 = jnp.where(row == tm - 1, n0, pltpu.roll(dc, tm - 1, axis=0))
        f2 = jnp.where(row == tm - 2, n0, jnp.where(row == tm - 1, n1, pltpu.roll(dc, tm - 2, axis=0)))
        dcu = w_ref[2:3, :] * dc + w_ref[1:2, :] * f1 + w_ref[0:1, :] * f2
        dcg_ref[...] = (dcu * u).astype(bf16)
        du_ref[...] = (dcu * cg).astype(bf16)
        dw_ref[0:1, :] += jnp.sum(dc * r2, axis=0, keepdims=True)
        dw_ref[1:2, :] += jnp.sum(dc * r1, axis=0, keepdims=True)
        dw_ref[2:3, :] += jnp.sum(dc * cu, axis=0, keepdims=True)

    tile = lambda slab: pl.BlockSpec((tm, E_MIX), lambda i: (i, slab))
    prev = lambda slab: pl.BlockSpec((16, E_MIX), lambda i: (jnp.maximum(i * (tm // 16) - 1, 0), slab))
    nxt = lambda slab: pl.BlockSpec((16, E_MIX), lambda i: (jnp.minimum((i + 1) * (tm // 16), T // 16 - 1), slab))
    return _pcall(
        body, name=f"conv_bwd_{layer}", grid=(n_tiles,),
        out_shape=(jax.ShapeDtypeStruct((T, E_MIX), bf16),) * 3 + (jax.ShapeDtypeStruct((8, E_MIX), f32),),
        in_specs=[tile(0), tile(1), tile(2), prev(1), prev(2), tile(0), nxt(0), nxt(0),
                  pl.BlockSpec((CONV_W, E_MIX), lambda i: (0, 0))],
        out_specs=(tile(0), tile(0), tile(0), pl.BlockSpec((8, E_MIX), lambda i: (0, 0))),
        compiler_params=_params("arbitrary"))(h, h, h, h, h, dmix, h, dmix, w)


def _inproj_bwd_dx(da, db, dc, dhb, dxp, win, layer, below=None, comm=None):
    T = dxp.shape[0]
    tm = 512

    def body(da_ref, db_ref, dc_ref, dhb_ref, dxp_ref, w_ref, *rest):
        dh = jnp.concatenate([da_ref[...], db_ref[...], dc_ref[...], dhb_ref[...]], axis=1)
        acc = dxp_ref[...]
        for j in range(N_CHIPS):
            acc = acc + lax.dot_general(dh[:, j * W_IN_COLS:(j + 1) * W_IN_COLS], w_ref[j], NT, preferred_element_type=f32)
        if below is None:
            rest[0][...] = acc
        else:
            xh_ref, rs_ref, g_ref, dob_ref, dxo_ref, dg_ref, db_out = rest

            @pl.when(pl.program_id(0) == 0)
            def _():
                dg_ref[...] = jnp.zeros_like(dg_ref)
                db_out[...] = jnp.zeros_like(db_out)

            _ln_bwd(acc, xh_ref[...], rs_ref[...], g_ref[...], dob_ref, dxo_ref, dg_ref, db_out)

    tile = lambda w: pl.BlockSpec((tm, w), lambda i: (i, 0))
    const = lambda r, c: pl.BlockSpec((r, c), lambda i: (0, 0))
    in_specs = [tile(E_MIX), tile(E_MIX), tile(E_MIX), tile(E_MEM + E_BRANCH), tile(D_MODEL),
                pl.BlockSpec((N_CHIPS, D_MODEL, W_IN_COLS), lambda i: (0, 0, 0), pipeline_mode=pl.Buffered(1))]
    if below is None:
        return _pcall_carry(
            body, comm, n_in=6, n_out=1, name=f"inproj_bwd_dx_{layer}", grid=(T // tm,),
            out_shape=(jax.ShapeDtypeStruct((T, D_MODEL), f32),), in_specs=in_specs, out_specs=(tile(D_MODEL),),
            compiler_params=_params("arbitrary"))(da, db, dc, dhb, dxp, win)
    return _pcall_carry(
        body, comm, n_in=9, n_out=4, name=f"inproj_bwd_dx_{layer}", grid=(T // tm,),
        out_shape=(jax.ShapeDtypeStruct((T, D_MODEL), bf16), jax.ShapeDtypeStruct((T, D_MODEL), f32),
                   jax.ShapeDtypeStruct((8, D_MODEL), f32), jax.ShapeDtypeStruct((8, D_MODEL), f32)),
        in_specs=in_specs + [tile(D_MODEL), tile(1), const(1, D_MODEL)],
        out_specs=(tile(D_MODEL), tile(D_MODEL), const(8, D_MODEL), const(8, D_MODEL)),
        compiler_params=_params("arbitrary"))(da, db, dc, dhb, dxp, win, *below)


def _dh_pieces():
    pieces, col = [], 0
    for src, width in enumerate((E_MIX, E_MIX, E_MIX, E_MEM + E_BRANCH)):
        lo = 0
        while lo < width:
            j, c0 = divmod(col + lo, W_IN_COLS)
            n = min(width - lo, W_IN_COLS - c0)
            pieces.append((src, lo, lo + n, j, c0, c0 + n))
            lo += n
        col += width
    return pieces


def _inproj_bwd_dw(da, db, dc, dhb, xb, layer, comm=None):
    T = xb.shape[0]
    tm = 1024 if T % 1024 == 0 else 512
    n_tiles = T // tm

    def body(da_ref, db_ref, dc_ref, dhb_ref, x_ref, o_ref, acc, stage, sem):
        i = pl.program_id(0)

        @pl.when(i == 0)
        def _():
            acc[...] = jnp.zeros_like(acc)

        srcs = (da_ref, db_ref, dc_ref, dhb_ref)
        xt = x_ref[...]
        for s, s0, s1, j, c0, c1 in _dh_pieces():
            acc[j, :, c0:c1] += lax.dot_general(xt, srcs[s][:, s0:s1], TN, preferred_element_type=f32)

        @pl.when(i == n_tiles - 1)
        def _():
            for j in range(N_CHIPS):
                stage[...] = acc[j].astype(bf16)
                cp = pltpu.make_async_copy(stage, o_ref.at[j], sem)
                cp.start()
                cp.wait()

    tile = lambda w: pl.BlockSpec((tm, w), lambda i: (i, 0))
    (dw,), carried = _pcall_carry(
        body, comm, n_in=5, n_out=1, name=f"inproj_bwd_dw_{layer}", grid=(n_tiles,),
        out_shape=(jax.ShapeDtypeStruct((N_CHIPS, D_MODEL, W_IN_COLS), bf16),),
        in_specs=[tile(E_MIX), tile(E_MIX), tile(E_MIX), tile(E_MEM + E_BRANCH), tile(D_MODEL)],
        out_specs=(ANY,),
        scratch_shapes=[pltpu.VMEM((N_CHIPS, D_MODEL, W_IN_COLS), f32), pltpu.VMEM((D_MODEL, W_IN_COLS), bf16),
                        pltpu.SemaphoreType.DMA],
        compiler_params=_params("arbitrary"))(da, db, dc, dhb, xb)
    return dw, carried


def _kv_mem_bwd(memb, dkv, layer):
    def body(m_ref, d_ref, o_ref):
        o_ref[...] = lax.dot_general(m_ref[...], d_ref[...].astype(bf16), TN, preferred_element_type=f32).astype(bf16)

    return _pcall(body, name=f"kv_mem_bwd_{layer}", out_shape=jax.ShapeDtypeStruct((D_MODEL, 2 * E_MEM), bf16),
                  compiler_params=pltpu.CompilerParams(vmem_limit_bytes=VMEM_LIMIT))(memb, dkv)


def _adamw(w, g, m, v, name):
    shape = w.shape
    cols = shape[-1]
    rows = w.size // cols
    args = [a.reshape(rows, cols) for a in (w, g, m, v)]
    br = 256 if rows % 256 == 0 and rows > 256 else rows

    def body(w_ref, g_ref, m_ref, v_ref, go_ref, d_ref, nm_ref, nv_ref):
        gg = g_ref[...]
        nm = ADAM_B1 * m_ref[...] + (1.0 - ADAM_B1) * gg
        nv = ADAM_B2 * v_ref[...] + (1.0 - ADAM_B2) * jnp.square(gg)
        m_hat = nm / (1.0 - ADAM_B1 ** ADAM_STEP)
        v_hat = nv / (1.0 - ADAM_B2 ** ADAM_STEP)
        go_ref[...] = gg
        d_ref[...] = -ADAM_LR * (m_hat / (jnp.sqrt(v_hat) + ADAM_EPS) + ADAM_WD * w_ref[...])
        nm_ref[...] = nm
        nv_ref[...] = nv

    spec = pl.BlockSpec((br, cols), lambda i: (i, 0))
    outs = _pcall(body, name=name, grid=(rows // br,), out_shape=(jax.ShapeDtypeStruct((rows, cols), f32),) * 4,
                  in_specs=[spec] * 4, out_specs=(spec,) * 4, compiler_params=_params("arbitrary"))(*args)
    return tuple(o.reshape(shape) for o in outs)


def kernel(x, mem, w_in, w_mem_kv, w_out, rel_bias, conv_w, ln_g, ln_b, loss_target, m_w_in, m_w_mem_kv, m_w_out, m_rel_bias, m_conv_w, m_ln_g, m_ln_b, v_w_in, v_w_mem_kv, v_w_out, v_rel_bias, v_conv_w, v_ln_g, v_ln_b):
    T = x.shape[1]
    x0 = x.reshape(T, D_MODEL)
    target = loss_target.reshape(T, D_MODEL)
    memb = mem.reshape(N_MEM, D_MODEL).astype(bf16)
    chip = 2 * lax.axis_index("x") + lax.axis_index("y")
    core = lax.axis_index("c")
    chip_arr = jnp.reshape(chip, (1,)).astype(jnp.int32)
    core_arr = jnp.reshape(core, (1,)).astype(jnp.int32)

    place = jnp.concatenate([chip_arr, core_arr])
    tables = jnp.pad(rel_bias, ((0, 0), (0, 0), (0, N_REL_PAD - N_REL)))

    shards = [w_in.astype(bf16), w_mem_kv.astype(bf16), w_out.astype(bf16)]
    biases = {}
    biases[0], near = _bias_expand(tables[0], 0, _gather_ici(shards, 0, conv_w, peers=(0, 1)))
    biases[2], (*arrived, cw_g) = _bias_expand(tables[1], 2, _gather_ici(shards, 0, conv_w, peers=(2,), own=False, into=near))
    gathered = {0: _comm_call(_gather_d2d(arrived), "gather_d2d_0")}
    conv_full = jnp.transpose(cw_g, (1, 2, 0, 3)).reshape(DEPTH // 2, CONV_W, E_MIX)

    xs, xbs, hs, mixes, kvs, kepts, xhats, rstds, lses = [x0], [x0], [], [], [], [], [], [], {}
    for layer in range(DEPTH):
        win, wkv, wout = gathered[layer]
        more = layer + 1 < DEPTH
        attention = layer % 2 == 0
        h, xbs[layer], mix, arrived = _inproj(xbs[layer], win, layer, _gather_ici(shards, layer + 1) if more else None,
                                              None if attention else conv_full[layer // 2])
        passing = _gather_d2d(list(arrived)) if more else None
        if attention:
            mix, lses[layer], done = _attn_fwd(h, biases[layer], layer, passing)
        kv = _kv_mem(memb, wkv.reshape(D_MODEL, 2 * E_MEM))
        result = _post_fwd(h, mix, kv, wout.reshape(E_BRANCH, D_MODEL), xs[layer], ln_g[layer][None, :],
                           ln_b[layer][None, :], layer, None if more else target, None if attention else passing)
        if more:
            (xn, xnb, xhat, rstd, *kept), done = (result[0], done) if attention else result
            xs.append(xn); xbs.append(xnb); xhats.append(xhat); rstds.append(rstd)
            gathered[layer + 1] = list(done)
        else:
            lsum, dob, dxp, dg_last, db_last, *kept = result
        hs.append(h); mixes.append(mix); kvs.append(kv); kepts.append(kept)

    dgs, dbs, dconvs, dtables = [None] * DEPTH, [None] * DEPTH, [None] * (DEPTH // 2), [None] * ((DEPTH + 1) // 2)
    dgs[DEPTH - 1], dbs[DEPTH - 1] = dg_last, db_last
    loss = lax.psum(lsum[0, 0], ("x", "y", "c")) * (0.5 / D_MODEL)
    finals = [None, None, None]
    above = None
    for layer in reversed(range(DEPTH)):
        h = hs[layer]
        win, wkv, wout = gathered[layer]
        (dhb, dmix, dkv, dwo), from_sibling = _post_bwd(
            dob, h, mixes[layer], kvs[layer], wout.reshape(E_BRANCH, D_MODEL), kepts[layer], layer,
            _sibling_exchange(above) if above else None)
        sums = [_pair_sum(g, r, core_arr, layer + 1) for g, r in zip(above, from_sibling)] if above else None
        scatter = _chip_scatter(sums) if above else None
        if layer % 2 == 0:
            (da, db, dc, dbias), from_chips = _attn_bwd(h, biases[layer], dmix, lses[layer], layer, scatter)
            dtables[layer // 2] = _bias_reduce(dbias, layer)
        else:
            da, db, dc, dconvs[layer // 2] = _conv_bwd(h, conv_full[layer // 2], dmix, layer)
        if layer > 0:
            (dob_below, dxp_below, dgs[layer - 1], dbs[layer - 1]), landed = _inproj_bwd_dx(
                da, db, dc, dhb, dxp, win, layer, (xhats[layer - 1], rstds[layer - 1], ln_g[layer - 1][None, :]),
                scatter if layer % 2 == 1 else None)
            from_chips = landed if layer % 2 == 1 else from_chips
        share = None
        if above:
            finals = [_chip_sum(s, r, place, layer + 1, f) for s, r, f in zip(sums, from_chips, finals)]
            share = _sibling_share(finals, layer + 1)
        if layer == 0:
            pad8 = lambda a: jnp.pad(a, ((0, 8 - a.shape[0]), (0, 0)))
            small_mine = jnp.concatenate(dgs + dbs + dconvs + [pad8(t.reshape(-1, D_MODEL)) for t in dtables], axis=0)
            share = _both(share, _small_exchange(small_mine))
        g_win, shared = _inproj_bwd_dw(da, db, dc, dhb, xbs[layer], layer, share)
        if layer == 0:
            *shared, small_slots = shared
        finals = list(shared) if above else finals
        above = [g_win, _kv_mem_bwd(memb, dkv, layer).reshape(N_CHIPS, W_KV_ROWS, 2 * E_MEM),
                 dwo.reshape(N_CHIPS, W_OUT_ROWS, D_MODEL)]
        if layer > 0:
            dob, dxp = dob_below, dxp_below
    from_sibling = _comm_call(_sibling_exchange(above), "sibling_exchange_0")
    sums = [_pair_sum(g, r, core_arr, 0) for g, r in zip(above, from_sibling)]
    (dx,), from_chips = _inproj_bwd_dx(da, db, dc, dhb, dxp, win, 0, None, _chip_scatter(sums))
    finals = [_chip_sum(s, r, place, 0, f) for s, r, f in zip(sums, from_chips, finals)]
    grad_w_in, grad_w_mem_kv, grad_w_out = _comm_call(_sibling_share(finals, 0), "sibling_share_0")
    grad_x = dx.reshape(1, T, D_MODEL)

    device_arr = jnp.reshape(2 * chip + core, (1,)).astype(jnp.int32)
    small = _small_sum(small_slots, small_mine, device_arr)
    grad_ln_g = jnp.stack([small[8 * l] for l in range(DEPTH)])
    grad_ln_b = jnp.stack([small[8 * (DEPTH + l)] for l in range(DEPTH)])
    conv_all = jnp.stack([small[8 * (2 * DEPTH + a):8 * (2 * DEPTH + a) + CONV_W] for a in range(DEPTH // 2)])
    grad_conv_w = lax.dynamic_slice_in_dim(conv_all, chip * (E_MIX // N_CHIPS), E_MIX // N_CHIPS, axis=2)
    t0 = 8 * (2 * DEPTH + DEPTH // 2)
    grad_rel_bias = jnp.stack([small[t0 + 8 * a:t0 + 8 * a + 6].reshape(N_HEADS, N_REL_PAD)[:, :N_REL]
                               for a in range((DEPTH + 1) // 2)])

    grads = [grad_w_in, grad_w_mem_kv, grad_w_out, grad_rel_bias, grad_conv_w, grad_ln_g, grad_ln_b]
    weights = [w_in, w_mem_kv, w_out, rel_bias, conv_w, ln_g, ln_b]
    moms = [m_w_in, m_w_mem_kv, m_w_out, m_rel_bias, m_conv_w, m_ln_g, m_ln_b]
    vels = [v_w_in, v_w_mem_kv, v_w_out, v_rel_bias, v_conv_w, v_ln_g, v_ln_b]
    names = ["w_in", "w_mem_kv", "w_out", "rel_bias", "conv_w", "ln_g", "ln_b"]
    upd = [_adamw(w, g, m, v, f"adamw_{n}") for w, g, m, v, n in zip(weights, grads, moms, vels, names)]
    grads, deltas, new_m, new_v = zip(*upd)
    return (loss, grad_x, *grads, *deltas, *new_m, *new_v)
```

```python
import functools
import math

import jax
import jax.numpy as jnp
from jax import lax
from jax.experimental import pallas as pl
from jax.experimental.pallas import tpu as pltpu

f32, bf16 = jnp.float32, jnp.bfloat16

D_MODEL = 1024
DEPTH = 4
CHUNK = 64
N_PREV = 8
N_HEADS = 16
HEAD_DIM = 64
E_MIX = 1024
REL_CLIP = 128
N_REL = 2 * REL_CLIP + 1
N_REL_PAD = 384
CONV_W = 3
N_MEM = 256
MEM_HEADS = 4
MEM_HEAD_DIM = 128
E_MEM = 512
E_BRANCH = E_MIX + E_MEM
N_IN = 3 * E_MIX + E_MEM + E_BRANCH
N_CHIPS = 4
W_IN_COLS = N_IN // N_CHIPS
W_KV_ROWS = D_MODEL // N_CHIPS
W_OUT_ROWS = E_BRANCH // N_CHIPS
DN_ALPHA = (2.0 * DEPTH) ** 0.25
LN_EPS = 1e-5
ADAM_LR, ADAM_B1, ADAM_B2, ADAM_EPS, ADAM_WD, ADAM_STEP = 0.001, 0.9, 0.999, 1e-08, 0.01, 10

QG = 4 * CHUNK
KG = QG + N_PREV * CHUNK
DB_COLS = KG // 2
NEG = -1e30
VMEM_LIMIT = 56 * 1024 * 1024

NT = (((1,), (1,)), ((), ()))
TN = (((0,), (0,)), ((), ()))
MESH = pl.DeviceIdType.MESH
ANY = pl.BlockSpec(memory_space=pl.ANY)


def _pcall(body, **kw):
    return pl.pallas_call(body, **kw)


def _params(*sem):
    return pltpu.CompilerParams(dimension_semantics=sem, vmem_limit_bytes=VMEM_LIMIT)


def _silu_parts(z):
    sig = 1.0 / (1.0 + jnp.exp(-z))
    return z * sig, sig


class _Comm:
    def __init__(self, inputs, out_shapes, aliases, n_sems, copies):
        self.inputs, self.out_shapes, self.aliases, self.n_sems, self.copies = inputs, out_shapes, aliases, n_sems, copies

    def start(self, cin, cout, send, recv):
        for cp in self.copies(cin, cout, send, recv)[0]:
            cp.start()

    def wait(self, cin, cout, send, recv):
        sends, recvs = self.copies(cin, cout, send, recv)
        for cp in recvs:
            cp.wait_recv()
        for cp in sends:
            cp.wait_send()


def _pcall_carry(body, comm, *, n_in, n_out, **kw):
    if comm is None:
        return lambda *args: (_pcall(body, **kw)(*args), ())
    grid = kw["grid"]
    k_in, k_out = len(comm.inputs), len(comm.out_shapes)

    def carried(*refs):
        ins, cin = refs[:n_in], refs[n_in:n_in + k_in]
        outs = refs[n_in + k_in:n_in + k_in + n_out]
        cout = refs[n_in + k_in + n_out:n_in + k_in + n_out + k_out]
        scratch, send, recv = refs[n_in + k_in + n_out + k_out:-2], refs[-2], refs[-1]
        ids = [pl.program_id(a) for a in range(len(grid))]
        first = functools.reduce(jnp.logical_and, [i == 0 for i in ids])
        last = functools.reduce(jnp.logical_and, [i == n - 1 for i, n in zip(ids, grid)])

        @pl.when(first)
        def _():
            comm.start(cin, cout, send, recv)

        body(*ins, *outs, *scratch)

        @pl.when(last)
        def _():
            comm.wait(cin, cout, send, recv)

    kw = dict(kw)
    kw["in_specs"] = list(kw["in_specs"]) + [ANY] * k_in
    kw["out_specs"] = tuple(kw["out_specs"]) + (ANY,) * k_out
    kw["out_shape"] = tuple(kw["out_shape"]) + tuple(comm.out_shapes)
    kw["scratch_shapes"] = list(kw.get("scratch_shapes", ())) + [pltpu.SemaphoreType.DMA((comm.n_sems,))] * 2
    aliases = dict(kw.get("input_output_aliases", {}))
    aliases.update({n_in + ci: n_out + co for ci, co in comm.aliases.items()})
    kw["input_output_aliases"] = aliases

    def run(*args):
        res = _pcall(carried, **kw)(*args, *comm.inputs)
        return res[:n_out], res[n_out:]

    return run


def _comm_call(comm, name):
    k_in = len(comm.inputs)

    def body(*refs):
        cin, cout, send, recv = refs[:k_in], refs[k_in:-2], refs[-2], refs[-1]
        comm.start(cin, cout, send, recv)
        comm.wait(cin, cout, send, recv)

    return _pcall(body, name=name, out_shape=tuple(comm.out_shapes), in_specs=[ANY] * k_in,
                  out_specs=(ANY,) * len(comm.out_shapes), input_output_aliases=dict(comm.aliases),
                  scratch_shapes=[pltpu.SemaphoreType.DMA((comm.n_sems,))] * 2)(*comm.inputs)


def _place():
    x, y, c = lax.axis_index("x"), lax.axis_index("y"), lax.axis_index("c")
    return x, y, c, 2 * x + y, (x, y, 1 - c), [(1 - x, y), (x, 1 - y), (1 - x, 1 - y)]


def _rcopy(send, recv, k, src, dst, to):
    return pltpu.make_async_remote_copy(src_ref=src, dst_ref=dst, send_sem=send.at[k], recv_sem=recv.at[k],
                                        device_id=to, device_id_type=MESH)


def _half(ref_rows, core):
    return pl.ds(core * (ref_rows // 2), ref_rows // 2)


def _gather_ici(shards, layer, extra=None, peers=(0, 1, 2), own=True, into=None):
    extras = [] if extra is None else [extra]
    n = len(shards)

    def copies(cin, cout, send, recv):
        x, y, c, me, sibling, chips = _place()
        sends, recvs = [], []
        for a in range(n + len(extras)):
            s, g = cin[a], cout[a]
            whole = a >= n
            src = s if whole else s.at[layer]
            if own:
                sends.append(_rcopy(send, recv, 4 * a, src, g.at[me], sibling))
                recvs.append(_rcopy(send, recv, 4 * a, src, g.at[me], sibling))
            for p in peers:
                px, py = chips[p]
                if whole:
                    sends.append(_rcopy(send, recv, 4 * a + 1 + p, src, g.at[me], (px, py, c)))
                    recvs.append(_rcopy(send, recv, 4 * a + 1 + p, src, g.at[2 * px + py], (px, py, c)))
                else:
                    mine = _half(s.shape[1], c)
                    sends.append(_rcopy(send, recv, 4 * a + 1 + p, s.at[layer, mine], g.at[me, mine], (px, py, c)))
                    recvs.append(_rcopy(send, recv, 4 * a + 1 + p, s.at[layer, mine], g.at[2 * px + py, mine], (px, py, c)))
        return sends, recvs

    sources = list(shards) + extras
    if into is None:
        out_shapes = [jax.ShapeDtypeStruct((N_CHIPS,) + s.shape[1:], s.dtype) for s in shards]
        out_shapes += [jax.ShapeDtypeStruct((N_CHIPS,) + e.shape, e.dtype) for e in extras]
        return _Comm(sources, out_shapes, {}, 4 * len(sources), copies)
    return _Comm(sources + list(into), [jax.ShapeDtypeStruct(g.shape, g.dtype) for g in into],
                 {len(sources) + a: a for a in range(len(into))}, 4 * len(sources), copies)


def _gather_d2d(gathered):
    n = len(gathered)

    def copies(cin, cout, send, recv):
        x, y, c, me, sibling, chips = _place()
        sends, recvs = [], []
        for a in range(n):
            g = cout[a]
            rows = g.shape[1]
            for p, (px, py) in enumerate(chips):
                mine, theirs = g.at[2 * px + py, _half(rows, c)], g.at[2 * px + py, _half(rows, 1 - c)]
                sends.append(_rcopy(send, recv, 3 * a + p, mine, mine, sibling))
                recvs.append(_rcopy(send, recv, 3 * a + p, theirs, theirs, sibling))
        return sends, recvs

    return _Comm(list(gathered), [jax.ShapeDtypeStruct(g.shape, g.dtype) for g in gathered],
                 {a: a for a in range(n)}, 3 * n, copies)


class _SemView:
    def __init__(self, ref, base):
        self.ref, self.base, self.at = ref, base, self

    def __getitem__(self, k):
        return self.ref.at[self.base + k]


def _both(a, b):
    ka, ma = len(a.inputs), len(a.out_shapes)

    def copies(cin, cout, send, recv):
        sa, ra = a.copies(cin[:ka], cout[:ma], send, recv)
        sb, rb = b.copies(cin[ka:], cout[ma:], _SemView(send, a.n_sems), _SemView(recv, a.n_sems))
        return sa + sb, ra + rb

    aliases = dict(a.aliases)
    aliases.update({ka + ci: ma + co for ci, co in b.aliases.items()})
    return _Comm(a.inputs + b.inputs, a.out_shapes + b.out_shapes, aliases, a.n_sems + b.n_sems, copies)


def _small_exchange(buf):
    def copies(cin, cout, send, recv):
        x, y, c, me, sibling, chips = _place()
        sends, recvs = [], []
        for r in range(1, 8):
            px, py, pc = x ^ ((r >> 2) & 1), y ^ ((r >> 1) & 1), c ^ (r & 1)
            sends.append(_rcopy(send, recv, r - 1, cin[0], cout[0].at[2 * me + c], (px, py, pc)))
            recvs.append(_rcopy(send, recv, r - 1, cin[0], cout[0].at[4 * px + 2 * py + pc], (px, py, pc)))
        return sends, recvs

    return _Comm([buf], [jax.ShapeDtypeStruct((8,) + buf.shape, buf.dtype)], {}, 7, copies)


def _small_sum(slots, buf, device_arr):
    rows, cols = buf.shape

    def body(d_ref, s_ref, b_ref, o_ref):
        d = pl.program_id(0)
        val = jnp.where(d == d_ref[0], b_ref[...], s_ref[...])

        @pl.when(d == 0)
        def _():
            o_ref[...] = val

        @pl.when(d > 0)
        def _():
            o_ref[...] += val

    return _pcall(
        body, name="small_sum", out_shape=jax.ShapeDtypeStruct((rows, cols), f32),
        grid_spec=pltpu.PrefetchScalarGridSpec(
            num_scalar_prefetch=1, grid=(8,),
            in_specs=[pl.BlockSpec((None, rows, cols), lambda d, d_ref: (jnp.where(d == d_ref[0], (d + 1) % 8, d), 0, 0)),
                      pl.BlockSpec((rows, cols), lambda d, d_ref: (0, 0))],
            out_specs=pl.BlockSpec((rows, cols), lambda d, d_ref: (0, 0))),
        compiler_params=_params("arbitrary"))(device_arr, slots, buf)


def _sibling_exchange(gs):
    def copies(cin, cout, send, recv):
        x, y, c, me, sibling, chips = _place()
        sends = [_rcopy(send, recv, a, g.at[:, _half(g.shape[1], 1 - c)], r, sibling) for a, (g, r) in enumerate(zip(cin, cout))]
        return sends, sends

    shapes = [jax.ShapeDtypeStruct((N_CHIPS, g.shape[1] // 2, g.shape[2]), g.dtype) for g in gs]
    return _Comm(list(gs), shapes, {}, len(gs), copies)


def _chip_scatter(ss):
    def copies(cin, cout, send, recv):
        x, y, c, me, sibling, chips = _place()
        sends = [_rcopy(send, recv, 3 * a + p, s.at[2 * px + py], r.at[p], (px, py, c))
                 for a, (s, r) in enumerate(zip(cin, cout)) for p, (px, py) in enumerate(chips)]
        return sends, sends

    shapes = [jax.ShapeDtypeStruct((3,) + s.shape[1:], s.dtype) for s in ss]
    return _Comm(list(ss), shapes, {}, 3 * len(ss), copies)


def _sibling_share(fs, layer):
    def copies(cin, cout, send, recv):
        x, y, c, me, sibling, chips = _place()
        sends, recvs = [], []
        for a, f in enumerate(cout):
            mine, theirs = f.at[layer, _half(f.shape[1], c)], f.at[layer, _half(f.shape[1], 1 - c)]
            sends.append(_rcopy(send, recv, a, mine, mine, sibling))
            recvs.append(_rcopy(send, recv, a, theirs, theirs, sibling))
        return sends, recvs

    return _Comm(list(fs), [jax.ShapeDtypeStruct(f.shape, f.dtype) for f in fs], {a: a for a in range(len(fs))},
                 len(fs), copies)


def _sum_rows(rows):
    return next(b for b in (256, 192, 128) if rows % b == 0)


def _pair_sum(g, r, core_arr, layer):
    _, rows, cols = r.shape
    br = _sum_rows(rows)
    nb = rows // br

    def body(c_ref, g_ref, r_ref, o_ref):
        o_ref[...] = (g_ref[...].astype(f32) + r_ref[...].astype(f32)).astype(bf16)

    return _pcall(
        body, name=f"pair_sum_{layer}", out_shape=jax.ShapeDtypeStruct(r.shape, bf16),
        grid_spec=pltpu.PrefetchScalarGridSpec(
            num_scalar_prefetch=1, grid=(N_CHIPS, nb),
            in_specs=[pl.BlockSpec((1, br, cols), lambda j, i, c_ref: (j, c_ref[0] * nb + i, 0)),
                      pl.BlockSpec((1, br, cols), lambda j, i, c_ref: (j, i, 0))],
            out_specs=pl.BlockSpec((1, br, cols), lambda j, i, c_ref: (j, i, 0))),
        compiler_params=_params("arbitrary", "arbitrary"))(core_arr, g, r)


def _chip_sum(s, r, place, layer, final):
    _, rows, cols = s.shape
    br = _sum_rows(rows)
    nb = rows // br

    def body(place_ref, s_ref, r_ref, *rest):
        o_ref = rest[-1]
        acc = s_ref[0].astype(f32)
        for p in range(3):
            acc = acc + r_ref[p].astype(f32)
        o_ref[...] = acc

    carried = [] if final is None else [final]
    return _pcall(
        body, name=f"chip_sum_{layer}", out_shape=jax.ShapeDtypeStruct((DEPTH, 2 * rows, cols), f32),
        grid_spec=pltpu.PrefetchScalarGridSpec(
            num_scalar_prefetch=1, grid=(nb,),
            in_specs=[pl.BlockSpec((1, br, cols), lambda i, place_ref: (place_ref[0], i, 0)),
                      pl.BlockSpec((3, br, cols), lambda i, place_ref: (0, i, 0))] + [ANY] * len(carried),
            out_specs=pl.BlockSpec((None, br, cols), lambda i, place_ref: (layer, place_ref[1] * nb + i, 0))),
        input_output_aliases={3: 0} if carried else {},
        compiler_params=_params("arbitrary"))(place, s, r, *carried)


def _inproj(xin, win, layer, comm=None, conv_w=None):
    T = xin.shape[0]
    tm = 512
    cast = xin.dtype != bf16
    conv = conv_w is not None

    def body(x_ref, w_ref, *rest):
        rest = list(rest)
        cw_ref = rest.pop(0) if conv else None
        o_ref = rest.pop(0)
        xt = x_ref[...].astype(bf16)
        if cast:
            rest.pop(0)[...] = xt
            rest.pop(0)[...] = xt.T
        for j in range(N_CHIPS):
            o_ref[:, j * W_IN_COLS:(j + 1) * W_IN_COLS] = jnp.dot(xt, w_ref[j], preferred_element_type=f32).astype(bf16)
        if conv:
            mix_ref, halo = rest

            @pl.when(pl.program_id(0) == 0)
            def _():
                halo[...] = jnp.zeros_like(halo)

            bg, cg, u = (o_ref[:, s * E_MIX:(s + 1) * E_MIX].astype(f32) for s in range(3))
            cu = cg * u
            out, _, _ = _conv_taps(cu, halo[6:7, :], halo[7:8, :], cw_ref)
            mix_ref[...] = (bg * out).astype(bf16)
            halo[...] = cu[tm - 8:, :]

    tile = pl.BlockSpec((tm, D_MODEL), lambda i: (i, 0))
    rows = jax.ShapeDtypeStruct((T, D_MODEL), bf16)
    casts = ((rows, jax.ShapeDtypeStruct((D_MODEL, T), bf16)), (tile, pl.BlockSpec((D_MODEL, tm), lambda i: (0, i)))) if cast else ((), ())
    outs, carried = _pcall_carry(
        body, comm, n_in=2 + conv, n_out=1 + 2 * cast + conv, name=f"inproj_{layer}", grid=(T // tm,),
        out_shape=(jax.ShapeDtypeStruct((T, N_IN), bf16),) + casts[0] + (rows,) * conv,
        in_specs=[tile, pl.BlockSpec((N_CHIPS, D_MODEL, W_IN_COLS), lambda i: (0, 0, 0), pipeline_mode=pl.Buffered(1))]
        + [pl.BlockSpec((CONV_W, E_MIX), lambda i: (0, 0))] * conv,
        out_specs=(pl.BlockSpec((tm, N_IN), lambda i: (i, 0)),) + casts[1] + (tile,) * conv,
        scratch_shapes=[pltpu.VMEM((8, E_MIX), f32)] * conv,
        compiler_params=_params("arbitrary"))(xin, win, *([conv_w] if conv else []))
    return outs[0], (outs[1] if cast else xin), (outs[2] if cast else None), (outs[-1] if conv else None), carried


def _rel_index_rows():
    j = lax.broadcasted_iota(jnp.int32, (N_REL_PAD, KG), 1)
    r = lax.broadcasted_iota(jnp.int32, (N_REL_PAD, KG), 0)
    off = jnp.where(j < KG - 2 * CHUNK, j, j - KG)
    idx = jnp.clip(N_PREV * CHUNK - off, -REL_CLIP, REL_CLIP) + REL_CLIP
    return (idx == r).astype(f32)


def _bias_expand(table_pad, layer, comm=None):
    def body(t_ref, o_ref, row_scr):
        h = pl.program_id(0)

        @pl.when(h == 0)
        def _():
            row_scr[...] = jnp.dot(t_ref[...], _rel_index_rows(), precision=lax.Precision.HIGHEST,
                                   preferred_element_type=f32)

        q = lax.broadcasted_iota(jnp.int32, (QG, KG), 0)
        k = lax.broadcasted_iota(jnp.int32, (QG, KG), 1)
        band = (k // CHUNK >= q // CHUNK) & (k // CHUNK <= q // CHUNK + N_PREV)
        t = jnp.broadcast_to(row_scr[pl.ds(h, 1), :], (QG, KG))
        for b in range(8):
            t = jnp.where(((q >> b) & 1) == 1, pltpu.roll(t, 1 << b, axis=1), t)
        for v in range(3):
            o_ref[v] = jnp.where(band & (k >= (2 - v) * QG), t, NEG)

    (bias,), carried = _pcall_carry(
        body, comm, n_in=1, n_out=1, name=f"bias_expand_{layer}", grid=(N_HEADS,),
        out_shape=(jax.ShapeDtypeStruct((3, N_HEADS, QG, KG), f32),),
        in_specs=[pl.BlockSpec((N_HEADS, N_REL_PAD), lambda h: (0, 0))],
        out_specs=(pl.BlockSpec((3, None, QG, KG), lambda h: (0, h, 0, 0)),),
        scratch_shapes=[pltpu.VMEM((N_HEADS, KG), f32)], compiler_params=_params("arbitrary"))(table_pad)
    return bias, carried


def _bias_reduce(dbias, layer):
    def body(d_ref, o_ref, row_scr):
        q = lax.broadcasted_iota(jnp.int32, (QG, DB_COLS), 0)
        k = lax.broadcasted_iota(jnp.int32, (QG, DB_COLS), 1)
        for h in range(N_HEADS):
            t = jnp.where(k > q, d_ref[h], 0.0)
            for b in range(8):
                t = jnp.where(((q >> b) & 1) == 1, pltpu.roll(t, DB_COLS - (1 << b), axis=1), t)
            row_scr[h:h + 1, :] = jnp.sum(t, axis=0, keepdims=True)
        r = lax.broadcasted_iota(jnp.int32, (N_REL_PAD, DB_COLS), 0)
        off = lax.broadcasted_iota(jnp.int32, (N_REL_PAD, DB_COLS), 1)
        own = (off >= 1) & (off < REL_CLIP + CHUNK)
        sel = jnp.where(own & (r == 2 * REL_CLIP - off), 1.0, 0.0) - jnp.where(own & (r == 2 * REL_CLIP), 1.0, 0.0)
        o_ref[...] = lax.dot_general(row_scr[...], sel, NT, precision=lax.Precision.HIGHEST, preferred_element_type=f32)

    return _pcall(body, name=f"bias_reduce_{layer}", out_shape=jax.ShapeDtypeStruct((N_HEADS, N_REL_PAD), f32),
                  scratch_shapes=[pltpu.VMEM((N_HEADS, DB_COLS), f32)],
                  compiler_params=pltpu.CompilerParams(vmem_limit_bytes=VMEM_LIMIT))(dbias)


FWD_PAIRS = 8
BWD_PAIRS = 4


def _key_specs(n_groups, npairs, slab):
    per_slab = E_MIX // (128 * npairs)
    return [pl.BlockSpec((QG, 128 * npairs), functools.partial(
        lambda hp, g, jj: (jnp.clip(g - 2 + jj, 0, n_groups - 1), slab * per_slab + hp), jj=jj)) for jj in range(3)]


def _bias_spec(npairs):
    return pl.BlockSpec((None, 2 * npairs, QG, KG), lambda hp, g: (jnp.minimum(g, 2), hp, 0, 0))


def _attn_fwd(h, bias, layer, comm=None):
    T = h.shape[0]
    n_groups = T // QG
    scale = 1.0 / math.sqrt(HEAD_DIM)

    def body(q_ref, k0, k1, k2, v0, v1, v2, b_ref, o_ref, lse_ref):
        lane = lax.broadcasted_iota(jnp.int32, (1, 128), 1)
        ones = jnp.ones((KG, 128), bf16)
        lse = jnp.zeros((QG, 128), f32)
        for pp in range(FWD_PAIRS):
            cs = slice(pp * 128, (pp + 1) * 128)
            q2 = q_ref[:, cs] * scale
            kc = jnp.concatenate([k0[:, cs], k1[:, cs], k2[:, cs]], axis=0)
            vc = jnp.concatenate([jnp.concatenate([v0[:, cs], v1[:, cs], v2[:, cs]], axis=0), ones], axis=1)
            outs = []
            for hh in range(2):
                qm = jnp.where(lane // HEAD_DIM == hh, q2, jnp.zeros_like(q2))
                s = lax.dot_general(qm, kc, NT, preferred_element_type=f32) + b_ref[2 * pp + hh]
                m = jnp.max(s, axis=1, keepdims=True)
                ol = jnp.dot(jnp.exp(s - m).astype(bf16), vc, preferred_element_type=f32)
                outs.append(ol[:, :128] / ol[:, 128:])
                lse = jnp.where(lane == 2 * pp + hh, m + jnp.log(ol[:, 128:]), lse)
            o_ref[:, cs] = jnp.where(lane // HEAD_DIM == 0, outs[0], outs[1]).astype(bf16)
        lse_ref[...] = lse

    (mix, lse), carried = _pcall_carry(
        body, comm, n_in=8, n_out=2, name=f"attn_fwd_{layer}", grid=(N_HEADS // (2 * FWD_PAIRS), n_groups),
        out_shape=(jax.ShapeDtypeStruct((T, E_MIX), bf16), jax.ShapeDtypeStruct((T, 128), f32)),
        in_specs=[pl.BlockSpec((QG, 128 * FWD_PAIRS), lambda hp, g: (g, hp))] + _key_specs(n_groups, FWD_PAIRS, 1)
        + _key_specs(n_groups, FWD_PAIRS, 2) + [_bias_spec(FWD_PAIRS)],
        out_specs=(pl.BlockSpec((QG, 128 * FWD_PAIRS), lambda hp, g: (g, hp)), pl.BlockSpec((QG, 128), lambda hp, g: (g, 0))),
        compiler_params=_params("arbitrary", "arbitrary"))(h, h, h, h, h, h, h, bias)
    return mix, lse, carried


def _halo_rows(ref, r):
    return ref[r:r + 1, :].astype(f32)


def _conv_taps(cu, p6, p7, w_ref):
    row = lax.broadcasted_iota(jnp.int32, cu.shape, 0)
    r1 = jnp.where(row == 0, p7, pltpu.roll(cu, 1, axis=0))
    r2 = jnp.where(row == 0, p6, jnp.where(row == 1, p7, pltpu.roll(cu, 2, axis=0)))
    return w_ref[2:3, :] * cu + w_ref[1:2, :] * r1 + w_ref[0:1, :] * r2, r1, r2


def _kv_mem(memb, wkv):
    def body(m_ref, w_ref, o_ref):
        o_ref[...] = jnp.dot(m_ref[...], w_ref[...], preferred_element_type=f32).astype(bf16)

    return _pcall(body, name="kv_mem", out_shape=jax.ShapeDtypeStruct((N_MEM, 2 * E_MEM), bf16),
                  compiler_params=pltpu.CompilerParams(vmem_limit_bytes=VMEM_LIMIT))(memb, wkv)


def _mem_probs(qm_ref, kv_ref, hh):
    qh = qm_ref[:, hh * MEM_HEAD_DIM:(hh + 1) * MEM_HEAD_DIM]
    kh = kv_ref[:, hh * MEM_HEAD_DIM:(hh + 1) * MEM_HEAD_DIM]
    vh = kv_ref[:, E_MEM + hh * MEM_HEAD_DIM:E_MEM + (hh + 1) * MEM_HEAD_DIM]
    s = lax.dot_general(qh, kh, NT, preferred_element_type=f32) * (1.0 / math.sqrt(MEM_HEAD_DIM))
    e = jnp.exp(s - jnp.max(s, axis=1, keepdims=True))
    return e / jnp.sum(e, axis=1, keepdims=True), qh, kh, vh


def _h_tail_specs(tm):
    return [pl.BlockSpec((tm, E_MEM), functools.partial(lambda i, cb: (i, cb), cb=cb)) for cb in (6, 7, 8, 9)]


def _ln_bwd(dy, xhat, rstd, g, dob_ref, dxp_ref, dg_ref, db_ref):
    dg_ref[0:1, :] += jnp.sum(dy * xhat, axis=0, keepdims=True)
    db_ref[0:1, :] += jnp.sum(dy, axis=0, keepdims=True)
    gx = dy * g
    dr = rstd * (gx - jnp.mean(gx, axis=1, keepdims=True) - xhat * jnp.mean(gx * xhat, axis=1, keepdims=True))
    dxp_ref[...] = DN_ALPHA * dr
    dob_ref[...] = dr.astype(bf16)


def _post_fwd(h, mix, kv, wout, x, g, b, layer, target=None, comm=None):
    T = x.shape[0]
    tm = 512

    def body(qm_ref, z0, z1, z2, mix_ref, kv_ref, w_ref, x_ref, g_ref, b_ref, *rest):
        *rest, ps_ref, ms_ref = rest
        mems = []
        for hh in range(MEM_HEADS):
            p, _, _, vh = _mem_probs(qm_ref, kv_ref, hh)
            pb = p.astype(bf16)
            ps_ref[:, hh * N_MEM:(hh + 1) * N_MEM] = pb
            mems.append(jnp.dot(pb, vh, preferred_element_type=f32))
        memb = jnp.concatenate(mems, axis=1).astype(bf16)
        ms_ref[...] = memb
        z = jnp.concatenate([z0[...], z1[...], z2[...]], axis=1)
        one = jnp.ones((), bf16)
        y = jnp.concatenate([mix_ref[...], memb], axis=1) * (z * (one / (one + jnp.exp(-z))))
        out = jnp.dot(y, w_ref[...], preferred_element_type=f32)
        r = DN_ALPHA * x_ref[...] + out
        mu = jnp.mean(r, axis=1, keepdims=True)
        var = jnp.mean(jnp.square(r - mu), axis=1, keepdims=True)
        rstd = lax.rsqrt(var + LN_EPS)
        xhat = (r - mu) * rstd
        xn = xhat * g_ref[...] + b_ref[...]
        if target is None:
            xn_ref, xb_ref, xt_ref, xh_ref, rs_ref = rest
            xb = xn.astype(bf16)
            xn_ref[...] = xn
            xb_ref[...] = xb
            xt_ref[...] = xb.T
            xh_ref[...] = xhat
            rs_ref[...] = rstd
        else:
            t_ref, l_ref, dob_ref, dxp_ref, dg_ref, db_ref = rest

            @pl.when(pl.program_id(0) == 0)
            def _():
                l_ref[...] = jnp.zeros_like(l_ref)
                dg_ref[...] = jnp.zeros_like(dg_ref)
                db_ref[...] = jnp.zeros_like(db_ref)

            err = xn - t_ref[...]
            l_ref[...] += jnp.sum(jnp.square(err))
            _ln_bwd(err * (1.0 / D_MODEL), xhat, rstd, g_ref[...], dob_ref, dxp_ref, dg_ref, db_ref)

    tile = lambda w: pl.BlockSpec((tm, w), lambda i: (i, 0))
    const = lambda r, c: pl.BlockSpec((r, c), lambda i: (0, 0))
    resident = lambda r, c: pl.BlockSpec((r, c), lambda i: (0, 0), pipeline_mode=pl.Buffered(1))
    in_specs = _h_tail_specs(tm) + [tile(E_MIX), resident(N_MEM, 2 * E_MEM), resident(E_BRANCH, D_MODEL), tile(D_MODEL),
                                    const(1, D_MODEL), const(1, D_MODEL)]
    kept = (jax.ShapeDtypeStruct((T, MEM_HEADS * N_MEM), bf16), jax.ShapeDtypeStruct((T, E_MEM), bf16))
    kept_specs = (tile(MEM_HEADS * N_MEM), tile(E_MEM))
    if target is None:
        return _pcall_carry(
            body, comm, n_in=10, n_out=7, name=f"post_fwd_{layer}", grid=(T // tm,),
            out_shape=(jax.ShapeDtypeStruct((T, D_MODEL), f32), jax.ShapeDtypeStruct((T, D_MODEL), bf16),
                       jax.ShapeDtypeStruct((D_MODEL, T), bf16), jax.ShapeDtypeStruct((T, D_MODEL), f32),
                       jax.ShapeDtypeStruct((T, 1), f32)) + kept,
            in_specs=in_specs,
            out_specs=(tile(D_MODEL), tile(D_MODEL), pl.BlockSpec((D_MODEL, tm), lambda i: (0, i)), tile(D_MODEL), tile(1))
            + kept_specs,
            compiler_params=_params("arbitrary"))(h, h, h, h, mix, kv, wout, x, g, b)
    return _pcall(
        body, name=f"post_fwd_loss_{layer}", grid=(T // tm,),
        out_shape=(jax.ShapeDtypeStruct((8, 128), f32), jax.ShapeDtypeStruct((T, D_MODEL), bf16),
                   jax.ShapeDtypeStruct((T, D_MODEL), f32), jax.ShapeDtypeStruct((8, D_MODEL), f32),
                   jax.ShapeDtypeStruct((8, D_MODEL), f32)) + kept,
        in_specs=in_specs + [tile(D_MODEL)],
        out_specs=(const(8, 128), tile(D_MODEL), tile(D_MODEL), const(8, D_MODEL), const(8, D_MODEL)) + kept_specs,
        compiler_params=_params("arbitrary"))(h, h, h, h, mix, kv, wout, x, g, b, target)


def _post_bwd(dob, h, mix, kv, wout, kept, layer, comm=None):
    T = dob.shape[0]
    tm = 512
    n_tiles = T // tm
    inv = 1.0 / math.sqrt(MEM_HEAD_DIM)

    def body(dob_ref, qm_ref, z0, z1, z2, mix_ref, kv_ref, w_ref, ps_ref, ms_ref,
             dhb_ref, dmix_ref, dkv_ref, dwo_out, dwo_ref):
        @pl.when(pl.program_id(0) == 0)
        def _():
            dkv_ref[...] = jnp.zeros_like(dkv_ref)
            dwo_ref[...] = jnp.zeros_like(dwo_ref)

        dob = dob_ref[...]
        z = jnp.concatenate([z0[...], z1[...], z2[...]], axis=1).astype(f32)
        act, sig = _silu_parts(z)
        cat = jnp.concatenate([mix_ref[...].astype(f32), ms_ref[...].astype(f32)], axis=1)
        yb = (cat * act).astype(bf16)
        dwo_ref[...] += lax.dot_general(yb, dob, TN, preferred_element_type=f32)
        dyv = lax.dot_general(dob, w_ref[...], NT, preferred_element_type=f32)
        dz = dyv * cat * (sig * (1.0 + z * (1.0 - sig)))
        dcat = dyv * act
        dmix_ref[...] = dcat[:, :E_MIX].astype(bf16)
        dqs = []
        for hh in range(MEM_HEADS):
            cols = slice(hh * MEM_HEAD_DIM, (hh + 1) * MEM_HEAD_DIM)
            qh, kh, vh = qm_ref[:, cols], kv_ref[:, cols], kv_ref[:, E_MEM + hh * MEM_HEAD_DIM:E_MEM + (hh + 1) * MEM_HEAD_DIM]
            pb = ps_ref[:, hh * N_MEM:(hh + 1) * N_MEM]
            p = pb.astype(f32)
            dmem = dcat[:, E_MIX + hh * MEM_HEAD_DIM:E_MIX + (hh + 1) * MEM_HEAD_DIM].astype(bf16)
            dp = lax.dot_general(dmem, vh, NT, preferred_element_type=f32)
            ds = (p * (dp - jnp.sum(p * dp, axis=1, keepdims=True))).astype(bf16)
            dqs.append(jnp.dot(ds, kh, preferred_element_type=f32) * inv)
            dkv_ref[:, cols] += lax.dot_general(ds, qh, TN, preferred_element_type=f32) * inv
            dkv_ref[:, E_MEM + hh * MEM_HEAD_DIM:E_MEM + (hh + 1) * MEM_HEAD_DIM] += lax.dot_general(
                pb, dmem, TN, preferred_element_type=f32)
        dhb_ref[...] = jnp.concatenate(dqs + [dz], axis=1).astype(bf16)

        @pl.when(pl.program_id(0) == n_tiles - 1)
        def _():
            dwo_out[...] = dwo_ref[...].astype(bf16)

    tile = lambda w: pl.BlockSpec((tm, w), lambda i: (i, 0))
    const = lambda r, c: pl.BlockSpec((r, c), lambda i: (0, 0))
    resident = lambda r, c: pl.BlockSpec((r, c), lambda i: (0, 0), pipeline_mode=pl.Buffered(1))
    return _pcall_carry(
        body, comm, n_in=10, n_out=4, name=f"post_bwd_{layer}", grid=(n_tiles,),
        out_shape=(jax.ShapeDtypeStruct((T, E_MEM + E_BRANCH), bf16), jax.ShapeDtypeStruct((T, E_MIX), bf16),
                   jax.ShapeDtypeStruct((N_MEM, 2 * E_MEM), f32), jax.ShapeDtypeStruct((E_BRANCH, D_MODEL), bf16)),
        in_specs=[tile(D_MODEL)] + _h_tail_specs(tm) + [tile(E_MIX), resident(N_MEM, 2 * E_MEM), resident(E_BRANCH, D_MODEL),
                                                        tile(MEM_HEADS * N_MEM), tile(E_MEM)],
        out_specs=(tile(E_MEM + E_BRANCH), tile(E_MIX), const(N_MEM, 2 * E_MEM), const(E_BRANCH, D_MODEL)),
        scratch_shapes=[pltpu.VMEM((E_BRANCH, D_MODEL), f32)],
        compiler_params=_params("arbitrary"))(dob, h, h, h, h, mix, kv, wout, *kept)


def _attn_bwd(h, bias, dmix, lse, layer, comm=None):
    T = h.shape[0]
    n_groups = T // QG
    scale = 1.0 / math.sqrt(HEAD_DIM)

    def body(q_ref, k0, k1, k2, v0, v1, v2, do_ref, b_ref, lse_ref, dq_ref, dk_ref, dv_ref, db_ref, acck, accv):
        g = pl.program_id(1)

        @pl.when(g == 0)
        def _():
            acck[...] = jnp.zeros_like(acck)
            accv[...] = jnp.zeros_like(accv)
            db_ref[...] = jnp.zeros_like(db_ref)

        @pl.when(g < n_groups)
        def _():
            lane = lax.broadcasted_iota(jnp.int32, (1, 128), 1)
            first = lane // HEAD_DIM == 0
            for pp in range(BWD_PAIRS):
                cs = slice(pp * 128, (pp + 1) * 128)
                do2 = do_ref[:, cs]
                q2 = q_ref[:, cs] * scale
                kc = jnp.concatenate([k0[:, cs], k1[:, cs], k2[:, cs]], axis=0)
                vc = jnp.concatenate([v0[:, cs], v1[:, cs], v2[:, cs]], axis=0)
                q2t, do2t = q2.T, do2.T
                dqs, dks, dvs = [], [], []
                for hh in range(2):
                    hm = lane // HEAD_DIM == hh
                    head = (pl.program_id(0) * BWD_PAIRS + pp) * 2 + hh
                    lse = jnp.sum(jnp.where(lane == head, lse_ref[...], 0.0), axis=1, keepdims=True)
                    qm = jnp.where(hm, q2, jnp.zeros_like(q2))
                    dom = jnp.where(hm, do2, jnp.zeros_like(do2))
                    s = lax.dot_general(qm, kc, NT, preferred_element_type=f32) + b_ref[2 * pp + hh]
                    p = jnp.exp(s - lse)
                    dp = lax.dot_general(dom, vc, NT, preferred_element_type=f32)
                    ds = p * (dp - jnp.sum(p * dp, axis=1, keepdims=True))
                    db_ref[2 * pp + hh] += ds[:, KG - DB_COLS:]
                    dsb, pb = ds.astype(bf16), p.astype(bf16)
                    dqs.append(jnp.dot(dsb, kc, preferred_element_type=f32) * scale)
                    dks.append(jnp.dot(q2t[hh * HEAD_DIM:(hh + 1) * HEAD_DIM], dsb, preferred_element_type=f32))
                    dvs.append(jnp.dot(do2t[hh * HEAD_DIM:(hh + 1) * HEAD_DIM], pb, preferred_element_type=f32))
                dq_ref[:, cs] = jnp.where(first, dqs[0], dqs[1]).astype(bf16)
                dkc = jnp.concatenate(dks, axis=0).T
                dvc = jnp.concatenate(dvs, axis=0).T
                for jj in range(3):
                    slot = (g + 1 + jj) % 3
                    if jj == 2:
                        acck[slot, :, cs] = dkc[jj * QG:(jj + 1) * QG]
                        accv[slot, :, cs] = dvc[jj * QG:(jj + 1) * QG]
                    else:
                        acck[slot, :, cs] += dkc[jj * QG:(jj + 1) * QG]
                        accv[slot, :, cs] += dvc[jj * QG:(jj + 1) * QG]

        done = (g + 1) % 3
        dk_ref[...] = acck[done].astype(bf16)
        dv_ref[...] = accv[done].astype(bf16)

    last = n_groups - 1
    width = 128 * BWD_PAIRS
    qspec = pl.BlockSpec((QG, width), lambda hp, g: (jnp.minimum(g, last), hp))
    kout = pl.BlockSpec((QG, width), lambda hp, g: (jnp.clip(g - 2, 0, last), hp))
    dbspec = pl.BlockSpec((2 * BWD_PAIRS, QG, DB_COLS), lambda hp, g: (hp, 0, 0))
    lspec = pl.BlockSpec((QG, 128), lambda hp, g: (jnp.minimum(g, last), 0))
    return _pcall_carry(
        body, comm, n_in=10, n_out=4, name=f"attn_bwd_{layer}", grid=(N_HEADS // (2 * BWD_PAIRS), n_groups + 2),
        out_shape=(jax.ShapeDtypeStruct((T, E_MIX), bf16),) * 3 + (jax.ShapeDtypeStruct((N_HEADS, QG, DB_COLS), f32),),
        in_specs=[qspec] + _key_specs(n_groups, BWD_PAIRS, 1) + _key_specs(n_groups, BWD_PAIRS, 2)
        + [qspec, _bias_spec(BWD_PAIRS), lspec],
        out_specs=(qspec, kout, kout, dbspec),
        scratch_shapes=[pltpu.VMEM((3, QG, width), f32), pltpu.VMEM((3, QG, width), f32)],
        compiler_params=_params("arbitrary", "arbitrary"))(h, h, h, h, h, h, h, dmix, bias, lse)


def _conv_bwd(h, w, dmix, layer):
    T = h.shape[0]
    tm = 512
    n_tiles = T // tm

    def body(bg_ref, cg_ref, u_ref, cgp_ref, up_ref, dy_ref, bgn_ref, dyn_ref, w_ref, dbg_ref, dcg_ref, du_ref, dw_ref):
        i = pl.program_id(0)

        @pl.when(i == 0)
        def _():
            dw_ref[...] = jnp.zeros_like(dw_ref)

        first = (i == 0).astype(f32)
        final = (i == n_tiles - 1).astype(f32)
        bg, cg, u = bg_ref[...].astype(f32), cg_ref[...].astype(f32), u_ref[...].astype(f32)
        dy = dy_ref[...].astype(f32)
        cu = cg * u
        p6 = _halo_rows(cgp_ref, 14) * _halo_rows(up_ref, 14) * (1.0 - first)
        p7 = _halo_rows(cgp_ref, 15) * _halo_rows(up_ref, 15) * (1.0 - first)
        conv, r1, r2 = _conv_taps(cu, p6, p7, w_ref)
        dbg_ref[...] = (dy * conv).astype(bf16)
        dc = dy * bg
        n0 = _halo_rows(dyn_ref, 0) * _halo_rows(bgn_ref, 0) * (1.0 - final)
        n1 = _halo_rows(dyn_ref, 1) * _halo_rows(bgn_ref, 1) * (1.0 - final)
        row = lax.broadcasted_iota(jnp.int32, dc.shape, 0)
        f1 = jnp.where(row == tm - 1, n0, pltpu.roll(dc, tm - 1, axis=0))
        f2 = jnp.where(row == tm - 2, n0, jnp.where(row == tm - 1, n1, pltpu.roll(dc, tm - 2, axis=0)))
        dcu = w_ref[2:3, :] * dc + w_ref[1:2, :] * f1 + w_ref[0:1, :] * f2
        dcg_ref[...] = (dcu * u).astype(bf16)
        du_ref[...] = (dcu * cg).astype(bf16)
        dw_ref[0:1, :] += jnp.sum(dc * r2, axis=0, keepdims=True)
        dw_ref[1:2, :] += jnp.sum(dc * r1, axis=0, keepdims=True)
        dw_ref[2:3, :] += jnp.sum(dc * cu, axis=0, keepdims=True)

    tile = lambda slab: pl.BlockSpec((tm, E_MIX), lambda i: (i, slab))
    prev = lambda slab: pl.BlockSpec((16, E_MIX), lambda i: (jnp.maximum(i * (tm // 16) - 1, 0), slab))
    nxt = lambda slab: pl.BlockSpec((16, E_MIX), lambda i: (jnp.minimum((i + 1) * (tm // 16), T // 16 - 1), slab))
    return _pcall(
        body, name=f"conv_bwd_{layer}", grid=(n_tiles,),
        out_shape=(jax.ShapeDtypeStruct((T, E_MIX), bf16),) * 3 + (jax.ShapeDtypeStruct((8, E_MIX), f32),),
        in_specs=[tile(0), tile(1), tile(2), prev(1), prev(2), tile(0), nxt(0), nxt(0),
                  pl.BlockSpec((CONV_W, E_MIX), lambda i: (0, 0))],
        out_specs=(tile(0), tile(0), tile(0), pl.BlockSpec((8, E_MIX), lambda i: (0, 0))),
        compiler_params=_params("arbitrary"))(h, h, h, h, h, dmix, h, dmix, w)


def _inproj_bwd_dx(da, db, dc, dhb, dxp, win, layer, below=None, comm=None):
    T = dxp.shape[0]
    tm = 512

    def body(da_ref, db_ref, dc_ref, dhb_ref, dxp_ref, w_ref, *rest):
        dh = jnp.concatenate([da_ref[...], db_ref[...], dc_ref[...], dhb_ref[...]], axis=1)
        acc = dxp_ref[...]
        for j in range(N_CHIPS):
            acc = acc + lax.dot_general(dh[:, j * W_IN_COLS:(j + 1) * W_IN_COLS], w_ref[j], NT, preferred_element_type=f32)
        if below is None:
            rest[0][...] = acc
        else:
            xh_ref, rs_ref, g_ref, dob_ref, dxo_ref, dg_ref, db_out = rest

            @pl.when(pl.program_id(0) == 0)
            def _():
                dg_ref[...] = jnp.zeros_like(dg_ref)
                db_out[...] = jnp.zeros_like(db_out)

            _ln_bwd(acc, xh_ref[...], rs_ref[...], g_ref[...], dob_ref, dxo_ref, dg_ref, db_out)

    tile = lambda w: pl.BlockSpec((tm, w), lambda i: (i, 0))
    const = lambda r, c: pl.BlockSpec((r, c), lambda i: (0, 0))
    in_specs = [tile(E_MIX), tile(E_MIX), tile(E_MIX), tile(E_MEM + E_BRANCH), tile(D_MODEL),
                pl.BlockSpec((N_CHIPS, D_MODEL, W_IN_COLS), lambda i: (0, 0, 0), pipeline_mode=pl.Buffered(1))]
    if below is None:
        return _pcall_carry(
            body, comm, n_in=6, n_out=1, name=f"inproj_bwd_dx_{layer}", grid=(T // tm,),
            out_shape=(jax.ShapeDtypeStruct((T, D_MODEL), f32),), in_specs=in_specs, out_specs=(tile(D_MODEL),),
            compiler_params=_params("arbitrary"))(da, db, dc, dhb, dxp, win)
    return _pcall_carry(
        body, comm, n_in=9, n_out=4, name=f"inproj_bwd_dx_{layer}", grid=(T // tm,),
        out_shape=(jax.ShapeDtypeStruct((T, D_MODEL), bf16), jax.ShapeDtypeStruct((T, D_MODEL), f32),
                   jax.ShapeDtypeStruct((8, D_MODEL), f32), jax.ShapeDtypeStruct((8, D_MODEL), f32)),
        in_specs=in_specs + [tile(D_MODEL), tile(1), const(1, D_MODEL)],
        out_specs=(tile(D_MODEL), tile(D_MODEL), const(8, D_MODEL), const(8, D_MODEL)),
        compiler_params=_params("arbitrary"))(da, db, dc, dhb, dxp, win, *below)


DW_COLS = 256


def _inproj_bwd_dw(da, db, dc, dhb, xt, layer, comm=None):
    T = xt.shape[1]
    widths = (E_MIX, E_MIX, E_MIX, E_MEM + E_BRANCH)
    starts = [sum(widths[:a]) // DW_COLS for a in range(4)]
    counts = [w // DW_COLS for w in widths]
    per_block = W_IN_COLS // DW_COLS

    def body(x_ref, da_ref, db_ref, dc_ref, dhb_ref, o_ref):
        t = pl.program_id(0)
        for a, ref in enumerate((da_ref, db_ref, dc_ref, dhb_ref)):
            @pl.when((t >= starts[a]) & (t < starts[a] + counts[a]))
            def _():
                o_ref[...] = jnp.dot(x_ref[...], ref[...], preferred_element_type=f32).astype(bf16)

    def source(a):
        return pl.BlockSpec((T, DW_COLS), lambda t: (0, jnp.clip(t - starts[a], 0, counts[a] - 1)))

    (dw,), carried = _pcall_carry(
        body, comm, n_in=5, n_out=1, name=f"inproj_bwd_dw_{layer}", grid=(N_IN // DW_COLS,),
        out_shape=(jax.ShapeDtypeStruct((N_CHIPS, D_MODEL, W_IN_COLS), bf16),),
        in_specs=[pl.BlockSpec((D_MODEL, T), lambda t: (0, 0), pipeline_mode=pl.Buffered(1))] + [source(a) for a in range(4)],
        out_specs=(pl.BlockSpec((None, D_MODEL, DW_COLS), lambda t: (t // per_block, 0, t % per_block)),),
        compiler_params=_params("arbitrary"))(xt, da, db, dc, dhb)
    return dw, carried


def _kv_mem_bwd(memb, dkv, layer):
    def body(m_ref, d_ref, o_ref):
        o_ref[...] = lax.dot_general(m_ref[...], d_ref[...].astype(bf16), TN, preferred_element_type=f32).astype(bf16)

    return _pcall(body, name=f"kv_mem_bwd_{layer}", out_shape=jax.ShapeDtypeStruct((D_MODEL, 2 * E_MEM), bf16),
                  compiler_params=pltpu.CompilerParams(vmem_limit_bytes=VMEM_LIMIT))(memb, dkv)


def _adamw(w, g, m, v, name):
    shape = w.shape
    cols = shape[-1]
    rows = w.size // cols
    args = [a.reshape(rows, cols) for a in (w, g, m, v)]
    br = 256 if rows % 256 == 0 and rows > 256 else rows

    def body(w_ref, g_ref, m_ref, v_ref, go_ref, d_ref, nm_ref, nv_ref):
        gg = g_ref[...]
        nm = ADAM_B1 * m_ref[...] + (1.0 - ADAM_B1) * gg
        nv = ADAM_B2 * v_ref[...] + (1.0 - ADAM_B2) * jnp.square(gg)
        m_hat = nm / (1.0 - ADAM_B1 ** ADAM_STEP)
        v_hat = nv / (1.0 - ADAM_B2 ** ADAM_STEP)
        go_ref[...] = gg
        d_ref[...] = -ADAM_LR * (m_hat / (jnp.sqrt(v_hat) + ADAM_EPS) + ADAM_WD * w_ref[...])
        nm_ref[...] = nm
        nv_ref[...] = nv

    spec = pl.BlockSpec((br, cols), lambda i: (i, 0))
    outs = _pcall(body, name=name, grid=(rows // br,), out_shape=(jax.ShapeDtypeStruct((rows, cols), f32),) * 4,
                  in_specs=[spec] * 4, out_specs=(spec,) * 4, compiler_params=_params("arbitrary"))(*args)
    return tuple(o.reshape(shape) for o in outs)


def kernel(x, mem, w_in, w_mem_kv, w_out, rel_bias, conv_w, ln_g, ln_b, loss_target, m_w_in, m_w_mem_kv, m_w_out, m_rel_bias, m_conv_w, m_ln_g, m_ln_b, v_w_in, v_w_mem_kv, v_w_out, v_rel_bias, v_conv_w, v_ln_g, v_ln_b):
    T = x.shape[1]
    x0 = x.reshape(T, D_MODEL)
    target = loss_target.reshape(T, D_MODEL)
    memb = mem.reshape(N_MEM, D_MODEL).astype(bf16)
    chip = 2 * lax.axis_index("x") + lax.axis_index("y")
    core = lax.axis_index("c")
    chip_arr = jnp.reshape(chip, (1,)).astype(jnp.int32)
    core_arr = jnp.reshape(core, (1,)).astype(jnp.int32)

    place = jnp.concatenate([chip_arr, core_arr])
    tables = jnp.pad(rel_bias, ((0, 0), (0, 0), (0, N_REL_PAD - N_REL)))

    shards = [w_in.astype(bf16), w_mem_kv.astype(bf16), w_out.astype(bf16)]
    biases = {}
    biases[0], near = _bias_expand(tables[0], 0, _gather_ici(shards, 0, conv_w, peers=(0, 1)))
    biases[2], (*arrived, cw_g) = _bias_expand(tables[1], 2, _gather_ici(shards, 0, conv_w, peers=(2,), own=False, into=near))
    gathered = {0: _comm_call(_gather_d2d(arrived), "gather_d2d_0")}
    conv_full = jnp.transpose(cw_g, (1, 2, 0, 3)).reshape(DEPTH // 2, CONV_W, E_MIX)

    xs, xbs, xts, hs, mixes, kvs, kepts, xhats, rstds, lses = [x0], [x0], [], [], [], [], [], [], [], {}
    for layer in range(DEPTH):
        win, wkv, wout = gathered[layer]
        more = layer + 1 < DEPTH
        attention = layer % 2 == 0
        h, xbs[layer], xt0, mix, arrived = _inproj(xbs[layer], win, layer, _gather_ici(shards, layer + 1) if more else None,
                                                   None if attention else conv_full[layer // 2])
        if layer == 0:
            xts.append(xt0)
        passing = _gather_d2d(list(arrived)) if more else None
        if attention:
            mix, lses[layer], done = _attn_fwd(h, biases[layer], layer, passing)
        kv = _kv_mem(memb, wkv.reshape(D_MODEL, 2 * E_MEM))
        result = _post_fwd(h, mix, kv, wout.reshape(E_BRANCH, D_MODEL), xs[layer], ln_g[layer][None, :],
                           ln_b[layer][None, :], layer, None if more else target, None if attention else passing)
        if more:
            (xn, xnb, xnt, xhat, rstd, *kept), done = (result[0], done) if attention else result
            xs.append(xn); xbs.append(xnb); xts.append(xnt); xhats.append(xhat); rstds.append(rstd)
            gathered[layer + 1] = list(done)
        else:
            lsum, dob, dxp, dg_last, db_last, *kept = result
        hs.append(h); mixes.append(mix); kvs.append(kv); kepts.append(kept)

    dgs, dbs, dconvs, dtables = [None] * DEPTH, [None] * DEPTH, [None] * (DEPTH // 2), [None] * ((DEPTH + 1) // 2)
    dgs[DEPTH - 1], dbs[DEPTH - 1] = dg_last, db_last
    loss = lax.psum(lsum[0, 0], ("x", "y", "c")) * (0.5 / D_MODEL)
    finals = [None, None, None]
    above = None
    for layer in reversed(range(DEPTH)):
        h = hs[layer]
        win, wkv, wout = gathered[layer]
        (dhb, dmix, dkv, dwo), from_sibling = _post_bwd(
            dob, h, mixes[layer], kvs[layer], wout.reshape(E_BRANCH, D_MODEL), kepts[layer], layer,
            _sibling_exchange(above) if above else None)
        sums = [_pair_sum(g, r, core_arr, layer + 1) for g, r in zip(above, from_sibling)] if above else None
        scatter = _chip_scatter(sums) if above else None
        if layer % 2 == 0:
            (da, db, dc, dbias), from_chips = _attn_bwd(h, biases[layer], dmix, lses[layer], layer, scatter)
            dtables[layer // 2] = _bias_reduce(dbias, layer)
        else:
            da, db, dc, dconvs[layer // 2] = _conv_bwd(h, conv_full[layer // 2], dmix, layer)
        if layer > 0:
            (dob_below, dxp_below, dgs[layer - 1], dbs[layer - 1]), landed = _inproj_bwd_dx(
                da, db, dc, dhb, dxp, win, layer, (xhats[layer - 1], rstds[layer - 1], ln_g[layer - 1][None, :]),
                scatter if layer % 2 == 1 else None)
            from_chips = landed if layer % 2 == 1 else from_chips
        share = None
        if above:
            finals = [_chip_sum(s, r, place, layer + 1, f) for s, r, f in zip(sums, from_chips, finals)]
            share = _sibling_share(finals, layer + 1)
        if layer == 0:
            pad8 = lambda a: jnp.pad(a, ((0, 8 - a.shape[0]), (0, 0)))
            small_mine = jnp.concatenate(dgs + dbs + dconvs + [pad8(t.reshape(-1, D_MODEL)) for t in dtables], axis=0)
            share = _both(share, _small_exchange(small_mine))
        g_win, shared = _inproj_bwd_dw(da, db, dc, dhb, xts[layer], layer, share)
        if layer == 0:
            *shared, small_slots = shared
        finals = list(shared) if above else finals
        above = [g_win, _kv_mem_bwd(memb, dkv, layer).reshape(N_CHIPS, W_KV_ROWS, 2 * E_MEM),
                 dwo.reshape(N_CHIPS, W_OUT_ROWS, D_MODEL)]
        if layer > 0:
            dob, dxp = dob_below, dxp_below
    from_sibling = _comm_call(_sibling_exchange(above), "sibling_exchange_0")
    sums = [_pair_sum(g, r, core_arr, 0) for g, r in zip(above, from_sibling)]
    (dx,), from_chips = _inproj_bwd_dx(da, db, dc, dhb, dxp, win, 0, None, _chip_scatter(sums))
    finals = [_chip_sum(s, r, place, 0, f) for s, r, f in zip(sums, from_chips, finals)]
    grad_w_in, grad_w_mem_kv, grad_w_out = _comm_call(_sibling_share(finals, 0), "sibling_share_0")
    grad_x = dx.reshape(1, T, D_MODEL)

    device_arr = jnp.reshape(2 * chip + core, (1,)).astype(jnp.int32)
    small = _small_sum(small_slots, small_mine, device_arr)
    grad_ln_g = jnp.stack([small[8 * l] for l in range(DEPTH)])
    grad_ln_b = jnp.stack([small[8 * (DEPTH + l)] for l in range(DEPTH)])
    conv_all = jnp.stack([small[8 * (2 * DEPTH + a):8 * (2 * DEPTH + a) + CONV_W] for a in range(DEPTH // 2)])
    grad_conv_w = lax.dynamic_slice_in_dim(conv_all, chip * (E_MIX // N_CHIPS), E_MIX // N_CHIPS, axis=2)
    t0 = 8 * (2 * DEPTH + DEPTH // 2)
    grad_rel_bias = jnp.stack([small[t0 + 8 * a:t0 + 8 * a + 6].reshape(N_HEADS, N_REL_PAD)[:, :N_REL]
                               for a in range((DEPTH + 1) // 2)])

    grads = [grad_w_in, grad_w_mem_kv, grad_w_out, grad_rel_bias, grad_conv_w, grad_ln_g, grad_ln_b]
    weights = [w_in, w_mem_kv, w_out, rel_bias, conv_w, ln_g, ln_b]
    moms = [m_w_in, m_w_mem_kv, m_w_out, m_rel_bias, m_conv_w, m_ln_g, m_ln_b]
    vels = [v_w_in, v_w_mem_kv, v_w_out, v_rel_bias, v_conv_w, v_ln_g, v_ln_b]
    names = ["w_in", "w_mem_kv", "w_out", "rel_bias", "conv_w", "ln_g", "ln_b"]
    upd = [_adamw(w, g, m, v, f"adamw_{n}") for w, g, m, v, n in zip(weights, grads, moms, vels, names)]
    grads, deltas, new_m, new_v = zip(*upd)
    return (loss, grad_x, *grads, *deltas, *new_m, *new_v)
```

```python
import functools
import math

import jax
import jax.numpy as jnp
from jax import lax
from jax.experimental import pallas as pl
from jax.experimental.pallas import tpu as pltpu

f32, bf16 = jnp.float32, jnp.bfloat16

D_MODEL = 1024
DEPTH = 4
CHUNK = 64
N_PREV = 8
N_HEADS = 16
HEAD_DIM = 64
E_MIX = 1024
REL_CLIP = 128
N_REL = 2 * REL_CLIP + 1
N_REL_PAD = 384
CONV_W = 3
N_MEM = 256
MEM_HEADS = 4
MEM_HEAD_DIM = 128
E_MEM = 512
E_BRANCH = E_MIX + E_MEM
N_IN = 3 * E_MIX + E_MEM + E_BRANCH
N_CHIPS = 4
W_IN_COLS = N_IN // N_CHIPS
W_KV_ROWS = D_MODEL // N_CHIPS
W_OUT_ROWS = E_BRANCH // N_CHIPS
DN_ALPHA = (2.0 * DEPTH) ** 0.25
LN_EPS = 1e-5
ADAM_LR, ADAM_B1, ADAM_B2, ADAM_EPS, ADAM_WD, ADAM_STEP = 0.001, 0.9, 0.999, 1e-08, 0.01, 10

QG = 4 * CHUNK
KG = QG + N_PREV * CHUNK
DB_COLS = KG // 2
NEG = -1e30
VMEM_LIMIT = 56 * 1024 * 1024

NT = (((1,), (1,)), ((), ()))
TN = (((0,), (0,)), ((), ()))
MESH = pl.DeviceIdType.MESH
ANY = pl.BlockSpec(memory_space=pl.ANY)


def _pcall(body, **kw):
    return pl.pallas_call(body, **kw)


def _params(*sem):
    return pltpu.CompilerParams(dimension_semantics=sem, vmem_limit_bytes=VMEM_LIMIT)


def _silu_parts(z):
    sig = 1.0 / (1.0 + jnp.exp(-z))
    return z * sig, sig


class _Comm:
    def __init__(self, inputs, out_shapes, aliases, n_sems, copies):
        self.inputs, self.out_shapes, self.aliases, self.n_sems, self.copies = inputs, out_shapes, aliases, n_sems, copies

    def start(self, cin, cout, send, recv):
        for cp in self.copies(cin, cout, send, recv)[0]:
            cp.start()

    def wait(self, cin, cout, send, recv):
        sends, recvs = self.copies(cin, cout, send, recv)
        for cp in recvs:
            cp.wait_recv()
        for cp in sends:
            cp.wait_send()


def _pcall_carry(body, comm, *, n_in, n_out, **kw):
    if comm is None:
        return lambda *args: (_pcall(body, **kw)(*args), ())
    grid = kw["grid"]
    k_in, k_out = len(comm.inputs), len(comm.out_shapes)

    def carried(*refs):
        ins, cin = refs[:n_in], refs[n_in:n_in + k_in]
        outs = refs[n_in + k_in:n_in + k_in + n_out]
        cout = refs[n_in + k_in + n_out:n_in + k_in + n_out + k_out]
        scratch, send, recv = refs[n_in + k_in + n_out + k_out:-2], refs[-2], refs[-1]
        ids = [pl.program_id(a) for a in range(len(grid))]
        first = functools.reduce(jnp.logical_and, [i == 0 for i in ids])
        last = functools.reduce(jnp.logical_and, [i == n - 1 for i, n in zip(ids, grid)])

        @pl.when(first)
        def _():
            comm.start(cin, cout, send, recv)

        body(*ins, *outs, *scratch)

        @pl.when(last)
        def _():
            comm.wait(cin, cout, send, recv)

    kw = dict(kw)
    kw["in_specs"] = list(kw["in_specs"]) + [ANY] * k_in
    kw["out_specs"] = tuple(kw["out_specs"]) + (ANY,) * k_out
    kw["out_shape"] = tuple(kw["out_shape"]) + tuple(comm.out_shapes)
    kw["scratch_shapes"] = list(kw.get("scratch_shapes", ())) + [pltpu.SemaphoreType.DMA((comm.n_sems,))] * 2
    aliases = dict(kw.get("input_output_aliases", {}))
    aliases.update({n_in + ci: n_out + co for ci, co in comm.aliases.items()})
    kw["input_output_aliases"] = aliases

    def run(*args):
        res = _pcall(carried, **kw)(*args, *comm.inputs)
        return res[:n_out], res[n_out:]

    return run


def _comm_call(comm, name):
    k_in = len(comm.inputs)

    def body(*refs):
        cin, cout, send, recv = refs[:k_in], refs[k_in:-2], refs[-2], refs[-1]
        comm.start(cin, cout, send, recv)
        comm.wait(cin, cout, send, recv)

    return _pcall(body, name=name, out_shape=tuple(comm.out_shapes), in_specs=[ANY] * k_in,
                  out_specs=(ANY,) * len(comm.out_shapes), input_output_aliases=dict(comm.aliases),
                  scratch_shapes=[pltpu.SemaphoreType.DMA((comm.n_sems,))] * 2)(*comm.inputs)


def _place():
    x, y, c = lax.axis_index("x"), lax.axis_index("y"), lax.axis_index("c")
    return x, y, c, 2 * x + y, (x, y, 1 - c), [(1 - x, y), (x, 1 - y), (1 - x, 1 - y)]


def _rcopy(send, recv, k, src, dst, to):
    return pltpu.make_async_remote_copy(src_ref=src, dst_ref=dst, send_sem=send.at[k], recv_sem=recv.at[k],
                                        device_id=to, device_id_type=MESH)


def _half(ref_rows, core):
    return pl.ds(core * (ref_rows // 2), ref_rows // 2)


def _gather_ici(shards, layer, extra=None, peers=(0, 1, 2), own=True, into=None):
    extras = [] if extra is None else [extra]
    n = len(shards)

    def copies(cin, cout, send, recv):
        x, y, c, me, sibling, chips = _place()
        sends, recvs = [], []
        for a in range(n + len(extras)):
            s, g = cin[a], cout[a]
            whole = a >= n
            src = s if whole else s.at[layer]
            if own:
                sends.append(_rcopy(send, recv, 4 * a, src, g.at[me], sibling))
                recvs.append(_rcopy(send, recv, 4 * a, src, g.at[me], sibling))
            for p in peers:
                px, py = chips[p]
                if whole:
                    sends.append(_rcopy(send, recv, 4 * a + 1 + p, src, g.at[me], (px, py, c)))
                    recvs.append(_rcopy(send, recv, 4 * a + 1 + p, src, g.at[2 * px + py], (px, py, c)))
                else:
                    mine = _half(s.shape[1], c)
                    sends.append(_rcopy(send, recv, 4 * a + 1 + p, s.at[layer, mine], g.at[me, mine], (px, py, c)))
                    recvs.append(_rcopy(send, recv, 4 * a + 1 + p, s.at[layer, mine], g.at[2 * px + py, mine], (px, py, c)))
        return sends, recvs

    sources = list(shards) + extras
    if into is None:
        out_shapes = [jax.ShapeDtypeStruct((N_CHIPS,) + s.shape[1:], s.dtype) for s in shards]
        out_shapes += [jax.ShapeDtypeStruct((N_CHIPS,) + e.shape, e.dtype) for e in extras]
        return _Comm(sources, out_shapes, {}, 4 * len(sources), copies)
    return _Comm(sources + list(into), [jax.ShapeDtypeStruct(g.shape, g.dtype) for g in into],
                 {len(sources) + a: a for a in range(len(into))}, 4 * len(sources), copies)


def _gather_d2d(gathered):
    n = len(gathered)

    def copies(cin, cout, send, recv):
        x, y, c, me, sibling, chips = _place()
        sends, recvs = [], []
        for a in range(n):
            g = cout[a]
            rows = g.shape[1]
            for p, (px, py) in enumerate(chips):
                mine, theirs = g.at[2 * px + py, _half(rows, c)], g.at[2 * px + py, _half(rows, 1 - c)]
                sends.append(_rcopy(send, recv, 3 * a + p, mine, mine, sibling))
                recvs.append(_rcopy(send, recv, 3 * a + p, theirs, theirs, sibling))
        return sends, recvs

    return _Comm(list(gathered), [jax.ShapeDtypeStruct(g.shape, g.dtype) for g in gathered],
                 {a: a for a in range(n)}, 3 * n, copies)


class _SemView:
    def __init__(self, ref, base):
        self.ref, self.base, self.at = ref, base, self

    def __getitem__(self, k):
        return self.ref.at[self.base + k]


def _both(a, b):
    ka, ma = len(a.inputs), len(a.out_shapes)

    def copies(cin, cout, send, recv):
        sa, ra = a.copies(cin[:ka], cout[:ma], send, recv)
        sb, rb = b.copies(cin[ka:], cout[ma:], _SemView(send, a.n_sems), _SemView(recv, a.n_sems))
        return sa + sb, ra + rb

    aliases = dict(a.aliases)
    aliases.update({ka + ci: ma + co for ci, co in b.aliases.items()})
    return _Comm(a.inputs + b.inputs, a.out_shapes + b.out_shapes, aliases, a.n_sems + b.n_sems, copies)


def _small_exchange(buf):
    def copies(cin, cout, send, recv):
        x, y, c, me, sibling, chips = _place()
        sends, recvs = [], []
        for r in range(1, 8):
            px, py, pc = x ^ ((r >> 2) & 1), y ^ ((r >> 1) & 1), c ^ (r & 1)
            sends.append(_rcopy(send, recv, r - 1, cin[0], cout[0].at[2 * me + c], (px, py, pc)))
            recvs.append(_rcopy(send, recv, r - 1, cin[0], cout[0].at[4 * px + 2 * py + pc], (px, py, pc)))
        return sends, recvs

    return _Comm([buf], [jax.ShapeDtypeStruct((8,) + buf.shape, buf.dtype)], {}, 7, copies)


def _small_sum(slots, buf, device_arr):
    rows, cols = buf.shape

    def body(d_ref, s_ref, b_ref, o_ref):
        d = pl.program_id(0)
        val = jnp.where(d == d_ref[0], b_ref[...], s_ref[...])

        @pl.when(d == 0)
        def _():
            o_ref[...] = val

        @pl.when(d > 0)
        def _():
            o_ref[...] += val

    return _pcall(
        body, name="small_sum", out_shape=jax.ShapeDtypeStruct((rows, cols), f32),
        grid_spec=pltpu.PrefetchScalarGridSpec(
            num_scalar_prefetch=1, grid=(8,),
            in_specs=[pl.BlockSpec((None, rows, cols), lambda d, d_ref: (jnp.where(d == d_ref[0], (d + 1) % 8, d), 0, 0)),
                      pl.BlockSpec((rows, cols), lambda d, d_ref: (0, 0))],
            out_specs=pl.BlockSpec((rows, cols), lambda d, d_ref: (0, 0))),
        compiler_params=_params("arbitrary"))(device_arr, slots, buf)


def _sibling_exchange(gs):
    def copies(cin, cout, send, recv):
        x, y, c, me, sibling, chips = _place()
        sends = [_rcopy(send, recv, a, g.at[:, _half(g.shape[1], 1 - c)], r, sibling) for a, (g, r) in enumerate(zip(cin, cout))]
        return sends, sends

    shapes = [jax.ShapeDtypeStruct((N_CHIPS, g.shape[1] // 2, g.shape[2]), g.dtype) for g in gs]
    return _Comm(list(gs), shapes, {}, len(gs), copies)


def _chip_scatter(ss):
    def copies(cin, cout, send, recv):
        x, y, c, me, sibling, chips = _place()
        sends = [_rcopy(send, recv, 3 * a + p, s.at[2 * px + py], r.at[p], (px, py, c))
                 for a, (s, r) in enumerate(zip(cin, cout)) for p, (px, py) in enumerate(chips)]
        return sends, sends

    shapes = [jax.ShapeDtypeStruct((3,) + s.shape[1:], s.dtype) for s in ss]
    return _Comm(list(ss), shapes, {}, 3 * len(ss), copies)


def _sibling_share(fs, layer):
    def copies(cin, cout, send, recv):
        x, y, c, me, sibling, chips = _place()
        sends, recvs = [], []
        for a, f in enumerate(cout):
            mine, theirs = f.at[layer, _half(f.shape[1], c)], f.at[layer, _half(f.shape[1], 1 - c)]
            sends.append(_rcopy(send, recv, a, mine, mine, sibling))
            recvs.append(_rcopy(send, recv, a, theirs, theirs, sibling))
        return sends, recvs

    return _Comm(list(fs), [jax.ShapeDtypeStruct(f.shape, f.dtype) for f in fs], {a: a for a in range(len(fs))},
                 len(fs), copies)


def _sum_rows(rows):
    return next(b for b in (256, 192, 128) if rows % b == 0)


def _pair_sum(g, r, core_arr, layer):
    _, rows, cols = r.shape
    br = _sum_rows(rows)
    nb = rows // br

    def body(c_ref, g_ref, r_ref, o_ref):
        o_ref[...] = (g_ref[...].astype(f32) + r_ref[...].astype(f32)).astype(bf16)

    return _pcall(
        body, name=f"pair_sum_{layer}", out_shape=jax.ShapeDtypeStruct(r.shape, bf16),
        grid_spec=pltpu.PrefetchScalarGridSpec(
            num_scalar_prefetch=1, grid=(N_CHIPS, nb),
            in_specs=[pl.BlockSpec((1, br, cols), lambda j, i, c_ref: (j, c_ref[0] * nb + i, 0)),
                      pl.BlockSpec((1, br, cols), lambda j, i, c_ref: (j, i, 0))],
            out_specs=pl.BlockSpec((1, br, cols), lambda j, i, c_ref: (j, i, 0))),
        compiler_params=_params("arbitrary", "arbitrary"))(core_arr, g, r)


def _chip_sum(s, r, place, layer, final):
    _, rows, cols = s.shape
    br = _sum_rows(rows)
    nb = rows // br

    def body(place_ref, s_ref, r_ref, *rest):
        o_ref = rest[-1]
        acc = s_ref[0].astype(f32)
        for p in range(3):
            acc = acc + r_ref[p].astype(f32)
        o_ref[...] = acc

    carried = [] if final is None else [final]
    return _pcall(
        body, name=f"chip_sum_{layer}", out_shape=jax.ShapeDtypeStruct((DEPTH, 2 * rows, cols), f32),
        grid_spec=pltpu.PrefetchScalarGridSpec(
            num_scalar_prefetch=1, grid=(nb,),
            in_specs=[pl.BlockSpec((1, br, cols), lambda i, place_ref: (place_ref[0], i, 0)),
                      pl.BlockSpec((3, br, cols), lambda i, place_ref: (0, i, 0))] + [ANY] * len(carried),
            out_specs=pl.BlockSpec((None, br, cols), lambda i, place_ref: (layer, place_ref[1] * nb + i, 0))),
        input_output_aliases={3: 0} if carried else {},
        compiler_params=_params("arbitrary"))(place, s, r, *carried)


def _inproj(xin, win, layer, comm=None, conv_w=None):
    T = xin.shape[0]
    tm = 512
    cast = xin.dtype != bf16
    conv = conv_w is not None

    def body(x_ref, w_ref, *rest):
        rest = list(rest)
        cw_ref = rest.pop(0) if conv else None
        o_ref = rest.pop(0)
        xt = x_ref[...].astype(bf16)
        if cast:
            rest.pop(0)[...] = xt
        for j in range(N_CHIPS):
            o_ref[:, j * W_IN_COLS:(j + 1) * W_IN_COLS] = jnp.dot(xt, w_ref[j], preferred_element_type=f32).astype(bf16)
        if conv:
            mix_ref, halo = rest

            @pl.when(pl.program_id(0) == 0)
            def _():
                halo[...] = jnp.zeros_like(halo)

            bg, cg, u = (o_ref[:, s * E_MIX:(s + 1) * E_MIX].astype(f32) for s in range(3))
            cu = cg * u
            out, _, _ = _conv_taps(cu, halo[6:7, :], halo[7:8, :], cw_ref)
            mix_ref[...] = (bg * out).astype(bf16)
            halo[...] = cu[tm - 8:, :]

    tile = pl.BlockSpec((tm, D_MODEL), lambda i: (i, 0))
    outs, carried = _pcall_carry(
        body, comm, n_in=2 + conv, n_out=1 + cast + conv, name=f"inproj_{layer}", grid=(T // tm,),
        out_shape=(jax.ShapeDtypeStruct((T, N_IN), bf16),) + (jax.ShapeDtypeStruct((T, D_MODEL), bf16),) * (cast + conv),
        in_specs=[tile, pl.BlockSpec((N_CHIPS, D_MODEL, W_IN_COLS), lambda i: (0, 0, 0), pipeline_mode=pl.Buffered(1))]
        + [pl.BlockSpec((CONV_W, E_MIX), lambda i: (0, 0))] * conv,
        out_specs=(pl.BlockSpec((tm, N_IN), lambda i: (i, 0)),) + (tile,) * (cast + conv),
        scratch_shapes=[pltpu.VMEM((8, E_MIX), f32)] * conv,
        compiler_params=_params("arbitrary"))(xin, win, *([conv_w] if conv else []))
    return outs[0], (outs[1] if cast else xin), (outs[-1] if conv else None), carried


def _rel_index_rows():
    j = lax.broadcasted_iota(jnp.int32, (N_REL_PAD, KG), 1)
    r = lax.broadcasted_iota(jnp.int32, (N_REL_PAD, KG), 0)
    off = jnp.where(j < KG - 2 * CHUNK, j, j - KG)
    idx = jnp.clip(N_PREV * CHUNK - off, -REL_CLIP, REL_CLIP) + REL_CLIP
    return (idx == r).astype(f32)


def _bias_expand(table_pad, layer, comm=None):
    def body(t_ref, o_ref, row_scr):
        h = pl.program_id(0)

        @pl.when(h == 0)
        def _():
            row_scr[...] = jnp.dot(t_ref[...], _rel_index_rows(), precision=lax.Precision.HIGHEST,
                                   preferred_element_type=f32)

        q = lax.broadcasted_iota(jnp.int32, (QG, KG), 0)
        k = lax.broadcasted_iota(jnp.int32, (QG, KG), 1)
        band = (k // CHUNK >= q // CHUNK) & (k // CHUNK <= q // CHUNK + N_PREV)
        t = jnp.broadcast_to(row_scr[pl.ds(h, 1), :], (QG, KG))
        for b in range(8):
            t = jnp.where(((q >> b) & 1) == 1, pltpu.roll(t, 1 << b, axis=1), t)
        for v in range(3):
            o_ref[v] = jnp.where(band & (k >= (2 - v) * QG), t, NEG)

    (bias,), carried = _pcall_carry(
        body, comm, n_in=1, n_out=1, name=f"bias_expand_{layer}", grid=(N_HEADS,),
        out_shape=(jax.ShapeDtypeStruct((3, N_HEADS, QG, KG), f32),),
        in_specs=[pl.BlockSpec((N_HEADS, N_REL_PAD), lambda h: (0, 0))],
        out_specs=(pl.BlockSpec((3, None, QG, KG), lambda h: (0, h, 0, 0)),),
        scratch_shapes=[pltpu.VMEM((N_HEADS, KG), f32)], compiler_params=_params("arbitrary"))(table_pad)
    return bias, carried


def _bias_reduce(dbias, layer):
    def body(d_ref, o_ref, row_scr):
        q = lax.broadcasted_iota(jnp.int32, (QG, DB_COLS), 0)
        k = lax.broadcasted_iota(jnp.int32, (QG, DB_COLS), 1)
        for h in range(N_HEADS):
            t = jnp.where(k > q, d_ref[h], 0.0)
            for b in range(8):
                t = jnp.where(((q >> b) & 1) == 1, pltpu.roll(t, DB_COLS - (1 << b), axis=1), t)
            row_scr[h:h + 1, :] = jnp.sum(t, axis=0, keepdims=True)
        r = lax.broadcasted_iota(jnp.int32, (N_REL_PAD, DB_COLS), 0)
        off = lax.broadcasted_iota(jnp.int32, (N_REL_PAD, DB_COLS), 1)
        own = (off >= 1) & (off < REL_CLIP + CHUNK)
        sel = jnp.where(own & (r == 2 * REL_CLIP - off), 1.0, 0.0) - jnp.where(own & (r == 2 * REL_CLIP), 1.0, 0.0)
        o_ref[...] = lax.dot_general(row_scr[...], sel, NT, precision=lax.Precision.HIGHEST, preferred_element_type=f32)

    return _pcall(body, name=f"bias_reduce_{layer}", out_shape=jax.ShapeDtypeStruct((N_HEADS, N_REL_PAD), f32),
                  scratch_shapes=[pltpu.VMEM((N_HEADS, DB_COLS), f32)],
                  compiler_params=pltpu.CompilerParams(vmem_limit_bytes=VMEM_LIMIT))(dbias)


FWD_PAIRS = 8
BWD_PAIRS = 4


def _key_specs(n_groups, npairs, slab):
    per_slab = E_MIX // (128 * npairs)
    return [pl.BlockSpec((QG, 128 * npairs), functools.partial(
        lambda hp, g, jj: (jnp.clip(g - 2 + jj, 0, n_groups - 1), slab * per_slab + hp), jj=jj)) for jj in range(3)]


def _bias_spec(npairs):
    return pl.BlockSpec((None, 2 * npairs, QG, KG), lambda hp, g: (jnp.minimum(g, 2), hp, 0, 0))


def _attn_fwd(h, bias, layer, comm=None):
    T = h.shape[0]
    n_groups = T // QG
    scale = 1.0 / math.sqrt(HEAD_DIM)

    def body(q_ref, k0, k1, k2, v0, v1, v2, b_ref, o_ref, lse_ref):
        lane = lax.broadcasted_iota(jnp.int32, (1, 128), 1)
        ones = jnp.ones((KG, 128), bf16)
        lse = jnp.zeros((QG, 128), f32)
        for pp in range(FWD_PAIRS):
            cs = slice(pp * 128, (pp + 1) * 128)
            q2 = q_ref[:, cs] * scale
            kc = jnp.concatenate([k0[:, cs], k1[:, cs], k2[:, cs]], axis=0)
            vc = jnp.concatenate([jnp.concatenate([v0[:, cs], v1[:, cs], v2[:, cs]], axis=0), ones], axis=1)
            outs = []
            for hh in range(2):
                qm = jnp.where(lane // HEAD_DIM == hh, q2, jnp.zeros_like(q2))
                s = lax.dot_general(qm, kc, NT, preferred_element_type=f32) + b_ref[2 * pp + hh]
                m = jnp.max(s, axis=1, keepdims=True)
                ol = jnp.dot(jnp.exp(s - m).astype(bf16), vc, preferred_element_type=f32)
                outs.append(ol[:, :128] / ol[:, 128:])
                lse = jnp.where(lane == 2 * pp + hh, m + jnp.log(ol[:, 128:]), lse)
            o_ref[:, cs] = jnp.where(lane // HEAD_DIM == 0, outs[0], outs[1]).astype(bf16)
        lse_ref[...] = lse

    (mix, lse), carried = _pcall_carry(
        body, comm, n_in=8, n_out=2, name=f"attn_fwd_{layer}", grid=(N_HEADS // (2 * FWD_PAIRS), n_groups),
        out_shape=(jax.ShapeDtypeStruct((T, E_MIX), bf16), jax.ShapeDtypeStruct((T, 128), f32)),
        in_specs=[pl.BlockSpec((QG, 128 * FWD_PAIRS), lambda hp, g: (g, hp))] + _key_specs(n_groups, FWD_PAIRS, 1)
        + _key_specs(n_groups, FWD_PAIRS, 2) + [_bias_spec(FWD_PAIRS)],
        out_specs=(pl.BlockSpec((QG, 128 * FWD_PAIRS), lambda hp, g: (g, hp)), pl.BlockSpec((QG, 128), lambda hp, g: (g, 0))),
        compiler_params=_params("arbitrary", "arbitrary"))(h, h, h, h, h, h, h, bias)
    return mix, lse, carried


def _halo_rows(ref, r):
    return ref[r:r + 1, :].astype(f32)


def _conv_taps(cu, p6, p7, w_ref):
    row = lax.broadcasted_iota(jnp.int32, cu.shape, 0)
    r1 = jnp.where(row == 0, p7, pltpu.roll(cu, 1, axis=0))
    r2 = jnp.where(row == 0, p6, jnp.where(row == 1, p7, pltpu.roll(cu, 2, axis=0)))
    return w_ref[2:3, :] * cu + w_ref[1:2, :] * r1 + w_ref[0:1, :] * r2, r1, r2


def _kv_mem(memb, wkv):
    def body(m_ref, w_ref, o_ref):
        o_ref[...] = jnp.dot(m_ref[...], w_ref[...], preferred_element_type=f32).astype(bf16)

    return _pcall(body, name="kv_mem", out_shape=jax.ShapeDtypeStruct((N_MEM, 2 * E_MEM), bf16),
                  compiler_params=pltpu.CompilerParams(vmem_limit_bytes=VMEM_LIMIT))(memb, wkv)


def _mem_probs(qm_ref, kv_ref, hh):
    qh = qm_ref[:, hh * MEM_HEAD_DIM:(hh + 1) * MEM_HEAD_DIM]
    kh = kv_ref[:, hh * MEM_HEAD_DIM:(hh + 1) * MEM_HEAD_DIM]
    vh = kv_ref[:, E_MEM + hh * MEM_HEAD_DIM:E_MEM + (hh + 1) * MEM_HEAD_DIM]
    s = lax.dot_general(qh, kh, NT, preferred_element_type=f32) * (1.0 / math.sqrt(MEM_HEAD_DIM))
    e = jnp.exp(s - jnp.max(s, axis=1, keepdims=True))
    return e / jnp.sum(e, axis=1, keepdims=True), qh, kh, vh


def _h_tail_specs(tm):
    return [pl.BlockSpec((tm, E_MEM), functools.partial(lambda i, cb: (i, cb), cb=cb)) for cb in (6, 7, 8, 9)]


def _ln_bwd(dy, xhat, rstd, g, dob_ref, dxp_ref, dg_ref, db_ref):
    dg_ref[0:1, :] += jnp.sum(dy * xhat, axis=0, keepdims=True)
    db_ref[0:1, :] += jnp.sum(dy, axis=0, keepdims=True)
    gx = dy * g
    dr = rstd * (gx - jnp.mean(gx, axis=1, keepdims=True) - xhat * jnp.mean(gx * xhat, axis=1, keepdims=True))
    dxp_ref[...] = DN_ALPHA * dr
    dob_ref[...] = dr.astype(bf16)


def _post_fwd(h, mix, kv, wout, x, g, b, layer, target=None, comm=None):
    T = x.shape[0]
    tm = 512

    def body(qm_ref, z0, z1, z2, mix_ref, kv_ref, w_ref, x_ref, g_ref, b_ref, *rest):
        *rest, ps_ref, ms_ref = rest
        mems = []
        for hh in range(MEM_HEADS):
            p, _, _, vh = _mem_probs(qm_ref, kv_ref, hh)
            pb = p.astype(bf16)
            ps_ref[:, hh * N_MEM:(hh + 1) * N_MEM] = pb
            mems.append(jnp.dot(pb, vh, preferred_element_type=f32))
        memb = jnp.concatenate(mems, axis=1).astype(bf16)
        ms_ref[...] = memb
        z = jnp.concatenate([z0[...], z1[...], z2[...]], axis=1)
        one = jnp.ones((), bf16)
        y = jnp.concatenate([mix_ref[...], memb], axis=1) * (z * (one / (one + jnp.exp(-z))))
        out = jnp.dot(y, w_ref[...], preferred_element_type=f32)
        r = DN_ALPHA * x_ref[...] + out
        mu = jnp.mean(r, axis=1, keepdims=True)
        var = jnp.mean(jnp.square(r - mu), axis=1, keepdims=True)
        rstd = lax.rsqrt(var + LN_EPS)
        xhat = (r - mu) * rstd
        xn = xhat * g_ref[...] + b_ref[...]
        if target is None:
            xn_ref, xb_ref, xh_ref, rs_ref = rest
            xn_ref[...] = xn
            xb_ref[...] = xn.astype(bf16)
            xh_ref[...] = xhat
            rs_ref[...] = rstd
        else:
            t_ref, l_ref, dob_ref, dxp_ref, dg_ref, db_ref = rest

            @pl.when(pl.program_id(0) == 0)
            def _():
                l_ref[...] = jnp.zeros_like(l_ref)
                dg_ref[...] = jnp.zeros_like(dg_ref)
                db_ref[...] = jnp.zeros_like(db_ref)

            err = xn - t_ref[...]
            l_ref[...] += jnp.sum(jnp.square(err))
            _ln_bwd(err * (1.0 / D_MODEL), xhat, rstd, g_ref[...], dob_ref, dxp_ref, dg_ref, db_ref)

    tile = lambda w: pl.BlockSpec((tm, w), lambda i: (i, 0))
    const = lambda r, c: pl.BlockSpec((r, c), lambda i: (0, 0))
    resident = lambda r, c: pl.BlockSpec((r, c), lambda i: (0, 0), pipeline_mode=pl.Buffered(1))
    in_specs = _h_tail_specs(tm) + [tile(E_MIX), resident(N_MEM, 2 * E_MEM), resident(E_BRANCH, D_MODEL), tile(D_MODEL),
                                    const(1, D_MODEL), const(1, D_MODEL)]
    kept = (jax.ShapeDtypeStruct((T, MEM_HEADS * N_MEM), bf16), jax.ShapeDtypeStruct((T, E_MEM), bf16))
    kept_specs = (tile(MEM_HEADS * N_MEM), tile(E_MEM))
    if target is None:
        return _pcall_carry(
            body, comm, n_in=10, n_out=6, name=f"post_fwd_{layer}", grid=(T // tm,),
            out_shape=(jax.ShapeDtypeStruct((T, D_MODEL), f32), jax.ShapeDtypeStruct((T, D_MODEL), bf16),
                       jax.ShapeDtypeStruct((T, D_MODEL), f32), jax.ShapeDtypeStruct((T, 1), f32)) + kept,
            in_specs=in_specs, out_specs=(tile(D_MODEL), tile(D_MODEL), tile(D_MODEL), tile(1)) + kept_specs,
            compiler_params=_params("arbitrary"))(h, h, h, h, mix, kv, wout, x, g, b)
    return _pcall(
        body, name=f"post_fwd_loss_{layer}", grid=(T // tm,),
        out_shape=(jax.ShapeDtypeStruct((8, 128), f32), jax.ShapeDtypeStruct((T, D_MODEL), bf16),
                   jax.ShapeDtypeStruct((T, D_MODEL), f32), jax.ShapeDtypeStruct((8, D_MODEL), f32),
                   jax.ShapeDtypeStruct((8, D_MODEL), f32)) + kept,
        in_specs=in_specs + [tile(D_MODEL)],
        out_specs=(const(8, 128), tile(D_MODEL), tile(D_MODEL), const(8, D_MODEL), const(8, D_MODEL)) + kept_specs,
        compiler_params=_params("arbitrary"))(h, h, h, h, mix, kv, wout, x, g, b, target)


def _post_bwd(dob, h, mix, kv, wout, kept, layer, comm=None):
    T = dob.shape[0]
    tm = 512
    n_tiles = T // tm
    inv = 1.0 / math.sqrt(MEM_HEAD_DIM)

    def body(dob_ref, qm_ref, z0, z1, z2, mix_ref, kv_ref, w_ref, ps_ref, ms_ref,
             dhb_ref, dmix_ref, dkv_ref, dwo_out, dwo_ref):
        @pl.when(pl.program_id(0) == 0)
        def _():
            dkv_ref[...] = jnp.zeros_like(dkv_ref)
            dwo_ref[...] = jnp.zeros_like(dwo_ref)

        dob = dob_ref[...]
        z = jnp.concatenate([z0[...], z1[...], z2[...]], axis=1).astype(f32)
        act, sig = _silu_parts(z)
        cat = jnp.concatenate([mix_ref[...].astype(f32), ms_ref[...].astype(f32)], axis=1)
        yb = (cat * act).astype(bf16)
        dwo_ref[...] += lax.dot_general(yb, dob, TN, preferred_element_type=f32)
        dyv = lax.dot_general(dob, w_ref[...], NT, preferred_element_type=f32)
        dz = dyv * cat * (sig * (1.0 + z * (1.0 - sig)))
        dcat = dyv * act
        dmix_ref[...] = dcat[:, :E_MIX].astype(bf16)
        dqs = []
        for hh in range(MEM_HEADS):
            cols = slice(hh * MEM_HEAD_DIM, (hh + 1) * MEM_HEAD_DIM)
            qh, kh, vh = qm_ref[:, cols], kv_ref[:, cols], kv_ref[:, E_MEM + hh * MEM_HEAD_DIM:E_MEM + (hh + 1) * MEM_HEAD_DIM]
            pb = ps_ref[:, hh * N_MEM:(hh + 1) * N_MEM]
            p = pb.astype(f32)
            dmem = dcat[:, E_MIX + hh * MEM_HEAD_DIM:E_MIX + (hh + 1) * MEM_HEAD_DIM].astype(bf16)
            dp = lax.dot_general(dmem, vh, NT, preferred_element_type=f32)
            ds = (p * (dp - jnp.sum(p * dp, axis=1, keepdims=True))).astype(bf16)
            dqs.append(jnp.dot(ds, kh, preferred_element_type=f32) * inv)
            dkv_ref[:, cols] += lax.dot_general(ds, qh, TN, preferred_element_type=f32) * inv
            dkv_ref[:, E_MEM + hh * MEM_HEAD_DIM:E_MEM + (hh + 1) * MEM_HEAD_DIM] += lax.dot_general(
                pb, dmem, TN, preferred_element_type=f32)
        dhb_ref[...] = jnp.concatenate(dqs + [dz], axis=1).astype(bf16)

        @pl.when(pl.program_id(0) == n_tiles - 1)
        def _():
            dwo_out[...] = dwo_ref[...].astype(bf16)

    tile = lambda w: pl.BlockSpec((tm, w), lambda i: (i, 0))
    const = lambda r, c: pl.BlockSpec((r, c), lambda i: (0, 0))
    resident = lambda r, c: pl.BlockSpec((r, c), lambda i: (0, 0), pipeline_mode=pl.Buffered(1))
    return _pcall_carry(
        body, comm, n_in=10, n_out=4, name=f"post_bwd_{layer}", grid=(n_tiles,),
        out_shape=(jax.ShapeDtypeStruct((T, E_MEM + E_BRANCH), bf16), jax.ShapeDtypeStruct((T, E_MIX), bf16),
                   jax.ShapeDtypeStruct((N_MEM, 2 * E_MEM), f32), jax.ShapeDtypeStruct((E_BRANCH, D_MODEL), bf16)),
        in_specs=[tile(D_MODEL)] + _h_tail_specs(tm) + [tile(E_MIX), resident(N_MEM, 2 * E_MEM), resident(E_BRANCH, D_MODEL),
                                                        tile(MEM_HEADS * N_MEM), tile(E_MEM)],
        out_specs=(tile(E_MEM + E_BRANCH), tile(E_MIX), const(N_MEM, 2 * E_MEM), const(E_BRANCH, D_MODEL)),
        scratch_shapes=[pltpu.VMEM((E_BRANCH, D_MODEL), f32)],
        compiler_params=_params("arbitrary"))(dob, h, h, h, h, mix, kv, wout, *kept)


def _attn_bwd(h, bias, dmix, lse, layer, comm=None):
    T = h.shape[0]
    n_groups = T // QG
    scale = 1.0 / math.sqrt(HEAD_DIM)

    def body(q_ref, k0, k1, k2, v0, v1, v2, do_ref, b_ref, lse_ref, dq_ref, dk_ref, dv_ref, db_ref, acck, accv):
        g = pl.program_id(1)

        @pl.when(g == 0)
        def _():
            acck[...] = jnp.zeros_like(acck)
            accv[...] = jnp.zeros_like(accv)
            db_ref[...] = jnp.zeros_like(db_ref)

        @pl.when(g < n_groups)
        def _():
            lane = lax.broadcasted_iota(jnp.int32, (1, 128), 1)
            first = lane // HEAD_DIM == 0
            for pp in range(BWD_PAIRS):
                cs = slice(pp * 128, (pp + 1) * 128)
                do2 = do_ref[:, cs]
                q2 = q_ref[:, cs] * scale
                kc = jnp.concatenate([k0[:, cs], k1[:, cs], k2[:, cs]], axis=0)
                vc = jnp.concatenate([v0[:, cs], v1[:, cs], v2[:, cs]], axis=0)
                q2t, do2t = q2.T, do2.T
                dqs, dks, dvs = [], [], []
                for hh in range(2):
                    hm = lane // HEAD_DIM == hh
                    head = (pl.program_id(0) * BWD_PAIRS + pp) * 2 + hh
                    lse = jnp.sum(jnp.where(lane == head, lse_ref[...], 0.0), axis=1, keepdims=True)
                    qm = jnp.where(hm, q2, jnp.zeros_like(q2))
                    dom = jnp.where(hm, do2, jnp.zeros_like(do2))
                    s = lax.dot_general(qm, kc, NT, preferred_element_type=f32) + b_ref[2 * pp + hh]
                    p = jnp.exp(s - lse)
                    dp = lax.dot_general(dom, vc, NT, preferred_element_type=f32)
                    ds = p * (dp - jnp.sum(p * dp, axis=1, keepdims=True))
                    db_ref[2 * pp + hh] += ds[:, KG - DB_COLS:]
                    dsb, pb = ds.astype(bf16), p.astype(bf16)
                    dqs.append(jnp.dot(dsb, kc, preferred_element_type=f32) * scale)
                    dks.append(jnp.dot(q2t[hh * HEAD_DIM:(hh + 1) * HEAD_DIM], dsb, preferred_element_type=f32))
                    dvs.append(jnp.dot(do2t[hh * HEAD_DIM:(hh + 1) * HEAD_DIM], pb, preferred_element_type=f32))
                dq_ref[:, cs] = jnp.where(first, dqs[0], dqs[1]).astype(bf16)
                dkc = jnp.concatenate(dks, axis=0).T
                dvc = jnp.concatenate(dvs, axis=0).T
                for jj in range(3):
                    slot = (g + 1 + jj) % 3
                    if jj == 2:
                        acck[slot, :, cs] = dkc[jj * QG:(jj + 1) * QG]
                        accv[slot, :, cs] = dvc[jj * QG:(jj + 1) * QG]
                    else:
                        acck[slot, :, cs] += dkc[jj * QG:(jj + 1) * QG]
                        accv[slot, :, cs] += dvc[jj * QG:(jj + 1) * QG]

        done = (g + 1) % 3
        dk_ref[...] = acck[done].astype(bf16)
        dv_ref[...] = accv[done].astype(bf16)

    last = n_groups - 1
    width = 128 * BWD_PAIRS
    qspec = pl.BlockSpec((QG, width), lambda hp, g: (jnp.minimum(g, last), hp))
    kout = pl.BlockSpec((QG, width), lambda hp, g: (jnp.clip(g - 2, 0, last), hp))
    dbspec = pl.BlockSpec((2 * BWD_PAIRS, QG, DB_COLS), lambda hp, g: (hp, 0, 0))
    lspec = pl.BlockSpec((QG, 128), lambda hp, g: (jnp.minimum(g, last), 0))
    return _pcall_carry(
        body, comm, n_in=10, n_out=4, name=f"attn_bwd_{layer}", grid=(N_HEADS // (2 * BWD_PAIRS), n_groups + 2),
        out_shape=(jax.ShapeDtypeStruct((T, E_MIX), bf16),) * 3 + (jax.ShapeDtypeStruct((N_HEADS, QG, DB_COLS), f32),),
        in_specs=[qspec] + _key_specs(n_groups, BWD_PAIRS, 1) + _key_specs(n_groups, BWD_PAIRS, 2)
        + [qspec, _bias_spec(BWD_PAIRS), lspec],
        out_specs=(qspec, kout, kout, dbspec),
        scratch_shapes=[pltpu.VMEM((3, QG, width), f32), pltpu.VMEM((3, QG, width), f32)],
        compiler_params=_params("arbitrary", "arbitrary"))(h, h, h, h, h, h, h, dmix, bias, lse)


def _conv_bwd(h, w, dmix, layer):
    T = h.shape[0]
    tm = 512
    n_tiles = T // tm

    def body(bg_ref, cg_ref, u_ref, cgp_ref, up_ref, dy_ref, bgn_ref, dyn_ref, w_ref, dbg_ref, dcg_ref, du_ref, dw_ref):
        i = pl.program_id(0)

        @pl.when(i == 0)
        def _():
            dw_ref[...] = jnp.zeros_like(dw_ref)

        first = (i == 0).astype(f32)
        final = (i == n_tiles - 1).astype(f32)
        bg, cg, u = bg_ref[...].astype(f32), cg_ref[...].astype(f32), u_ref[...].astype(f32)
        dy = dy_ref[...].astype(f32)
        cu = cg * u
        p6 = _halo_rows(cgp_ref, 14) * _halo_rows(up_ref, 14) * (1.0 - first)
        p7 = _halo_rows(cgp_ref, 15) * _halo_rows(up_ref, 15) * (1.0 - first)
        conv, r1, r2 = _conv_taps(cu, p6, p7, w_ref)
        dbg_ref[...] = (dy * conv).astype(bf16)
        dc = dy * bg
        n0 = _halo_rows(dyn_ref, 0) * _halo_rows(bgn_ref, 0) * (1.0 - final)
        n1 = _halo_rows(dyn_ref, 1) * _halo_rows(bgn_ref, 1) * (1.0 - final)
        row = lax.broadcasted_iota(jnp.int32, dc.shape, 0)
        f1 = jnp.where(row == tm - 1, n0, pltpu.roll(dc, tm - 1, axis=0))
        f2 = jnp.where(row == tm - 2, n0, jnp.where(row == tm - 1, n1, pltpu.roll(dc, tm - 2, axis=0)))
        dcu = w_ref[2:3, :] * dc + w_ref[1:2, :] * f1 + w_ref[0:1, :] * f2
        dcg_ref[...] = (dcu * u).astype(bf16)
        du_ref[...] = (dcu * cg).astype(bf16)
        dw_ref[0:1, :] += jnp.sum(dc * r2, axis=0, keepdims=True)
        dw_ref[1:2, :] += jnp.sum(dc * r1, axis=0, keepdims=True)
        dw_ref[2:3, :] += jnp.sum(dc * cu, axis=0, keepdims=True)

    tile = lambda slab: pl.BlockSpec((tm, E_MIX), lambda i: (i, slab))
    prev = lambda slab: pl.BlockSpec((16, E_MIX), lambda i: (jnp.maximum(i * (tm // 16) - 1, 0), slab))
    nxt = lambda slab: pl.BlockSpec((16, E_MIX), lambda i: (jnp.minimum((i + 1) * (tm // 16), T // 16 - 1), slab))
    return _pcall(
        body, name=f"conv_bwd_{layer}", grid=(n_tiles,),
        out_shape=(jax.ShapeDtypeStruct((T, E_MIX), bf16),) * 3 + (jax.ShapeDtypeStruct((8, E_MIX), f32),),
        in_specs=[tile(0), tile(1), tile(2), prev(1), prev(2), tile(0), nxt(0), nxt(0),
                  pl.BlockSpec((CONV_W, E_MIX), lambda i: (0, 0))],
        out_specs=(tile(0), tile(0), tile(0), pl.BlockSpec((8, E_MIX), lambda i: (0, 0))),
        compiler_params=_params("arbitrary"))(h, h, h, h, h, dmix, h, dmix, w)


def _inproj_bwd_dx(da, db, dc, dhb, dxp, win, layer, below=None, comm=None):
    T = dxp.shape[0]
    tm = 512

    def body(da_ref, db_ref, dc_ref, dhb_ref, dxp_ref, w_ref, *rest):
        dh = jnp.concatenate([da_ref[...], db_ref[...], dc_ref[...], dhb_ref[...]], axis=1)
        acc = dxp_ref[...]
        for j in range(N_CHIPS):
            acc = acc + lax.dot_general(dh[:, j * W_IN_COLS:(j + 1) * W_IN_COLS], w_ref[j], NT, preferred_element_type=f32)
        if below is None:
            rest[0][...] = acc
        else:
            xh_ref, rs_ref, g_ref, dob_ref, dxo_ref, dg_ref, db_out = rest

            @pl.when(pl.program_id(0) == 0)
            def _():
                dg_ref[...] = jnp.zeros_like(dg_ref)
                db_out[...] = jnp.zeros_like(db_out)

            _ln_bwd(acc, xh_ref[...], rs_ref[...], g_ref[...], dob_ref, dxo_ref, dg_ref, db_out)

    tile = lambda w: pl.BlockSpec((tm, w), lambda i: (i, 0))
    const = lambda r, c: pl.BlockSpec((r, c), lambda i: (0, 0))
    in_specs = [tile(E_MIX), tile(E_MIX), tile(E_MIX), tile(E_MEM + E_BRANCH), tile(D_MODEL),
                pl.BlockSpec((N_CHIPS, D_MODEL, W_IN_COLS), lambda i: (0, 0, 0), pipeline_mode=pl.Buffered(1))]
    if below is None:
        return _pcall_carry(
            body, comm, n_in=6, n_out=1, name=f"inproj_bwd_dx_{layer}", grid=(T // tm,),
            out_shape=(jax.ShapeDtypeStruct((T, D_MODEL), f32),), in_specs=in_specs, out_specs=(tile(D_MODEL),),
            compiler_params=_params("arbitrary"))(da, db, dc, dhb, dxp, win)
    return _pcall_carry(
        body, comm, n_in=9, n_out=4, name=f"inproj_bwd_dx_{layer}", grid=(T // tm,),
        out_shape=(jax.ShapeDtypeStruct((T, D_MODEL), bf16), jax.ShapeDtypeStruct((T, D_MODEL), f32),
                   jax.ShapeDtypeStruct((8, D_MODEL), f32), jax.ShapeDtypeStruct((8, D_MODEL), f32)),
        in_specs=in_specs + [tile(D_MODEL), tile(1), const(1, D_MODEL)],
        out_specs=(tile(D_MODEL), tile(D_MODEL), const(8, D_MODEL), const(8, D_MODEL)),
        compiler_params=_params("arbitrary"))(da, db, dc, dhb, dxp, win, *below)


def _dh_pieces():
    pieces, col = [], 0
    for src, width in enumerate((E_MIX, E_MIX, E_MIX, E_MEM + E_BRANCH)):
        lo = 0
        while lo < width:
            j, c0 = divmod(col + lo, W_IN_COLS)
            n = min(width - lo, W_IN_COLS - c0)
            pieces.append((src, lo, lo + n, j, c0, c0 + n))
            lo += n
        col += width
    return pieces


def _inproj_bwd_dw(da, db, dc, dhb, xb, layer, comm=None):
    T = xb.shape[0]
    tm = 1024 if T % 1024 == 0 else 512
    n_tiles = T // tm

    def body(da_ref, db_ref, dc_ref, dhb_ref, x_ref, o_ref, acc, stage, sem):
        i = pl.program_id(0)

        @pl.when(i == 0)
        def _():
            acc[...] = jnp.zeros_like(acc)

        srcs = (da_ref, db_ref, dc_ref, dhb_ref)
        xt = x_ref[...]
        for s, s0, s1, j, c0, c1 in _dh_pieces():
            acc[j, :, c0:c1] += lax.dot_general(xt, srcs[s][:, s0:s1], TN, preferred_element_type=f32)

        @pl.when(i == n_tiles - 1)
        def _():
            for j in range(N_CHIPS):
                stage[...] = acc[j].astype(bf16)
                cp = pltpu.make_async_copy(stage, o_ref.at[j], sem)
                cp.start()
                cp.wait()

    tile = lambda w: pl.BlockSpec((tm, w), lambda i: (i, 0))
    (dw,), carried = _pcall_carry(
        body, comm, n_in=5, n_out=1, name=f"inproj_bwd_dw_{layer}", grid=(n_tiles,),
        out_shape=(jax.ShapeDtypeStruct((N_CHIPS, D_MODEL, W_IN_COLS), bf16),),
        in_specs=[tile(E_MIX), tile(E_MIX), tile(E_MIX), tile(E_MEM + E_BRANCH), tile(D_MODEL)],
        out_specs=(ANY,),
        scratch_shapes=[pltpu.VMEM((N_CHIPS, D_MODEL, W_IN_COLS), f32), pltpu.VMEM((D_MODEL, W_IN_COLS), bf16),
                        pltpu.SemaphoreType.DMA],
        compiler_params=_params("arbitrary"))(da, db, dc, dhb, xb)
    return dw, carried


def _kv_mem_bwd(memb, dkv, layer):
    def body(m_ref, d_ref, o_ref):
        o_ref[...] = lax.dot_general(m_ref[...], d_ref[...].astype(bf16), TN, preferred_element_type=f32).astype(bf16)

    return _pcall(body, name=f"kv_mem_bwd_{layer}", out_shape=jax.ShapeDtypeStruct((D_MODEL, 2 * E_MEM), bf16),
                  compiler_params=pltpu.CompilerParams(vmem_limit_bytes=VMEM_LIMIT))(memb, dkv)


def _adamw(w, g, m, v, name):
    shape = w.shape
    cols = shape[-1]
    rows = w.size // cols
    args = [a.reshape(rows, cols) for a in (w, g, m, v)]
    br = 256 if rows % 256 == 0 and rows > 256 else rows

    def body(w_ref, g_ref, m_ref, v_ref, go_ref, d_ref, nm_ref, nv_ref):
        gg = g_ref[...]
        nm = ADAM_B1 * m_ref[...] + (1.0 - ADAM_B1) * gg
        nv = ADAM_B2 * v_ref[...] + (1.0 - ADAM_B2) * jnp.square(gg)
        m_hat = nm / (1.0 - ADAM_B1 ** ADAM_STEP)
        v_hat = nv / (1.0 - ADAM_B2 ** ADAM_STEP)
        go_ref[...] = gg
        d_ref[...] = -ADAM_LR * (m_hat / (jnp.sqrt(v_hat) + ADAM_EPS) + ADAM_WD * w_ref[...])
        nm_ref[...] = nm
        nv_ref[...] = nv

    spec = pl.BlockSpec((br, cols), lambda i: (i, 0))
    outs = _pcall(body, name=name, grid=(rows // br,), out_shape=(jax.ShapeDtypeStruct((rows, cols), f32),) * 4,
                  in_specs=[spec] * 4, out_specs=(spec,) * 4, compiler_params=_params("arbitrary"))(*args)
    return tuple(o.reshape(shape) for o in outs)


def kernel(x, mem, w_in, w_mem_kv, w_out, rel_bias, conv_w, ln_g, ln_b, loss_target, m_w_in, m_w_mem_kv, m_w_out, m_rel_bias, m_conv_w, m_ln_g, m_ln_b, v_w_in, v_w_mem_kv, v_w_out, v_rel_bias, v_conv_w, v_ln_g, v_ln_b):
    T = x.shape[1]
    x0 = x.reshape(T, D_MODEL)
    target = loss_target.reshape(T, D_MODEL)
    memb = mem.reshape(N_MEM, D_MODEL).astype(bf16)
    chip = 2 * lax.axis_index("x") + lax.axis_index("y")
    core = lax.axis_index("c")
    chip_arr = jnp.reshape(chip, (1,)).astype(jnp.int32)
    core_arr = jnp.reshape(core, (1,)).astype(jnp.int32)

    place = jnp.concatenate([chip_arr, core_arr])
    tables = jnp.pad(rel_bias, ((0, 0), (0, 0), (0, N_REL_PAD - N_REL)))

    shards = [w_in.astype(bf16), w_mem_kv.astype(bf16), w_out.astype(bf16)]
    biases = {}
    biases[0], near = _bias_expand(tables[0], 0, _gather_ici(shards[:1], 0, peers=(0, 1)))
    biases[2], far = _bias_expand(tables[1], 2, _gather_ici(shards[:1], 0, peers=(2,), own=False, into=near))
    gathered = {0: list(_comm_call(_gather_d2d(list(far)), "gather_d2d_0")) + [None, None]}
    conv_full = None

    xs, xbs, hs, mixes, kvs, kepts, xhats, rstds, lses = [x0], [x0], [], [], [], [], [], [], {}
    for layer in range(DEPTH):
        more = layer + 1 < DEPTH
        attention = layer % 2 == 0
        h, xbs[layer], mix, arrived = _inproj(xbs[layer], gathered[layer][0], layer,
                                              _gather_ici(shards, layer + 1) if more else None,
                                              None if attention else conv_full[layer // 2])
        passing = _gather_d2d(list(arrived)) if more else None
        if layer == 0:
            passing = _both(passing, _gather_ici(shards[1:], 0, extra=conv_w))
        if attention:
            mix, lses[layer], done = _attn_fwd(h, biases[layer], layer, passing)
        if layer == 0:
            *done, wkv_in, wout_in, cw_g = done
            gathered[0][1:] = _comm_call(_gather_d2d([wkv_in, wout_in]), "gather_d2d_0_rest")
            conv_full = jnp.transpose(cw_g, (1, 2, 0, 3)).reshape(DEPTH // 2, CONV_W, E_MIX)
        win, wkv, wout = gathered[layer]
        kv = _kv_mem(memb, wkv.reshape(D_MODEL, 2 * E_MEM))
        result = _post_fwd(h, mix, kv, wout.reshape(E_BRANCH, D_MODEL), xs[layer], ln_g[layer][None, :],
                           ln_b[layer][None, :], layer, None if more else target, None if attention else passing)
        if more:
            (xn, xnb, xhat, rstd, *kept), done = (result[0], done) if attention else result
            xs.append(xn); xbs.append(xnb); xhats.append(xhat); rstds.append(rstd)
            gathered[layer + 1] = list(done)
        else:
            lsum, dob, dxp, dg_last, db_last, *kept = result
        hs.append(h); mixes.append(mix); kvs.append(kv); kepts.append(kept)

    dgs, dbs, dconvs, dtables = [None] * DEPTH, [None] * DEPTH, [None] * (DEPTH // 2), [None] * ((DEPTH + 1) // 2)
    dgs[DEPTH - 1], dbs[DEPTH - 1] = dg_last, db_last
    loss = lax.psum(lsum[0, 0], ("x", "y", "c")) * (0.5 / D_MODEL)
    finals = [None, None, None]
    above = None
    for layer in reversed(range(DEPTH)):
        h = hs[layer]
        win, wkv, wout = gathered[layer]
        (dhb, dmix, dkv, dwo), from_sibling = _post_bwd(
            dob, h, mixes[layer], kvs[layer], wout.reshape(E_BRANCH, D_MODEL), kepts[layer], layer,
            _sibling_exchange(above) if above else None)
        sums = [_pair_sum(g, r, core_arr, layer + 1) for g, r in zip(above, from_sibling)] if above else None
        scatter = _chip_scatter(sums) if above else None
        if layer % 2 == 0:
            (da, db, dc, dbias), from_chips = _attn_bwd(h, biases[layer], dmix, lses[layer], layer, scatter)
            dtables[layer // 2] = _bias_reduce(dbias, layer)
        else:
            da, db, dc, dconvs[layer // 2] = _conv_bwd(h, conv_full[layer // 2], dmix, layer)
        if layer > 0:
            (dob_below, dxp_below, dgs[layer - 1], dbs[layer - 1]), landed = _inproj_bwd_dx(
                da, db, dc, dhb, dxp, win, layer, (xhats[layer - 1], rstds[layer - 1], ln_g[layer - 1][None, :]),
                scatter if layer % 2 == 1 else None)
            from_chips = landed if layer % 2 == 1 else from_chips
        share = None
        if above:
            finals = [_chip_sum(s, r, place, layer + 1, f) for s, r, f in zip(sums, from_chips, finals)]
            share = _sibling_share(finals, layer + 1)
        if layer == 0:
            pad8 = lambda a: jnp.pad(a, ((0, 8 - a.shape[0]), (0, 0)))
            small_mine = jnp.concatenate(dgs + dbs + dconvs + [pad8(t.reshape(-1, D_MODEL)) for t in dtables], axis=0)
            share = _both(share, _small_exchange(small_mine))
        g_win, shared = _inproj_bwd_dw(da, db, dc, dhb, xbs[layer], layer, share)
        if layer == 0:
            *shared, small_slots = shared
        finals = list(shared) if above else finals
        above = [g_win, _kv_mem_bwd(memb, dkv, layer).reshape(N_CHIPS, W_KV_ROWS, 2 * E_MEM),
                 dwo.reshape(N_CHIPS, W_OUT_ROWS, D_MODEL)]
        if layer > 0:
            dob, dxp = dob_below, dxp_below
    from_sibling = _comm_call(_sibling_exchange(above), "sibling_exchange_0")
    sums = [_pair_sum(g, r, core_arr, 0) for g, r in zip(above, from_sibling)]
    (dx,), from_chips = _inproj_bwd_dx(da, db, dc, dhb, dxp, win, 0, None, _chip_scatter(sums))
    finals = [_chip_sum(s, r, place, 0, f) for s, r, f in zip(sums, from_chips, finals)]
    grad_w_in, grad_w_mem_kv, grad_w_out = _comm_call(_sibling_share(finals, 0), "sibling_share_0")
    grad_x = dx.reshape(1, T, D_MODEL)

    device_arr = jnp.reshape(2 * chip + core, (1,)).astype(jnp.int32)
    small = _small_sum(small_slots, small_mine, device_arr)
    grad_ln_g = jnp.stack([small[8 * l] for l in range(DEPTH)])
    grad_ln_b = jnp.stack([small[8 * (DEPTH + l)] for l in range(DEPTH)])
    conv_all = jnp.stack([small[8 * (2 * DEPTH + a):8 * (2 * DEPTH + a) + CONV_W] for a in range(DEPTH // 2)])
    grad_conv_w = lax.dynamic_slice_in_dim(conv_all, chip * (E_MIX // N_CHIPS), E_MIX // N_CHIPS, axis=2)
    t0 = 8 * (2 * DEPTH + DEPTH // 2)
    grad_rel_bias = jnp.stack([small[t0 + 8 * a:t0 + 8 * a + 6].reshape(N_HEADS, N_REL_PAD)[:, :N_REL]
                               for a in range((DEPTH + 1) // 2)])

    grads = [grad_w_in, grad_w_mem_kv, grad_w_out, grad_rel_bias, grad_conv_w, grad_ln_g, grad_ln_b]
    weights = [w_in, w_mem_kv, w_out, rel_bias, conv_w, ln_g, ln_b]
    moms = [m_w_in, m_w_mem_kv, m_w_out, m_rel_bias, m_conv_w, m_ln_g, m_ln_b]
    vels = [v_w_in, v_w_mem_kv, v_w_out, v_rel_bias, v_conv_w, v_ln_g, v_ln_b]
    names = ["w_in", "w_mem_kv", "w_out", "rel_bias", "conv_w", "ln_g", "ln_b"]
    upd = [_adamw(w, g, m, v, f"adamw_{n}") for w, g, m, v, n in zip(weights, grads, moms, vels, names)]
    grads, deltas, new_m, new_v = zip(*upd)
    return (loss, grad_x, *grads, *deltas, *new_m, *new_v)
```

```python
import functools
import math

import jax
import jax.numpy as jnp
from jax import lax
from jax.experimental import pallas as pl
from jax.experimental.pallas import tpu as pltpu

f32, bf16 = jnp.float32, jnp.bfloat16

D_MODEL = 1024
DEPTH = 4
CHUNK = 64
N_PREV = 8
N_HEADS = 16
HEAD_DIM = 64
E_MIX = 1024
REL_CLIP = 128
N_REL = 2 * REL_CLIP + 1
N_REL_PAD = 384
CONV_W = 3
N_MEM = 256
MEM_HEADS = 4
MEM_HEAD_DIM = 128
E_MEM = 512
E_BRANCH = E_MIX + E_MEM
N_IN = 3 * E_MIX + E_MEM + E_BRANCH
N_CHIPS = 4
W_IN_COLS = N_IN // N_CHIPS
W_KV_ROWS = D_MODEL // N_CHIPS
W_OUT_ROWS = E_BRANCH // N_CHIPS
DN_ALPHA = (2.0 * DEPTH) ** 0.25
LN_EPS = 1e-5
ADAM_LR, ADAM_B1, ADAM_B2, ADAM_EPS, ADAM_WD, ADAM_STEP = 0.001, 0.9, 0.999, 1e-08, 0.01, 10

QG = 4 * CHUNK
KG = QG + N_PREV * CHUNK
DB_COLS = KG // 2
NEG = -1e30
VMEM_LIMIT = 56 * 1024 * 1024

NT = (((1,), (1,)), ((), ()))
TN = (((0,), (0,)), ((), ()))
MESH = pl.DeviceIdType.MESH
ANY = pl.BlockSpec(memory_space=pl.ANY)


def _pcall(body, **kw):
    return pl.pallas_call(body, **kw)


def _params(*sem):
    return pltpu.CompilerParams(dimension_semantics=sem, vmem_limit_bytes=VMEM_LIMIT)


def _silu_parts(z):
    sig = 1.0 / (1.0 + jnp.exp(-z))
    return z * sig, sig


class _Comm:
    def __init__(self, inputs, out_shapes, aliases, n_sems, copies):
        self.inputs, self.out_shapes, self.aliases, self.n_sems, self.copies = inputs, out_shapes, aliases, n_sems, copies

    def start(self, cin, cout, send, recv):
        for cp in self.copies(cin, cout, send, recv)[0]:
            cp.start()

    def wait(self, cin, cout, send, recv):
        sends, recvs = self.copies(cin, cout, send, recv)
        for cp in recvs:
            cp.wait_recv()
        for cp in sends:
            cp.wait_send()


def _pcall_carry(body, comm, *, n_in, n_out, **kw):
    if comm is None:
        return lambda *args: (_pcall(body, **kw)(*args), ())
    grid = kw["grid"]
    k_in, k_out = len(comm.inputs), len(comm.out_shapes)

    def carried(*refs):
        ins, cin = refs[:n_in], refs[n_in:n_in + k_in]
        outs = refs[n_in + k_in:n_in + k_in + n_out]
        cout = refs[n_in + k_in + n_out:n_in + k_in + n_out + k_out]
        scratch, send, recv = refs[n_in + k_in + n_out + k_out:-2], refs[-2], refs[-1]
        ids = [pl.program_id(a) for a in range(len(grid))]
        first = functools.reduce(jnp.logical_and, [i == 0 for i in ids])
        last = functools.reduce(jnp.logical_and, [i == n - 1 for i, n in zip(ids, grid)])

        @pl.when(first)
        def _():
            comm.start(cin, cout, send, recv)

        body(*ins, *outs, *scratch)

        @pl.when(last)
        def _():
            comm.wait(cin, cout, send, recv)

    kw = dict(kw)
    kw["in_specs"] = list(kw["in_specs"]) + [ANY] * k_in
    kw["out_specs"] = tuple(kw["out_specs"]) + (ANY,) * k_out
    kw["out_shape"] = tuple(kw["out_shape"]) + tuple(comm.out_shapes)
    kw["scratch_shapes"] = list(kw.get("scratch_shapes", ())) + [pltpu.SemaphoreType.DMA((comm.n_sems,))] * 2
    aliases = dict(kw.get("input_output_aliases", {}))
    aliases.update({n_in + ci: n_out + co for ci, co in comm.aliases.items()})
    kw["input_output_aliases"] = aliases

    def run(*args):
        res = _pcall(carried, **kw)(*args, *comm.inputs)
        return res[:n_out], res[n_out:]

    return run


def _comm_call(comm, name):
    k_in = len(comm.inputs)

    def body(*refs):
        cin, cout, send, recv = refs[:k_in], refs[k_in:-2], refs[-2], refs[-1]
        comm.start(cin, cout, send, recv)
        comm.wait(cin, cout, send, recv)

    return _pcall(body, name=name, out_shape=tuple(comm.out_shapes), in_specs=[ANY] * k_in,
                  out_specs=(ANY,) * len(comm.out_shapes), input_output_aliases=dict(comm.aliases),
                  scratch_shapes=[pltpu.SemaphoreType.DMA((comm.n_sems,))] * 2)(*comm.inputs)


def _place():
    x, y, c = lax.axis_index("x"), lax.axis_index("y"), lax.axis_index("c")
    return x, y, c, 2 * x + y, (x, y, 1 - c), [(1 - x, y), (x, 1 - y), (1 - x, 1 - y)]


def _rcopy(send, recv, k, src, dst, to):
    return pltpu.make_async_remote_copy(src_ref=src, dst_ref=dst, send_sem=send.at[k], recv_sem=recv.at[k],
                                        device_id=to, device_id_type=MESH)


def _half(ref_rows, core):
    return pl.ds(core * (ref_rows // 2), ref_rows // 2)


def _gather_ici(shards, layer, extra=None, peers=(0, 1, 2), own=True, into=None):
    extras = [] if extra is None else [extra]
    n = len(shards)

    def copies(cin, cout, send, recv):
        x, y, c, me, sibling, chips = _place()
        sends, recvs = [], []
        for a in range(n + len(extras)):
            s, g = cin[a], cout[a]
            whole = a >= n
            src = s if whole else s.at[layer]
            if own:
                sends.append(_rcopy(send, recv, 4 * a, src, g.at[me], sibling))
                recvs.append(_rcopy(send, recv, 4 * a, src, g.at[me], sibling))
            for p in peers:
                px, py = chips[p]
                if whole:
                    sends.append(_rcopy(send, recv, 4 * a + 1 + p, src, g.at[me], (px, py, c)))
                    recvs.append(_rcopy(send, recv, 4 * a + 1 + p, src, g.at[2 * px + py], (px, py, c)))
                else:
                    mine = _half(s.shape[1], c)
                    sends.append(_rcopy(send, recv, 4 * a + 1 + p, s.at[layer, mine], g.at[me, mine], (px, py, c)))
                    recvs.append(_rcopy(send, recv, 4 * a + 1 + p, s.at[layer, mine], g.at[2 * px + py, mine], (px, py, c)))
        return sends, recvs

    sources = list(shards) + extras
    if into is None:
        out_shapes = [jax.ShapeDtypeStruct((N_CHIPS,) + s.shape[1:], s.dtype) for s in shards]
        out_shapes += [jax.ShapeDtypeStruct((N_CHIPS,) + e.shape, e.dtype) for e in extras]
        return _Comm(sources, out_shapes, {}, 4 * len(sources), copies)
    return _Comm(sources + list(into), [jax.ShapeDtypeStruct(g.shape, g.dtype) for g in into],
                 {len(sources) + a: a for a in range(len(into))}, 4 * len(sources), copies)


def _gather_d2d(gathered):
    n = len(gathered)

    def copies(cin, cout, send, recv):
        x, y, c, me, sibling, chips = _place()
        sends, recvs = [], []
        for a in range(n):
            g = cout[a]
            rows = g.shape[1]
            for p, (px, py) in enumerate(chips):
                mine, theirs = g.at[2 * px + py, _half(rows, c)], g.at[2 * px + py, _half(rows, 1 - c)]
                sends.append(_rcopy(send, recv, 3 * a + p, mine, mine, sibling))
                recvs.append(_rcopy(send, recv, 3 * a + p, theirs, theirs, sibling))
        return sends, recvs

    return _Comm(list(gathered), [jax.ShapeDtypeStruct(g.shape, g.dtype) for g in gathered],
                 {a: a for a in range(n)}, 3 * n, copies)


class _SemView:
    def __init__(self, ref, base):
        self.ref, self.base, self.at = ref, base, self

    def __getitem__(self, k):
        return self.ref.at[self.base + k]


def _both(a, b):
    ka, ma = len(a.inputs), len(a.out_shapes)

    def copies(cin, cout, send, recv):
        sa, ra = a.copies(cin[:ka], cout[:ma], send, recv)
        sb, rb = b.copies(cin[ka:], cout[ma:], _SemView(send, a.n_sems), _SemView(recv, a.n_sems))
        return sa + sb, ra + rb

    aliases = dict(a.aliases)
    aliases.update({ka + ci: ma + co for ci, co in b.aliases.items()})
    return _Comm(a.inputs + b.inputs, a.out_shapes + b.out_shapes, aliases, a.n_sems + b.n_sems, copies)


def _small_exchange(buf):
    def copies(cin, cout, send, recv):
        x, y, c, me, sibling, chips = _place()
        sends, recvs = [], []
        for r in range(1, 8):
            px, py, pc = x ^ ((r >> 2) & 1), y ^ ((r >> 1) & 1), c ^ (r & 1)
            sends.append(_rcopy(send, recv, r - 1, cin[0], cout[0].at[2 * me + c], (px, py, pc)))
            recvs.append(_rcopy(send, recv, r - 1, cin[0], cout[0].at[4 * px + 2 * py + pc], (px, py, pc)))
        return sends, recvs

    return _Comm([buf], [jax.ShapeDtypeStruct((8,) + buf.shape, buf.dtype)], {}, 7, copies)


def _small_sum(slots, buf, device_arr):
    rows, cols = buf.shape

    def body(d_ref, s_ref, b_ref, o_ref):
        d = pl.program_id(0)
        val = jnp.where(d == d_ref[0], b_ref[...], s_ref[...])

        @pl.when(d == 0)
        def _():
            o_ref[...] = val

        @pl.when(d > 0)
        def _():
            o_ref[...] += val

    return _pcall(
        body, name="small_sum", out_shape=jax.ShapeDtypeStruct((rows, cols), f32),
        grid_spec=pltpu.PrefetchScalarGridSpec(
            num_scalar_prefetch=1, grid=(8,),
            in_specs=[pl.BlockSpec((None, rows, cols), lambda d, d_ref: (jnp.where(d == d_ref[0], (d + 1) % 8, d), 0, 0)),
                      pl.BlockSpec((rows, cols), lambda d, d_ref: (0, 0))],
            out_specs=pl.BlockSpec((rows, cols), lambda d, d_ref: (0, 0))),
        compiler_params=_params("arbitrary"))(device_arr, slots, buf)


def _sibling_exchange(gs):
    def copies(cin, cout, send, recv):
        x, y, c, me, sibling, chips = _place()
        sends = [_rcopy(send, recv, a, g.at[:, _half(g.shape[1], 1 - c)], r, sibling) for a, (g, r) in enumerate(zip(cin, cout))]
        return sends, sends

    shapes = [jax.ShapeDtypeStruct((N_CHIPS, g.shape[1] // 2, g.shape[2]), g.dtype) for g in gs]
    return _Comm(list(gs), shapes, {}, len(gs), copies)


def _chip_scatter(ss):
    def copies(cin, cout, send, recv):
        x, y, c, me, sibling, chips = _place()
        sends = [_rcopy(send, recv, 3 * a + p, s.at[2 * px + py], r.at[p], (px, py, c))
                 for a, (s, r) in enumerate(zip(cin, cout)) for p, (px, py) in enumerate(chips)]
        return sends, sends

    shapes = [jax.ShapeDtypeStruct((3,) + s.shape[1:], s.dtype) for s in ss]
    return _Comm(list(ss), shapes, {}, 3 * len(ss), copies)


def _sibling_share(fs, layer):
    def copies(cin, cout, send, recv):
        x, y, c, me, sibling, chips = _place()
        sends, recvs = [], []
        for a, f in enumerate(cout):
            mine, theirs = f.at[layer, _half(f.shape[1], c)], f.at[layer, _half(f.shape[1], 1 - c)]
            sends.append(_rcopy(send, recv, a, mine, mine, sibling))
            recvs.append(_rcopy(send, recv, a, theirs, theirs, sibling))
        return sends, recvs

    return _Comm(list(fs), [jax.ShapeDtypeStruct(f.shape, f.dtype) for f in fs], {a: a for a in range(len(fs))},
                 len(fs), copies)


def _sum_rows(rows):
    return next(b for b in (256, 192, 128) if rows % b == 0)


def _pair_sum(g, r, core_arr, layer):
    _, rows, cols = r.shape
    br = _sum_rows(rows)
    nb = rows // br

    def body(c_ref, g_ref, r_ref, o_ref):
        o_ref[...] = (g_ref[...].astype(f32) + r_ref[...].astype(f32)).astype(bf16)

    return _pcall(
        body, name=f"pair_sum_{layer}", out_shape=jax.ShapeDtypeStruct(r.shape, bf16),
        grid_spec=pltpu.PrefetchScalarGridSpec(
            num_scalar_prefetch=1, grid=(N_CHIPS, nb),
            in_specs=[pl.BlockSpec((1, br, cols), lambda j, i, c_ref: (j, c_ref[0] * nb + i, 0)),
                      pl.BlockSpec((1, br, cols), lambda j, i, c_ref: (j, i, 0))],
            out_specs=pl.BlockSpec((1, br, cols), lambda j, i, c_ref: (j, i, 0))),
        compiler_params=_params("arbitrary", "arbitrary"))(core_arr, g, r)


def _chip_sum(s, r, place, layer, final):
    _, rows, cols = s.shape
    br = _sum_rows(rows)
    nb = rows // br

    def body(place_ref, s_ref, r_ref, *rest):
        o_ref = rest[-1]
        acc = s_ref[0].astype(f32)
        for p in range(3):
            acc = acc + r_ref[p].astype(f32)
        o_ref[...] = acc

    carried = [] if final is None else [final]
    return _pcall(
        body, name=f"chip_sum_{layer}", out_shape=jax.ShapeDtypeStruct((DEPTH, 2 * rows, cols), f32),
        grid_spec=pltpu.PrefetchScalarGridSpec(
            num_scalar_prefetch=1, grid=(nb,),
            in_specs=[pl.BlockSpec((1, br, cols), lambda i, place_ref: (place_ref[0], i, 0)),
                      pl.BlockSpec((3, br, cols), lambda i, place_ref: (0, i, 0))] + [ANY] * len(carried),
            out_specs=pl.BlockSpec((None, br, cols), lambda i, place_ref: (layer, place_ref[1] * nb + i, 0))),
        input_output_aliases={3: 0} if carried else {},
        compiler_params=_params("arbitrary"))(place, s, r, *carried)


def _inproj(xin, win, layer, comm=None, conv_w=None):
    T = xin.shape[0]
    tm = 512
    cast = xin.dtype != bf16
    conv = conv_w is not None

    def body(x_ref, w_ref, *rest):
        rest = list(rest)
        cw_ref = rest.pop(0) if conv else None
        o_ref = rest.pop(0)
        xt = x_ref[...].astype(bf16)
        if cast:
            rest.pop(0)[...] = xt
        for j in range(N_CHIPS):
            o_ref[:, j * W_IN_COLS:(j + 1) * W_IN_COLS] = jnp.dot(xt, w_ref[j], preferred_element_type=f32).astype(bf16)
        if conv:
            mix_ref, halo = rest

            @pl.when(pl.program_id(0) == 0)
            def _():
                halo[...] = jnp.zeros_like(halo)

            bg, cg, u = (o_ref[:, s * E_MIX:(s + 1) * E_MIX].astype(f32) for s in range(3))
            cu = cg * u
            out, _, _ = _conv_taps(cu, halo[6:7, :], halo[7:8, :], cw_ref)
            mix_ref[...] = (bg * out).astype(bf16)
            halo[...] = cu[tm - 8:, :]

    tile = pl.BlockSpec((tm, D_MODEL), lambda i: (i, 0))
    outs, carried = _pcall_carry(
        body, comm, n_in=2 + conv, n_out=1 + cast + conv, name=f"inproj_{layer}", grid=(T // tm,),
        out_shape=(jax.ShapeDtypeStruct((T, N_IN), bf16),) + (jax.ShapeDtypeStruct((T, D_MODEL), bf16),) * (cast + conv),
        in_specs=[tile, pl.BlockSpec((N_CHIPS, D_MODEL, W_IN_COLS), lambda i: (0, 0, 0), pipeline_mode=pl.Buffered(1))]
        + [pl.BlockSpec((CONV_W, E_MIX), lambda i: (0, 0))] * conv,
        out_specs=(pl.BlockSpec((tm, N_IN), lambda i: (i, 0)),) + (tile,) * (cast + conv),
        scratch_shapes=[pltpu.VMEM((8, E_MIX), f32)] * conv,
        compiler_params=_params("arbitrary"))(xin, win, *([conv_w] if conv else []))
    return outs[0], (outs[1] if cast else xin), (outs[-1] if conv else None), carried


def _rel_index_rows():
    j = lax.broadcasted_iota(jnp.int32, (N_REL_PAD, KG), 1)
    r = lax.broadcasted_iota(jnp.int32, (N_REL_PAD, KG), 0)
    off = jnp.where(j < KG - 2 * CHUNK, j, j - KG)
    idx = jnp.clip(N_PREV * CHUNK - off, -REL_CLIP, REL_CLIP) + REL_CLIP
    return (idx == r).astype(f32)


def _bias_expand(table_pad, layer, comm=None):
    def body(t_ref, o_ref, row_scr):
        h = pl.program_id(0)

        @pl.when(h == 0)
        def _():
            row_scr[...] = jnp.dot(t_ref[...], _rel_index_rows(), precision=lax.Precision.HIGHEST,
                                   preferred_element_type=f32)

        q = lax.broadcasted_iota(jnp.int32, (QG, KG), 0)
        k = lax.broadcasted_iota(jnp.int32, (QG, KG), 1)
        band = (k // CHUNK >= q // CHUNK) & (k // CHUNK <= q // CHUNK + N_PREV)
        t = jnp.broadcast_to(row_scr[pl.ds(h, 1), :], (QG, KG))
        for b in range(8):
            t = jnp.where(((q >> b) & 1) == 1, pltpu.roll(t, 1 << b, axis=1), t)
        for v in range(3):
            o_ref[v] = jnp.where(band & (k >= (2 - v) * QG), t, NEG)

    (bias,), carried = _pcall_carry(
        body, comm, n_in=1, n_out=1, name=f"bias_expand_{layer}", grid=(N_HEADS,),
        out_shape=(jax.ShapeDtypeStruct((3, N_HEADS, QG, KG), f32),),
        in_specs=[pl.BlockSpec((N_HEADS, N_REL_PAD), lambda h: (0, 0))],
        out_specs=(pl.BlockSpec((3, None, QG, KG), lambda h: (0, h, 0, 0)),),
        scratch_shapes=[pltpu.VMEM((N_HEADS, KG), f32)], compiler_params=_params("arbitrary"))(table_pad)
    return bias, carried


def _bias_reduce(dbias, layer):
    def body(d_ref, o_ref, row_scr):
        q = lax.broadcasted_iota(jnp.int32, (QG, DB_COLS), 0)
        k = lax.broadcasted_iota(jnp.int32, (QG, DB_COLS), 1)
        for h in range(N_HEADS):
            t = jnp.where(k > q, d_ref[h], 0.0)
            for b in range(8):
                t = jnp.where(((q >> b) & 1) == 1, pltpu.roll(t, DB_COLS - (1 << b), axis=1), t)
            row_scr[h:h + 1, :] = jnp.sum(t, axis=0, keepdims=True)
        r = lax.broadcasted_iota(jnp.int32, (N_REL_PAD, DB_COLS), 0)
        off = lax.broadcasted_iota(jnp.int32, (N_REL_PAD, DB_COLS), 1)
        own = (off >= 1) & (off < REL_CLIP + CHUNK)
        sel = jnp.where(own & (r == 2 * REL_CLIP - off), 1.0, 0.0) - jnp.where(own & (r == 2 * REL_CLIP), 1.0, 0.0)
        o_ref[...] = lax.dot_general(row_scr[...], sel, NT, precision=lax.Precision.HIGHEST, preferred_element_type=f32)

    return _pcall(body, name=f"bias_reduce_{layer}", out_shape=jax.ShapeDtypeStruct((N_HEADS, N_REL_PAD), f32),
                  scratch_shapes=[pltpu.VMEM((N_HEADS, DB_COLS), f32)],
                  compiler_params=pltpu.CompilerParams(vmem_limit_bytes=VMEM_LIMIT))(dbias)


FWD_PAIRS = 8
BWD_PAIRS = 4


def _key_specs(n_groups, npairs, slab):
    per_slab = E_MIX // (128 * npairs)
    return [pl.BlockSpec((QG, 128 * npairs), functools.partial(
        lambda hp, g, jj: (jnp.clip(g - 2 + jj, 0, n_groups - 1), slab * per_slab + hp), jj=jj)) for jj in range(3)]


def _bias_spec(npairs):
    return pl.BlockSpec((None, 2 * npairs, QG, KG), lambda hp, g: (jnp.minimum(g, 2), hp, 0, 0))


def _attn_fwd(h, bias, layer, comm=None):
    T = h.shape[0]
    n_groups = T // QG
    scale = 1.0 / math.sqrt(HEAD_DIM)

    def body(q_ref, k0, k1, k2, v0, v1, v2, b_ref, o_ref, lse_ref):
        lane = lax.broadcasted_iota(jnp.int32, (1, 128), 1)
        ones = jnp.ones((KG, 128), bf16)
        lse = jnp.zeros((QG, 128), f32)
        for pp in range(FWD_PAIRS):
            cs = slice(pp * 128, (pp + 1) * 128)
            q2 = q_ref[:, cs] * scale
            kc = jnp.concatenate([k0[:, cs], k1[:, cs], k2[:, cs]], axis=0)
            vc = jnp.concatenate([jnp.concatenate([v0[:, cs], v1[:, cs], v2[:, cs]], axis=0), ones], axis=1)
            outs = []
            for hh in range(2):
                qm = jnp.where(lane // HEAD_DIM == hh, q2, jnp.zeros_like(q2))
                s = lax.dot_general(qm, kc, NT, preferred_element_type=f32) + b_ref[2 * pp + hh]
                m = jnp.max(s, axis=1, keepdims=True)
                ol = jnp.dot(jnp.exp(s - m).astype(bf16), vc, preferred_element_type=f32)
                outs.append(ol[:, :128] / ol[:, 128:])
                lse = jnp.where(lane == 2 * pp + hh, m + jnp.log(ol[:, 128:]), lse)
            o_ref[:, cs] = jnp.where(lane // HEAD_DIM == 0, outs[0], outs[1]).astype(bf16)
        lse_ref[...] = lse

    (mix, lse), carried = _pcall_carry(
        body, comm, n_in=8, n_out=2, name=f"attn_fwd_{layer}", grid=(N_HEADS // (2 * FWD_PAIRS), n_groups),
        out_shape=(jax.ShapeDtypeStruct((T, E_MIX), bf16), jax.ShapeDtypeStruct((T, 128), f32)),
        in_specs=[pl.BlockSpec((QG, 128 * FWD_PAIRS), lambda hp, g: (g, hp))] + _key_specs(n_groups, FWD_PAIRS, 1)
        + _key_specs(n_groups, FWD_PAIRS, 2) + [_bias_spec(FWD_PAIRS)],
        out_specs=(pl.BlockSpec((QG, 128 * FWD_PAIRS), lambda hp, g: (g, hp)), pl.BlockSpec((QG, 128), lambda hp, g: (g, 0))),
        compiler_params=_params("arbitrary", "arbitrary"))(h, h, h, h, h, h, h, bias)
    return mix, lse, carried


def _halo_rows(ref, r):
    return ref[r:r + 1, :].astype(f32)


def _conv_taps(cu, p6, p7, w_ref):
    row = lax.broadcasted_iota(jnp.int32, cu.shape, 0)
    r1 = jnp.where(row == 0, p7, pltpu.roll(cu, 1, axis=0))
    r2 = jnp.where(row == 0, p6, jnp.where(row == 1, p7, pltpu.roll(cu, 2, axis=0)))
    return w_ref[2:3, :] * cu + w_ref[1:2, :] * r1 + w_ref[0:1, :] * r2, r1, r2


def _mem_probs(qm_ref, kv_ref, hh):
    qh = qm_ref[:, hh * MEM_HEAD_DIM:(hh + 1) * MEM_HEAD_DIM]
    kh = kv_ref[:, hh * MEM_HEAD_DIM:(hh + 1) * MEM_HEAD_DIM]
    vh = kv_ref[:, E_MEM + hh * MEM_HEAD_DIM:E_MEM + (hh + 1) * MEM_HEAD_DIM]
    s = lax.dot_general(qh, kh, NT, preferred_element_type=f32) * (1.0 / math.sqrt(MEM_HEAD_DIM))
    e = jnp.exp(s - jnp.max(s, axis=1, keepdims=True))
    return e / jnp.sum(e, axis=1, keepdims=True), qh, kh, vh


def _h_tail_specs(tm):
    return [pl.BlockSpec((tm, E_MEM), functools.partial(lambda i, cb: (i, cb), cb=cb)) for cb in (6, 7, 8, 9)]


def _ln_bwd(dy, xhat, rstd, g, dob_ref, dxp_ref, dg_ref, db_ref):
    dg_ref[0:1, :] += jnp.sum(dy * xhat, axis=0, keepdims=True)
    db_ref[0:1, :] += jnp.sum(dy, axis=0, keepdims=True)
    gx = dy * g
    dr = rstd * (gx - jnp.mean(gx, axis=1, keepdims=True) - xhat * jnp.mean(gx * xhat, axis=1, keepdims=True))
    dxp_ref[...] = DN_ALPHA * dr
    dob_ref[...] = dr.astype(bf16)


def _post_fwd(h, mix, memb, wkv, wout, x, g, b, layer, target=None, comm=None):
    T = x.shape[0]
    tm = 512

    def body(qm_ref, z0, z1, z2, mix_ref, mem_ref, wkv_ref, w_ref, x_ref, g_ref, b_ref, *rest):
        *rest, ps_ref, ms_ref, kv_out, kv_ref = rest

        @pl.when(pl.program_id(0) == 0)
        def _():
            kv = jnp.dot(mem_ref[...], wkv_ref[...], preferred_element_type=f32).astype(bf16)
            kv_ref[...] = kv
            kv_out[...] = kv

        mems = []
        for hh in range(MEM_HEADS):
            p, _, _, vh = _mem_probs(qm_ref, kv_ref, hh)
            pb = p.astype(bf16)
            ps_ref[:, hh * N_MEM:(hh + 1) * N_MEM] = pb
            mems.append(jnp.dot(pb, vh, preferred_element_type=f32))
        memb = jnp.concatenate(mems, axis=1).astype(bf16)
        ms_ref[...] = memb
        z = jnp.concatenate([z0[...], z1[...], z2[...]], axis=1)
        one = jnp.ones((), bf16)
        y = jnp.concatenate([mix_ref[...], memb], axis=1) * (z * (one / (one + jnp.exp(-z))))
        out = jnp.dot(y, w_ref[...], preferred_element_type=f32)
        r = DN_ALPHA * x_ref[...] + out
        mu = jnp.mean(r, axis=1, keepdims=True)
        var = jnp.mean(jnp.square(r - mu), axis=1, keepdims=True)
        rstd = lax.rsqrt(var + LN_EPS)
        xhat = (r - mu) * rstd
        xn = xhat * g_ref[...] + b_ref[...]
        if target is None:
            xn_ref, xb_ref, xh_ref, rs_ref = rest
            xn_ref[...] = xn
            xb_ref[...] = xn.astype(bf16)
            xh_ref[...] = xhat
            rs_ref[...] = rstd
        else:
            t_ref, l_ref, dob_ref, dxp_ref, dg_ref, db_ref = rest

            @pl.when(pl.program_id(0) == 0)
            def _():
                l_ref[...] = jnp.zeros_like(l_ref)
                dg_ref[...] = jnp.zeros_like(dg_ref)
                db_ref[...] = jnp.zeros_like(db_ref)

            err = xn - t_ref[...]
            l_ref[...] += jnp.sum(jnp.square(err))
            _ln_bwd(err * (1.0 / D_MODEL), xhat, rstd, g_ref[...], dob_ref, dxp_ref, dg_ref, db_ref)

    tile = lambda w: pl.BlockSpec((tm, w), lambda i: (i, 0))
    const = lambda r, c: pl.BlockSpec((r, c), lambda i: (0, 0))
    resident = lambda r, c: pl.BlockSpec((r, c), lambda i: (0, 0), pipeline_mode=pl.Buffered(1))
    in_specs = _h_tail_specs(tm) + [tile(E_MIX), resident(N_MEM, D_MODEL), resident(D_MODEL, 2 * E_MEM),
                                    resident(E_BRANCH, D_MODEL), tile(D_MODEL), const(1, D_MODEL), const(1, D_MODEL)]
    kept = (jax.ShapeDtypeStruct((T, MEM_HEADS * N_MEM), bf16), jax.ShapeDtypeStruct((T, E_MEM), bf16),
            jax.ShapeDtypeStruct((N_MEM, 2 * E_MEM), bf16))
    kept_specs = (tile(MEM_HEADS * N_MEM), tile(E_MEM), const(N_MEM, 2 * E_MEM))
    kv_scratch = [pltpu.VMEM((N_MEM, 2 * E_MEM), bf16)]
    if target is None:
        return _pcall_carry(
            body, comm, n_in=11, n_out=7, name=f"post_fwd_{layer}", grid=(T // tm,),
            out_shape=(jax.ShapeDtypeStruct((T, D_MODEL), f32), jax.ShapeDtypeStruct((T, D_MODEL), bf16),
                       jax.ShapeDtypeStruct((T, D_MODEL), f32), jax.ShapeDtypeStruct((T, 1), f32)) + kept,
            in_specs=in_specs, out_specs=(tile(D_MODEL), tile(D_MODEL), tile(D_MODEL), tile(1)) + kept_specs,
            scratch_shapes=kv_scratch,
            compiler_params=_params("arbitrary"))(h, h, h, h, mix, memb, wkv, wout, x, g, b)
    return _pcall(
        body, name=f"post_fwd_loss_{layer}", grid=(T // tm,),
        out_shape=(jax.ShapeDtypeStruct((8, 128), f32), jax.ShapeDtypeStruct((T, D_MODEL), bf16),
                   jax.ShapeDtypeStruct((T, D_MODEL), f32), jax.ShapeDtypeStruct((8, D_MODEL), f32),
                   jax.ShapeDtypeStruct((8, D_MODEL), f32)) + kept,
        in_specs=in_specs + [tile(D_MODEL)],
        out_specs=(const(8, 128), tile(D_MODEL), tile(D_MODEL), const(8, D_MODEL), const(8, D_MODEL)) + kept_specs,
        scratch_shapes=kv_scratch,
        compiler_params=_params("arbitrary"))(h, h, h, h, mix, memb, wkv, wout, x, g, b, target)


def _post_bwd(dob, h, mix, memb, wout, kept, layer, comm=None):
    T = dob.shape[0]
    tm = 512
    n_tiles = T // tm
    inv = 1.0 / math.sqrt(MEM_HEAD_DIM)

    def body(dob_ref, qm_ref, z0, z1, z2, mix_ref, kv_ref, w_ref, ps_ref, ms_ref, mem_ref,
             dhb_ref, dmix_ref, gkv_out, dwo_out, dwo_ref, dkv_ref):
        @pl.when(pl.program_id(0) == 0)
        def _():
            dkv_ref[...] = jnp.zeros_like(dkv_ref)
            dwo_ref[...] = jnp.zeros_like(dwo_ref)

        dob = dob_ref[...]
        z = jnp.concatenate([z0[...], z1[...], z2[...]], axis=1).astype(f32)
        act, sig = _silu_parts(z)
        cat = jnp.concatenate([mix_ref[...].astype(f32), ms_ref[...].astype(f32)], axis=1)
        yb = (cat * act).astype(bf16)
        dwo_ref[...] += lax.dot_general(yb, dob, TN, preferred_element_type=f32)
        dyv = lax.dot_general(dob, w_ref[...], NT, preferred_element_type=f32)
        dz = dyv * cat * (sig * (1.0 + z * (1.0 - sig)))
        dcat = dyv * act
        dmix_ref[...] = dcat[:, :E_MIX].astype(bf16)
        dqs = []
        for hh in range(MEM_HEADS):
            cols = slice(hh * MEM_HEAD_DIM, (hh + 1) * MEM_HEAD_DIM)
            qh, kh, vh = qm_ref[:, cols], kv_ref[:, cols], kv_ref[:, E_MEM + hh * MEM_HEAD_DIM:E_MEM + (hh + 1) * MEM_HEAD_DIM]
            pb = ps_ref[:, hh * N_MEM:(hh + 1) * N_MEM]
            p = pb.astype(f32)
            dmem = dcat[:, E_MIX + hh * MEM_HEAD_DIM:E_MIX + (hh + 1) * MEM_HEAD_DIM].astype(bf16)
            dp = lax.dot_general(dmem, vh, NT, preferred_element_type=f32)
            ds = (p * (dp - jnp.sum(p * dp, axis=1, keepdims=True))).astype(bf16)
            dqs.append(jnp.dot(ds, kh, preferred_element_type=f32) * inv)
            dkv_ref[:, cols] += lax.dot_general(ds, qh, TN, preferred_element_type=f32) * inv
            dkv_ref[:, E_MEM + hh * MEM_HEAD_DIM:E_MEM + (hh + 1) * MEM_HEAD_DIM] += lax.dot_general(
                pb, dmem, TN, preferred_element_type=f32)
        dhb_ref[...] = jnp.concatenate(dqs + [dz], axis=1).astype(bf16)

        @pl.when(pl.program_id(0) == n_tiles - 1)
        def _():
            dwo_out[...] = dwo_ref[...].astype(bf16)
            gkv_out[...] = lax.dot_general(mem_ref[...], dkv_ref[...].astype(bf16), TN,
                                           preferred_element_type=f32).astype(bf16)

    tile = lambda w: pl.BlockSpec((tm, w), lambda i: (i, 0))
    const = lambda r, c: pl.BlockSpec((r, c), lambda i: (0, 0))
    resident = lambda r, c: pl.BlockSpec((r, c), lambda i: (0, 0), pipeline_mode=pl.Buffered(1))
    ps, ms, kv = kept
    return _pcall_carry(
        body, comm, n_in=11, n_out=4, name=f"post_bwd_{layer}", grid=(n_tiles,),
        out_shape=(jax.ShapeDtypeStruct((T, E_MEM + E_BRANCH), bf16), jax.ShapeDtypeStruct((T, E_MIX), bf16),
                   jax.ShapeDtypeStruct((D_MODEL, 2 * E_MEM), bf16), jax.ShapeDtypeStruct((E_BRANCH, D_MODEL), bf16)),
        in_specs=[tile(D_MODEL)] + _h_tail_specs(tm) + [tile(E_MIX), resident(N_MEM, 2 * E_MEM), resident(E_BRANCH, D_MODEL),
                                                        tile(MEM_HEADS * N_MEM), tile(E_MEM), resident(N_MEM, D_MODEL)],
        out_specs=(tile(E_MEM + E_BRANCH), tile(E_MIX), const(D_MODEL, 2 * E_MEM), const(E_BRANCH, D_MODEL)),
        scratch_shapes=[pltpu.VMEM((E_BRANCH, D_MODEL), f32), pltpu.VMEM((N_MEM, 2 * E_MEM), f32)],
        compiler_params=_params("arbitrary"))(dob, h, h, h, h, mix, kv, wout, ps, ms, memb)


def _attn_bwd(h, bias, dmix, lse, layer, comm=None):
    T = h.shape[0]
    n_groups = T // QG
    scale = 1.0 / math.sqrt(HEAD_DIM)

    def body(q_ref, k0, k1, k2, v0, v1, v2, do_ref, b_ref, lse_ref, dq_ref, dk_ref, dv_ref, db_ref, acck, accv):
        g = pl.program_id(1)

        @pl.when(g == 0)
        def _():
            acck[...] = jnp.zeros_like(acck)
            accv[...] = jnp.zeros_like(accv)
            db_ref[...] = jnp.zeros_like(db_ref)

        @pl.when(g < n_groups)
        def _():
            lane = lax.broadcasted_iota(jnp.int32, (1, 128), 1)
            first = lane // HEAD_DIM == 0
            for pp in range(BWD_PAIRS):
                cs = slice(pp * 128, (pp + 1) * 128)
                do2 = do_ref[:, cs]
                q2 = q_ref[:, cs] * scale
                kc = jnp.concatenate([k0[:, cs], k1[:, cs], k2[:, cs]], axis=0)
                vc = jnp.concatenate([v0[:, cs], v1[:, cs], v2[:, cs]], axis=0)
                q2t, do2t = q2.T, do2.T
                dqs, dks, dvs = [], [], []
                for hh in range(2):
                    hm = lane // HEAD_DIM == hh
                    head = (pl.program_id(0) * BWD_PAIRS + pp) * 2 + hh
                    lse = jnp.sum(jnp.where(lane == head, lse_ref[...], 0.0), axis=1, keepdims=True)
                    qm = jnp.where(hm, q2, jnp.zeros_like(q2))
                    dom = jnp.where(hm, do2, jnp.zeros_like(do2))
                    s = lax.dot_general(qm, kc, NT, preferred_element_type=f32) + b_ref[2 * pp + hh]
                    p = jnp.exp(s - lse)
                    dp = lax.dot_general(dom, vc, NT, preferred_element_type=f32)
                    ds = p * (dp - jnp.sum(p * dp, axis=1, keepdims=True))
                    db_ref[2 * pp + hh] += ds[:, KG - DB_COLS:]
                    dsb, pb = ds.astype(bf16), p.astype(bf16)
                    dqs.append(jnp.dot(dsb, kc, preferred_element_type=f32) * scale)
                    dks.append(jnp.dot(q2t[hh * HEAD_DIM:(hh + 1) * HEAD_DIM], dsb, preferred_element_type=f32))
                    dvs.append(jnp.dot(do2t[hh * HEAD_DIM:(hh + 1) * HEAD_DIM], pb, preferred_element_type=f32))
                dq_ref[:, cs] = jnp.where(first, dqs[0], dqs[1]).astype(bf16)
                dkc = jnp.concatenate(dks, axis=0).T
                dvc = jnp.concatenate(dvs, axis=0).T
                for jj in range(3):
                    slot = (g + 1 + jj) % 3
                    if jj == 2:
                        acck[slot, :, cs] = dkc[jj * QG:(jj + 1) * QG]
                        accv[slot, :, cs] = dvc[jj * QG:(jj + 1) * QG]
                    else:
                        acck[slot, :, cs] += dkc[jj * QG:(jj + 1) * QG]
                        accv[slot, :, cs] += dvc[jj * QG:(jj + 1) * QG]

        done = (g + 1) % 3
        dk_ref[...] = acck[done].astype(bf16)
        dv_ref[...] = accv[done].astype(bf16)

    last = n_groups - 1
    width = 128 * BWD_PAIRS
    qspec = pl.BlockSpec((QG, width), lambda hp, g: (jnp.minimum(g, last), hp))
    kout = pl.BlockSpec((QG, width), lambda hp, g: (jnp.clip(g - 2, 0, last), hp))
    dbspec = pl.BlockSpec((2 * BWD_PAIRS, QG, DB_COLS), lambda hp, g: (hp, 0, 0))
    lspec = pl.BlockSpec((QG, 128), lambda hp, g: (jnp.minimum(g, last), 0))
    return _pcall_carry(
        body, comm, n_in=10, n_out=4, name=f"attn_bwd_{layer}", grid=(N_HEADS // (2 * BWD_PAIRS), n_groups + 2),
        out_shape=(jax.ShapeDtypeStruct((T, E_MIX), bf16),) * 3 + (jax.ShapeDtypeStruct((N_HEADS, QG, DB_COLS), f32),),
        in_specs=[qspec] + _key_specs(n_groups, BWD_PAIRS, 1) + _key_specs(n_groups, BWD_PAIRS, 2)
        + [qspec, _bias_spec(BWD_PAIRS), lspec],
        out_specs=(qspec, kout, kout, dbspec),
        scratch_shapes=[pltpu.VMEM((3, QG, width), f32), pltpu.VMEM((3, QG, width), f32)],
        compiler_params=_params("arbitrary", "arbitrary"))(h, h, h, h, h, h, h, dmix, bias, lse)


def _conv_bwd(h, w, dmix, layer):
    T = h.shape[0]
    tm = 512
    n_tiles = T // tm

    def body(bg_ref, cg_ref, u_ref, cgp_ref, up_ref, dy_ref, bgn_ref, dyn_ref, w_ref, dbg_ref, dcg_ref, du_ref, dw_ref):
        i = pl.program_id(0)

        @pl.when(i == 0)
        def _():
            dw_ref[...] = jnp.zeros_like(dw_ref)

        first = (i == 0).astype(f32)
        final = (i == n_tiles - 1).astype(f32)
        bg, cg, u = bg_ref[...].astype(f32), cg_ref[...].astype(f32), u_ref[...].astype(f32)
        dy = dy_ref[...].astype(f32)
        cu = cg * u
        p6 = _halo_rows(cgp_ref, 14) * _halo_rows(up_ref, 14) * (1.0 - first)
        p7 = _halo_rows(cgp_ref, 15) * _halo_rows(up_ref, 15) * (1.0 - first)
        conv, r1, r2 = _conv_taps(cu, p6, p7, w_ref)
        dbg_ref[...] = (dy * conv).astype(bf16)
        dc = dy * bg
        n0 = _halo_rows(dyn_ref, 0) * _halo_rows(bgn_ref, 0) * (1.0 - final)
        n1 = _halo_rows(dyn_ref, 1) * _halo_rows(bgn_ref, 1) * (1.0 - final)
        row = lax.broadcasted_iota(jnp.int32, dc.shape, 0)
        f1 = jnp.where(row == tm - 1, n0, pltpu.roll(dc, tm - 1, axis=0))
        f2 = jnp.where(row == tm - 2, n0, jnp.where(row == tm - 1, n1, pltpu.roll(dc, tm - 2, axis=0)))
        dcu = w_ref[2:3, :] * dc + w_ref[1:2, :] * f1 + w_ref[0:1, :] * f2
        dcg_ref[...] = (dcu * u).astype(bf16)
        du_ref[...] = (dcu * cg).astype(bf16)
        dw_ref[0:1, :] += jnp.sum(dc * r2, axis=0, keepdims=True)
        dw_ref[1:2, :] += jnp.sum(dc * r1, axis=0, keepdims=True)
        dw_ref[2:3, :] += jnp.sum(dc * cu, axis=0, keepdims=True)

    tile = lambda slab: pl.BlockSpec((tm, E_MIX), lambda i: (i, slab))
    prev = lambda slab: pl.BlockSpec((16, E_MIX), lambda i: (jnp.maximum(i * (tm // 16) - 1, 0), slab))
    nxt = lambda slab: pl.BlockSpec((16, E_MIX), lambda i: (jnp.minimum((i + 1) * (tm // 16), T // 16 - 1), slab))
    return _pcall(
        body, name=f"conv_bwd_{layer}", grid=(n_tiles,),
        out_shape=(jax.ShapeDtypeStruct((T, E_MIX), bf16),) * 3 + (jax.ShapeDtypeStruct((8, E_MIX), f32),),
        in_specs=[tile(0), tile(1), tile(2), prev(1), prev(2), tile(0), nxt(0), nxt(0),
                  pl.BlockSpec((CONV_W, E_MIX), lambda i: (0, 0))],
        out_specs=(tile(0), tile(0), tile(0), pl.BlockSpec((8, E_MIX), lambda i: (0, 0))),
        compiler_params=_params("arbitrary"))(h, h, h, h, h, dmix, h, dmix, w)


def _inproj_bwd_dx(da, db, dc, dhb, dxp, win, layer, below=None, comm=None):
    T = dxp.shape[0]
    tm = 512

    def body(da_ref, db_ref, dc_ref, dhb_ref, dxp_ref, w_ref, *rest):
        dh = jnp.concatenate([da_ref[...], db_ref[...], dc_ref[...], dhb_ref[...]], axis=1)
        acc = dxp_ref[...]
        for j in range(N_CHIPS):
            acc = acc + lax.dot_general(dh[:, j * W_IN_COLS:(j + 1) * W_IN_COLS], w_ref[j], NT, preferred_element_type=f32)
        if below is None:
            rest[0][...] = acc
        else:
            xh_ref, rs_ref, g_ref, dob_ref, dxo_ref, dg_ref, db_out = rest

            @pl.when(pl.program_id(0) == 0)
            def _():
                dg_ref[...] = jnp.zeros_like(dg_ref)
                db_out[...] = jnp.zeros_like(db_out)

            _ln_bwd(acc, xh_ref[...], rs_ref[...], g_ref[...], dob_ref, dxo_ref, dg_ref, db_out)

    tile = lambda w: pl.BlockSpec((tm, w), lambda i: (i, 0))
    const = lambda r, c: pl.BlockSpec((r, c), lambda i: (0, 0))
    in_specs = [tile(E_MIX), tile(E_MIX), tile(E_MIX), tile(E_MEM + E_BRANCH), tile(D_MODEL),
                pl.BlockSpec((N_CHIPS, D_MODEL, W_IN_COLS), lambda i: (0, 0, 0), pipeline_mode=pl.Buffered(1))]
    if below is None:
        return _pcall_carry(
            body, comm, n_in=6, n_out=1, name=f"inproj_bwd_dx_{layer}", grid=(T // tm,),
            out_shape=(jax.ShapeDtypeStruct((T, D_MODEL), f32),), in_specs=in_specs, out_specs=(tile(D_MODEL),),
            compiler_params=_params("arbitrary"))(da, db, dc, dhb, dxp, win)
    return _pcall_carry(
        body, comm, n_in=9, n_out=4, name=f"inproj_bwd_dx_{layer}", grid=(T // tm,),
        out_shape=(jax.ShapeDtypeStruct((T, D_MODEL), bf16), jax.ShapeDtypeStruct((T, D_MODEL), f32),
                   jax.ShapeDtypeStruct((8, D_MODEL), f32), jax.ShapeDtypeStruct((8, D_MODEL), f32)),
        in_specs=in_specs + [tile(D_MODEL), tile(1), const(1, D_MODEL)],
        out_specs=(tile(D_MODEL), tile(D_MODEL), const(8, D_MODEL), const(8, D_MODEL)),
        compiler_params=_params("arbitrary"))(da, db, dc, dhb, dxp, win, *below)


def _dh_pieces():
    pieces, col = [], 0
    for src, width in enumerate((E_MIX, E_MIX, E_MIX, E_MEM + E_BRANCH)):
        lo = 0
        while lo < width:
            j, c0 = divmod(col + lo, W_IN_COLS)
            n = min(width - lo, W_IN_COLS - c0)
            pieces.append((src, lo, lo + n, j, c0, c0 + n))
            lo += n
        col += width
    return pieces


def _inproj_bwd_dw(da, db, dc, dhb, xb, layer, comm=None):
    T = xb.shape[0]
    tm = 1024 if T % 1024 == 0 else 512
    n_tiles = T // tm

    def body(da_ref, db_ref, dc_ref, dhb_ref, x_ref, o_ref, acc, stage, sem):
        i = pl.program_id(0)

        @pl.when(i == 0)
        def _():
            acc[...] = jnp.zeros_like(acc)

        srcs = (da_ref, db_ref, dc_ref, dhb_ref)
        xt = x_ref[...]
        for s, s0, s1, j, c0, c1 in _dh_pieces():
            acc[j, :, c0:c1] += lax.dot_general(xt, srcs[s][:, s0:s1], TN, preferred_element_type=f32)

        @pl.when(i == n_tiles - 1)
        def _():
            for j in range(N_CHIPS):
                stage[...] = acc[j].astype(bf16)
                cp = pltpu.make_async_copy(stage, o_ref.at[j], sem)
                cp.start()
                cp.wait()

    tile = lambda w: pl.BlockSpec((tm, w), lambda i: (i, 0))
    (dw,), carried = _pcall_carry(
        body, comm, n_in=5, n_out=1, name=f"inproj_bwd_dw_{layer}", grid=(n_tiles,),
        out_shape=(jax.ShapeDtypeStruct((N_CHIPS, D_MODEL, W_IN_COLS), bf16),),
        in_specs=[tile(E_MIX), tile(E_MIX), tile(E_MIX), tile(E_MEM + E_BRANCH), tile(D_MODEL)],
        out_specs=(ANY,),
        scratch_shapes=[pltpu.VMEM((N_CHIPS, D_MODEL, W_IN_COLS), f32), pltpu.VMEM((D_MODEL, W_IN_COLS), bf16),
                        pltpu.SemaphoreType.DMA],
        compiler_params=_params("arbitrary"))(da, db, dc, dhb, xb)
    return dw, carried


def _adamw(w, g, m, v, name):
    shape = w.shape
    cols = shape[-1]
    rows = w.size // cols
    args = [a.reshape(rows, cols) for a in (w, g, m, v)]
    br = 256 if rows % 256 == 0 and rows > 256 else rows

    def body(w_ref, g_ref, m_ref, v_ref, go_ref, d_ref, nm_ref, nv_ref):
        gg = g_ref[...]
        nm = ADAM_B1 * m_ref[...] + (1.0 - ADAM_B1) * gg
        nv = ADAM_B2 * v_ref[...] + (1.0 - ADAM_B2) * jnp.square(gg)
        m_hat = nm / (1.0 - ADAM_B1 ** ADAM_STEP)
        v_hat = nv / (1.0 - ADAM_B2 ** ADAM_STEP)
        go_ref[...] = gg
        d_ref[...] = -ADAM_LR * (m_hat / (jnp.sqrt(v_hat) + ADAM_EPS) + ADAM_WD * w_ref[...])
        nm_ref[...] = nm
        nv_ref[...] = nv

    spec = pl.BlockSpec((br, cols), lambda i: (i, 0))
    outs = _pcall(body, name=name, grid=(rows // br,), out_shape=(jax.ShapeDtypeStruct((rows, cols), f32),) * 4,
                  in_specs=[spec] * 4, out_specs=(spec,) * 4, compiler_params=_params("arbitrary"))(*args)
    return tuple(o.reshape(shape) for o in outs)


def kernel(x, mem, w_in, w_mem_kv, w_out, rel_bias, conv_w, ln_g, ln_b, loss_target, m_w_in, m_w_mem_kv, m_w_out, m_rel_bias, m_conv_w, m_ln_g, m_ln_b, v_w_in, v_w_mem_kv, v_w_out, v_rel_bias, v_conv_w, v_ln_g, v_ln_b):
    T = x.shape[1]
    x0 = x.reshape(T, D_MODEL)
    target = loss_target.reshape(T, D_MODEL)
    memb = mem.reshape(N_MEM, D_MODEL).astype(bf16)
    chip = 2 * lax.axis_index("x") + lax.axis_index("y")
    core = lax.axis_index("c")
    chip_arr = jnp.reshape(chip, (1,)).astype(jnp.int32)
    core_arr = jnp.reshape(core, (1,)).astype(jnp.int32)

    place = jnp.concatenate([chip_arr, core_arr])
    tables = jnp.pad(rel_bias, ((0, 0), (0, 0), (0, N_REL_PAD - N_REL)))

    shards = [w_in.astype(bf16), w_mem_kv.astype(bf16), w_out.astype(bf16)]
    biases = {}
    biases[0], near = _bias_expand(tables[0], 0, _gather_ici(shards[:1], 0, peers=(0, 1)))
    biases[2], far = _bias_expand(tables[1], 2, _gather_ici(shards[:1], 0, peers=(2,), own=False, into=near))
    gathered = {0: list(_comm_call(_gather_d2d(list(far)), "gather_d2d_0")) + [None, None]}
    conv_full = None

    xs, xbs, hs, mixes, kepts, xhats, rstds, lses = [x0], [x0], [], [], [], [], [], {}
    for layer in range(DEPTH):
        more = layer + 1 < DEPTH
        attention = layer % 2 == 0
        h, xbs[layer], mix, arrived = _inproj(xbs[layer], gathered[layer][0], layer,
                                              _gather_ici(shards, layer + 1) if more else None,
                                              None if attention else conv_full[layer // 2])
        passing = _gather_d2d(list(arrived)) if more else None
        if layer == 0:
            passing = _both(passing, _gather_ici(shards[1:], 0, extra=conv_w))
        if attention:
            mix, lses[layer], done = _attn_fwd(h, biases[layer], layer, passing)
        if layer == 0:
            *done, wkv_in, wout_in, cw_g = done
            gathered[0][1:] = _comm_call(_gather_d2d([wkv_in, wout_in]), "gather_d2d_0_rest")
            conv_full = jnp.transpose(cw_g, (1, 2, 0, 3)).reshape(DEPTH // 2, CONV_W, E_MIX)
        win, wkv, wout = gathered[layer]
        result = _post_fwd(h, mix, memb, wkv.reshape(D_MODEL, 2 * E_MEM), wout.reshape(E_BRANCH, D_MODEL), xs[layer],
                           ln_g[layer][None, :], ln_b[layer][None, :], layer, None if more else target,
                           None if attention else passing)
        if more:
            (xn, xnb, xhat, rstd, *kept), done = (result[0], done) if attention else result
            xs.append(xn); xbs.append(xnb); xhats.append(xhat); rstds.append(rstd)
            gathered[layer + 1] = list(done)
        else:
            lsum, dob, dxp, dg_last, db_last, *kept = result
        hs.append(h); mixes.append(mix); kepts.append(kept)

    dgs, dbs, dconvs, dtables = [None] * DEPTH, [None] * DEPTH, [None] * (DEPTH // 2), [None] * ((DEPTH + 1) // 2)
    dgs[DEPTH - 1], dbs[DEPTH - 1] = dg_last, db_last
    loss = lax.psum(lsum[0, 0], ("x", "y", "c")) * (0.5 / D_MODEL)
    finals = [None, None, None]
    above = None
    for layer in reversed(range(DEPTH)):
        h = hs[layer]
        win, wkv, wout = gathered[layer]
        (dhb, dmix, g_wkv, dwo), from_sibling = _post_bwd(
            dob, h, mixes[layer], memb, wout.reshape(E_BRANCH, D_MODEL), kepts[layer], layer,
            _sibling_exchange(above) if above else None)
        sums = [_pair_sum(g, r, core_arr, layer + 1) for g, r in zip(above, from_sibling)] if above else None
        scatter = _chip_scatter(sums) if above else None
        if layer % 2 == 0:
            (da, db, dc, dbias), from_chips = _attn_bwd(h, biases[layer], dmix, lses[layer], layer, scatter)
            dtables[layer // 2] = _bias_reduce(dbias, layer)
        else:
            da, db, dc, dconvs[layer // 2] = _conv_bwd(h, conv_full[layer // 2], dmix, layer)
        if layer > 0:
            (dob_below, dxp_below, dgs[layer - 1], dbs[layer - 1]), landed = _inproj_bwd_dx(
                da, db, dc, dhb, dxp, win, layer, (xhats[layer - 1], rstds[layer - 1], ln_g[layer - 1][None, :]),
                scatter if layer % 2 == 1 else None)
            from_chips = landed if layer % 2 == 1 else from_chips
        share = None
        if above:
            finals = [_chip_sum(s, r, place, layer + 1, f) for s, r, f in zip(sums, from_chips, finals)]
            share = _sibling_share(finals, layer + 1)
        if layer == 0:
            pad8 = lambda a: jnp.pad(a, ((0, 8 - a.shape[0]), (0, 0)))
            small_mine = jnp.concatenate(dgs + dbs + dconvs + [pad8(t.reshape(-1, D_MODEL)) for t in dtables], axis=0)
            share = _both(share, _small_exchange(small_mine))
        g_win, shared = _inproj_bwd_dw(da, db, dc, dhb, xbs[layer], layer, share)
        if layer == 0:
            *shared, small_slots = shared
        finals = list(shared) if above else finals
        above = [g_win, g_wkv.reshape(N_CHIPS, W_KV_ROWS, 2 * E_MEM),
                 dwo.reshape(N_CHIPS, W_OUT_ROWS, D_MODEL)]
        if layer > 0:
            dob, dxp = dob_below, dxp_below
    from_sibling = _comm_call(_sibling_exchange(above), "sibling_exchange_0")
    sums = [_pair_sum(g, r, core_arr, 0) for g, r in zip(above, from_sibling)]
    (dx,), from_chips = _inproj_bwd_dx(da, db, dc, dhb, dxp, win, 0, None, _chip_scatter(sums))
    finals = [_chip_sum(s, r, place, 0, f) for s, r, f in zip(sums, from_chips, finals)]
    grad_w_in, grad_w_mem_kv, grad_w_out = _comm_call(_sibling_share(finals, 0), "sibling_share_0")
    grad_x = dx.reshape(1, T, D_MODEL)

    device_arr = jnp.reshape(2 * chip + core, (1,)).astype(jnp.int32)
    small = _small_sum(small_slots, small_mine, device_arr)
    grad_ln_g = jnp.stack([small[8 * l] for l in range(DEPTH)])
    grad_ln_b = jnp.stack([small[8 * (DEPTH + l)] for l in range(DEPTH)])
    conv_all = jnp.stack([small[8 * (2 * DEPTH + a):8 * (2 * DEPTH + a) + CONV_W] for a in range(DEPTH // 2)])
    grad_conv_w = lax.dynamic_slice_in_dim(conv_all, chip * (E_MIX // N_CHIPS), E_MIX // N_CHIPS, axis=2)
    t0 = 8 * (2 * DEPTH + DEPTH // 2)
    grad_rel_bias = jnp.stack([small[t0 + 8 * a:t0 + 8 * a + 6].reshape(N_HEADS, N_REL_PAD)[:, :N_REL]
                               for a in range((DEPTH + 1) // 2)])

    grads = [grad_w_in, grad_w_mem_kv, grad_w_out, grad_rel_bias, grad_conv_w, grad_ln_g, grad_ln_b]
    weights = [w_in, w_mem_kv, w_out, rel_bias, conv_w, ln_g, ln_b]
    moms = [m_w_in, m_w_mem_kv, m_w_out, m_rel_bias, m_conv_w, m_ln_g, m_ln_b]
    vels = [v_w_in, v_w_mem_kv, v_w_out, v_rel_bias, v_conv_w, v_ln_g, v_ln_b]
    names = ["w_in", "w_mem_kv", "w_out", "rel_bias", "conv_w", "ln_g", "ln_b"]
    upd = [_adamw(w, g, m, v, f"adamw_{n}") for w, g, m, v, n in zip(weights, grads, moms, vels, names)]
    grads, deltas, new_m, new_v = zip(*upd)
    return (loss, grad_x, *grads, *deltas, *new_m, *new_v)
```

```python
import functools
import math

import jax
import jax.numpy as jnp
from jax import lax
from jax.experimental import pallas as pl
from jax.experimental.pallas import tpu as pltpu

f32, bf16 = jnp.float32, jnp.bfloat16

D_MODEL = 1024
DEPTH = 4
CHUNK = 64
N_PREV = 8
N_HEADS = 16
HEAD_DIM = 64
E_MIX = 1024
REL_CLIP = 128
N_REL = 2 * REL_CLIP + 1
N_REL_PAD = 384
CONV_W = 3
N_MEM = 256
MEM_HEADS = 4
MEM_HEAD_DIM = 128
E_MEM = 512
E_BRANCH = E_MIX + E_MEM
N_IN = 3 * E_MIX + E_MEM + E_BRANCH
N_CHIPS = 4
W_IN_COLS = N_IN // N_CHIPS
W_KV_ROWS = D_MODEL // N_CHIPS
W_OUT_ROWS = E_BRANCH // N_CHIPS
DN_ALPHA = (2.0 * DEPTH) ** 0.25
LN_EPS = 1e-5
ADAM_LR, ADAM_B1, ADAM_B2, ADAM_EPS, ADAM_WD, ADAM_STEP = 0.001, 0.9, 0.999, 1e-08, 0.01, 10

QG = 4 * CHUNK
KG = QG + N_PREV * CHUNK
DB_COLS = KG // 2
NEG = -1e30
VMEM_LIMIT = 56 * 1024 * 1024

NT = (((1,), (1,)), ((), ()))
TN = (((0,), (0,)), ((), ()))
MESH = pl.DeviceIdType.MESH
ANY = pl.BlockSpec(memory_space=pl.ANY)


def _pcall(body, **kw):
    return pl.pallas_call(body, **kw)


def _params(*sem):
    return pltpu.CompilerParams(dimension_semantics=sem, vmem_limit_bytes=VMEM_LIMIT)


def _silu_parts(z):
    sig = 1.0 / (1.0 + jnp.exp(-z))
    return z * sig, sig


class _Comm:
    def __init__(self, inputs, out_shapes, aliases, n_sems, copies):
        self.inputs, self.out_shapes, self.aliases, self.n_sems, self.copies = inputs, out_shapes, aliases, n_sems, copies

    def start(self, cin, cout, send, recv):
        for cp in self.copies(cin, cout, send, recv)[0]:
            cp.start()

    def wait(self, cin, cout, send, recv):
        sends, recvs = self.copies(cin, cout, send, recv)
        for cp in recvs:
            cp.wait_recv()
        for cp in sends:
            cp.wait_send()


def _pcall_carry(body, comm, *, n_in, n_out, **kw):
    if comm is None:
        return lambda *args: (_pcall(body, **kw)(*args), ())
    grid = kw["grid"]
    k_in, k_out = len(comm.inputs), len(comm.out_shapes)

    def carried(*refs):
        ins, cin = refs[:n_in], refs[n_in:n_in + k_in]
        outs = refs[n_in + k_in:n_in + k_in + n_out]
        cout = refs[n_in + k_in + n_out:n_in + k_in + n_out + k_out]
        scratch, send, recv = refs[n_in + k_in + n_out + k_out:-2], refs[-2], refs[-1]
        ids = [pl.program_id(a) for a in range(len(grid))]
        first = functools.reduce(jnp.logical_and, [i == 0 for i in ids])
        last = functools.reduce(jnp.logical_and, [i == n - 1 for i, n in zip(ids, grid)])

        @pl.when(first)
        def _():
            comm.start(cin, cout, send, recv)

        body(*ins, *outs, *scratch)

        @pl.when(last)
        def _():
            comm.wait(cin, cout, send, recv)

    kw = dict(kw)
    kw["in_specs"] = list(kw["in_specs"]) + [ANY] * k_in
    kw["out_specs"] = tuple(kw["out_specs"]) + (ANY,) * k_out
    kw["out_shape"] = tuple(kw["out_shape"]) + tuple(comm.out_shapes)
    kw["scratch_shapes"] = list(kw.get("scratch_shapes", ())) + [pltpu.SemaphoreType.DMA((comm.n_sems,))] * 2
    aliases = dict(kw.get("input_output_aliases", {}))
    aliases.update({n_in + ci: n_out + co for ci, co in comm.aliases.items()})
    kw["input_output_aliases"] = aliases

    def run(*args):
        res = _pcall(carried, **kw)(*args, *comm.inputs)
        return res[:n_out], res[n_out:]

    return run


def _comm_call(comm, name):
    k_in = len(comm.inputs)

    def body(*refs):
        cin, cout, send, recv = refs[:k_in], refs[k_in:-2], refs[-2], refs[-1]
        comm.start(cin, cout, send, recv)
        comm.wait(cin, cout, send, recv)

    return _pcall(body, name=name, out_shape=tuple(comm.out_shapes), in_specs=[ANY] * k_in,
                  out_specs=(ANY,) * len(comm.out_shapes), input_output_aliases=dict(comm.aliases),
                  scratch_shapes=[pltpu.SemaphoreType.DMA((comm.n_sems,))] * 2)(*comm.inputs)


def _place():
    x, y, c = lax.axis_index("x"), lax.axis_index("y"), lax.axis_index("c")
    return x, y, c, 2 * x + y, (x, y, 1 - c), [(1 - x, y), (x, 1 - y), (1 - x, 1 - y)]


def _rcopy(send, recv, k, src, dst, to):
    return pltpu.make_async_remote_copy(src_ref=src, dst_ref=dst, send_sem=send.at[k], recv_sem=recv.at[k],
                                        device_id=to, device_id_type=MESH)


def _half(ref_rows, core):
    return pl.ds(core * (ref_rows // 2), ref_rows // 2)


def _gather_ici(shards, layer, extra=None, peers=(0, 1, 2), own=True, into=None):
    extras = [] if extra is None else [extra]
    n = len(shards)

    def copies(cin, cout, send, recv):
        x, y, c, me, sibling, chips = _place()
        sends, recvs = [], []
        for a in range(n + len(extras)):
            s, g = cin[a], cout[a]
            whole = a >= n
            src = s if whole else s.at[layer]
            if own:
                sends.append(_rcopy(send, recv, 4 * a, src, g.at[me], sibling))
                recvs.append(_rcopy(send, recv, 4 * a, src, g.at[me], sibling))
            for p in peers:
                px, py = chips[p]
                if whole:
                    sends.append(_rcopy(send, recv, 4 * a + 1 + p, src, g.at[me], (px, py, c)))
                    recvs.append(_rcopy(send, recv, 4 * a + 1 + p, src, g.at[2 * px + py], (px, py, c)))
                else:
                    mine = _half(s.shape[1], c)
                    sends.append(_rcopy(send, recv, 4 * a + 1 + p, s.at[layer, mine], g.at[me, mine], (px, py, c)))
                    recvs.append(_rcopy(send, recv, 4 * a + 1 + p, s.at[layer, mine], g.at[2 * px + py, mine], (px, py, c)))
        return sends, recvs

    sources = list(shards) + extras
    if into is None:
        out_shapes = [jax.ShapeDtypeStruct((N_CHIPS,) + s.shape[1:], s.dtype) for s in shards]
        out_shapes += [jax.ShapeDtypeStruct((N_CHIPS,) + e.shape, e.dtype) for e in extras]
        return _Comm(sources, out_shapes, {}, 4 * len(sources), copies)
    return _Comm(sources + list(into), [jax.ShapeDtypeStruct(g.shape, g.dtype) for g in into],
                 {len(sources) + a: a for a in range(len(into))}, 4 * len(sources), copies)


def _gather_d2d(gathered):
    n = len(gathered)

    def copies(cin, cout, send, recv):
        x, y, c, me, sibling, chips = _place()
        sends, recvs = [], []
        for a in range(n):
            g = cout[a]
            rows = g.shape[1]
            for p, (px, py) in enumerate(chips):
                mine, theirs = g.at[2 * px + py, _half(rows, c)], g.at[2 * px + py, _half(rows, 1 - c)]
                sends.append(_rcopy(send, recv, 3 * a + p, mine, mine, sibling))
                recvs.append(_rcopy(send, recv, 3 * a + p, theirs, theirs, sibling))
        return sends, recvs

    return _Comm(list(gathered), [jax.ShapeDtypeStruct(g.shape, g.dtype) for g in gathered],
                 {a: a for a in range(n)}, 3 * n, copies)


class _SemView:
    def __init__(self, ref, base):
        self.ref, self.base, self.at = ref, base, self

    def __getitem__(self, k):
        return self.ref.at[self.base + k]


def _both(a, b):
    ka, ma = len(a.inputs), len(a.out_shapes)

    def copies(cin, cout, send, recv):
        sa, ra = a.copies(cin[:ka], cout[:ma], send, recv)
        sb, rb = b.copies(cin[ka:], cout[ma:], _SemView(send, a.n_sems), _SemView(recv, a.n_sems))
        return sa + sb, ra + rb

    aliases = dict(a.aliases)
    aliases.update({ka + ci: ma + co for ci, co in b.aliases.items()})
    return _Comm(a.inputs + b.inputs, a.out_shapes + b.out_shapes, aliases, a.n_sems + b.n_sems, copies)


def _small_exchange(buf):
    def copies(cin, cout, send, recv):
        x, y, c, me, sibling, chips = _place()
        sends, recvs = [], []
        for r in range(1, 8):
            px, py, pc = x ^ ((r >> 2) & 1), y ^ ((r >> 1) & 1), c ^ (r & 1)
            sends.append(_rcopy(send, recv, r - 1, cin[0], cout[0].at[2 * me + c], (px, py, pc)))
            recvs.append(_rcopy(send, recv, r - 1, cin[0], cout[0].at[4 * px + 2 * py + pc], (px, py, pc)))
        return sends, recvs

    return _Comm([buf], [jax.ShapeDtypeStruct((8,) + buf.shape, buf.dtype)], {}, 7, copies)


def _small_sum(slots, buf, device_arr):
    rows, cols = buf.shape

    def body(d_ref, s_ref, b_ref, o_ref):
        d = pl.program_id(0)
        val = jnp.where(d == d_ref[0], b_ref[...], s_ref[...])

        @pl.when(d == 0)
        def _():
            o_ref[...] = val

        @pl.when(d > 0)
        def _():
            o_ref[...] += val

    return _pcall(
        body, name="small_sum", out_shape=jax.ShapeDtypeStruct((rows, cols), f32),
        grid_spec=pltpu.PrefetchScalarGridSpec(
            num_scalar_prefetch=1, grid=(8,),
            in_specs=[pl.BlockSpec((None, rows, cols), lambda d, d_ref: (jnp.where(d == d_ref[0], (d + 1) % 8, d), 0, 0)),
                      pl.BlockSpec((rows, cols), lambda d, d_ref: (0, 0))],
            out_specs=pl.BlockSpec((rows, cols), lambda d, d_ref: (0, 0))),
        compiler_params=_params("arbitrary"))(device_arr, slots, buf)


def _sibling_exchange(gs):
    def copies(cin, cout, send, recv):
        x, y, c, me, sibling, chips = _place()
        sends = [_rcopy(send, recv, a, g.at[:, _half(g.shape[1], 1 - c)], r, sibling) for a, (g, r) in enumerate(zip(cin, cout))]
        return sends, sends

    shapes = [jax.ShapeDtypeStruct((N_CHIPS, g.shape[1] // 2, g.shape[2]), g.dtype) for g in gs]
    return _Comm(list(gs), shapes, {}, len(gs), copies)


def _chip_scatter(ss):
    def copies(cin, cout, send, recv):
        x, y, c, me, sibling, chips = _place()
        sends = [_rcopy(send, recv, 3 * a + p, s.at[2 * px + py], r.at[p], (px, py, c))
                 for a, (s, r) in enumerate(zip(cin, cout)) for p, (px, py) in enumerate(chips)]
        return sends, sends

    shapes = [jax.ShapeDtypeStruct((3,) + s.shape[1:], s.dtype) for s in ss]
    return _Comm(list(ss), shapes, {}, 3 * len(ss), copies)


def _sibling_share(fs, layer):
    def copies(cin, cout, send, recv):
        x, y, c, me, sibling, chips = _place()
        sends, recvs = [], []
        for a, f in enumerate(cout):
            mine, theirs = f.at[layer, _half(f.shape[1], c)], f.at[layer, _half(f.shape[1], 1 - c)]
            sends.append(_rcopy(send, recv, a, mine, mine, sibling))
            recvs.append(_rcopy(send, recv, a, theirs, theirs, sibling))
        return sends, recvs

    return _Comm(list(fs), [jax.ShapeDtypeStruct(f.shape, f.dtype) for f in fs], {a: a for a in range(len(fs))},
                 len(fs), copies)


def _sum_rows(rows):
    return next(b for b in (256, 192, 128) if rows % b == 0)


def _pair_sum(g, r, core_arr, layer):
    _, rows, cols = r.shape
    br = _sum_rows(rows)
    nb = rows // br

    def body(c_ref, g_ref, r_ref, o_ref):
        o_ref[...] = (g_ref[...].astype(f32) + r_ref[...].astype(f32)).astype(bf16)

    return _pcall(
        body, name=f"pair_sum_{layer}", out_shape=jax.ShapeDtypeStruct(r.shape, bf16),
        grid_spec=pltpu.PrefetchScalarGridSpec(
            num_scalar_prefetch=1, grid=(N_CHIPS, nb),
            in_specs=[pl.BlockSpec((1, br, cols), lambda j, i, c_ref: (j, c_ref[0] * nb + i, 0)),
                      pl.BlockSpec((1, br, cols), lambda j, i, c_ref: (j, i, 0))],
            out_specs=pl.BlockSpec((1, br, cols), lambda j, i, c_ref: (j, i, 0))),
        compiler_params=_params("arbitrary", "arbitrary"))(core_arr, g, r)


def _chip_sum(s, r, place, layer, final):
    _, rows, cols = s.shape
    br = _sum_rows(rows)
    nb = rows // br

    def body(place_ref, s_ref, r_ref, *rest):
        o_ref = rest[-1]
        acc = s_ref[0].astype(f32)
        for p in range(3):
            acc = acc + r_ref[p].astype(f32)
        o_ref[...] = acc

    carried = [] if final is None else [final]
    return _pcall(
        body, name=f"chip_sum_{layer}", out_shape=jax.ShapeDtypeStruct((DEPTH, 2 * rows, cols), f32),
        grid_spec=pltpu.PrefetchScalarGridSpec(
            num_scalar_prefetch=1, grid=(nb,),
            in_specs=[pl.BlockSpec((1, br, cols), lambda i, place_ref: (place_ref[0], i, 0)),
                      pl.BlockSpec((3, br, cols), lambda i, place_ref: (0, i, 0))] + [ANY] * len(carried),
            out_specs=pl.BlockSpec((None, br, cols), lambda i, place_ref: (layer, place_ref[1] * nb + i, 0))),
        input_output_aliases={3: 0} if carried else {},
        compiler_params=_params("arbitrary"))(place, s, r, *carried)


def _inproj(xin, win, layer, comm=None, conv_w=None):
    T = xin.shape[0]
    tm = 512
    cast = xin.dtype != bf16
    conv = conv_w is not None

    def body(x_ref, w_ref, *rest):
        rest = list(rest)
        cw_ref = rest.pop(0) if conv else None
        o_ref = rest.pop(0)
        xt = x_ref[...].astype(bf16)
        if cast:
            rest.pop(0)[...] = xt
        if conv:
            mix_ref, halo = rest

            @pl.when(pl.program_id(0) == 0)
            def _():
                halo[...] = jnp.zeros_like(halo)

        accs = []
        for j in range(N_CHIPS):
            if conv and j == N_CHIPS - 1:
                cut1, cut2 = 2 * E_MIX - W_IN_COLS, 3 * E_MIX - 2 * W_IN_COLS
                bg = accs[0][:, :E_MIX]
                cg = jnp.concatenate([accs[0][:, E_MIX:], accs[1][:, :cut1]], axis=1)
                u = jnp.concatenate([accs[1][:, cut1:], accs[2][:, :cut2]], axis=1)
                cu = cg * u
                out, _, _ = _conv_taps(cu, halo[6:7, :], halo[7:8, :], cw_ref)
                mix_ref[...] = (bg * out).astype(bf16)
                halo[...] = cu[tm - 8:, :]
            acc = jnp.dot(xt, w_ref[j], preferred_element_type=f32)
            o_ref[:, j * W_IN_COLS:(j + 1) * W_IN_COLS] = acc.astype(bf16)
            if conv:
                accs.append(acc)

    tile = pl.BlockSpec((tm, D_MODEL), lambda i: (i, 0))
    outs, carried = _pcall_carry(
        body, comm, n_in=2 + conv, n_out=1 + cast + conv, name=f"inproj_{layer}", grid=(T // tm,),
        out_shape=(jax.ShapeDtypeStruct((T, N_IN), bf16),) + (jax.ShapeDtypeStruct((T, D_MODEL), bf16),) * (cast + conv),
        in_specs=[tile, pl.BlockSpec((N_CHIPS, D_MODEL, W_IN_COLS), lambda i: (0, 0, 0), pipeline_mode=pl.Buffered(1))]
        + [pl.BlockSpec((CONV_W, E_MIX), lambda i: (0, 0))] * conv,
        out_specs=(pl.BlockSpec((tm, N_IN), lambda i: (i, 0)),) + (tile,) * (cast + conv),
        scratch_shapes=[pltpu.VMEM((8, E_MIX), f32)] * conv,
        compiler_params=_params("arbitrary"))(xin, win, *([conv_w] if conv else []))
    return outs[0], (outs[1] if cast else xin), (outs[-1] if conv else None), carried


def _rel_index_rows():
    j = lax.broadcasted_iota(jnp.int32, (N_REL_PAD, KG), 1)
    r = lax.broadcasted_iota(jnp.int32, (N_REL_PAD, KG), 0)
    off = jnp.where(j < KG - 2 * CHUNK, j, j - KG)
    idx = jnp.clip(N_PREV * CHUNK - off, -REL_CLIP, REL_CLIP) + REL_CLIP
    return (idx == r).astype(f32)


def _bias_expand(table_pad, layer, comm=None):
    def body(t_ref, o_ref, row_scr):
        h = pl.program_id(0)

        @pl.when(h == 0)
        def _():
            row_scr[...] = jnp.dot(t_ref[...], _rel_index_rows(), precision=lax.Precision.HIGHEST,
                                   preferred_element_type=f32)

        q = lax.broadcasted_iota(jnp.int32, (QG, KG), 0)
        k = lax.broadcasted_iota(jnp.int32, (QG, KG), 1)
        band = (k // CHUNK >= q // CHUNK) & (k // CHUNK <= q // CHUNK + N_PREV)
        t = jnp.broadcast_to(row_scr[pl.ds(h, 1), :], (QG, KG))
        for b in range(8):
            t = jnp.where(((q >> b) & 1) == 1, pltpu.roll(t, 1 << b, axis=1), t)
        for v in range(3):
            o_ref[v] = jnp.where(band & (k >= (2 - v) * QG), t, NEG)

    (bias,), carried = _pcall_carry(
        body, comm, n_in=1, n_out=1, name=f"bias_expand_{layer}", grid=(N_HEADS,),
        out_shape=(jax.ShapeDtypeStruct((3, N_HEADS, QG, KG), f32),),
        in_specs=[pl.BlockSpec((N_HEADS, N_REL_PAD), lambda h: (0, 0))],
        out_specs=(pl.BlockSpec((3, None, QG, KG), lambda h: (0, h, 0, 0)),),
        scratch_shapes=[pltpu.VMEM((N_HEADS, KG), f32)], compiler_params=_params("arbitrary"))(table_pad)
    return bias, carried


def _bias_reduce(dbias, layer):
    def body(d_ref, o_ref, row_scr):
        q = lax.broadcasted_iota(jnp.int32, (QG, DB_COLS), 0)
        k = lax.broadcasted_iota(jnp.int32, (QG, DB_COLS), 1)
        for h in range(N_HEADS):
            t = jnp.where(k > q, d_ref[h], 0.0)
            for b in range(8):
                t = jnp.where(((q >> b) & 1) == 1, pltpu.roll(t, DB_COLS - (1 << b), axis=1), t)
            row_scr[h:h + 1, :] = jnp.sum(t, axis=0, keepdims=True)
        r = lax.broadcasted_iota(jnp.int32, (N_REL_PAD, DB_COLS), 0)
        off = lax.broadcasted_iota(jnp.int32, (N_REL_PAD, DB_COLS), 1)
        own = (off >= 1) & (off < REL_CLIP + CHUNK)
        sel = jnp.where(own & (r == 2 * REL_CLIP - off), 1.0, 0.0) - jnp.where(own & (r == 2 * REL_CLIP), 1.0, 0.0)
        o_ref[...] = lax.dot_general(row_scr[...], sel, NT, precision=lax.Precision.HIGHEST, preferred_element_type=f32)

    return _pcall(body, name=f"bias_reduce_{layer}", out_shape=jax.ShapeDtypeStruct((N_HEADS, N_REL_PAD), f32),
                  scratch_shapes=[pltpu.VMEM((N_HEADS, DB_COLS), f32)],
                  compiler_params=pltpu.CompilerParams(vmem_limit_bytes=VMEM_LIMIT))(dbias)


FWD_PAIRS = 8
BWD_PAIRS = 4


def _key_specs(n_groups, npairs, slab):
    per_slab = E_MIX // (128 * npairs)
    return [pl.BlockSpec((QG, 128 * npairs), functools.partial(
        lambda hp, g, jj: (jnp.clip(g - 2 + jj, 0, n_groups - 1), slab * per_slab + hp), jj=jj)) for jj in range(3)]


def _bias_spec(npairs):
    return pl.BlockSpec((None, 2 * npairs, QG, KG), lambda hp, g: (jnp.minimum(g, 2), hp, 0, 0))


def _attn_fwd(h, bias, layer, comm=None):
    T = h.shape[0]
    n_groups = T // QG
    scale = 1.0 / math.sqrt(HEAD_DIM)

    def body(q_ref, k0, k1, k2, v0, v1, v2, b_ref, o_ref, lse_ref):
        lane = lax.broadcasted_iota(jnp.int32, (1, 128), 1)
        ones = jnp.ones((KG, 128), bf16)
        lse = jnp.zeros((QG, 128), f32)
        for pp in range(FWD_PAIRS):
            cs = slice(pp * 128, (pp + 1) * 128)
            q2 = q_ref[:, cs] * scale
            kc = jnp.concatenate([k0[:, cs], k1[:, cs], k2[:, cs]], axis=0)
            vc = jnp.concatenate([jnp.concatenate([v0[:, cs], v1[:, cs], v2[:, cs]], axis=0), ones], axis=1)
            outs = []
            for hh in range(2):
                qm = jnp.where(lane // HEAD_DIM == hh, q2, jnp.zeros_like(q2))
                s = lax.dot_general(qm, kc, NT, preferred_element_type=f32) + b_ref[2 * pp + hh]
                m = jnp.max(s, axis=1, keepdims=True)
                ol = jnp.dot(jnp.exp(s - m).astype(bf16), vc, preferred_element_type=f32)
                outs.append(ol[:, :128] / ol[:, 128:])
                lse = jnp.where(lane == 2 * pp + hh, m + jnp.log(ol[:, 128:]), lse)
            o_ref[:, cs] = jnp.where(lane // HEAD_DIM == 0, outs[0], outs[1]).astype(bf16)
        lse_ref[...] = lse

    (mix, lse), carried = _pcall_carry(
        body, comm, n_in=8, n_out=2, name=f"attn_fwd_{layer}", grid=(N_HEADS // (2 * FWD_PAIRS), n_groups),
        out_shape=(jax.ShapeDtypeStruct((T, E_MIX), bf16), jax.ShapeDtypeStruct((T, 128), f32)),
        in_specs=[pl.BlockSpec((QG, 128 * FWD_PAIRS), lambda hp, g: (g, hp))] + _key_specs(n_groups, FWD_PAIRS, 1)
        + _key_specs(n_groups, FWD_PAIRS, 2) + [_bias_spec(FWD_PAIRS)],
        out_specs=(pl.BlockSpec((QG, 128 * FWD_PAIRS), lambda hp, g: (g, hp)), pl.BlockSpec((QG, 128), lambda hp, g: (g, 0))),
        compiler_params=_params("arbitrary", "arbitrary"))(h, h, h, h, h, h, h, bias)
    return mix, lse, carried


def _halo_rows(ref, r):
    return ref[r:r + 1, :].astype(f32)


def _conv_taps(cu, p6, p7, w_ref):
    row = lax.broadcasted_iota(jnp.int32, cu.shape, 0)
    r1 = jnp.where(row == 0, p7, pltpu.roll(cu, 1, axis=0))
    r2 = jnp.where(row == 0, p6, jnp.where(row == 1, p7, pltpu.roll(cu, 2, axis=0)))
    return w_ref[2:3, :] * cu + w_ref[1:2, :] * r1 + w_ref[0:1, :] * r2, r1, r2


def _mem_probs(qm_ref, kv_ref, hh):
    qh = qm_ref[:, hh * MEM_HEAD_DIM:(hh + 1) * MEM_HEAD_DIM]
    kh = kv_ref[:, hh * MEM_HEAD_DIM:(hh + 1) * MEM_HEAD_DIM]
    vh = kv_ref[:, E_MEM + hh * MEM_HEAD_DIM:E_MEM + (hh + 1) * MEM_HEAD_DIM]
    s = lax.dot_general(qh, kh, NT, preferred_element_type=f32) * (1.0 / math.sqrt(MEM_HEAD_DIM))
    e = jnp.exp(s - jnp.max(s, axis=1, keepdims=True))
    return e / jnp.sum(e, axis=1, keepdims=True), qh, kh, vh


def _h_tail_specs(tm):
    return [pl.BlockSpec((tm, E_MEM), functools.partial(lambda i, cb: (i, cb), cb=cb)) for cb in (6, 7, 8, 9)]


def _ln_bwd(dy, xhat, rstd, g, dob_ref, dxp_ref, dg_ref, db_ref):
    dg_ref[0:1, :] += jnp.sum(dy * xhat, axis=0, keepdims=True)
    db_ref[0:1, :] += jnp.sum(dy, axis=0, keepdims=True)
    gx = dy * g
    dr = rstd * (gx - jnp.mean(gx, axis=1, keepdims=True) - xhat * jnp.mean(gx * xhat, axis=1, keepdims=True))
    dxp_ref[...] = DN_ALPHA * dr
    dob_ref[...] = dr.astype(bf16)


def _post_fwd(h, mix, memb, wkv, wout, x, g, b, layer, target=None, comm=None):
    T = x.shape[0]
    tm = 512

    def body(qm_ref, z0, z1, z2, mix_ref, mem_ref, wkv_ref, w_ref, x_ref, g_ref, b_ref, *rest):
        *rest, ps_ref, ms_ref, kv_out, kv_ref = rest

        @pl.when(pl.program_id(0) == 0)
        def _():
            kv = jnp.dot(mem_ref[...], wkv_ref[...], preferred_element_type=f32).astype(bf16)
            kv_ref[...] = kv
            kv_out[...] = kv
            if target is not None:
                for acc_ref in rest[1], rest[4], rest[5]:
                    acc_ref[...] = jnp.zeros_like(acc_ref)

        mems = []
        for hh in range(MEM_HEADS):
            p, _, _, vh = _mem_probs(qm_ref, kv_ref, hh)
            pb = p.astype(bf16)
            ps_ref[:, hh * N_MEM:(hh + 1) * N_MEM] = pb
            mems.append(jnp.dot(pb, vh, preferred_element_type=f32))
        memb = jnp.concatenate(mems, axis=1).astype(bf16)
        ms_ref[...] = memb
        z = jnp.concatenate([z0[...], z1[...], z2[...]], axis=1)
        one = jnp.ones((), bf16)
        y = jnp.concatenate([mix_ref[...], memb], axis=1) * (z * (one / (one + jnp.exp(-z))))
        out = jnp.dot(y, w_ref[...], preferred_element_type=f32)
        r = DN_ALPHA * x_ref[...] + out
        mu = jnp.mean(r, axis=1, keepdims=True)
        var = jnp.mean(jnp.square(r - mu), axis=1, keepdims=True)
        rstd = lax.rsqrt(var + LN_EPS)
        xhat = (r - mu) * rstd
        xn = xhat * g_ref[...] + b_ref[...]
        if target is None:
            xn_ref, xb_ref, xh_ref, rs_ref = rest
            xn_ref[...] = xn
            xb_ref[...] = xn.astype(bf16)
            xh_ref[...] = xhat
            rs_ref[...] = rstd
        else:
            t_ref, l_ref, dob_ref, dxp_ref, dg_ref, db_ref = rest
            err = xn - t_ref[...]
            l_ref[...] += jnp.sum(jnp.square(err))
            _ln_bwd(err * (1.0 / D_MODEL), xhat, rstd, g_ref[...], dob_ref, dxp_ref, dg_ref, db_ref)

    tile = lambda w: pl.BlockSpec((tm, w), lambda i: (i, 0))
    const = lambda r, c: pl.BlockSpec((r, c), lambda i: (0, 0))
    resident = lambda r, c: pl.BlockSpec((r, c), lambda i: (0, 0), pipeline_mode=pl.Buffered(1))
    in_specs = _h_tail_specs(tm) + [tile(E_MIX), resident(N_MEM, D_MODEL), resident(D_MODEL, 2 * E_MEM),
                                    resident(E_BRANCH, D_MODEL), tile(D_MODEL), const(1, D_MODEL), const(1, D_MODEL)]
    kept = (jax.ShapeDtypeStruct((T, MEM_HEADS * N_MEM), bf16), jax.ShapeDtypeStruct((T, E_MEM), bf16),
            jax.ShapeDtypeStruct((N_MEM, 2 * E_MEM), bf16))
    kept_specs = (tile(MEM_HEADS * N_MEM), tile(E_MEM), const(N_MEM, 2 * E_MEM))
    kv_scratch = [pltpu.VMEM((N_MEM, 2 * E_MEM), bf16)]
    if target is None:
        return _pcall_carry(
            body, comm, n_in=11, n_out=7, name=f"post_fwd_{layer}", grid=(T // tm,),
            out_shape=(jax.ShapeDtypeStruct((T, D_MODEL), f32), jax.ShapeDtypeStruct((T, D_MODEL), bf16),
                       jax.ShapeDtypeStruct((T, D_MODEL), f32), jax.ShapeDtypeStruct((T, 1), f32)) + kept,
            in_specs=in_specs, out_specs=(tile(D_MODEL), tile(D_MODEL), tile(D_MODEL), tile(1)) + kept_specs,
            scratch_shapes=kv_scratch,
            compiler_params=_params("arbitrary"))(h, h, h, h, mix, memb, wkv, wout, x, g, b)
    return _pcall(
        body, name=f"post_fwd_loss_{layer}", grid=(T // tm,),
        out_shape=(jax.ShapeDtypeStruct((8, 128), f32), jax.ShapeDtypeStruct((T, D_MODEL), bf16),
                   jax.ShapeDtypeStruct((T, D_MODEL), f32), jax.ShapeDtypeStruct((8, D_MODEL), f32),
                   jax.ShapeDtypeStruct((8, D_MODEL), f32)) + kept,
        in_specs=in_specs + [tile(D_MODEL)],
        out_specs=(const(8, 128), tile(D_MODEL), tile(D_MODEL), const(8, D_MODEL), const(8, D_MODEL)) + kept_specs,
        scratch_shapes=kv_scratch,
        compiler_params=_params("arbitrary"))(h, h, h, h, mix, memb, wkv, wout, x, g, b, target)


def _post_bwd(dob, h, mix, memb, wout, kept, layer, comm=None):
    T = dob.shape[0]
    tm = 512
    n_tiles = T // tm
    inv = 1.0 / math.sqrt(MEM_HEAD_DIM)

    def body(dob_ref, qm_ref, z0, z1, z2, mix_ref, kv_ref, w_ref, ps_ref, ms_ref, mem_ref,
             dhb_ref, dmix_ref, gkv_out, dwo_out, dwo_ref, dkv_ref):
        @pl.when(pl.program_id(0) == 0)
        def _():
            dkv_ref[...] = jnp.zeros_like(dkv_ref)
            dwo_ref[...] = jnp.zeros_like(dwo_ref)

        dob = dob_ref[...]
        z = jnp.concatenate([z0[...], z1[...], z2[...]], axis=1).astype(f32)
        act, sig = _silu_parts(z)
        cat = jnp.concatenate([mix_ref[...].astype(f32), ms_ref[...].astype(f32)], axis=1)
        yb = (cat * act).astype(bf16)
        dwo_ref[...] += lax.dot_general(yb, dob, TN, preferred_element_type=f32)
        dyv = lax.dot_general(dob, w_ref[...], NT, preferred_element_type=f32)
        dz = dyv * cat * (sig * (1.0 + z * (1.0 - sig)))
        dcat = dyv * act
        dmix_ref[...] = dcat[:, :E_MIX].astype(bf16)
        dqs = []
        for hh in range(MEM_HEADS):
            cols = slice(hh * MEM_HEAD_DIM, (hh + 1) * MEM_HEAD_DIM)
            qh, kh, vh = qm_ref[:, cols], kv_ref[:, cols], kv_ref[:, E_MEM + hh * MEM_HEAD_DIM:E_MEM + (hh + 1) * MEM_HEAD_DIM]
            pb = ps_ref[:, hh * N_MEM:(hh + 1) * N_MEM]
            p = pb.astype(f32)
            dmem = dcat[:, E_MIX + hh * MEM_HEAD_DIM:E_MIX + (hh + 1) * MEM_HEAD_DIM].astype(bf16)
            dp = lax.dot_general(dmem, vh, NT, preferred_element_type=f32)
            ds = (p * (dp - jnp.sum(p * dp, axis=1, keepdims=True))).astype(bf16)
            dqs.append(jnp.dot(ds, kh, preferred_element_type=f32) * inv)
            dkv_ref[:, cols] += lax.dot_general(ds, qh, TN, preferred_element_type=f32) * inv
            dkv_ref[:, E_MEM + hh * MEM_HEAD_DIM:E_MEM + (hh + 1) * MEM_HEAD_DIM] += lax.dot_general(
                pb, dmem, TN, preferred_element_type=f32)
        dhb_ref[...] = jnp.concatenate(dqs + [dz], axis=1).astype(bf16)

        @pl.when(pl.program_id(0) == n_tiles - 1)
        def _():
            dwo_out[...] = dwo_ref[...].astype(bf16)
            gkv_out[...] = lax.dot_general(mem_ref[...], dkv_ref[...].astype(bf16), TN,
                                           preferred_element_type=f32).astype(bf16)

    tile = lambda w: pl.BlockSpec((tm, w), lambda i: (i, 0))
    const = lambda r, c: pl.BlockSpec((r, c), lambda i: (0, 0))
    resident = lambda r, c: pl.BlockSpec((r, c), lambda i: (0, 0), pipeline_mode=pl.Buffered(1))
    ps, ms, kv = kept
    return _pcall_carry(
        body, comm, n_in=11, n_out=4, name=f"post_bwd_{layer}", grid=(n_tiles,),
        out_shape=(jax.ShapeDtypeStruct((T, E_MEM + E_BRANCH), bf16), jax.ShapeDtypeStruct((T, E_MIX), bf16),
                   jax.ShapeDtypeStruct((D_MODEL, 2 * E_MEM), bf16), jax.ShapeDtypeStruct((E_BRANCH, D_MODEL), bf16)),
        in_specs=[tile(D_MODEL)] + _h_tail_specs(tm) + [tile(E_MIX), resident(N_MEM, 2 * E_MEM), resident(E_BRANCH, D_MODEL),
                                                        tile(MEM_HEADS * N_MEM), tile(E_MEM), resident(N_MEM, D_MODEL)],
        out_specs=(tile(E_MEM + E_BRANCH), tile(E_MIX), const(D_MODEL, 2 * E_MEM), const(E_BRANCH, D_MODEL)),
        scratch_shapes=[pltpu.VMEM((E_BRANCH, D_MODEL), f32), pltpu.VMEM((N_MEM, 2 * E_MEM), f32)],
        compiler_params=_params("arbitrary"))(dob, h, h, h, h, mix, kv, wout, ps, ms, memb)


def _attn_bwd(h, bias, dmix, lse, layer, comm=None):
    T = h.shape[0]
    n_groups = T // QG
    scale = 1.0 / math.sqrt(HEAD_DIM)

    def body(q_ref, k0, k1, k2, v0, v1, v2, do_ref, b_ref, lse_ref, dq_ref, dk_ref, dv_ref, db_ref, acck, accv):
        g = pl.program_id(1)

        @pl.when(g == 0)
        def _():
            acck[...] = jnp.zeros_like(acck)
            accv[...] = jnp.zeros_like(accv)
            db_ref[...] = jnp.zeros_like(db_ref)

        @pl.when(g < n_groups)
        def _():
            lane = lax.broadcasted_iota(jnp.int32, (1, 128), 1)
            first = lane // HEAD_DIM == 0
            for pp in range(BWD_PAIRS):
                cs = slice(pp * 128, (pp + 1) * 128)
                do2 = do_ref[:, cs]
                q2 = q_ref[:, cs] * scale
                kc = jnp.concatenate([k0[:, cs], k1[:, cs], k2[:, cs]], axis=0)
                vc = jnp.concatenate([v0[:, cs], v1[:, cs], v2[:, cs]], axis=0)
                q2t, do2t = q2.T, do2.T
                dqs, dks, dvs = [], [], []
                for hh in range(2):
                    hm = lane // HEAD_DIM == hh
                    head = (pl.program_id(0) * BWD_PAIRS + pp) * 2 + hh
                    lse = jnp.sum(jnp.where(lane == head, lse_ref[...], 0.0), axis=1, keepdims=True)
                    qm = jnp.where(hm, q2, jnp.zeros_like(q2))
                    dom = jnp.where(hm, do2, jnp.zeros_like(do2))
                    s = lax.dot_general(qm, kc, NT, preferred_element_type=f32) + b_ref[2 * pp + hh]
                    p = jnp.exp(s - lse)
                    dp = lax.dot_general(dom, vc, NT, preferred_element_type=f32)
                    ds = p * (dp - jnp.sum(p * dp, axis=1, keepdims=True))
                    db_ref[2 * pp + hh] += ds[:, KG - DB_COLS:]
                    dsb, pb = ds.astype(bf16), p.astype(bf16)
                    dqs.append(jnp.dot(dsb, kc, preferred_element_type=f32) * scale)
                    dks.append(jnp.dot(q2t[hh * HEAD_DIM:(hh + 1) * HEAD_DIM], dsb, preferred_element_type=f32))
                    dvs.append(jnp.dot(do2t[hh * HEAD_DIM:(hh + 1) * HEAD_DIM], pb, preferred_element_type=f32))
                dq_ref[:, cs] = jnp.where(first, dqs[0], dqs[1]).astype(bf16)
                dkc = jnp.concatenate(dks, axis=0).T
                dvc = jnp.concatenate(dvs, axis=0).T
                for jj in range(3):
                    slot = (g + 1 + jj) % 3
                    if jj == 2:
                        acck[slot, :, cs] = dkc[jj * QG:(jj + 1) * QG]
                        accv[slot, :, cs] = dvc[jj * QG:(jj + 1) * QG]
                    else:
                        acck[slot, :, cs] += dkc[jj * QG:(jj + 1) * QG]
                        accv[slot, :, cs] += dvc[jj * QG:(jj + 1) * QG]

        done = (g + 1) % 3
        dk_ref[...] = acck[done].astype(bf16)
        dv_ref[...] = accv[done].astype(bf16)

    last = n_groups - 1
    width = 128 * BWD_PAIRS
    qspec = pl.BlockSpec((QG, width), lambda hp, g: (jnp.minimum(g, last), hp))
    kout = pl.BlockSpec((QG, width), lambda hp, g: (jnp.clip(g - 2, 0, last), hp))
    dbspec = pl.BlockSpec((2 * BWD_PAIRS, QG, DB_COLS), lambda hp, g: (hp, 0, 0))
    lspec = pl.BlockSpec((QG, 128), lambda hp, g: (jnp.minimum(g, last), 0))
    return _pcall_carry(
        body, comm, n_in=10, n_out=4, name=f"attn_bwd_{layer}", grid=(N_HEADS // (2 * BWD_PAIRS), n_groups + 2),
        out_shape=(jax.ShapeDtypeStruct((T, E_MIX), bf16),) * 3 + (jax.ShapeDtypeStruct((N_HEADS, QG, DB_COLS), f32),),
        in_specs=[qspec] + _key_specs(n_groups, BWD_PAIRS, 1) + _key_specs(n_groups, BWD_PAIRS, 2)
        + [qspec, _bias_spec(BWD_PAIRS), lspec],
        out_specs=(qspec, kout, kout, dbspec),
        scratch_shapes=[pltpu.VMEM((3, QG, width), f32), pltpu.VMEM((3, QG, width), f32)],
        compiler_params=_params("arbitrary", "arbitrary"))(h, h, h, h, h, h, h, dmix, bias, lse)


def _conv_bwd(h, w, dmix, layer):
    T = h.shape[0]
    tm = 512
    n_tiles = T // tm

    def body(bg_ref, cg_ref, u_ref, cgp_ref, up_ref, dy_ref, bgn_ref, dyn_ref, w_ref, dbg_ref, dcg_ref, du_ref, dw_ref):
        i = pl.program_id(0)

        @pl.when(i == 0)
        def _():
            dw_ref[...] = jnp.zeros_like(dw_ref)

        first = (i == 0).astype(f32)
        final = (i == n_tiles - 1).astype(f32)
        bg, cg, u = bg_ref[...].astype(f32), cg_ref[...].astype(f32), u_ref[...].astype(f32)
        dy = dy_ref[...].astype(f32)
        cu = cg * u
        p6 = _halo_rows(cgp_ref, 14) * _halo_rows(up_ref, 14) * (1.0 - first)
        p7 = _halo_rows(cgp_ref, 15) * _halo_rows(up_ref, 15) * (1.0 - first)
        conv, r1, r2 = _conv_taps(cu, p6, p7, w_ref)
        dbg_ref[...] = (dy * conv).astype(bf16)
        dc = dy * bg
        n0 = _halo_rows(dyn_ref, 0) * _halo_rows(bgn_ref, 0) * (1.0 - final)
        n1 = _halo_rows(dyn_ref, 1) * _halo_rows(bgn_ref, 1) * (1.0 - final)
        row = lax.broadcasted_iota(jnp.int32, dc.shape, 0)
        f1 = jnp.where(row == tm - 1, n0, pltpu.roll(dc, tm - 1, axis=0))
        f2 = jnp.where(row == tm - 2, n0, jnp.where(row == tm - 1, n1, pltpu.roll(dc, tm - 2, axis=0)))
        dcu = w_ref[2:3, :] * dc + w_ref[1:2, :] * f1 + w_ref[0:1, :] * f2
        dcg_ref[...] = (dcu * u).astype(bf16)
        du_ref[...] = (dcu * cg).astype(bf16)
        dw_ref[0:1, :] += jnp.sum(dc * r2, axis=0, keepdims=True)
        dw_ref[1:2, :] += jnp.sum(dc * r1, axis=0, keepdims=True)
        dw_ref[2:3, :] += jnp.sum(dc * cu, axis=0, keepdims=True)

    tile = lambda slab: pl.BlockSpec((tm, E_MIX), lambda i: (i, slab))
    prev = lambda slab: pl.BlockSpec((16, E_MIX), lambda i: (jnp.maximum(i * (tm // 16) - 1, 0), slab))
    nxt = lambda slab: pl.BlockSpec((16, E_MIX), lambda i: (jnp.minimum((i + 1) * (tm // 16), T // 16 - 1), slab))
    return _pcall(
        body, name=f"conv_bwd_{layer}", grid=(n_tiles,),
        out_shape=(jax.ShapeDtypeStruct((T, E_MIX), bf16),) * 3 + (jax.ShapeDtypeStruct((8, E_MIX), f32),),
        in_specs=[tile(0), tile(1), tile(2), prev(1), prev(2), tile(0), nxt(0), nxt(0),
                  pl.BlockSpec((CONV_W, E_MIX), lambda i: (0, 0))],
        out_specs=(tile(0), tile(0), tile(0), pl.BlockSpec((8, E_MIX), lambda i: (0, 0))),
        compiler_params=_params("arbitrary"))(h, h, h, h, h, dmix, h, dmix, w)


def _inproj_bwd_dx(da, db, dc, dhb, dxp, win, layer, below=None, comm=None):
    T = dxp.shape[0]
    tm = 512

    def body(da_ref, db_ref, dc_ref, dhb_ref, dxp_ref, w_ref, *rest):
        if below is not None:
            xh_ref, rs_ref, g_ref, dob_ref, dxo_ref, dg_ref, db_out = rest

            @pl.when(pl.program_id(0) == 0)
            def _():
                dg_ref[...] = jnp.zeros_like(dg_ref)
                db_out[...] = jnp.zeros_like(db_out)

        dh = jnp.concatenate([da_ref[...], db_ref[...], dc_ref[...], dhb_ref[...]], axis=1)
        acc = dxp_ref[...]
        for j in range(N_CHIPS):
            acc = acc + lax.dot_general(dh[:, j * W_IN_COLS:(j + 1) * W_IN_COLS], w_ref[j], NT, preferred_element_type=f32)
        if below is None:
            rest[0][...] = acc
        else:
            _ln_bwd(acc, xh_ref[...], rs_ref[...], g_ref[...], dob_ref, dxo_ref, dg_ref, db_out)

    tile = lambda w: pl.BlockSpec((tm, w), lambda i: (i, 0))
    const = lambda r, c: pl.BlockSpec((r, c), lambda i: (0, 0))
    in_specs = [tile(E_MIX), tile(E_MIX), tile(E_MIX), tile(E_MEM + E_BRANCH), tile(D_MODEL),
                pl.BlockSpec((N_CHIPS, D_MODEL, W_IN_COLS), lambda i: (0, 0, 0), pipeline_mode=pl.Buffered(1))]
    if below is None:
        return _pcall_carry(
            body, comm, n_in=6, n_out=1, name=f"inproj_bwd_dx_{layer}", grid=(T // tm,),
            out_shape=(jax.ShapeDtypeStruct((T, D_MODEL), f32),), in_specs=in_specs, out_specs=(tile(D_MODEL),),
            compiler_params=_params("arbitrary"))(da, db, dc, dhb, dxp, win)
    return _pcall_carry(
        body, comm, n_in=9, n_out=4, name=f"inproj_bwd_dx_{layer}", grid=(T // tm,),
        out_shape=(jax.ShapeDtypeStruct((T, D_MODEL), bf16), jax.ShapeDtypeStruct((T, D_MODEL), f32),
                   jax.ShapeDtypeStruct((8, D_MODEL), f32), jax.ShapeDtypeStruct((8, D_MODEL), f32)),
        in_specs=in_specs + [tile(D_MODEL), tile(1), const(1, D_MODEL)],
        out_specs=(tile(D_MODEL), tile(D_MODEL), const(8, D_MODEL), const(8, D_MODEL)),
        compiler_params=_params("arbitrary"))(da, db, dc, dhb, dxp, win, *below)


def _dh_pieces():
    pieces, col = [], 0
    for src, width in enumerate((E_MIX, E_MIX, E_MIX, E_MEM + E_BRANCH)):
        lo = 0
        while lo < width:
            j, c0 = divmod(col + lo, W_IN_COLS)
            n = min(width - lo, W_IN_COLS - c0)
            pieces.append((src, lo, lo + n, j, c0, c0 + n))
            lo += n
        col += width
    return pieces


def _inproj_bwd_dw(da, db, dc, dhb, xb, layer, comm=None):
    T = xb.shape[0]
    tm = 1024 if T % 1024 == 0 else 512
    n_tiles = T // tm

    def body(da_ref, db_ref, dc_ref, dhb_ref, x_ref, o_ref, acc, stage, sem):
        i = pl.program_id(0)

        @pl.when(i == 0)
        def _():
            acc[...] = jnp.zeros_like(acc)

        srcs = (da_ref, db_ref, dc_ref, dhb_ref)
        xt = x_ref[...]
        for s, s0, s1, j, c0, c1 in _dh_pieces():
            acc[j, :, c0:c1] += lax.dot_general(xt, srcs[s][:, s0:s1], TN, preferred_element_type=f32)

        @pl.when(i == n_tiles - 1)
        def _():
            for j in range(N_CHIPS):
                stage[...] = acc[j].astype(bf16)
                cp = pltpu.make_async_copy(stage, o_ref.at[j], sem)
                cp.start()
                cp.wait()

    tile = lambda w: pl.BlockSpec((tm, w), lambda i: (i, 0))
    (dw,), carried = _pcall_carry(
        body, comm, n_in=5, n_out=1, name=f"inproj_bwd_dw_{layer}", grid=(n_tiles,),
        out_shape=(jax.ShapeDtypeStruct((N_CHIPS, D_MODEL, W_IN_COLS), bf16),),
        in_specs=[tile(E_MIX), tile(E_MIX), tile(E_MIX), tile(E_MEM + E_BRANCH), tile(D_MODEL)],
        out_specs=(ANY,),
        scratch_shapes=[pltpu.VMEM((N_CHIPS, D_MODEL, W_IN_COLS), f32), pltpu.VMEM((D_MODEL, W_IN_COLS), bf16),
                        pltpu.SemaphoreType.DMA],
        compiler_params=_params("arbitrary"))(da, db, dc, dhb, xb)
    return dw, carried


def _adamw(w, g, m, v, name):
    shape = w.shape
    cols = shape[-1]
    rows = w.size // cols
    args = [a.reshape(rows, cols) for a in (w, g, m, v)]
    br = 256 if rows % 256 == 0 and rows > 256 else rows

    def body(w_ref, g_ref, m_ref, v_ref, go_ref, d_ref, nm_ref, nv_ref):
        gg = g_ref[...]
        nm = ADAM_B1 * m_ref[...] + (1.0 - ADAM_B1) * gg
        nv = ADAM_B2 * v_ref[...] + (1.0 - ADAM_B2) * jnp.square(gg)
        m_hat = nm / (1.0 - ADAM_B1 ** ADAM_STEP)
        v_hat = nv / (1.0 - ADAM_B2 ** ADAM_STEP)
        go_ref[...] = gg
        d_ref[...] = -ADAM_LR * (m_hat / (jnp.sqrt(v_hat) + ADAM_EPS) + ADAM_WD * w_ref[...])
        nm_ref[...] = nm
        nv_ref[...] = nv

    spec = pl.BlockSpec((br, cols), lambda i: (i, 0))
    outs = _pcall(body, name=name, grid=(rows // br,), out_shape=(jax.ShapeDtypeStruct((rows, cols), f32),) * 4,
                  in_specs=[spec] * 4, out_specs=(spec,) * 4, compiler_params=_params("arbitrary"))(*args)
    return tuple(o.reshape(shape) for o in outs)


def kernel(x, mem, w_in, w_mem_kv, w_out, rel_bias, conv_w, ln_g, ln_b, loss_target, m_w_in, m_w_mem_kv, m_w_out, m_rel_bias, m_conv_w, m_ln_g, m_ln_b, v_w_in, v_w_mem_kv, v_w_out, v_rel_bias, v_conv_w, v_ln_g, v_ln_b):
    T = x.shape[1]
    x0 = x.reshape(T, D_MODEL)
    target = loss_target.reshape(T, D_MODEL)
    memb = mem.reshape(N_MEM, D_MODEL).astype(bf16)
    chip = 2 * lax.axis_index("x") + lax.axis_index("y")
    core = lax.axis_index("c")
    chip_arr = jnp.reshape(chip, (1,)).astype(jnp.int32)
    core_arr = jnp.reshape(core, (1,)).astype(jnp.int32)

    place = jnp.concatenate([chip_arr, core_arr])
    tables = jnp.pad(rel_bias, ((0, 0), (0, 0), (0, N_REL_PAD - N_REL)))

    shards = [w_in.astype(bf16), w_mem_kv.astype(bf16), w_out.astype(bf16)]
    biases = {}
    biases[0], near = _bias_expand(tables[0], 0, _gather_ici(shards[:1], 0, peers=(0, 1)))
    biases[2], far = _bias_expand(tables[1], 2, _gather_ici(shards[:1], 0, peers=(2,), own=False, into=near))
    gathered = {0: list(_comm_call(_gather_d2d(list(far)), "gather_d2d_0")) + [None, None]}
    conv_full = None

    xs, xbs, hs, mixes, kepts, xhats, rstds, lses = [x0], [x0], [], [], [], [], [], {}
    for layer in range(DEPTH):
        more = layer + 1 < DEPTH
        attention = layer % 2 == 0
        h, xbs[layer], mix, arrived = _inproj(xbs[layer], gathered[layer][0], layer,
                                              _gather_ici(shards, layer + 1) if more else None,
                                              None if attention else conv_full[layer // 2])
        passing = _gather_d2d(list(arrived)) if more else None
        if layer == 0:
            passing = _both(passing, _gather_ici(shards[1:], 0, extra=conv_w))
        if attention:
            mix, lses[layer], done = _attn_fwd(h, biases[layer], layer, passing)
        if layer == 0:
            *done, wkv_in, wout_in, cw_g = done
            gathered[0][1:] = _comm_call(_gather_d2d([wkv_in, wout_in]), "gather_d2d_0_rest")
            conv_full = jnp.transpose(cw_g, (1, 2, 0, 3)).reshape(DEPTH // 2, CONV_W, E_MIX)
        win, wkv, wout = gathered[layer]
        result = _post_fwd(h, mix, memb, wkv.reshape(D_MODEL, 2 * E_MEM), wout.reshape(E_BRANCH, D_MODEL), xs[layer],
                           ln_g[layer][None, :], ln_b[layer][None, :], layer, None if more else target,
                           None if attention else passing)
        if more:
            (xn, xnb, xhat, rstd, *kept), done = (result[0], done) if attention else result
            xs.append(xn); xbs.append(xnb); xhats.append(xhat); rstds.append(rstd)
            gathered[layer + 1] = list(done)
        else:
            lsum, dob, dxp, dg_last, db_last, *kept = result
        hs.append(h); mixes.append(mix); kepts.append(kept)

    dgs, dbs, dconvs, dtables = [None] * DEPTH, [None] * DEPTH, [None] * (DEPTH // 2), [None] * ((DEPTH + 1) // 2)
    dgs[DEPTH - 1], dbs[DEPTH - 1] = dg_last, db_last
    loss = lax.psum(lsum[0, 0], ("x", "y", "c")) * (0.5 / D_MODEL)
    finals = [None, None, None]
    above = None
    for layer in reversed(range(DEPTH)):
        h = hs[layer]
        win, wkv, wout = gathered[layer]
        (dhb, dmix, g_wkv, dwo), from_sibling = _post_bwd(
            dob, h, mixes[layer], memb, wout.reshape(E_BRANCH, D_MODEL), kepts[layer], layer,
            _sibling_exchange(above) if above else None)
        sums = [_pair_sum(g, r, core_arr, layer + 1) for g, r in zip(above, from_sibling)] if above else None
        scatter = _chip_scatter(sums) if above else None
        if layer % 2 == 0:
            (da, db, dc, dbias), from_chips = _attn_bwd(h, biases[layer], dmix, lses[layer], layer, scatter)
            dtables[layer // 2] = _bias_reduce(dbias, layer)
        else:
            da, db, dc, dconvs[layer // 2] = _conv_bwd(h, conv_full[layer // 2], dmix, layer)
        if layer > 0:
            (dob_below, dxp_below, dgs[layer - 1], dbs[layer - 1]), landed = _inproj_bwd_dx(
                da, db, dc, dhb, dxp, win, layer, (xhats[layer - 1], rstds[layer - 1], ln_g[layer - 1][None, :]),
                scatter if layer % 2 == 1 else None)
            from_chips = landed if layer % 2 == 1 else from_chips
        share = None
        if above:
            finals = [_chip_sum(s, r, place, layer + 1, f) for s, r, f in zip(sums, from_chips, finals)]
            share = _sibling_share(finals, layer + 1)
        if layer == 0:
            pad8 = lambda a: jnp.pad(a, ((0, 8 - a.shape[0]), (0, 0)))
            small_mine = jnp.concatenate(dgs + dbs + dconvs + [pad8(t.reshape(-1, D_MODEL)) for t in dtables], axis=0)
            share = _both(share, _small_exchange(small_mine))
        g_win, shared = _inproj_bwd_dw(da, db, dc, dhb, xbs[layer], layer, share)
        if layer == 0:
            *shared, small_slots = shared
        finals = list(shared) if above else finals
        above = [g_win, g_wkv.reshape(N_CHIPS, W_KV_ROWS, 2 * E_MEM),
                 dwo.reshape(N_CHIPS, W_OUT_ROWS, D_MODEL)]
        if layer > 0:
            dob, dxp = dob_below, dxp_below
    from_sibling = _comm_call(_sibling_exchange(above), "sibling_exchange_0")
    sums = [_pair_sum(g, r, core_arr, 0) for g, r in zip(above, from_sibling)]
    (dx,), from_chips = _inproj_bwd_dx(da, db, dc, dhb, dxp, win, 0, None, _chip_scatter(sums))
    finals = [_chip_sum(s, r, place, 0, f) for s, r, f in zip(sums, from_chips, finals)]
    grad_w_in, grad_w_mem_kv, grad_w_out = _comm_call(_sibling_share(finals, 0), "sibling_share_0")
    grad_x = dx.reshape(1, T, D_MODEL)

    device_arr = jnp.reshape(2 * chip + core, (1,)).astype(jnp.int32)
    small = _small_sum(small_slots, small_mine, device_arr)
    grad_ln_g = jnp.stack([small[8 * l] for l in range(DEPTH)])
    grad_ln_b = jnp.stack([small[8 * (DEPTH + l)] for l in range(DEPTH)])
    conv_all = jnp.stack([small[8 * (2 * DEPTH + a):8 * (2 * DEPTH + a) + CONV_W] for a in range(DEPTH // 2)])
    grad_conv_w = lax.dynamic_slice_in_dim(conv_all, chip * (E_MIX // N_CHIPS), E_MIX // N_CHIPS, axis=2)
    t0 = 8 * (2 * DEPTH + DEPTH // 2)
    grad_rel_bias = jnp.stack([small[t0 + 8 * a:t0 + 8 * a + 6].reshape(N_HEADS, N_REL_PAD)[:, :N_REL]
                               for a in range((DEPTH + 1) // 2)])

    grads = [grad_w_in, grad_w_mem_kv, grad_w_out, grad_rel_bias, grad_conv_w, grad_ln_g, grad_ln_b]
    weights = [w_in, w_mem_kv, w_out, rel_bias, conv_w, ln_g, ln_b]
    moms = [m_w_in, m_w_mem_kv, m_w_out, m_rel_bias, m_conv_w, m_ln_g, m_ln_b]
    vels = [v_w_in, v_w_mem_kv, v_w_out, v_rel_bias, v_conv_w, v_ln_g, v_ln_b]
    names = ["w_in", "w_mem_kv", "w_out", "rel_bias", "conv_w", "ln_g", "ln_b"]
    upd = [_adamw(w, g, m, v, f"adamw_{n}") for w, g, m, v, n in zip(weights, grads, moms, vels, names)]
    grads, deltas, new_m, new_v = zip(*upd)
    return (loss, grad_x, *grads, *deltas, *new_m, *new_v)
```

```python
import functools
import math

import jax
import jax.numpy as jnp
from jax import lax
from jax.experimental import pallas as pl
from jax.experimental.pallas import tpu as pltpu

f32, bf16 = jnp.float32, jnp.bfloat16

D_MODEL = 1024
DEPTH = 4
CHUNK = 64
N_PREV = 8
N_HEADS = 16
HEAD_DIM = 64
E_MIX = 1024
REL_CLIP = 128
N_REL = 2 * REL_CLIP + 1
N_REL_PAD = 384
CONV_W = 3
N_MEM = 256
MEM_HEADS = 4
MEM_HEAD_DIM = 128
E_MEM = 512
E_BRANCH = E_MIX + E_MEM
N_IN = 3 * E_MIX + E_MEM + E_BRANCH
N_CHIPS = 4
W_IN_COLS = N_IN // N_CHIPS
W_KV_ROWS = D_MODEL // N_CHIPS
W_OUT_ROWS = E_BRANCH // N_CHIPS
DN_ALPHA = (2.0 * DEPTH) ** 0.25
LN_EPS = 1e-5
ADAM_LR, ADAM_B1, ADAM_B2, ADAM_EPS, ADAM_WD, ADAM_STEP = 0.001, 0.9, 0.999, 1e-08, 0.01, 10

QG = 4 * CHUNK
KG = QG + N_PREV * CHUNK
DB_COLS = KG // 2
NEG = -1e30
VMEM_LIMIT = 56 * 1024 * 1024

NT = (((1,), (1,)), ((), ()))
TN = (((0,), (0,)), ((), ()))
MESH = pl.DeviceIdType.MESH
ANY = pl.BlockSpec(memory_space=pl.ANY)


def _pcall(body, **kw):
    return pl.pallas_call(body, **kw)


def _params(*sem):
    return pltpu.CompilerParams(dimension_semantics=sem, vmem_limit_bytes=VMEM_LIMIT)


def _silu_parts(z):
    sig = 1.0 / (1.0 + jnp.exp(-z))
    return z * sig, sig


class _Comm:
    def __init__(self, inputs, out_shapes, aliases, n_sems, copies):
        self.inputs, self.out_shapes, self.aliases, self.n_sems, self.copies = inputs, out_shapes, aliases, n_sems, copies

    def start(self, cin, cout, send, recv):
        for cp in self.copies(cin, cout, send, recv)[0]:
            cp.start()

    def wait(self, cin, cout, send, recv):
        sends, recvs = self.copies(cin, cout, send, recv)
        for cp in recvs:
            cp.wait_recv()
        for cp in sends:
            cp.wait_send()


def _pcall_carry(body, comm, *, n_in, n_out, **kw):
    if comm is None:
        return lambda *args: (_pcall(body, **kw)(*args), ())
    grid = kw["grid"]
    k_in, k_out = len(comm.inputs), len(comm.out_shapes)

    def carried(*refs):
        ins, cin = refs[:n_in], refs[n_in:n_in + k_in]
        outs = refs[n_in + k_in:n_in + k_in + n_out]
        cout = refs[n_in + k_in + n_out:n_in + k_in + n_out + k_out]
        scratch, send, recv = refs[n_in + k_in + n_out + k_out:-2], refs[-2], refs[-1]
        ids = [pl.program_id(a) for a in range(len(grid))]
        first = functools.reduce(jnp.logical_and, [i == 0 for i in ids])
        last = functools.reduce(jnp.logical_and, [i == n - 1 for i, n in zip(ids, grid)])

        @pl.when(first)
        def _():
            comm.start(cin, cout, send, recv)

        body(*ins, *outs, *scratch)

        @pl.when(last)
        def _():
            comm.wait(cin, cout, send, recv)

    kw = dict(kw)
    kw["in_specs"] = list(kw["in_specs"]) + [ANY] * k_in
    kw["out_specs"] = tuple(kw["out_specs"]) + (ANY,) * k_out
    kw["out_shape"] = tuple(kw["out_shape"]) + tuple(comm.out_shapes)
    kw["scratch_shapes"] = list(kw.get("scratch_shapes", ())) + [pltpu.SemaphoreType.DMA((comm.n_sems,))] * 2
    aliases = dict(kw.get("input_output_aliases", {}))
    aliases.update({n_in + ci: n_out + co for ci, co in comm.aliases.items()})
    kw["input_output_aliases"] = aliases

    def run(*args):
        res = _pcall(carried, **kw)(*args, *comm.inputs)
        return res[:n_out], res[n_out:]

    return run


def _comm_call(comm, name):
    k_in = len(comm.inputs)

    def body(*refs):
        cin, cout, send, recv = refs[:k_in], refs[k_in:-2], refs[-2], refs[-1]
        comm.start(cin, cout, send, recv)
        comm.wait(cin, cout, send, recv)

    return _pcall(body, name=name, out_shape=tuple(comm.out_shapes), in_specs=[ANY] * k_in,
                  out_specs=(ANY,) * len(comm.out_shapes), input_output_aliases=dict(comm.aliases),
                  scratch_shapes=[pltpu.SemaphoreType.DMA((comm.n_sems,))] * 2)(*comm.inputs)


def _place():
    x, y, c = lax.axis_index("x"), lax.axis_index("y"), lax.axis_index("c")
    return x, y, c, 2 * x + y, (x, y, 1 - c), [(1 - x, y), (x, 1 - y), (1 - x, 1 - y)]


def _rcopy(send, recv, k, src, dst, to):
    return pltpu.make_async_remote_copy(src_ref=src, dst_ref=dst, send_sem=send.at[k], recv_sem=recv.at[k],
                                        device_id=to, device_id_type=MESH)


def _half(ref_rows, core):
    return pl.ds(core * (ref_rows // 2), ref_rows // 2)


def _gather_ici(shards, layer, extra=None, peers=(0, 1, 2), own=True, into=None):
    extras = [] if extra is None else [extra]
    n = len(shards)

    def copies(cin, cout, send, recv):
        x, y, c, me, sibling, chips = _place()
        sends, recvs = [], []
        for a in range(n + len(extras)):
            s, g = cin[a], cout[a]
            whole = a >= n
            src = s if whole else s.at[layer]
            if own:
                sends.append(_rcopy(send, recv, 4 * a, src, g.at[me], sibling))
                recvs.append(_rcopy(send, recv, 4 * a, src, g.at[me], sibling))
            for p in peers:
                px, py = chips[p]
                if whole:
                    sends.append(_rcopy(send, recv, 4 * a + 1 + p, src, g.at[me], (px, py, c)))
                    recvs.append(_rcopy(send, recv, 4 * a + 1 + p, src, g.at[2 * px + py], (px, py, c)))
                else:
                    mine = _half(s.shape[1], c)
                    sends.append(_rcopy(send, recv, 4 * a + 1 + p, s.at[layer, mine], g.at[me, mine], (px, py, c)))
                    recvs.append(_rcopy(send, recv, 4 * a + 1 + p, s.at[layer, mine], g.at[2 * px + py, mine], (px, py, c)))
        return sends, recvs

    sources = list(shards) + extras
    if into is None:
        out_shapes = [jax.ShapeDtypeStruct((N_CHIPS,) + s.shape[1:], s.dtype) for s in shards]
        out_shapes += [jax.ShapeDtypeStruct((N_CHIPS,) + e.shape, e.dtype) for e in extras]
        return _Comm(sources, out_shapes, {}, 4 * len(sources), copies)
    return _Comm(sources + list(into), [jax.ShapeDtypeStruct(g.shape, g.dtype) for g in into],
                 {len(sources) + a: a for a in range(len(into))}, 4 * len(sources), copies)


def _gather_d2d(gathered):
    n = len(gathered)

    def copies(cin, cout, send, recv):
        x, y, c, me, sibling, chips = _place()
        sends, recvs = [], []
        for a in range(n):
            g = cout[a]
            rows = g.shape[1]
            for p, (px, py) in enumerate(chips):
                mine, theirs = g.at[2 * px + py, _half(rows, c)], g.at[2 * px + py, _half(rows, 1 - c)]
                sends.append(_rcopy(send, recv, 3 * a + p, mine, mine, sibling))
                recvs.append(_rcopy(send, recv, 3 * a + p, theirs, theirs, sibling))
        return sends, recvs

    return _Comm(list(gathered), [jax.ShapeDtypeStruct(g.shape, g.dtype) for g in gathered],
                 {a: a for a in range(n)}, 3 * n, copies)


class _SemView:
    def __init__(self, ref, base):
        self.ref, self.base, self.at = ref, base, self

    def __getitem__(self, k):
        return self.ref.at[self.base + k]


def _both(a, b):
    ka, ma = len(a.inputs), len(a.out_shapes)

    def copies(cin, cout, send, recv):
        sa, ra = a.copies(cin[:ka], cout[:ma], send, recv)
        sb, rb = b.copies(cin[ka:], cout[ma:], _SemView(send, a.n_sems), _SemView(recv, a.n_sems))
        return sa + sb, ra + rb

    aliases = dict(a.aliases)
    aliases.update({ka + ci: ma + co for ci, co in b.aliases.items()})
    return _Comm(a.inputs + b.inputs, a.out_shapes + b.out_shapes, aliases, a.n_sems + b.n_sems, copies)


def _small_exchange(buf):
    def copies(cin, cout, send, recv):
        x, y, c, me, sibling, chips = _place()
        sends, recvs = [], []
        for r in range(1, 8):
            px, py, pc = x ^ ((r >> 2) & 1), y ^ ((r >> 1) & 1), c ^ (r & 1)
            sends.append(_rcopy(send, recv, r - 1, cin[0], cout[0].at[2 * me + c], (px, py, pc)))
            recvs.append(_rcopy(send, recv, r - 1, cin[0], cout[0].at[4 * px + 2 * py + pc], (px, py, pc)))
        return sends, recvs

    return _Comm([buf], [jax.ShapeDtypeStruct((8,) + buf.shape, buf.dtype)], {}, 7, copies)


def _small_sum(slots, buf, device_arr):
    rows, cols = buf.shape

    def body(d_ref, s_ref, b_ref, o_ref):
        d = pl.program_id(0)
        val = jnp.where(d == d_ref[0], b_ref[...], s_ref[...])

        @pl.when(d == 0)
        def _():
            o_ref[...] = val

        @pl.when(d > 0)
        def _():
            o_ref[...] += val

    return _pcall(
        body, name="small_sum", out_shape=jax.ShapeDtypeStruct((rows, cols), f32),
        grid_spec=pltpu.PrefetchScalarGridSpec(
            num_scalar_prefetch=1, grid=(8,),
            in_specs=[pl.BlockSpec((None, rows, cols), lambda d, d_ref: (jnp.where(d == d_ref[0], (d + 1) % 8, d), 0, 0)),
                      pl.BlockSpec((rows, cols), lambda d, d_ref: (0, 0))],
            out_specs=pl.BlockSpec((rows, cols), lambda d, d_ref: (0, 0))),
        compiler_params=_params("arbitrary"))(device_arr, slots, buf)


def _sibling_exchange(gs):
    def copies(cin, cout, send, recv):
        x, y, c, me, sibling, chips = _place()
        sends = [_rcopy(send, recv, a, g.at[:, _half(g.shape[1], 1 - c)], r, sibling) for a, (g, r) in enumerate(zip(cin, cout))]
        return sends, sends

    shapes = [jax.ShapeDtypeStruct((N_CHIPS, g.shape[1] // 2, g.shape[2]), g.dtype) for g in gs]
    return _Comm(list(gs), shapes, {}, len(gs), copies)


def _chip_scatter(ss):
    def copies(cin, cout, send, recv):
        x, y, c, me, sibling, chips = _place()
        sends = [_rcopy(send, recv, 3 * a + p, s.at[2 * px + py], r.at[p], (px, py, c))
                 for a, (s, r) in enumerate(zip(cin, cout)) for p, (px, py) in enumerate(chips)]
        return sends, sends

    shapes = [jax.ShapeDtypeStruct((3,) + s.shape[1:], s.dtype) for s in ss]
    return _Comm(list(ss), shapes, {}, 3 * len(ss), copies)


def _sibling_share(fs, layer):
    def copies(cin, cout, send, recv):
        x, y, c, me, sibling, chips = _place()
        sends, recvs = [], []
        for a, f in enumerate(cout):
            mine, theirs = f.at[layer, _half(f.shape[1], c)], f.at[layer, _half(f.shape[1], 1 - c)]
            sends.append(_rcopy(send, recv, a, mine, mine, sibling))
            recvs.append(_rcopy(send, recv, a, theirs, theirs, sibling))
        return sends, recvs

    return _Comm(list(fs), [jax.ShapeDtypeStruct(f.shape, f.dtype) for f in fs], {a: a for a in range(len(fs))},
                 len(fs), copies)


def _sum_rows(rows):
    return next(b for b in (256, 192, 128) if rows % b == 0)


def _pair_sum(g, r, core_arr, layer):
    _, rows, cols = r.shape
    br = _sum_rows(rows)
    nb = rows // br

    def body(c_ref, g_ref, r_ref, o_ref):
        o_ref[...] = (g_ref[...].astype(f32) + r_ref[...].astype(f32)).astype(bf16)

    return _pcall(
        body, name=f"pair_sum_{layer}", out_shape=jax.ShapeDtypeStruct(r.shape, bf16),
        grid_spec=pltpu.PrefetchScalarGridSpec(
            num_scalar_prefetch=1, grid=(N_CHIPS, nb),
            in_specs=[pl.BlockSpec((1, br, cols), lambda j, i, c_ref: (j, c_ref[0] * nb + i, 0)),
                      pl.BlockSpec((1, br, cols), lambda j, i, c_ref: (j, i, 0))],
            out_specs=pl.BlockSpec((1, br, cols), lambda j, i, c_ref: (j, i, 0))),
        compiler_params=_params("arbitrary", "arbitrary"))(core_arr, g, r)


def _chip_sum(s, r, place, layer, final):
    _, rows, cols = s.shape
    br = _sum_rows(rows)
    nb = rows // br

    def body(place_ref, s_ref, r_ref, *rest):
        o_ref = rest[-1]
        acc = s_ref[0].astype(f32)
        for p in range(3):
            acc = acc + r_ref[p].astype(f32)
        o_ref[...] = acc

    carried = [] if final is None else [final]
    return _pcall(
        body, name=f"chip_sum_{layer}", out_shape=jax.ShapeDtypeStruct((DEPTH, 2 * rows, cols), f32),
        grid_spec=pltpu.PrefetchScalarGridSpec(
            num_scalar_prefetch=1, grid=(nb,),
            in_specs=[pl.BlockSpec((1, br, cols), lambda i, place_ref: (place_ref[0], i, 0)),
                      pl.BlockSpec((3, br, cols), lambda i, place_ref: (0, i, 0))] + [ANY] * len(carried),
            out_specs=pl.BlockSpec((None, br, cols), lambda i, place_ref: (layer, place_ref[1] * nb + i, 0))),
        input_output_aliases={3: 0} if carried else {},
        compiler_params=_params("arbitrary"))(place, s, r, *carried)


def _inproj(xin, win, layer, comm=None, conv_w=None):
    T = xin.shape[0]
    tm = 512
    cast = xin.dtype != bf16
    conv = conv_w is not None

    def body(x_ref, w_ref, *rest):
        rest = list(rest)
        cw_ref = rest.pop(0) if conv else None
        o_ref = rest.pop(0)
        xt = x_ref[...].astype(bf16)
        if cast:
            rest.pop(0)[...] = xt
        if conv:
            mix_ref, halo = rest

            @pl.when(pl.program_id(0) == 0)
            def _():
                halo[...] = jnp.zeros_like(halo)

        accs = []
        for j in range(N_CHIPS):
            if conv and j == N_CHIPS - 1:
                cut1, cut2 = 2 * E_MIX - W_IN_COLS, 3 * E_MIX - 2 * W_IN_COLS
                bg = accs[0][:, :E_MIX]
                cg = jnp.concatenate([accs[0][:, E_MIX:], accs[1][:, :cut1]], axis=1)
                u = jnp.concatenate([accs[1][:, cut1:], accs[2][:, :cut2]], axis=1)
                cu = cg * u
                out, _, _ = _conv_taps(cu, halo[6:7, :], halo[7:8, :], cw_ref)
                mix_ref[...] = (bg * out).astype(bf16)
                halo[...] = cu[tm - 8:, :]
            acc = jnp.dot(xt, w_ref[j], preferred_element_type=f32)
            o_ref[:, j * W_IN_COLS:(j + 1) * W_IN_COLS] = acc.astype(bf16)
            if conv:
                accs.append(acc)

    tile = pl.BlockSpec((tm, D_MODEL), lambda i: (i, 0))
    outs, carried = _pcall_carry(
        body, comm, n_in=2 + conv, n_out=1 + cast + conv, name=f"inproj_{layer}", grid=(T // tm,),
        out_shape=(jax.ShapeDtypeStruct((T, N_IN), bf16),) + (jax.ShapeDtypeStruct((T, D_MODEL), bf16),) * (cast + conv),
        in_specs=[tile, pl.BlockSpec((N_CHIPS, D_MODEL, W_IN_COLS), lambda i: (0, 0, 0), pipeline_mode=pl.Buffered(1))]
        + [pl.BlockSpec((CONV_W, E_MIX), lambda i: (0, 0))] * conv,
        out_specs=(pl.BlockSpec((tm, N_IN), lambda i: (i, 0)),) + (tile,) * (cast + conv),
        scratch_shapes=[pltpu.VMEM((8, E_MIX), f32)] * conv,
        compiler_params=_params("arbitrary"))(xin, win, *([conv_w] if conv else []))
    return outs[0], (outs[1] if cast else xin), (outs[-1] if conv else None), carried


def _rel_index_rows():
    j = lax.broadcasted_iota(jnp.int32, (N_REL_PAD, KG), 1)
    r = lax.broadcasted_iota(jnp.int32, (N_REL_PAD, KG), 0)
    off = jnp.where(j < KG - 2 * CHUNK, j, j - KG)
    idx = jnp.clip(N_PREV * CHUNK - off, -REL_CLIP, REL_CLIP) + REL_CLIP
    return (idx == r).astype(f32)


def _bias_expand(table_pad, layer, comm=None):
    def body(t_ref, o_ref, row_scr):
        h = pl.program_id(0)

        @pl.when(h == 0)
        def _():
            row_scr[...] = jnp.dot(t_ref[...], _rel_index_rows(), precision=lax.Precision.HIGHEST,
                                   preferred_element_type=f32)

        q = lax.broadcasted_iota(jnp.int32, (QG, KG), 0)
        k = lax.broadcasted_iota(jnp.int32, (QG, KG), 1)
        band = (k // CHUNK >= q // CHUNK) & (k // CHUNK <= q // CHUNK + N_PREV)
        t = jnp.broadcast_to(row_scr[pl.ds(h, 1), :], (QG, KG))
        for b in range(8):
            t = jnp.where(((q >> b) & 1) == 1, pltpu.roll(t, 1 << b, axis=1), t)
        for v in range(3):
            o_ref[v] = jnp.where(band & (k >= (2 - v) * QG), t, NEG)

    (bias,), carried = _pcall_carry(
        body, comm, n_in=1, n_out=1, name=f"bias_expand_{layer}", grid=(N_HEADS,),
        out_shape=(jax.ShapeDtypeStruct((3, N_HEADS, QG, KG), f32),),
        in_specs=[pl.BlockSpec((N_HEADS, N_REL_PAD), lambda h: (0, 0))],
        out_specs=(pl.BlockSpec((3, None, QG, KG), lambda h: (0, h, 0, 0)),),
        scratch_shapes=[pltpu.VMEM((N_HEADS, KG), f32)], compiler_params=_params("arbitrary"))(table_pad)
    return bias, carried


def _bias_reduce(dbias, layer):
    def body(d_ref, o_ref, row_scr):
        q = lax.broadcasted_iota(jnp.int32, (QG, DB_COLS), 0)
        k = lax.broadcasted_iota(jnp.int32, (QG, DB_COLS), 1)
        for h in range(N_HEADS):
            t = jnp.where(k > q, d_ref[h], 0.0)
            for b in range(8):
                t = jnp.where(((q >> b) & 1) == 1, pltpu.roll(t, DB_COLS - (1 << b), axis=1), t)
            row_scr[h:h + 1, :] = jnp.sum(t, axis=0, keepdims=True)
        r = lax.broadcasted_iota(jnp.int32, (N_REL_PAD, DB_COLS), 0)
        off = lax.broadcasted_iota(jnp.int32, (N_REL_PAD, DB_COLS), 1)
        own = (off >= 1) & (off < REL_CLIP + CHUNK)
        sel = jnp.where(own & (r == 2 * REL_CLIP - off), 1.0, 0.0) - jnp.where(own & (r == 2 * REL_CLIP), 1.0, 0.0)
        o_ref[...] = lax.dot_general(row_scr[...], sel, NT, precision=lax.Precision.HIGHEST, preferred_element_type=f32)

    return _pcall(body, name=f"bias_reduce_{layer}", out_shape=jax.ShapeDtypeStruct((N_HEADS, N_REL_PAD), f32),
                  scratch_shapes=[pltpu.VMEM((N_HEADS, DB_COLS), f32)],
                  compiler_params=pltpu.CompilerParams(vmem_limit_bytes=VMEM_LIMIT))(dbias)


FWD_PAIRS = 8
BWD_PAIRS = 4


def _key_specs(n_groups, npairs, slab):
    per_slab = E_MIX // (128 * npairs)
    return [pl.BlockSpec((QG, 128 * npairs), functools.partial(
        lambda hp, g, jj: (jnp.clip(g - 2 + jj, 0, n_groups - 1), slab * per_slab + hp), jj=jj)) for jj in range(3)]


def _bias_spec(npairs):
    return pl.BlockSpec((None, 2 * npairs, QG, KG), lambda hp, g: (jnp.minimum(g, 2), hp, 0, 0))


def _attn_fwd(h, bias, layer, comm=None):
    T = h.shape[0]
    n_groups = T // QG
    scale = 1.0 / math.sqrt(HEAD_DIM)

    def body(q_ref, k0, k1, k2, v0, v1, v2, b_ref, o_ref, lse_ref):
        lane = lax.broadcasted_iota(jnp.int32, (1, 128), 1)
        ones = jnp.ones((KG, 128), bf16)
        lse = jnp.zeros((QG, 128), f32)
        for pp in range(FWD_PAIRS):
            cs = slice(pp * 128, (pp + 1) * 128)
            q2 = q_ref[:, cs] * scale
            kc = jnp.concatenate([k0[:, cs], k1[:, cs], k2[:, cs]], axis=0)
            vc = jnp.concatenate([jnp.concatenate([v0[:, cs], v1[:, cs], v2[:, cs]], axis=0), ones], axis=1)
            outs = []
            for hh in range(2):
                qm = jnp.where(lane // HEAD_DIM == hh, q2, jnp.zeros_like(q2))
                s = lax.dot_general(qm, kc, NT, preferred_element_type=f32) + b_ref[2 * pp + hh]
                m = jnp.max(s, axis=1, keepdims=True)
                ol = jnp.dot(jnp.exp(s - m).astype(bf16), vc, preferred_element_type=f32)
                outs.append(ol[:, :128] / ol[:, 128:])
                lse = jnp.where(lane == 2 * pp + hh, m + jnp.log(ol[:, 128:]), lse)
            o_ref[:, cs] = jnp.where(lane // HEAD_DIM == 0, outs[0], outs[1]).astype(bf16)
        lse_ref[...] = lse

    (mix, lse), carried = _pcall_carry(
        body, comm, n_in=8, n_out=2, name=f"attn_fwd_{layer}", grid=(N_HEADS // (2 * FWD_PAIRS), n_groups),
        out_shape=(jax.ShapeDtypeStruct((T, E_MIX), bf16), jax.ShapeDtypeStruct((T, 128), f32)),
        in_specs=[pl.BlockSpec((QG, 128 * FWD_PAIRS), lambda hp, g: (g, hp))] + _key_specs(n_groups, FWD_PAIRS, 1)
        + _key_specs(n_groups, FWD_PAIRS, 2) + [_bias_spec(FWD_PAIRS)],
        out_specs=(pl.BlockSpec((QG, 128 * FWD_PAIRS), lambda hp, g: (g, hp)), pl.BlockSpec((QG, 128), lambda hp, g: (g, 0))),
        compiler_params=_params("arbitrary", "arbitrary"))(h, h, h, h, h, h, h, bias)
    return mix, lse, carried


def _halo_rows(ref, r):
    return ref[r:r + 1, :].astype(f32)


def _conv_taps(cu, p6, p7, w_ref):
    row = lax.broadcasted_iota(jnp.int32, cu.shape, 0)
    r1 = jnp.where(row == 0, p7, pltpu.roll(cu, 1, axis=0))
    r2 = jnp.where(row == 0, p6, jnp.where(row == 1, p7, pltpu.roll(cu, 2, axis=0)))
    return w_ref[2:3, :] * cu + w_ref[1:2, :] * r1 + w_ref[0:1, :] * r2, r1, r2


def _mem_probs(qm_ref, kv_ref, hh):
    qh = qm_ref[:, hh * MEM_HEAD_DIM:(hh + 1) * MEM_HEAD_DIM]
    kh = kv_ref[:, hh * MEM_HEAD_DIM:(hh + 1) * MEM_HEAD_DIM]
    vh = kv_ref[:, E_MEM + hh * MEM_HEAD_DIM:E_MEM + (hh + 1) * MEM_HEAD_DIM]
    s = lax.dot_general(qh, kh, NT, preferred_element_type=f32) * (1.0 / math.sqrt(MEM_HEAD_DIM))
    e = jnp.exp(s - jnp.max(s, axis=1, keepdims=True))
    return e / jnp.sum(e, axis=1, keepdims=True), qh, kh, vh


def _h_tail_specs(tm):
    return [pl.BlockSpec((tm, E_MEM), functools.partial(lambda i, cb: (i, cb), cb=cb)) for cb in (6, 7, 8, 9)]


def _ln_bwd(dy, xhat, rstd, g, dob_ref, dxp_ref, dg_ref, db_ref):
    dg_ref[0:1, :] += jnp.sum(dy * xhat, axis=0, keepdims=True)
    db_ref[0:1, :] += jnp.sum(dy, axis=0, keepdims=True)
    gx = dy * g
    dr = rstd * (gx - jnp.mean(gx, axis=1, keepdims=True) - xhat * jnp.mean(gx * xhat, axis=1, keepdims=True))
    dxp_ref[...] = DN_ALPHA * dr
    dob_ref[...] = dr.astype(bf16)


def _post_fwd(h, mix, memb, wkv, wout, x, g, b, layer, target=None, comm=None):
    T = x.shape[0]
    tm = 512

    def body(qm_ref, z0, z1, z2, mix_ref, mem_ref, wkv_ref, w_ref, x_ref, g_ref, b_ref, *rest):
        *rest, ps_ref, ms_ref, kv_out, kv_ref = rest

        @pl.when(pl.program_id(0) == 0)
        def _():
            kv = jnp.dot(mem_ref[...], wkv_ref[...], preferred_element_type=f32).astype(bf16)
            kv_ref[...] = kv
            kv_out[...] = kv
            if target is not None:
                for acc_ref in rest[1], rest[4], rest[5]:
                    acc_ref[...] = jnp.zeros_like(acc_ref)

        mems = []
        for hh in range(MEM_HEADS):
            p, _, _, vh = _mem_probs(qm_ref, kv_ref, hh)
            pb = p.astype(bf16)
            ps_ref[:, hh * N_MEM:(hh + 1) * N_MEM] = pb
            mems.append(jnp.dot(pb, vh, preferred_element_type=f32))
        memb = jnp.concatenate(mems, axis=1).astype(bf16)
        ms_ref[...] = memb
        z = jnp.concatenate([z0[...], z1[...], z2[...]], axis=1)
        one = jnp.ones((), bf16)
        y = jnp.concatenate([mix_ref[...], memb], axis=1) * (z * (one / (one + jnp.exp(-z))))
        out = jnp.dot(y, w_ref[...], preferred_element_type=f32)
        r = DN_ALPHA * x_ref[...] + out
        mu = jnp.mean(r, axis=1, keepdims=True)
        var = jnp.mean(jnp.square(r - mu), axis=1, keepdims=True)
        rstd = lax.rsqrt(var + LN_EPS)
        xhat = (r - mu) * rstd
        xn = xhat * g_ref[...] + b_ref[...]
        if target is None:
            xn_ref, xb_ref, xh_ref, rs_ref = rest
            xn_ref[...] = xn
            xb_ref[...] = xn.astype(bf16)
            xh_ref[...] = xhat
            rs_ref[...] = rstd
        else:
            t_ref, l_ref, dob_ref, dxp_ref, dg_ref, db_ref = rest
            err = xn - t_ref[...]
            l_ref[...] += jnp.sum(jnp.square(err))
            _ln_bwd(err * (1.0 / D_MODEL), xhat, rstd, g_ref[...], dob_ref, dxp_ref, dg_ref, db_ref)

    tile = lambda w: pl.BlockSpec((tm, w), lambda i: (i, 0))
    const = lambda r, c: pl.BlockSpec((r, c), lambda i: (0, 0))
    resident = lambda r, c: pl.BlockSpec((r, c), lambda i: (0, 0), pipeline_mode=pl.Buffered(1))
    in_specs = _h_tail_specs(tm) + [tile(E_MIX), resident(N_MEM, D_MODEL), resident(D_MODEL, 2 * E_MEM),
                                    resident(E_BRANCH, D_MODEL), tile(D_MODEL), const(1, D_MODEL), const(1, D_MODEL)]
    kept = (jax.ShapeDtypeStruct((T, MEM_HEADS * N_MEM), bf16), jax.ShapeDtypeStruct((T, E_MEM), bf16),
            jax.ShapeDtypeStruct((N_MEM, 2 * E_MEM), bf16))
    kept_specs = (tile(MEM_HEADS * N_MEM), tile(E_MEM), const(N_MEM, 2 * E_MEM))
    kv_scratch = [pltpu.VMEM((N_MEM, 2 * E_MEM), bf16)]
    if target is None:
        return _pcall_carry(
            body, comm, n_in=11, n_out=7, name=f"post_fwd_{layer}", grid=(T // tm,),
            out_shape=(jax.ShapeDtypeStruct((T, D_MODEL), f32), jax.ShapeDtypeStruct((T, D_MODEL), bf16),
                       jax.ShapeDtypeStruct((T, D_MODEL), f32), jax.ShapeDtypeStruct((T, 1), f32)) + kept,
            in_specs=in_specs, out_specs=(tile(D_MODEL), tile(D_MODEL), tile(D_MODEL), tile(1)) + kept_specs,
            scratch_shapes=kv_scratch,
            compiler_params=_params("arbitrary"))(h, h, h, h, mix, memb, wkv, wout, x, g, b)
    return _pcall(
        body, name=f"post_fwd_loss_{layer}", grid=(T // tm,),
        out_shape=(jax.ShapeDtypeStruct((8, 128), f32), jax.ShapeDtypeStruct((T, D_MODEL), bf16),
                   jax.ShapeDtypeStruct((T, D_MODEL), f32), jax.ShapeDtypeStruct((8, D_MODEL), f32),
                   jax.ShapeDtypeStruct((8, D_MODEL), f32)) + kept,
        in_specs=in_specs + [tile(D_MODEL)],
        out_specs=(const(8, 128), tile(D_MODEL), tile(D_MODEL), const(8, D_MODEL), const(8, D_MODEL)) + kept_specs,
        scratch_shapes=kv_scratch,
        compiler_params=_params("arbitrary"))(h, h, h, h, mix, memb, wkv, wout, x, g, b, target)


def _post_bwd(dob, h, mix, memb, wout, kept, layer, comm=None):
    T = dob.shape[0]
    tm = 512
    n_tiles = T // tm
    inv = 1.0 / math.sqrt(MEM_HEAD_DIM)

    def body(dob_ref, qm_ref, z0, z1, z2, mix_ref, kv_ref, w_ref, ps_ref, ms_ref, mem_ref,
             dhb_ref, dmix_ref, gkv_out, dwo_out, dwo_ref, dkv_ref):
        @pl.when(pl.program_id(0) == 0)
        def _():
            dkv_ref[...] = jnp.zeros_like(dkv_ref)
            dwo_ref[...] = jnp.zeros_like(dwo_ref)

        dob = dob_ref[...]
        z = jnp.concatenate([z0[...], z1[...], z2[...]], axis=1).astype(f32)
        act, sig = _silu_parts(z)
        cat = jnp.concatenate([mix_ref[...].astype(f32), ms_ref[...].astype(f32)], axis=1)
        yb = (cat * act).astype(bf16)
        dwo_ref[...] += lax.dot_general(yb, dob, TN, preferred_element_type=f32)
        dyv = lax.dot_general(dob, w_ref[...], NT, preferred_element_type=f32)
        dz = dyv * cat * (sig * (1.0 + z * (1.0 - sig)))
        dcat = dyv * act
        dmix_ref[...] = dcat[:, :E_MIX].astype(bf16)
        dqs = []
        for hh in range(MEM_HEADS):
            cols = slice(hh * MEM_HEAD_DIM, (hh + 1) * MEM_HEAD_DIM)
            qh, kh, vh = qm_ref[:, cols], kv_ref[:, cols], kv_ref[:, E_MEM + hh * MEM_HEAD_DIM:E_MEM + (hh + 1) * MEM_HEAD_DIM]
            pb = ps_ref[:, hh * N_MEM:(hh + 1) * N_MEM]
            p = pb.astype(f32)
            dmem = dcat[:, E_MIX + hh * MEM_HEAD_DIM:E_MIX + (hh + 1) * MEM_HEAD_DIM].astype(bf16)
            dp = lax.dot_general(dmem, vh, NT, preferred_element_type=f32)
            ds = (p * (dp - jnp.sum(p * dp, axis=1, keepdims=True))).astype(bf16)
            dqs.append(jnp.dot(ds, kh, preferred_element_type=f32) * inv)
            dkv_ref[:, cols] += lax.dot_general(ds, qh, TN, preferred_element_type=f32) * inv
            dkv_ref[:, E_MEM + hh * MEM_HEAD_DIM:E_MEM + (hh + 1) * MEM_HEAD_DIM] += lax.dot_general(
                pb, dmem, TN, preferred_element_type=f32)
        dhb_ref[...] = jnp.concatenate(dqs + [dz], axis=1).astype(bf16)

        @pl.when(pl.program_id(0) == n_tiles - 1)
        def _():
            dwo_out[...] = dwo_ref[...].astype(bf16)
            gkv_out[...] = lax.dot_general(mem_ref[...], dkv_ref[...].astype(bf16), TN,
                                           preferred_element_type=f32).astype(bf16)

    tile = lambda w: pl.BlockSpec((tm, w), lambda i: (i, 0))
    const = lambda r, c: pl.BlockSpec((r, c), lambda i: (0, 0))
    resident = lambda r, c: pl.BlockSpec((r, c), lambda i: (0, 0), pipeline_mode=pl.Buffered(1))
    ps, ms, kv = kept
    return _pcall_carry(
        body, comm, n_in=11, n_out=4, name=f"post_bwd_{layer}", grid=(n_tiles,),
        out_shape=(jax.ShapeDtypeStruct((T, E_MEM + E_BRANCH), bf16), jax.ShapeDtypeStruct((T, E_MIX), bf16),
                   jax.ShapeDtypeStruct((D_MODEL, 2 * E_MEM), bf16), jax.ShapeDtypeStruct((E_BRANCH, D_MODEL), bf16)),
        in_specs=[tile(D_MODEL)] + _h_tail_specs(tm) + [tile(E_MIX), resident(N_MEM, 2 * E_MEM), resident(E_BRANCH, D_MODEL),
                                                        tile(MEM_HEADS * N_MEM), tile(E_MEM), resident(N_MEM, D_MODEL)],
        out_specs=(tile(E_MEM + E_BRANCH), tile(E_MIX), const(D_MODEL, 2 * E_MEM), const(E_BRANCH, D_MODEL)),
        scratch_shapes=[pltpu.VMEM((E_BRANCH, D_MODEL), f32), pltpu.VMEM((N_MEM, 2 * E_MEM), f32)],
        compiler_params=_params("arbitrary"))(dob, h, h, h, h, mix, kv, wout, ps, ms, memb)


def _attn_bwd(h, bias, dmix, lse, layer, comm=None):
    T = h.shape[0]
    n_groups = T // QG
    scale = 1.0 / math.sqrt(HEAD_DIM)

    def body(q_ref, k0, k1, k2, v0, v1, v2, do_ref, b_ref, lse_ref, dq_ref, dk_ref, dv_ref, db_ref, acck, accv):
        g = pl.program_id(1)

        @pl.when(g == 0)
        def _():
            acck[...] = jnp.zeros_like(acck)
            accv[...] = jnp.zeros_like(accv)
            db_ref[...] = jnp.zeros_like(db_ref)

        @pl.when(g < n_groups)
        def _():
            lane = lax.broadcasted_iota(jnp.int32, (1, 128), 1)
            first = lane // HEAD_DIM == 0
            for pp in range(BWD_PAIRS):
                cs = slice(pp * 128, (pp + 1) * 128)
                do2 = do_ref[:, cs]
                q2 = q_ref[:, cs] * scale
                kc = jnp.concatenate([k0[:, cs], k1[:, cs], k2[:, cs]], axis=0)
                vc = jnp.concatenate([v0[:, cs], v1[:, cs], v2[:, cs]], axis=0)
                q2t, do2t = q2.T, do2.T
                dqs, dks, dvs = [], [], []
                for hh in range(2):
                    hm = lane // HEAD_DIM == hh
                    head = (pl.program_id(0) * BWD_PAIRS + pp) * 2 + hh
                    lse = jnp.sum(jnp.where(lane == head, lse_ref[...], 0.0), axis=1, keepdims=True)
                    qm = jnp.where(hm, q2, jnp.zeros_like(q2))
                    dom = jnp.where(hm, do2, jnp.zeros_like(do2))
                    s = lax.dot_general(qm, kc, NT, preferred_element_type=f32) + b_ref[2 * pp + hh]
                    p = jnp.exp(s - lse)
                    dp = lax.dot_general(dom, vc, NT, preferred_element_type=f32)
                    ds = p * (dp - jnp.sum(p * dp, axis=1, keepdims=True))
                    db_ref[2 * pp + hh] += ds[:, KG - DB_COLS:]
                    dsb, pb = ds.astype(bf16), p.astype(bf16)
                    dqs.append(jnp.dot(dsb, kc, preferred_element_type=f32) * scale)
                    dks.append(jnp.dot(q2t[hh * HEAD_DIM:(hh + 1) * HEAD_DIM], dsb, preferred_element_type=f32))
                    dvs.append(jnp.dot(do2t[hh * HEAD_DIM:(hh + 1) * HEAD_DIM], pb, preferred_element_type=f32))
                dq_ref[:, cs] = jnp.where(first, dqs[0], dqs[1]).astype(bf16)
                dkc = jnp.concatenate(dks, axis=0).T
                dvc = jnp.concatenate(dvs, axis=0).T
                for jj in range(3):
                    slot = (g + 1 + jj) % 3
                    if jj == 2:
                        acck[slot, :, cs] = dkc[jj * QG:(jj + 1) * QG]
                        accv[slot, :, cs] = dvc[jj * QG:(jj + 1) * QG]
                    else:
                        acck[slot, :, cs] += dkc[jj * QG:(jj + 1) * QG]
                        accv[slot, :, cs] += dvc[jj * QG:(jj + 1) * QG]

        done = (g + 1) % 3
        dk_ref[...] = acck[done].astype(bf16)
        dv_ref[...] = accv[done].astype(bf16)

    last = n_groups - 1
    width = 128 * BWD_PAIRS
    qspec = pl.BlockSpec((QG, width), lambda hp, g: (jnp.minimum(g, last), hp))
    kout = pl.BlockSpec((QG, width), lambda hp, g: (jnp.clip(g - 2, 0, last), hp))
    dbspec = pl.BlockSpec((2 * BWD_PAIRS, QG, DB_COLS), lambda hp, g: (hp, 0, 0))
    lspec = pl.BlockSpec((QG, 128), lambda hp, g: (jnp.minimum(g, last), 0))
    return _pcall_carry(
        body, comm, n_in=10, n_out=4, name=f"attn_bwd_{layer}", grid=(N_HEADS // (2 * BWD_PAIRS), n_groups + 2),
        out_shape=(jax.ShapeDtypeStruct((T, E_MIX), bf16),) * 3 + (jax.ShapeDtypeStruct((N_HEADS, QG, DB_COLS), f32),),
        in_specs=[qspec] + _key_specs(n_groups, BWD_PAIRS, 1) + _key_specs(n_groups, BWD_PAIRS, 2)
        + [qspec, _bias_spec(BWD_PAIRS), lspec],
        out_specs=(qspec, kout, kout, dbspec),
        scratch_shapes=[pltpu.VMEM((3, QG, width), f32), pltpu.VMEM((3, QG, width), f32)],
        compiler_params=_params("arbitrary", "arbitrary"))(h, h, h, h, h, h, h, dmix, bias, lse)


def _conv_bwd(h, w, dmix, layer):
    T = h.shape[0]
    tm = 512
    n_tiles = T // tm

    def body(bg_ref, cg_ref, u_ref, cgp_ref, up_ref, dy_ref, bgn_ref, dyn_ref, w_ref, dbg_ref, dcg_ref, du_ref, dw_ref):
        i = pl.program_id(0)

        @pl.when(i == 0)
        def _():
            dw_ref[...] = jnp.zeros_like(dw_ref)

        first = (i == 0).astype(f32)
        final = (i == n_tiles - 1).astype(f32)
        bg, cg, u = bg_ref[...].astype(f32), cg_ref[...].astype(f32), u_ref[...].astype(f32)
        dy = dy_ref[...].astype(f32)
        cu = cg * u
        p6 = _halo_rows(cgp_ref, 14) * _halo_rows(up_ref, 14) * (1.0 - first)
        p7 = _halo_rows(cgp_ref, 15) * _halo_rows(up_ref, 15) * (1.0 - first)
        conv, r1, r2 = _conv_taps(cu, p6, p7, w_ref)
        dbg_ref[...] = (dy * conv).astype(bf16)
        dc = dy * bg
        n0 = _halo_rows(dyn_ref, 0) * _halo_rows(bgn_ref, 0) * (1.0 - final)
        n1 = _halo_rows(dyn_ref, 1) * _halo_rows(bgn_ref, 1) * (1.0 - final)
        row = lax.broadcasted_iota(jnp.int32, dc.shape, 0)
        f1 = jnp.where(row == tm - 1, n0, pltpu.roll(dc, tm - 1, axis=0))
        f2 = jnp.where(row == tm - 2, n0, jnp.where(row == tm - 1, n1, pltpu.roll(dc, tm - 2, axis=0)))
        dcu = w_ref[2:3, :] * dc + w_ref[1:2, :] * f1 + w_ref[0:1, :] * f2
        dcg_ref[...] = (dcu * u).astype(bf16)
        du_ref[...] = (dcu * cg).astype(bf16)
        dw_ref[0:1, :] += jnp.sum(dc * r2, axis=0, keepdims=True)
        dw_ref[1:2, :] += jnp.sum(dc * r1, axis=0, keepdims=True)
        dw_ref[2:3, :] += jnp.sum(dc * cu, axis=0, keepdims=True)

    tile = lambda slab: pl.BlockSpec((tm, E_MIX), lambda i: (i, slab))
    prev = lambda slab: pl.BlockSpec((16, E_MIX), lambda i: (jnp.maximum(i * (tm // 16) - 1, 0), slab))
    nxt = lambda slab: pl.BlockSpec((16, E_MIX), lambda i: (jnp.minimum((i + 1) * (tm // 16), T // 16 - 1), slab))
    return _pcall(
        body, name=f"conv_bwd_{layer}", grid=(n_tiles,),
        out_shape=(jax.ShapeDtypeStruct((T, E_MIX), bf16),) * 3 + (jax.ShapeDtypeStruct((8, E_MIX), f32),),
        in_specs=[tile(0), tile(1), tile(2), prev(1), prev(2), tile(0), nxt(0), nxt(0),
                  pl.BlockSpec((CONV_W, E_MIX), lambda i: (0, 0))],
        out_specs=(tile(0), tile(0), tile(0), pl.BlockSpec((8, E_MIX), lambda i: (0, 0))),
        compiler_params=_params("arbitrary"))(h, h, h, h, h, dmix, h, dmix, w)


def _inproj_bwd_dx(da, db, dc, dhb, dxp, win, layer, below=None, comm=None):
    T = dxp.shape[0]
    tm = 512

    def body(da_ref, db_ref, dc_ref, dhb_ref, dxp_ref, w_ref, *rest):
        if below is not None:
            xh_ref, rs_ref, g_ref, dob_ref, dxo_ref, dg_ref, db_out = rest

            @pl.when(pl.program_id(0) == 0)
            def _():
                dg_ref[...] = jnp.zeros_like(dg_ref)
                db_out[...] = jnp.zeros_like(db_out)

        dh = jnp.concatenate([da_ref[...], db_ref[...], dc_ref[...], dhb_ref[...]], axis=1)
        acc = dxp_ref[...]
        for j in range(N_CHIPS):
            acc = acc + lax.dot_general(dh[:, j * W_IN_COLS:(j + 1) * W_IN_COLS], w_ref[j], NT, preferred_element_type=f32)
        if below is None:
            rest[0][...] = acc
        else:
            _ln_bwd(acc, xh_ref[...], rs_ref[...], g_ref[...], dob_ref, dxo_ref, dg_ref, db_out)

    tile = lambda w: pl.BlockSpec((tm, w), lambda i: (i, 0))
    const = lambda r, c: pl.BlockSpec((r, c), lambda i: (0, 0))
    in_specs = [tile(E_MIX), tile(E_MIX), tile(E_MIX), tile(E_MEM + E_BRANCH), tile(D_MODEL),
                pl.BlockSpec((N_CHIPS, D_MODEL, W_IN_COLS), lambda i: (0, 0, 0), pipeline_mode=pl.Buffered(1))]
    if below is None:
        return _pcall_carry(
            body, comm, n_in=6, n_out=1, name=f"inproj_bwd_dx_{layer}", grid=(T // tm,),
            out_shape=(jax.ShapeDtypeStruct((T, D_MODEL), f32),), in_specs=in_specs, out_specs=(tile(D_MODEL),),
            compiler_params=_params("arbitrary"))(da, db, dc, dhb, dxp, win)
    return _pcall_carry(
        body, comm, n_in=9, n_out=4, name=f"inproj_bwd_dx_{layer}", grid=(T // tm,),
        out_shape=(jax.ShapeDtypeStruct((T, D_MODEL), bf16), jax.ShapeDtypeStruct((T, D_MODEL), f32),
                   jax.ShapeDtypeStruct((8, D_MODEL), f32), jax.ShapeDtypeStruct((8, D_MODEL), f32)),
        in_specs=in_specs + [tile(D_MODEL), tile(1), const(1, D_MODEL)],
        out_specs=(tile(D_MODEL), tile(D_MODEL), const(8, D_MODEL), const(8, D_MODEL)),
        compiler_params=_params("arbitrary"))(da, db, dc, dhb, dxp, win, *below)


def _dh_pieces():
    pieces, col = [], 0
    for src, width in enumerate((E_MIX, E_MIX, E_MIX, E_MEM + E_BRANCH)):
        lo = 0
        while lo < width:
            j, c0 = divmod(col + lo, W_IN_COLS)
            n = min(width - lo, W_IN_COLS - c0)
            pieces.append((src, lo, lo + n, j, c0, c0 + n))
            lo += n
        col += width
    return pieces


def _inproj_bwd_dw(da, db, dc, dhb, xb, layer, comm=None):
    T = xb.shape[0]
    tm = 1024 if T % 1024 == 0 else 512
    n_tiles = T // tm

    def body(da_ref, db_ref, dc_ref, dhb_ref, x_ref, o_ref, acc, stage, sem):
        i = pl.program_id(0)

        @pl.when(i == 0)
        def _():
            acc[...] = jnp.zeros_like(acc)

        srcs = (da_ref, db_ref, dc_ref, dhb_ref)
        xt = x_ref[...]
        for s, s0, s1, j, c0, c1 in _dh_pieces():
            acc[j, :, c0:c1] += lax.dot_general(xt, srcs[s][:, s0:s1], TN, preferred_element_type=f32)

        @pl.when(i == n_tiles - 1)
        def _():
            for j in range(N_CHIPS):
                stage[...] = acc[j].astype(bf16)
                cp = pltpu.make_async_copy(stage, o_ref.at[j], sem)
                cp.start()
                cp.wait()

    tile = lambda w: pl.BlockSpec((tm, w), lambda i: (i, 0))
    (dw,), carried = _pcall_carry(
        body, comm, n_in=5, n_out=1, name=f"inproj_bwd_dw_{layer}", grid=(n_tiles,),
        out_shape=(jax.ShapeDtypeStruct((N_CHIPS, D_MODEL, W_IN_COLS), bf16),),
        in_specs=[tile(E_MIX), tile(E_MIX), tile(E_MIX), tile(E_MEM + E_BRANCH), tile(D_MODEL)],
        out_specs=(ANY,),
        scratch_shapes=[pltpu.VMEM((N_CHIPS, D_MODEL, W_IN_COLS), f32), pltpu.VMEM((D_MODEL, W_IN_COLS), bf16),
                        pltpu.SemaphoreType.DMA],
        compiler_params=_params("arbitrary"))(da, db, dc, dhb, xb)
    return dw, carried


def _adamw(w, g, m, v, name):
    shape = w.shape
    cols = shape[-1]
    rows = w.size // cols
    args = [a.reshape(rows, cols) for a in (w, g, m, v)]
    br = 256 if rows % 256 == 0 and rows > 256 else rows

    def body(w_ref, g_ref, m_ref, v_ref, go_ref, d_ref, nm_ref, nv_ref):
        gg = g_ref[...]
        nm = ADAM_B1 * m_ref[...] + (1.0 - ADAM_B1) * gg
        nv = ADAM_B2 * v_ref[...] + (1.0 - ADAM_B2) * jnp.square(gg)
        m_hat = nm / (1.0 - ADAM_B1 ** ADAM_STEP)
        v_hat = nv / (1.0 - ADAM_B2 ** ADAM_STEP)
        go_ref[...] = gg
        d_ref[...] = -ADAM_LR * (m_hat / (jnp.sqrt(v_hat) + ADAM_EPS) + ADAM_WD * w_ref[...])
        nm_ref[...] = nm
        nv_ref[...] = nv

    spec = pl.BlockSpec((br, cols), lambda i: (i, 0))
    outs = _pcall(body, name=name, grid=(rows // br,), out_shape=(jax.ShapeDtypeStruct((rows, cols), f32),) * 4,
                  in_specs=[spec] * 4, out_specs=(spec,) * 4, compiler_params=_params("arbitrary"))(*args)
    return tuple(o.reshape(shape) for o in outs)


def kernel(x, mem, w_in, w_mem_kv, w_out, rel_bias, conv_w, ln_g, ln_b, loss_target, m_w_in, m_w_mem_kv, m_w_out, m_rel_bias, m_conv_w, m_ln_g, m_ln_b, v_w_in, v_w_mem_kv, v_w_out, v_rel_bias, v_conv_w, v_ln_g, v_ln_b):
    T = x.shape[1]
    x0 = x.reshape(T, D_MODEL)
    target = loss_target.reshape(T, D_MODEL)
    memb = mem.reshape(N_MEM, D_MODEL).astype(bf16)
    chip = 2 * lax.axis_index("x") + lax.axis_index("y")
    core = lax.axis_index("c")
    chip_arr = jnp.reshape(chip, (1,)).astype(jnp.int32)
    core_arr = jnp.reshape(core, (1,)).astype(jnp.int32)

    place = jnp.concatenate([chip_arr, core_arr])
    tables = jnp.pad(rel_bias, ((0, 0), (0, 0), (0, N_REL_PAD - N_REL)))

    shards = [w_in.astype(bf16), w_mem_kv.astype(bf16), w_out.astype(bf16)]
    biases = {}
    biases[0], near = _bias_expand(tables[0], 0, _gather_ici(shards[:1], 0, peers=(0, 1)))
    biases[2], far = _bias_expand(tables[1], 2, _gather_ici(shards[:1], 0, peers=(2,), own=False, into=near))
    gathered = {0: list(_comm_call(_gather_d2d(list(far)), "gather_d2d_0")) + [None, None]}
    conv_full = None

    xs, xbs, hs, mixes, kepts, xhats, rstds, lses = [x0], [x0], [], [], [], [], [], {}
    for layer in range(DEPTH):
        more = layer + 1 < DEPTH
        attention = layer % 2 == 0
        h, xbs[layer], mix, arrived = _inproj(xbs[layer], gathered[layer][0], layer,
                                              _gather_ici(shards, layer + 1) if more else None,
                                              None if attention else conv_full[layer // 2])
        passing = _gather_d2d(list(arrived)) if more else None
        if layer == 0:
            passing = _both(passing, _gather_ici(shards[1:], 0, extra=conv_w))
        if attention:
            mix, lses[layer], done = _attn_fwd(h, biases[layer], layer, passing)
        if layer == 0:
            *done, wkv_in, wout_in, cw_g = done
            gathered[0][1:] = _comm_call(_gather_d2d([wkv_in, wout_in]), "gather_d2d_0_rest")
            conv_full = jnp.transpose(cw_g, (1, 2, 0, 3)).reshape(DEPTH // 2, CONV_W, E_MIX)
        win, wkv, wout = gathered[layer]
        result = _post_fwd(h, mix, memb, wkv.reshape(D_MODEL, 2 * E_MEM), wout.reshape(E_BRANCH, D_MODEL), xs[layer],
                           ln_g[layer][None, :], ln_b[layer][None, :], layer, None if more else target,
                           None if attention else passing)
        if more:
            (xn, xnb, xhat, rstd, *kept), done = (result[0], done) if attention else result
            xs.append(xn); xbs.append(xnb); xhats.append(xhat); rstds.append(rstd)
            gathered[layer + 1] = list(done)
        else:
            lsum, dob, dxp, dg_last, db_last, *kept = result
        hs.append(h); mixes.append(mix); kepts.append(kept)

    dgs, dbs, dconvs, dtables = [None] * DEPTH, [None] * DEPTH, [None] * (DEPTH // 2), [None] * ((DEPTH + 1) // 2)
    dgs[DEPTH - 1], dbs[DEPTH - 1] = dg_last, db_last
    finals = [None, None, None]
    above = None
    for layer in reversed(range(DEPTH)):
        h = hs[layer]
        win, wkv, wout = gathered[layer]
        (dhb, dmix, g_wkv, dwo), from_sibling = _post_bwd(
            dob, h, mixes[layer], memb, wout.reshape(E_BRANCH, D_MODEL), kepts[layer], layer,
            _sibling_exchange(above) if above else None)
        sums = [_pair_sum(g, r, core_arr, layer + 1) for g, r in zip(above, from_sibling)] if above else None
        scatter = _chip_scatter(sums) if above else None
        if layer % 2 == 0:
            (da, db, dc, dbias), from_chips = _attn_bwd(h, biases[layer], dmix, lses[layer], layer, scatter)
            dtables[layer // 2] = _bias_reduce(dbias, layer)
        else:
            da, db, dc, dconvs[layer // 2] = _conv_bwd(h, conv_full[layer // 2], dmix, layer)
        if layer > 0:
            (dob_below, dxp_below, dgs[layer - 1], dbs[layer - 1]), landed = _inproj_bwd_dx(
                da, db, dc, dhb, dxp, win, layer, (xhats[layer - 1], rstds[layer - 1], ln_g[layer - 1][None, :]),
                scatter if layer % 2 == 1 else None)
            from_chips = landed if layer % 2 == 1 else from_chips
        share = None
        if above:
            finals = [_chip_sum(s, r, place, layer + 1, f) for s, r, f in zip(sums, from_chips, finals)]
            share = _sibling_share(finals, layer + 1)
        if layer == 0:
            pad8 = lambda a: jnp.pad(a, ((0, 8 - a.shape[0]), (0, 0)))
            small_mine = jnp.concatenate(dgs + dbs + dconvs + [pad8(t.reshape(-1, D_MODEL)) for t in dtables]
                                         + [jnp.tile(lsum, (1, D_MODEL // 128))], axis=0)
            share = _both(share, _small_exchange(small_mine))
        g_win, shared = _inproj_bwd_dw(da, db, dc, dhb, xbs[layer], layer, share)
        if layer == 0:
            *shared, small_slots = shared
        finals = list(shared) if above else finals
        above = [g_win, g_wkv.reshape(N_CHIPS, W_KV_ROWS, 2 * E_MEM),
                 dwo.reshape(N_CHIPS, W_OUT_ROWS, D_MODEL)]
        if layer > 0:
            dob, dxp = dob_below, dxp_below
    from_sibling = _comm_call(_sibling_exchange(above), "sibling_exchange_0")
    sums = [_pair_sum(g, r, core_arr, 0) for g, r in zip(above, from_sibling)]
    (dx,), from_chips = _inproj_bwd_dx(da, db, dc, dhb, dxp, win, 0, None, _chip_scatter(sums))
    finals = [_chip_sum(s, r, place, 0, f) for s, r, f in zip(sums, from_chips, finals)]
    grad_w_in, grad_w_mem_kv, grad_w_out = _comm_call(_sibling_share(finals, 0), "sibling_share_0")
    grad_x = dx.reshape(1, T, D_MODEL)

    device_arr = jnp.reshape(2 * chip + core, (1,)).astype(jnp.int32)
    small = _small_sum(small_slots, small_mine, device_arr)
    loss = small[8 * (2 * DEPTH + DEPTH // 2 + (DEPTH + 1) // 2), 0] * (0.5 / D_MODEL)
    grad_ln_g = jnp.stack([small[8 * l] for l in range(DEPTH)])
    grad_ln_b = jnp.stack([small[8 * (DEPTH + l)] for l in range(DEPTH)])
    conv_all = jnp.stack([small[8 * (2 * DEPTH + a):8 * (2 * DEPTH + a) + CONV_W] for a in range(DEPTH // 2)])
    grad_conv_w = lax.dynamic_slice_in_dim(conv_all, chip * (E_MIX // N_CHIPS), E_MIX // N_CHIPS, axis=2)
    t0 = 8 * (2 * DEPTH + DEPTH // 2)
    grad_rel_bias = jnp.stack([small[t0 + 8 * a:t0 + 8 * a + 6].reshape(N_HEADS, N_REL_PAD)[:, :N_REL]
                               for a in range((DEPTH + 1) // 2)])

    grads = [grad_w_in, grad_w_mem_kv, grad_w_out, grad_rel_bias, grad_conv_w, grad_ln_g, grad_ln_b]
    weights = [w_in, w_mem_kv, w_out, rel_bias, conv_w, ln_g, ln_b]
    moms = [m_w_in, m_w_mem_kv, m_w_out, m_rel_bias, m_conv_w, m_ln_g, m_ln_b]
    vels = [v_w_in, v_w_mem_kv, v_w_out, v_rel_bias, v_conv_w, v_ln_g, v_ln_b]
    names = ["w_in", "w_mem_kv", "w_out", "rel_bias", "conv_w", "ln_g", "ln_b"]
    upd = [_adamw(w, g, m, v, f"adamw_{n}") for w, g, m, v, n in zip(weights, grads, moms, vels, names)]
    grads, deltas, new_m, new_v = zip(*upd)
    return (loss, grad_x, *grads, *deltas, *new_m, *new_v)
```
